```python
import math
import jax, jax.numpy as jnp
from jax import lax
import numpy as np

D_MODEL = 2048
BATCH = 4
SEQ = 2048
DEPTH = 4
DEC_BATCH = 8
DEC_SEQ = 4
PAST_LEN = 16384
PAGE_SIZE = 128

N_A_LAYERS = DEPTH // 2
N_B_LAYERS = DEPTH - N_A_LAYERS
D_FF = ((8 * D_MODEL // 3 + 255) // 256) * 256
S5_GROUP = 16
S5_GROUPS = D_MODEL // S5_GROUP
S5_STATE = 64
S5_DT_MIN = 1e-3
S5_DT_MAX = 1e-1
HEAD_DIM = 128
HEADS = D_MODEL // HEAD_DIM
KV_HEADS = 4
REP = HEADS // KV_HEADS
DILATED_PATTERNS = ((128, 1), (512, 4), (2048, 16))
N_DIL = len(DILATED_PATTERNS)
Q_BLOCK = 128
N_MOD = 9 * DEPTH + 4
EPS = 1e-6

kernel_name = "yoco_s5_dilated_window_macaron_adaln_step"


def _rms_mod(x, g, shift, scale):
    xf = x.astype(jnp.float32)
    xn = xf * lax.rsqrt(jnp.mean(xf * xf, axis=-1, keepdims=True) + EPS)
    out = xn * g.astype(jnp.float32) * (1.0 + scale[:, None].astype(jnp.float32)) + shift[:, None].astype(jnp.float32)
    return out.astype(x.dtype)


def _swiglu(h, w_in, w_out):
    g, u = jnp.split(h @ w_in, 2, axis=-1)
    return (jax.nn.silu(g) * u) @ w_out


def _cplx_combine(e1, e2):
    a1r, a1i, b1r, b1i = e1
    a2r, a2i, b2r, b2i = e2
    return (a2r * a1r - a2i * a1i,
            a2r * a1i + a2i * a1r,
            a2r * b1r - a2i * b1i + b2r,
            a2r * b1i + a2i * b1r + b2i)


def _s5_mixer(u, h0, lam_re, lam_im, log_dt, b_re, b_im, c_re, c_im, d_skip, w_glu):
    f32 = jnp.float32
    Bn, L, D = u.shape
    lam_re = lam_re.astype(f32); lam_im = lam_im.astype(f32)
    b_re = b_re.astype(f32); b_im = b_im.astype(f32)
    dt = jnp.exp(log_dt.astype(f32))[:, None]
    decay = jnp.exp(lam_re * dt)
    ab_re = decay * jnp.cos(lam_im * dt)
    ab_im = decay * jnp.sin(lam_im * dt)
    den = lam_re * lam_re + lam_im * lam_im
    f_re = ((ab_re - 1.0) * lam_re + ab_im * lam_im) / den
    f_im = (ab_im * lam_re - (ab_re - 1.0) * lam_im) / den
    bb_re = f_re[..., None] * b_re - f_im[..., None] * b_im
    bb_im = f_re[..., None] * b_im + f_im[..., None] * b_re
    uf = u.astype(f32).reshape(Bn, L, S5_GROUPS, S5_GROUP)
    bu_re = jnp.einsum('blgc,gpc->blgp', uf, bb_re)
    bu_im = jnp.einsum('blgc,gpc->blgp', uf, bb_im)
    a_re = jnp.broadcast_to(ab_re, (1, L) + ab_re.shape)
    a_im = jnp.broadcast_to(ab_im, (1, L) + ab_im.shape)
    A_re, A_im, h_re, h_im = lax.associative_scan(_cplx_combine, (a_re, a_im, bu_re, bu_im), axis=1)
    if h0 is not None:
        h0_re = h0[0].astype(f32)[:, None]
        h0_im = h0[1].astype(f32)[:, None]
        h_re, h_im = (h_re + A_re * h0_re - A_im * h0_im,
                      h_im + A_re * h0_im + A_im * h0_re)
    y = (jnp.einsum('blgp,gcp->blgc', h_re, c_re.astype(f32))
         - jnp.einsum('blgp,gcp->blgc', h_im, c_im.astype(f32)))
    y = y.reshape(Bn, L, D) + d_skip.astype(f32) * u.astype(f32)
    z = jax.nn.gelu(y).astype(u.dtype)
    a, g = jnp.split(z @ w_glu, 2, axis=-1)
    return a * jax.nn.sigmoid(g), h_re[:, -1], h_im[:, -1]


def _dilated_prompt(q, k, v, dil, n_back):
    f32 = jnp.float32
    Bn, S, H, hd = q.shape
    n = S // dil
    nb = -(-n // Q_BLOCK)
    n_pad = nb * Q_BLOCK

    def classes(t):
        return t.reshape(Bn, n, dil, t.shape[2], hd).transpose(0, 2, 1, 3, 4)

    qc = jnp.pad(classes(q), ((0, 0), (0, 0), (0, n_pad - n), (0, 0), (0, 0)))
    pad_kv = ((0, 0), (0, 0), (Q_BLOCK, n_pad - n), (0, 0), (0, 0))
    kc = jnp.pad(classes(k), pad_kv)
    vc = jnp.pad(classes(v), pad_kv)
    qb = qc.reshape(Bn, dil, nb, Q_BLOCK, KV_HEADS, REP, hd)

    def band(t):
        tb = t.reshape(Bn, dil, nb + 1, Q_BLOCK, KV_HEADS, hd)
        return jnp.concatenate([tb[:, :, :-1], tb[:, :, 1:]], axis=3)

    kb = band(kc)
    vb = band(vc)
    s = jnp.einsum('brnigeh,brnjgh->brngeij', qb, kb, preferred_element_type=f32) * (hd ** -0.5)
    qi = jnp.arange(Q_BLOCK)[:, None]
    kj = jnp.arange(2 * Q_BLOCK)[None, :]
    dist = qi + Q_BLOCK - kj
    kidx = jnp.arange(nb)[:, None, None] * Q_BLOCK - Q_BLOCK + kj[None]
    mask = (dist >= 0) & (dist <= n_back) & (kidx >= 0)
    s = jnp.where(mask[:, None, None], s, -jnp.inf)
    m = jnp.max(s, axis=-1, keepdims=True)
    p = jnp.exp(s - m)
    l = jnp.sum(p, axis=-1, keepdims=True)
    o = jnp.einsum('brngeij,brnjgh->brngeih', p, vb.astype(f32)) / l
    lse = m[..., 0] + jnp.log(l[..., 0])
    o = o.transpose(0, 1, 2, 5, 3, 4, 6).reshape(Bn, dil, n_pad, H, hd)[:, :, :n]
    o = o.transpose(0, 2, 1, 3, 4).reshape(Bn, S, H, hd)
    lse = lse.transpose(0, 1, 2, 5, 3, 4).reshape(Bn, dil, n_pad, H)[:, :, :n]
    lse = lse.transpose(0, 2, 1, 3).reshape(Bn, S, H)
    return o, lse


def _dilated_sample(q, k_new, v_new, buf, dil, n_back):
    f32 = jnp.float32
    Bn, T, H, hd = q.shape
    Lw = buf.shape[1]
    k_all = jnp.concatenate([buf[:, :, 0].astype(k_new.dtype), k_new], axis=1)
    v_all = jnp.concatenate([buf[:, :, 1].astype(v_new.dtype), v_new], axis=1)
    idx = Lw + jnp.arange(T)[:, None] - dil * jnp.arange(n_back + 1)[None, :]
    valid = idx >= 0
    idx = jnp.maximum(idx, 0)
    kg = k_all[:, idx]
    vg = v_all[:, idx]
    qg = q.reshape(Bn, T, KV_HEADS, REP, hd)
    s = jnp.einsum('btgeh,btjgh->btgej', qg, kg, preferred_element_type=f32) * (hd ** -0.5)
    s = jnp.where(valid[:, None, None, :], s, -jnp.inf)
    m = jnp.max(s, axis=-1, keepdims=True)
    p = jnp.exp(s - m)
    l = jnp.sum(p, axis=-1, keepdims=True)
    o = jnp.einsum('btgej,btjgh->btgeh', p, vg.astype(f32)) / l
    lse = m[..., 0] + jnp.log(l[..., 0])
    return o.reshape(Bn, T, H, hd), lse.reshape(Bn, T, H)


def _dilated_mixer(h, kv, kv_caches, w_q, w_o):
    Bn, L, _ = h.shape
    q = (h @ w_q).reshape(Bn, L, N_DIL, HEADS, HEAD_DIM)
    outs, lses = [], []
    for g, (w, d) in enumerate(DILATED_PATTERNS):
        k = kv[:, :, g, 0]
        v = kv[:, :, g, 1]
        if kv_caches is None:
            o, lse = _dilated_prompt(q[:, :, g], k, v, d, w // d)
        else:
            o, lse = _dilated_sample(q[:, :, g], k, v, kv_caches[g], d, w // d)
        outs.append(o)
        lses.append(lse)
    wts = jax.nn.softmax(jnp.stack(lses, axis=0), axis=0)
    o = jnp.sum(wts[..., None] * jnp.stack(outs, axis=0), axis=0)
    return o.reshape(Bn, L, HEADS * HEAD_DIM).astype(h.dtype) @ w_o


def _trunk(x, c, s5_state, kv_caches, w_mod, b_mod, norm_g, ffn_w_in, ffn_w_out,
           s5_lambda_re, s5_lambda_im, s5_log_dt, s5_b_re, s5_b_im, s5_c_re, s5_c_im, s5_d, s5_w_glu,
           kv_norm_g, w_kv, attn_w_q, attn_w_o, final_norm_g):
    Bn, L, _ = x.shape
    mod = (jax.nn.silu(c) @ w_mod + b_mod).reshape(Bn, N_MOD, D_MODEL)
    new_re, new_im = [], []
    kv = None
    for layer in range(DEPTH):
        if layer == N_A_LAYERS:
            hk = _rms_mod(x, kv_norm_g, mod[:, 9 * DEPTH], mod[:, 9 * DEPTH + 1])
            kv = (hk @ w_kv).reshape(Bn, L, N_DIL, 2, KV_HEADS, HEAD_DIM)
        m = mod[:, 9 * layer: 9 * layer + 9]
        h = _rms_mod(x, norm_g[layer, 0], m[:, 0], m[:, 1])
        x = x + 0.5 * m[:, 2, None] * _swiglu(h, ffn_w_in[layer, 0], ffn_w_out[layer, 0])
        h = _rms_mod(x, norm_g[layer, 1], m[:, 3], m[:, 4])
        if layer < N_A_LAYERS:
            h0 = None if s5_state is None else (s5_state[0][layer], s5_state[1][layer])
            y, h_re, h_im = _s5_mixer(h, h0, s5_lambda_re[layer], s5_lambda_im[layer], s5_log_dt[layer],
                                      s5_b_re[layer], s5_b_im[layer], s5_c_re[layer], s5_c_im[layer],
                                      s5_d[layer], s5_w_glu[layer])
            new_re.append(h_re)
            new_im.append(h_im)
        else:
            bl = layer - N_A_LAYERS
            y = _dilated_mixer(h, kv, kv_caches, attn_w_q[bl], attn_w_o[bl])
        x = x + m[:, 5, None] * y
        h = _rms_mod(x, norm_g[layer, 2], m[:, 6], m[:, 7])
        x = x + 0.5 * m[:, 8, None] * _swiglu(h, ffn_w_in[layer, 1], ffn_w_out[layer, 1])
    y = _rms_mod(x, final_norm_g, mod[:, 9 * DEPTH + 2], mod[:, 9 * DEPTH + 3])
    return y, jnp.stack(new_re, axis=0), jnp.stack(new_im, axis=0), kv


def setup_inputs(seed: int = 0) -> dict:
    key = jax.random.key(seed)
    ks = jax.random.split(key, 32)
    f32 = jnp.float32

    def nrm(k, shape, scale):
        return scale * jax.random.normal(k, shape, f32)

    q_width = N_DIL * HEADS * HEAD_DIM
    kv_width = N_DIL * 2 * KV_HEADS * HEAD_DIM
    kv_caches = [nrm(ks[6 + g], (DEC_BATCH, min(w, PAST_LEN), 2, KV_HEADS, HEAD_DIM), 1.0)
                 for g, (w, _) in enumerate(DILATED_PATTERNS)]
    s5_shape = (N_A_LAYERS, S5_GROUPS, S5_STATE)
    lam_im = jnp.pi * jnp.arange(S5_STATE, dtype=f32) + nrm(ks[15], s5_shape, 0.01)
    return {
        "x_prompt": nrm(ks[0], (BATCH, SEQ, D_MODEL), 1.0),
        "x_sample": nrm(ks[1], (DEC_BATCH, DEC_SEQ, D_MODEL), 1.0),
        "c_prompt": nrm(ks[2], (BATCH, D_MODEL), 1.0),
        "c_sample": nrm(ks[3], (DEC_BATCH, D_MODEL), 1.0),
        "state_s5_re": nrm(ks[4], (N_A_LAYERS, DEC_BATCH, S5_GROUPS, S5_STATE), 0.3),
        "state_s5_im": nrm(ks[5], (N_A_LAYERS, DEC_BATCH, S5_GROUPS, S5_STATE), 0.3),
        "cache_kv_g0": kv_caches[0],
        "cache_kv_g1": kv_caches[1],
        "cache_kv_g2": kv_caches[2],
        "w_mod": nrm(ks[9], (D_MODEL, N_MOD * D_MODEL), 0.5 * D_MODEL ** -0.5),
        "b_mod": nrm(ks[10], (N_MOD * D_MODEL,), 0.02),
        "norm_g": 1.0 + nrm(ks[11], (DEPTH, 3, D_MODEL), 0.05),
        "ffn_w_in": nrm(ks[12], (DEPTH, 2, D_MODEL, 2 * D_FF), D_MODEL ** -0.5),
        "ffn_w_out": nrm(ks[13], (DEPTH, 2, D_FF, D_MODEL), D_FF ** -0.5),
        "s5_lambda_re": -0.5 + nrm(ks[14], s5_shape, 0.01),
        "s5_lambda_im": lam_im,
        "s5_log_dt": jax.random.uniform(ks[16], (N_A_LAYERS, S5_GROUPS), f32,
                                        math.log(S5_DT_MIN), math.log(S5_DT_MAX)),
        "s5_b_re": nrm(ks[17], (N_A_LAYERS, S5_GROUPS, S5_STATE, S5_GROUP), (2 * S5_GROUP) ** -0.5),
        "s5_b_im": nrm(ks[18], (N_A_LAYERS, S5_GROUPS, S5_STATE, S5_GROUP), (2 * S5_GROUP) ** -0.5),
        "s5_c_re": nrm(ks[19], (N_A_LAYERS, S5_GROUPS, S5_GROUP, S5_STATE), S5_STATE ** -0.5),
        "s5_c_im": nrm(ks[20], (N_A_LAYERS, S5_GROUPS, S5_GROUP, S5_STATE), S5_STATE ** -0.5),
        "s5_d": nrm(ks[21], (N_A_LAYERS, D_MODEL), 1.0),
        "s5_w_glu": nrm(ks[22], (N_A_LAYERS, D_MODEL, 2 * D_MODEL), D_MODEL ** -0.5),
        "kv_norm_g": 1.0 + nrm(ks[23], (D_MODEL,), 0.05),
        "w_kv": nrm(ks[24], (D_MODEL, kv_width), D_MODEL ** -0.5),
        "attn_w_q": nrm(ks[25], (N_B_LAYERS, D_MODEL, q_width), D_MODEL ** -0.5),
        "attn_w_o": nrm(ks[26], (N_B_LAYERS, HEADS * HEAD_DIM, D_MODEL), (HEADS * HEAD_DIM) ** -0.5),
        "final_norm_g": 1.0 + nrm(ks[27], (D_MODEL,), 0.05),
    }


def reference(x_prompt, x_sample, c_prompt, c_sample, state_s5_re, state_s5_im,
              cache_kv_g0, cache_kv_g1, cache_kv_g2, w_mod, b_mod, norm_g, ffn_w_in, ffn_w_out,
              s5_lambda_re, s5_lambda_im, s5_log_dt, s5_b_re, s5_b_im, s5_c_re, s5_c_im, s5_d, s5_w_glu,
              kv_norm_g, w_kv, attn_w_q, attn_w_o, final_norm_g):
    weights = (w_mod, b_mod, norm_g, ffn_w_in, ffn_w_out,
               s5_lambda_re, s5_lambda_im, s5_log_dt, s5_b_re, s5_b_im, s5_c_re, s5_c_im, s5_d, s5_w_glu,
               kv_norm_g, w_kv, attn_w_q, attn_w_o, final_norm_g)
    y_prompt, s5re_p, s5im_p, kv_p = _trunk(x_prompt, c_prompt, None, None, *weights)
    y_sample, s5re_s, s5im_s, kv_s = _trunk(x_sample, c_sample, (state_s5_re, state_s5_im),
                                            (cache_kv_g0, cache_kv_g1, cache_kv_g2), *weights)
    S = x_prompt.shape[1]
    kvp = [kv_p[:, S - min(w, S):, g] for g, (w, _) in enumerate(DILATED_PATTERNS)]
    return (y_prompt, y_sample, s5re_p, s5im_p, kvp[0], kvp[1], kvp[2],
            s5re_s, s5im_s, kv_s[:, :, 0], kv_s[:, :, 1], kv_s[:, :, 2])
```

```python
import functools
import math

import jax
import jax.numpy as jnp
from jax import lax
from jax.experimental import pallas as pl
from jax.experimental.pallas import tpu as pltpu

F32 = jnp.float32
BF16 = jnp.bfloat16

D_MODEL = 2048
DEPTH = 4
N_A_LAYERS = DEPTH // 2
D_FF = 5632
S5_GROUP = 16
S5_GROUPS = D_MODEL // S5_GROUP
S5_STATE = 64
S5_NSTATE = S5_GROUPS * S5_STATE
HEAD_DIM = 128
HEADS = 16
KV_HEADS = 4
REP = HEADS // KV_HEADS
DILATED_PATTERNS = ((128, 1), (512, 4), (2048, 16))
N_DIL = len(DILATED_PATTERNS)
Q_BLOCK = 128
N_MOD = 9 * DEPTH + 4
EPS = 1e-6
ATTN_SCALE = HEAD_DIM ** -0.5

VMEM_LIMIT_BYTES = 56 * 1024 * 1024
SUBLANES = 8
S5_COL_BLOCK = 1024
S5_CH_BLOCK = S5_COL_BLOCK // S5_STATE * S5_GROUP
S5_N_BLOCKS = S5_NSTATE // S5_COL_BLOCK


def _params(*sem):
    return pltpu.CompilerParams(dimension_semantics=sem, vmem_limit_bytes=VMEM_LIMIT_BYTES)


def _rms_mod(x, g, shift, scale):
    xn = x * lax.rsqrt(jnp.mean(x * x, axis=-1, keepdims=True) + EPS)
    return xn * g * (1.0 + scale) + shift


def _norm_rows(x_ref, g_ref, sh_ref, sc_ref, h_ref, row_chunk):
    rows = x_ref.shape[0]
    g = g_ref[...]
    mod_rows = sh_ref.shape[0]

    def body(c, carry):
        r0 = pl.multiple_of(c * row_chunk, row_chunk)
        x = x_ref[pl.ds(r0, row_chunk), :]
        if mod_rows == rows and rows != row_chunk:
            sh = sh_ref[pl.ds(r0, row_chunk), :]
            sc = sc_ref[pl.ds(r0, row_chunk), :]
        else:
            sh = sh_ref[...]
            sc = sc_ref[...]
        h_ref[pl.ds(r0, row_chunk), :] = _rms_mod(x, g, sh, sc).astype(h_ref.dtype)
        return carry

    lax.fori_loop(0, rows // row_chunk, body, 0)


def _mod_kernel(c_ref, w_ref, b_ref, o_ref):
    c = c_ref[...]
    a = (c * jax.nn.sigmoid(c)).astype(BF16)
    o_ref[...] = jnp.dot(a, w_ref[...].astype(BF16), preferred_element_type=F32) + b_ref[...]


def _mod_call(c_all, w_mod, b_mod):
    rows = c_all.shape[0]
    n = w_mod.shape[1]
    tn = 1024
    return pl.pallas_call(
        _mod_kernel,
        grid=(n // tn,),
        in_specs=[
            pl.BlockSpec((rows, D_MODEL), lambda j: (0, 0)),
            pl.BlockSpec((D_MODEL, tn), lambda j: (0, j)),
            pl.BlockSpec((1, tn), lambda j: (0, j)),
        ],
        out_specs=pl.BlockSpec((rows, tn), lambda j: (0, j)),
        out_shape=jax.ShapeDtypeStruct((rows, n), F32),
        compiler_params=_params("arbitrary"),
        name="adaln_mod",
    )(c_all, w_mod, b_mod.reshape(1, n))


def _ffn_kernel(x_ref, sh_ref, sc_ref, gt_ref, g_ref, wg_ref, wu_ref, wo_ref, o_ref, h_ref, *, row_chunk):
    k = pl.program_id(1)

    @pl.when(k == 0)
    def _():
        _norm_rows(x_ref, g_ref, sh_ref, sc_ref, h_ref, row_chunk)

    h = h_ref[...]
    g = jnp.dot(h, wg_ref[...], preferred_element_type=F32)
    u = jnp.dot(h, wu_ref[...], preferred_element_type=F32)
    a = (g * jax.nn.sigmoid(g) * u).astype(BF16)
    part = jnp.dot(a, wo_ref[...], preferred_element_type=F32)

    @pl.when(k == 0)
    def _():
        o_ref[...] = part

    @pl.when(k > 0)
    def _():
        o_ref[...] += part

    @pl.when(k == pl.num_programs(1) - 1)
    def _():
        o_ref[...] = x_ref[...] + (0.5 * gt_ref[...]) * o_ref[...]


def _ffn_call(x, shift, scale, gate, g_norm, w_in, w_out, layer, which, tm, tf):
    m = x.shape[0]
    nb, r, _ = shift.shape
    tiles_per_seq = (m // nb) // tm if r == 1 else 1
    nk = D_FF // tf
    mod_spec = pl.BlockSpec((None, r, D_MODEL), lambda i, k: (i // tiles_per_seq, 0, 0))
    return pl.pallas_call(
        functools.partial(_ffn_kernel, row_chunk=min(tm, 64)),
        grid=(m // tm, nk),
        in_specs=[
            pl.BlockSpec((tm, D_MODEL), lambda i, k: (i, 0)),
            mod_spec, mod_spec, mod_spec,
            pl.BlockSpec((1, D_MODEL), lambda i, k: (0, 0)),
            pl.BlockSpec((None, None, D_MODEL, tf), lambda i, k: (layer, which, 0, k)),
            pl.BlockSpec((None, None, D_MODEL, tf), lambda i, k: (layer, which, 0, k + nk)),
            pl.BlockSpec((None, None, tf, D_MODEL), lambda i, k: (layer, which, k, 0)),
        ],
        out_specs=pl.BlockSpec((tm, D_MODEL), lambda i, k: (i, 0)),
        out_shape=jax.ShapeDtypeStruct((m, D_MODEL), F32),
        scratch_shapes=[pltpu.VMEM((tm, D_MODEL), BF16)],
        compiler_params=_params("parallel", "arbitrary"),
        name="ffn_swiglu",
    )(x, shift, scale, gate, g_norm, w_in, w_in, w_out)


def _norm_matmul_kernel(x_ref, sh_ref, sc_ref, g_ref, w_ref, o_ref, h_ref, *, row_chunk):
    @pl.when(pl.program_id(1) == 0)
    def _():
        _norm_rows(x_ref, g_ref, sh_ref, sc_ref, h_ref, row_chunk)

    o_ref[...] = jnp.dot(h_ref[...], w_ref[...], preferred_element_type=F32).astype(o_ref.dtype)


def _norm_matmul_call(x, shift, scale, g_norm, w, out_dtype, tm, tn, group_major=False):
    m = x.shape[0]
    nb, r, _ = shift.shape
    n = w.shape[1]
    tiles_per_seq = (m // nb) // tm if r == 1 else 1
    mod_spec = pl.BlockSpec((None, r, D_MODEL), lambda i, j: (i // tiles_per_seq, 0, 0))
    return pl.pallas_call(
        functools.partial(_norm_matmul_kernel, row_chunk=min(tm, 64)),
        grid=(m // tm, n // tn),
        in_specs=[
            pl.BlockSpec((tm, D_MODEL), lambda i, j: (i, 0)),
            mod_spec, mod_spec,
            pl.BlockSpec((1, D_MODEL), lambda i, j: (0, 0)),
            pl.BlockSpec((D_MODEL, tn), lambda i, j: (0, j)),
        ],
        out_specs=(pl.BlockSpec((None, tm, tn), lambda i, j: (j, i, 0)) if group_major
                   else pl.BlockSpec((tm, tn), lambda i, j: (i, j))),
        out_shape=jax.ShapeDtypeStruct((n // tn, m, tn) if group_major else (m, n), out_dtype),
        scratch_shapes=[pltpu.VMEM((tm, D_MODEL), BF16)],
        compiler_params=_params("parallel", "arbitrary"),
        name="norm_matmul",
    )(x, shift, scale, g_norm, w)


def _proj_residual_kernel(z_ref, x_ref, gt_ref, *refs, glu):
    z = z_ref[...]
    if glu:
        wa_ref, wg_ref, o_ref = refs
        a = jnp.dot(z, wa_ref[...], preferred_element_type=F32)
        g = jnp.dot(z, wg_ref[...], preferred_element_type=F32)
        y = a * jax.nn.sigmoid(g)
    else:
        w_ref, o_ref = refs
        y = jnp.dot(z, w_ref[...], preferred_element_type=F32)
    o_ref[...] = x_ref[...] + gt_ref[...] * y


def _proj_residual_call(z, x, gate, w, glu, tm, tn):
    m = x.shape[0]
    nb, r, _ = gate.shape
    tiles_per_seq = (m // nb) // tm if r == 1 else 1
    n_blocks = D_MODEL // tn
    w_specs = [pl.BlockSpec((D_MODEL, tn), lambda i, j: (0, j))]
    w_args = [w]
    if glu:
        w_specs.append(pl.BlockSpec((D_MODEL, tn), lambda i, j: (0, j + n_blocks)))
        w_args.append(w)
    return pl.pallas_call(
        functools.partial(_proj_residual_kernel, glu=glu),
        grid=(m // tm, n_blocks),
        in_specs=[
            pl.BlockSpec((tm, D_MODEL), lambda i, j: (i, 0)),
            pl.BlockSpec((tm, tn), lambda i, j: (i, j)),
            pl.BlockSpec((None, r, tn), lambda i, j: (i // tiles_per_seq, 0, j)),
        ] + w_specs,
        out_specs=pl.BlockSpec((tm, tn), lambda i, j: (i, j)),
        out_shape=jax.ShapeDtypeStruct((m, D_MODEL), F32),
        compiler_params=_params("parallel", "arbitrary"),
        name="proj_residual_glu" if glu else "proj_residual",
    )(z, x, gate, *w_args)


def _final_norm_kernel(x_ref, sh_ref, sc_ref, g_ref, o_ref, *, row_chunk):
    _norm_rows(x_ref, g_ref, sh_ref, sc_ref, o_ref, row_chunk)


def _final_norm_call(x, shift, scale, g_norm, tm):
    m = x.shape[0]
    nb, r, _ = shift.shape
    tiles_per_seq = (m // nb) // tm if r == 1 else 1
    mod_spec = pl.BlockSpec((None, r, D_MODEL), lambda i: (i // tiles_per_seq, 0, 0))
    return pl.pallas_call(
        functools.partial(_final_norm_kernel, row_chunk=min(tm, 64)),
        grid=(m // tm,),
        in_specs=[
            pl.BlockSpec((tm, D_MODEL), lambda i: (i, 0)),
            mod_spec, mod_spec,
            pl.BlockSpec((1, D_MODEL), lambda i: (0, 0)),
        ],
        out_specs=pl.BlockSpec((tm, D_MODEL), lambda i: (i, 0)),
        out_shape=jax.ShapeDtypeStruct((m, D_MODEL), F32),
        compiler_params=_params("parallel"),
        name="final_norm",
    )(x, shift, scale, g_norm)


def _s5_disc_kernel(lre_ref, lim_ref, ldt_ref, bre_ref, bim_ref, are_ref, aim_ref, bbre_ref, bbim_ref):
    lam_re = lre_ref[...]
    lam_im = lim_ref[...]
    dt = jnp.exp(ldt_ref[...])
    decay = jnp.exp(lam_re * dt)
    ab_re = decay * jnp.cos(lam_im * dt)
    ab_im = decay * jnp.sin(lam_im * dt)
    den = lam_re * lam_re + lam_im * lam_im
    f_re = ((ab_re - 1.0) * lam_re + ab_im * lam_im) / den
    f_im = (ab_im * lam_re - (ab_re - 1.0) * lam_im) / den
    b_re = bre_ref[...]
    b_im = bim_ref[...]
    are_ref[...] = ab_re
    aim_ref[...] = ab_im
    bbre_ref[...] = f_re * b_re - f_im * b_im
    bbim_ref[...] = f_re * b_im + f_im * b_re


def _s5_disc_call(lam_re, lam_im, log_dt, b_re, b_im):
    shape = jax.ShapeDtypeStruct(lam_re.shape, F32)
    return pl.pallas_call(
        _s5_disc_kernel,
        out_shape=(shape, shape, shape, shape),
        name="s5_discretise",
    )(lam_re, lam_im, log_dt, b_re, b_im)


def _s5_kernel(x_ref, sh_ref, sc_ref, g_ref, bbre_ref, bbim_ref, are_ref, aim_ref, cre_ref, cim_ref,
               d_ref, h0re_ref, h0im_ref, z_ref, hre_out, him_out,
               u_scr, sre_scr, sim_scr, *, row_chunk):
    t = pl.program_id(0)
    rows = x_ref.shape[0]
    steps = rows // SUBLANES

    @pl.when(t == 0)
    def _():
        hre_out[...] = h0re_ref[...]
        him_out[...] = h0im_ref[...]

    _norm_rows(x_ref, g_ref, sh_ref, sc_ref, u_scr, row_chunk)

    for cb in range(S5_N_BLOCKS):
        ch = slice(cb * S5_CH_BLOCK, (cb + 1) * S5_CH_BLOCK)
        st = slice(cb * S5_COL_BLOCK, (cb + 1) * S5_COL_BLOCK)
        u_blk = u_scr[:, ch]
        u_bf = u_blk.astype(BF16)
        sre_scr[...] = jnp.dot(u_bf, bbre_ref[cb], preferred_element_type=F32)
        sim_scr[...] = jnp.dot(u_bf, bbim_ref[cb], preferred_element_type=F32)

        a_re = jnp.broadcast_to(are_ref[cb], (SUBLANES, S5_COL_BLOCK))
        a_im = jnp.broadcast_to(aim_ref[cb], (SUBLANES, S5_COL_BLOCK))

        def step(i, carry, a_re=a_re, a_im=a_im):
            h_re, h_im = carry
            r0 = pl.multiple_of(i * SUBLANES, SUBLANES)
            n_re = a_re * h_re - a_im * h_im + sre_scr[pl.ds(r0, SUBLANES), :]
            n_im = a_re * h_im + a_im * h_re + sim_scr[pl.ds(r0, SUBLANES), :]
            sre_scr[pl.ds(r0, SUBLANES), :] = n_re
            sim_scr[pl.ds(r0, SUBLANES), :] = n_im
            return n_re, n_im

        h_re, h_im = lax.fori_loop(0, steps, step, (hre_out[:, st], him_out[:, st]),
                                   unroll=min(steps, 4))
        hre_out[:, st] = h_re
        him_out[:, st] = h_im

        y = (jnp.dot(sre_scr[...].astype(BF16), cre_ref[cb], preferred_element_type=F32)
             - jnp.dot(sim_scr[...].astype(BF16), cim_ref[cb], preferred_element_type=F32))
        y = y + d_ref[:, ch] * u_blk
        z_ref[:, ch] = jax.nn.gelu(y).astype(z_ref.dtype)


def _s5_call(x_tm, shift_pat, scale_pat, g_norm, bb_re, bb_im, a_re, a_im, c_re, c_im, d_skip,
             h0_re, h0_im, steps_per_chunk):
    rows = x_tm.shape[0]
    tr = steps_per_chunk * SUBLANES
    row_chunk = shift_pat.shape[0]
    const2 = lambda t: (0, 0)
    const3 = lambda t: (0, 0, 0)
    state_spec = pl.BlockSpec((SUBLANES, S5_NSTATE), const2)
    return pl.pallas_call(
        functools.partial(_s5_kernel, row_chunk=row_chunk),
        grid=(rows // tr,),
        in_specs=[
            pl.BlockSpec((tr, D_MODEL), lambda t: (t, 0)),
            pl.BlockSpec((row_chunk, D_MODEL), const2),
            pl.BlockSpec((row_chunk, D_MODEL), const2),
            pl.BlockSpec((1, D_MODEL), const2),
            pl.BlockSpec((S5_N_BLOCKS, S5_CH_BLOCK, S5_COL_BLOCK), const3, pipeline_mode=pl.Buffered(1)),
            pl.BlockSpec((S5_N_BLOCKS, S5_CH_BLOCK, S5_COL_BLOCK), const3, pipeline_mode=pl.Buffered(1)),
            pl.BlockSpec((S5_N_BLOCKS, 1, S5_COL_BLOCK), const3),
            pl.BlockSpec((S5_N_BLOCKS, 1, S5_COL_BLOCK), const3),
            pl.BlockSpec((S5_N_BLOCKS, S5_COL_BLOCK, S5_CH_BLOCK), const3, pipeline_mode=pl.Buffered(1)),
            pl.BlockSpec((S5_N_BLOCKS, S5_COL_BLOCK, S5_CH_BLOCK), const3, pipeline_mode=pl.Buffered(1)),
            pl.BlockSpec((1, D_MODEL), const2),
            state_spec, state_spec,
        ],
        out_specs=(pl.BlockSpec((tr, D_MODEL), lambda t: (t, 0)), state_spec, state_spec),
        out_shape=(jax.ShapeDtypeStruct((rows, D_MODEL), BF16),
                   jax.ShapeDtypeStruct((SUBLANES, S5_NSTATE), F32),
                   jax.ShapeDtypeStruct((SUBLANES, S5_NSTATE), F32)),
        scratch_shapes=[pltpu.VMEM((tr, D_MODEL), F32),
                        pltpu.VMEM((tr, S5_COL_BLOCK), F32),
                        pltpu.VMEM((tr, S5_COL_BLOCK), F32)],
        compiler_params=_params("arbitrary"),
        name="s5_mixer",
    )(x_tm, shift_pat, scale_pat, g_norm, bb_re, bb_im, a_re, a_im, c_re, c_im, d_skip, h0_re, h0_im)


def _softmax_pv(s, v_bf):
    m = jnp.max(s, axis=-1, keepdims=True)
    p = jnp.exp(s - m)
    l = jnp.sum(p, axis=-1, keepdims=True)
    o = jnp.dot(p.astype(BF16), v_bf, preferred_element_type=F32) / l
    return o, jnp.broadcast_to(m + jnp.log(l), o.shape)


def _qk(q_bf, k_bf):
    return lax.dot_general(q_bf, k_bf, (((1,), (1,)), ((), ())), preferred_element_type=F32) * ATTN_SCALE


def _attn_prompt_kernel(q0_ref, q1_ref, q2_ref, k0_ref, v0_ref, k1_ref, v1_ref, k2_ref, v2_ref, o_ref,
                        qf_scr, o1_scr, l1_scr, o2_scr, l2_scr, band_scr, first_scr):
    seq = q0_ref.shape[0]
    qb = Q_BLOCK
    row = lax.broadcasted_iota(jnp.int32, (REP * qb, 2 * qb), 0) & (qb - 1)
    col = lax.broadcasted_iota(jnp.int32, (REP * qb, 2 * qb), 1)
    band_scr[...] = jnp.where((col >= row) & (col <= row + qb), 0.0, -jnp.inf)
    row1 = lax.broadcasted_iota(jnp.int32, (REP * qb, qb), 0) & (qb - 1)
    col1 = lax.broadcasted_iota(jnp.int32, (REP * qb, qb), 1)
    first_scr[...] = jnp.where(col1 <= row1, 0.0, -jnp.inf)

    def rows_of(start, d):
        return pl.ds(start, qb) if d == 1 else pl.ds(start, qb, stride=d)

    def load_q(src_ref, start, d):
        if d == 1:
            parts = [src_ref[pl.ds(start, qb), e * HEAD_DIM:(e + 1) * HEAD_DIM] for e in range(REP)]
        else:
            parts = [src_ref[e, rows_of(start, d), :] for e in range(REP)]
        return jnp.concatenate(parts, axis=0).astype(BF16)

    def unit(src_q, k_ref, v_ref, start, d, first):
        q4 = load_q(src_q, start, d)
        k_cur = k_ref[rows_of(start, d), :]
        v_cur = v_ref[rows_of(start, d), :]
        if first:
            s = _qk(q4, k_cur.astype(BF16)) + first_scr[...]
            return _softmax_pv(s, v_cur.astype(BF16))
        prev = start - qb * d
        k_band = jnp.concatenate([k_ref[rows_of(prev, d), :], k_cur], axis=0)
        v_band = jnp.concatenate([v_ref[rows_of(prev, d), :], v_cur], axis=0)
        s = _qk(q4, k_band.astype(BF16)) + band_scr[...]
        return _softmax_pv(s, v_band.astype(BF16))

    def stage_group(q_ref, k_ref, v_ref, d, o_scr, l_scr):
        n_blocks = seq // (qb * d)

        def copy(c, carry):
            r0 = pl.multiple_of(c * 256, 256)
            for e in range(REP):
                qf_scr[e, pl.ds(r0, 256), :] = q_ref[pl.ds(r0, 256), e * HEAD_DIM:(e + 1) * HEAD_DIM].astype(F32)
            return carry

        lax.fori_loop(0, seq // 256, copy, 0)

        def store(start, o, lse):
            for e in range(REP):
                sl = slice(e * qb, (e + 1) * qb)
                o_scr[e, rows_of(start, d), :] = o[sl]
                l_scr[e, rows_of(start, d), :] = lse[sl]

        def per_class(r, carry):
            store(r, *unit(qf_scr, k_ref, v_ref, r, d, True))

            def per_block(jb, c2):
                start = jb * (qb * d) + r
                store(start, *unit(qf_scr, k_ref, v_ref, start, d, False))
                return c2

            if n_blocks > 1:
                lax.fori_loop(1, n_blocks, per_block, 0)
            return carry

        lax.fori_loop(0, d, per_class, 0)

    stage_group(q2_ref, k2_ref, v2_ref, DILATED_PATTERNS[2][1], o2_scr, l2_scr)
    stage_group(q1_ref, k1_ref, v1_ref, DILATED_PATTERNS[1][1], o1_scr, l1_scr)

    def merge(start, o0, lse0):
        for e in range(REP):
            sl = slice(e * qb, (e + 1) * qb)
            cols = slice(e * HEAD_DIM, (e + 1) * HEAD_DIM)
            la = lse0[sl]
            lb = l1_scr[e, pl.ds(start, qb), :]
            lc = l2_scr[e, pl.ds(start, qb), :]
            mx = jnp.maximum(jnp.maximum(la, lb), lc)
            wa = jnp.exp(la - mx)
            wb = jnp.exp(lb - mx)
            wc = jnp.exp(lc - mx)
            inv = 1.0 / (wa + wb + wc)
            acc = (wa * inv) * o0[sl] + (wb * inv) * o1_scr[e, pl.ds(start, qb), :] \
                + (wc * inv) * o2_scr[e, pl.ds(start, qb), :]
            o_ref[pl.ds(start, qb), cols] = acc.astype(o_ref.dtype)

    merge(0, *unit(q0_ref, k0_ref, v0_ref, 0, 1, True))

    def per_block0(jb, carry):
        start = pl.multiple_of(jb * qb, qb)
        merge(start, *unit(q0_ref, k0_ref, v0_ref, start, 1, False))
        return carry

    lax.fori_loop(1, seq // qb, per_block0, 0)


def _attn_prompt_call(q, kv):
    b, seq, _ = q.shape
    qw = REP * HEAD_DIM
    q_specs = [pl.BlockSpec((None, seq, qw), functools.partial(lambda bi, h, g: (bi, 0, g * KV_HEADS + h), g=g))
               for g in range(N_DIL)]
    kv_specs = []
    for g in range(N_DIL):
        kv_specs.append(pl.BlockSpec((None, None, seq, HEAD_DIM),
                                     functools.partial(lambda bi, h, g: (g, bi, 0, h), g=g)))
        kv_specs.append(pl.BlockSpec((None, None, seq, HEAD_DIM),
                                     functools.partial(lambda bi, h, g: (g, bi, 0, KV_HEADS + h), g=g)))
    kv_args = [kv] * (2 * N_DIL)
    return pl.pallas_call(
        _attn_prompt_kernel,
        grid=(b, KV_HEADS),
        in_specs=q_specs + kv_specs,
        out_specs=pl.BlockSpec((None, seq, qw), lambda bi, h: (bi, 0, h)),
        out_shape=jax.ShapeDtypeStruct((b, seq, HEADS * HEAD_DIM), BF16),
        scratch_shapes=[pltpu.VMEM((REP, seq, HEAD_DIM), F32),
                        pltpu.VMEM((REP, seq, HEAD_DIM), F32), pltpu.VMEM((REP, seq, HEAD_DIM), F32),
                        pltpu.VMEM((REP, seq, HEAD_DIM), F32), pltpu.VMEM((REP, seq, HEAD_DIM), F32),
                        pltpu.VMEM((REP * Q_BLOCK, 2 * Q_BLOCK), F32), pltpu.VMEM((REP * Q_BLOCK, Q_BLOCK), F32)],
        compiler_params=_params("parallel", "arbitrary"),
        name="dilated_attention_prompt",
    )(q, q, q, *kv_args)


def _attn_sample_kernel(q_ref, kvn_ref, c0_ref, c1_ref, c2_ref, o_ref, *, n_tok):
    rows = REP * n_tok
    cache_refs = (c0_ref, c1_ref, c2_ref)
    for h in range(KV_HEADS):
        kcol = slice(h * HEAD_DIM, (h + 1) * HEAD_DIM)
        vcol = slice((KV_HEADS + h) * HEAD_DIM, (KV_HEADS + h + 1) * HEAD_DIM)
        outs, lses = [], []
        for g, (_, d) in enumerate(DILATED_PATTERNS):
            q = q_ref[g * KV_HEADS + h]
            cache = cache_refs[g]
            kvn = kvn_ref.at[g]
            lw = cache.shape[0]
            tq_c = lax.broadcasted_iota(jnp.int32, (rows, lw), 0) & (n_tok - 1)
            idx_c = lax.broadcasted_iota(jnp.int32, (rows, lw), 1)
            ok_c = (idx_c >= tq_c) & (((idx_c - tq_c) & (d - 1)) == 0)
            tq_n = lax.broadcasted_iota(jnp.int32, (rows, n_tok), 0) & (n_tok - 1)
            idx_n = lax.broadcasted_iota(jnp.int32, (rows, n_tok), 1)
            ok_n = (idx_n <= tq_n) & (((tq_n - idx_n) & (d - 1)) == 0)
            s_c = jnp.where(ok_c, _qk(q, cache[:, kcol].astype(BF16)), -jnp.inf)
            s_n = jnp.where(ok_n, _qk(q, kvn[:, kcol].astype(BF16)), -jnp.inf)
            m = jnp.maximum(jnp.max(s_c, axis=-1, keepdims=True), jnp.max(s_n, axis=-1, keepdims=True))
            p_c = jnp.exp(s_c - m)
            p_n = jnp.exp(s_n - m)
            l = jnp.sum(p_c, axis=-1, keepdims=True) + jnp.sum(p_n, axis=-1, keepdims=True)
            o = (jnp.dot(p_c.astype(BF16), cache[:, vcol].astype(BF16), preferred_element_type=F32)
                 + jnp.dot(p_n.astype(BF16), kvn[:, vcol].astype(BF16), preferred_element_type=F32)) / l
            outs.append(o)
            lses.append(m + jnp.log(l))
        mx = jnp.maximum(jnp.maximum(lses[0], lses[1]), lses[2])
        w = [jnp.exp(x - mx) for x in lses]
        inv = 1.0 / (w[0] + w[1] + w[2])
        acc = (w[0] * inv) * outs[0] + (w[1] * inv) * outs[1] + (w[2] * inv) * outs[2]
        o_ref[h] = acc.astype(o_ref.dtype)


def _attn_sample_call(q_heads, kv_new, caches, n_tok):
    b = q_heads.shape[0]
    kvw = 2 * KV_HEADS * HEAD_DIM
    rows = REP * n_tok
    assert n_tok & (n_tok - 1) == 0
    in_specs = [pl.BlockSpec((None, N_DIL * KV_HEADS, rows, HEAD_DIM), lambda bi: (bi, 0, 0, 0)),
                pl.BlockSpec((N_DIL, None, n_tok, kvw), lambda bi: (0, bi, 0, 0))]
    in_specs += [pl.BlockSpec((None, c.shape[1], kvw), lambda bi: (bi, 0, 0)) for c in caches]
    return pl.pallas_call(
        functools.partial(_attn_sample_kernel, n_tok=n_tok),
        grid=(b,),
        in_specs=in_specs,
        out_specs=pl.BlockSpec((None, KV_HEADS, rows, HEAD_DIM), lambda bi: (bi, 0, 0, 0)),
        out_shape=jax.ShapeDtypeStruct((b, KV_HEADS, rows, HEAD_DIM), BF16),
        compiler_params=_params("parallel"),
        name="dilated_attention_sample",
    )(q_heads, kv_new, *caches)


def _s5_weights(lam_re, lam_im, log_dt, b_re, b_im, c_re, c_im):
    g, p, c = S5_GROUPS, S5_STATE, S5_GROUP
    rep = lambda a: jnp.repeat(a, c, axis=1)
    a_re, a_im, bb_re, bb_im = _s5_disc_call(rep(lam_re), rep(lam_im), log_dt.reshape(g, 1),
                                             b_re.reshape(g, p * c), b_im.reshape(g, p * c))
    gl = S5_CH_BLOCK // c
    eye = jnp.eye(gl, dtype=F32)

    def in_blocks(bb):
        t = bb.reshape(S5_N_BLOCKS, gl, p, c).transpose(0, 1, 3, 2)
        return jnp.einsum('bgcp,gh->bgchp', t, eye).reshape(S5_N_BLOCKS, gl * c, gl * p).astype(BF16)

    def out_blocks(cm):
        t = cm.reshape(S5_N_BLOCKS, gl, c, p).transpose(0, 1, 3, 2)
        return jnp.einsum('bgpc,gh->bgphc', t, eye).reshape(S5_N_BLOCKS, gl * p, gl * c).astype(BF16)

    lam_bar = lambda a: a.reshape(g, p, c)[:, :, 0].reshape(S5_N_BLOCKS, 1, S5_COL_BLOCK)
    return (in_blocks(bb_re), in_blocks(bb_im), lam_bar(a_re), lam_bar(a_im),
            out_blocks(c_re), out_blocks(c_im))


def _trunk(x, mod, s5_state, kv_caches, wts, tiles):
    (norm_g, ffn_w_in, ffn_w_out, s5_mats, s5_d, s5_w_glu, kv_norm_g, w_kv,
     attn_w_q, attn_w_o, final_norm_g) = wts
    bn, seq, _ = x.shape
    m = bn * seq
    sample = kv_caches is not None
    tm = tiles["tm"]

    if sample:
        def mod_rows(k):
            return jnp.repeat(mod[:, k], seq, axis=0).reshape(1, m, D_MODEL)
    else:
        def mod_rows(k):
            return mod[:, k].reshape(bn, 1, D_MODEL)

    def norm_g_row(a):
        return a.reshape(1, D_MODEL)

    xf = x.reshape(m, D_MODEL)
    new_re, new_im = [], []
    kv_groups = None
    streams = SUBLANES
    for layer in range(DEPTH):
        base = 9 * layer
        if layer == N_A_LAYERS:
            kvw = 2 * KV_HEADS * HEAD_DIM
            kv_groups = _norm_matmul_call(xf, mod_rows(9 * DEPTH), mod_rows(9 * DEPTH + 1), norm_g_row(kv_norm_g),
                                          w_kv, F32, tm, kvw, group_major=True).reshape(N_DIL, bn, seq, kvw)
        xf = _ffn_call(xf, mod_rows(base), mod_rows(base + 1), mod_rows(base + 2), norm_g_row(norm_g[layer, 0]),
                       ffn_w_in, ffn_w_out, layer, 0, tm, tiles["tf"])
        if layer < N_A_LAYERS:
            x_tm = jnp.pad(xf.reshape(bn, seq, D_MODEL).transpose(1, 0, 2), ((0, 0), (0, streams - bn), (0, 0)))
            x_tm = x_tm.reshape(seq * streams, D_MODEL)
            row_chunk = min(seq * streams, 64)

            def pattern(k):
                pat = jnp.pad(mod[:, k], ((0, streams - bn), (0, 0)))
                return jnp.tile(pat, (row_chunk // streams, 1))

            if s5_state is None:
                h0_re = jnp.zeros((streams, S5_NSTATE), F32)
                h0_im = h0_re
            else:
                pad_streams = ((0, streams - bn), (0, 0))
                h0_re = jnp.pad(s5_state[0][layer].reshape(bn, S5_NSTATE), pad_streams)
                h0_im = jnp.pad(s5_state[1][layer].reshape(bn, S5_NSTATE), pad_streams)
            bb_re, bb_im, a_re, a_im, c_re, c_im = s5_mats[layer]
            z_tm, h_re, h_im = _s5_call(x_tm, pattern(base + 3), pattern(base + 4), norm_g_row(norm_g[layer, 1]),
                                        bb_re, bb_im, a_re, a_im, c_re, c_im, s5_d[layer].reshape(1, D_MODEL),
                                        h0_re, h0_im, min(seq, tiles["s5_steps"]))
            new_re.append(h_re[:bn].reshape(bn, S5_GROUPS, S5_STATE))
            new_im.append(h_im[:bn].reshape(bn, S5_GROUPS, S5_STATE))
            z = z_tm.reshape(seq, streams, D_MODEL)[:, :bn].transpose(1, 0, 2).reshape(m, D_MODEL)
            xf = _proj_residual_call(z, xf, mod_rows(base + 5), s5_w_glu[layer], True, tm, 512)
        else:
            bl = layer - N_A_LAYERS
            q = _norm_matmul_call(xf, mod_rows(base + 3), mod_rows(base + 4), norm_g_row(norm_g[layer, 1]),
                                  attn_w_q[bl], BF16, tm, 1024)
            if sample:
                qh = q.reshape(bn, seq, N_DIL, KV_HEADS, REP, HEAD_DIM).transpose(0, 2, 3, 4, 1, 5)
                qh = qh.reshape(bn, N_DIL * KV_HEADS, REP * seq, HEAD_DIM)
                o = _attn_sample_call(qh, kv_groups, kv_caches, seq)
                o = o.reshape(bn, KV_HEADS, REP, seq, HEAD_DIM).transpose(0, 3, 1, 2, 4).reshape(m, D_MODEL)
            else:
                o = _attn_prompt_call(q.reshape(bn, seq, N_DIL * HEADS * HEAD_DIM), kv_groups).reshape(m, D_MODEL)
            xf = _proj_residual_call(o, xf, mod_rows(base + 5), attn_w_o[bl], False, tm, 512)
        xf = _ffn_call(xf, mod_rows(base + 6), mod_rows(base + 7), mod_rows(base + 8), norm_g_row(norm_g[layer, 2]),
                       ffn_w_in, ffn_w_out, layer, 1, tm, tiles["tf"])
    y = _final_norm_call(xf, mod_rows(9 * DEPTH + 2), mod_rows(9 * DEPTH + 3), norm_g_row(final_norm_g), tm)
    return y.reshape(bn, seq, D_MODEL), jnp.stack(new_re, axis=0), jnp.stack(new_im, axis=0), kv_groups


def kernel(x_prompt, x_sample, c_prompt, c_sample, state_s5_re, state_s5_im, cache_kv_g0, cache_kv_g1, cache_kv_g2, w_mod, b_mod, norm_g, ffn_w_in, ffn_w_out, s5_lambda_re, s5_lambda_im, s5_log_dt, s5_b_re, s5_b_im, s5_c_re, s5_c_im, s5_d, s5_w_glu, kv_norm_g, w_kv, attn_w_q, attn_w_o, final_norm_g):
    bp, seq_p, _ = x_prompt.shape
    bs, seq_s, _ = x_sample.shape
    kvw = 2 * KV_HEADS * HEAD_DIM
    caches = (cache_kv_g0, cache_kv_g1, cache_kv_g2)
    for (w, d), c in zip(DILATED_PATTERNS, caches):
        assert c.shape[1] == w and w == d * Q_BLOCK and seq_p % (d * Q_BLOCK) == 0
    assert bp <= SUBLANES and bs <= SUBLANES

    n_c = bp + bs
    c_all = jnp.pad(jnp.concatenate([c_prompt, c_sample], axis=0), ((0, (-n_c) % SUBLANES), (0, 0)))
    mod_all = _mod_call(c_all, w_mod, b_mod)
    mod_p = mod_all[:bp].reshape(bp, N_MOD, D_MODEL)
    mod_s = mod_all[bp:n_c].reshape(bs, N_MOD, D_MODEL)

    s5_mats = [_s5_weights(s5_lambda_re[l], s5_lambda_im[l], s5_log_dt[l], s5_b_re[l], s5_b_im[l],
                           s5_c_re[l], s5_c_im[l]) for l in range(N_A_LAYERS)]
    wts = (norm_g, ffn_w_in.astype(BF16), ffn_w_out.astype(BF16), s5_mats, s5_d, s5_w_glu.astype(BF16),
           kv_norm_g, w_kv.astype(BF16), attn_w_q.astype(BF16), attn_w_o.astype(BF16), final_norm_g)

    y_p, re_p, im_p, kv_p = _trunk(x_prompt, mod_p, None, None, wts,
                                   {"tm": 512, "tf": 512, "s5_steps": 64})
    caches_flat = tuple(c.reshape(bs, c.shape[1], kvw) for c in caches)
    y_s, re_s, im_s, kv_s = _trunk(x_sample, mod_s, (state_s5_re, state_s5_im), caches_flat, wts,
                                   {"tm": bs * seq_s, "tf": 512, "s5_steps": seq_s})

    def kv_leaf(a):
        return a.reshape(a.shape[0], a.shape[1], 2, KV_HEADS, HEAD_DIM)

    kvp = [kv_leaf(kv_p[g][:, seq_p - min(w, seq_p):]) for g, (w, _) in enumerate(DILATED_PATTERNS)]
    kvs = [kv_leaf(kv_s[g]) for g in range(N_DIL)]
    return (y_p, y_s, re_p, im_p, kvp[0], kvp[1], kvp[2], re_s, im_s, kvs[0], kvs[1], kvs[2])
```

```python
import functools

import jax
import jax.numpy as jnp
from jax import lax
from jax.experimental import pallas as pl
from jax.experimental.pallas import tpu as pltpu

F32 = jnp.float32
BF16 = jnp.bfloat16

D_MODEL = 2048
DEPTH = 4
N_A_LAYERS = DEPTH // 2
D_FF = 5632
S5_GROUP = 16
S5_GROUPS = D_MODEL // S5_GROUP
S5_STATE = 64
S5_NSTATE = S5_GROUPS * S5_STATE
HEAD_DIM = 128
HEADS = 16
KV_HEADS = 4
REP = HEADS // KV_HEADS
DILATED_PATTERNS = ((128, 1), (512, 4), (2048, 16))
N_DIL = len(DILATED_PATTERNS)
Q_BLOCK = 128
N_MOD = 9 * DEPTH + 4
EPS = 1e-6
ATTN_SCALE = HEAD_DIM ** -0.5

VMEM_LIMIT_BYTES = 56 * 1024 * 1024
SUBLANES = 8
S5_COL_BLOCK = 1024
S5_CH_BLOCK = S5_COL_BLOCK // S5_STATE * S5_GROUP
S5_N_BLOCKS = S5_NSTATE // S5_COL_BLOCK


def _params(*sem):
    return pltpu.CompilerParams(dimension_semantics=sem, vmem_limit_bytes=VMEM_LIMIT_BYTES)


def _rms_mod(x, g, shift, scale):
    xn = x * lax.rsqrt(jnp.mean(x * x, axis=-1, keepdims=True) + EPS)
    return xn * g * (1.0 + scale) + shift


def _norm_rows(x_ref, g_ref, sh_ref, sc_ref, h_ref, row_chunk):
    rows = x_ref.shape[0]
    g = g_ref[...]
    mod_rows = sh_ref.shape[0]

    def body(c, carry):
        r0 = pl.multiple_of(c * row_chunk, row_chunk)
        x = x_ref[pl.ds(r0, row_chunk), :]
        if mod_rows == rows and rows != row_chunk:
            sh = sh_ref[pl.ds(r0, row_chunk), :]
            sc = sc_ref[pl.ds(r0, row_chunk), :]
        else:
            sh = sh_ref[...]
            sc = sc_ref[...]
        h_ref[pl.ds(r0, row_chunk), :] = _rms_mod(x, g, sh, sc).astype(h_ref.dtype)
        return carry

    lax.fori_loop(0, rows // row_chunk, body, 0)


def _mod_kernel(c_ref, w_ref, b_ref, o_ref):
    c = c_ref[...]
    a = (c * jax.nn.sigmoid(c)).astype(BF16)
    o_ref[...] = jnp.dot(a, w_ref[...].astype(BF16), preferred_element_type=F32) + b_ref[...]


def _mod_call(c_all, w_mod, b_mod):
    rows = c_all.shape[0]
    n = w_mod.shape[1]
    tn = 1024
    return pl.pallas_call(
        _mod_kernel,
        grid=(n // tn,),
        in_specs=[
            pl.BlockSpec((rows, D_MODEL), lambda j: (0, 0)),
            pl.BlockSpec((D_MODEL, tn), lambda j: (0, j)),
            pl.BlockSpec((1, tn), lambda j: (0, j)),
        ],
        out_specs=pl.BlockSpec((rows, tn), lambda j: (0, j)),
        out_shape=jax.ShapeDtypeStruct((rows, n), F32),
        compiler_params=_params("arbitrary"),
        name="adaln_mod",
    )(c_all, w_mod, b_mod.reshape(1, n))


def _ffn_up_kernel(x_ref, sh_ref, sc_ref, g_ref, wg_ref, wu_ref, a_ref, h_ref, *, row_chunk):
    @pl.when(pl.program_id(1) == 0)
    def _():
        _norm_rows(x_ref, g_ref, sh_ref, sc_ref, h_ref, row_chunk)

    h = h_ref[...]
    g = jnp.dot(h, wg_ref[...].astype(BF16), preferred_element_type=F32)
    u = jnp.dot(h, wu_ref[...].astype(BF16), preferred_element_type=F32)
    a_ref[...] = (g * jax.nn.sigmoid(g) * u).astype(a_ref.dtype)


def _ffn_call(x, shift, scale, gate, g_norm, w_in, w_out, layer, which, tm, tf, tn):
    m = x.shape[0]
    nb, r, _ = shift.shape
    tiles_per_seq = (m // nb) // tm if r == 1 else 1
    nk = D_FF // tf
    mod_spec = pl.BlockSpec((None, r, D_MODEL), lambda i, k: (i // tiles_per_seq, 0, 0))
    a = pl.pallas_call(
        functools.partial(_ffn_up_kernel, row_chunk=min(tm, 64)),
        grid=(m // tm, nk),
        in_specs=[
            pl.BlockSpec((tm, D_MODEL), lambda i, k: (i, 0)),
            mod_spec, mod_spec,
            pl.BlockSpec((1, D_MODEL), lambda i, k: (0, 0)),
            pl.BlockSpec((None, None, D_MODEL, tf), lambda i, k: (layer, which, 0, k)),
            pl.BlockSpec((None, None, D_MODEL, tf), lambda i, k: (layer, which, 0, k + nk)),
        ],
        out_specs=pl.BlockSpec((tm, tf), lambda i, k: (i, k)),
        out_shape=jax.ShapeDtypeStruct((m, D_FF), BF16),
        scratch_shapes=[pltpu.VMEM((tm, D_MODEL), BF16)],
        compiler_params=_params("parallel", "arbitrary"),
        name="ffn_up",
    )(x, shift, scale, g_norm, w_in, w_in)
    return _proj_residual_call(a, x, gate, w_out, (layer, which), False, tm, tn, gate_scale=0.5, name="ffn_down")


def _norm_matmul_kernel(x_ref, sh_ref, sc_ref, g_ref, w_ref, o_ref, *rest, row_chunk, head_rows):
    h_ref = rest[-1]

    @pl.when(pl.program_id(1) == 0)
    def _():
        _norm_rows(x_ref, g_ref, sh_ref, sc_ref, h_ref, row_chunk)

    res = jnp.dot(h_ref[...], w_ref[...].astype(BF16), preferred_element_type=F32)
    o_ref[...] = res.astype(o_ref.dtype)
    if head_rows:
        heads_ref = rest[0]
        rows, cols = res.shape
        n_heads = cols // HEAD_DIM
        for c in range(n_heads):
            heads_ref[pl.ds(c, rows, stride=n_heads), :] = res[:, c * HEAD_DIM:(c + 1) * HEAD_DIM]


def _norm_matmul_call(x, shift, scale, g_norm, w, w_lead, out_dtype, tm, tn, head_rows=False):
    m = x.shape[0]
    nb, r, _ = shift.shape
    n = w.shape[-1]
    tiles_per_seq = (m // nb) // tm if r == 1 else 1
    mod_spec = pl.BlockSpec((None, r, D_MODEL), lambda i, j: (i // tiles_per_seq, 0, 0))
    if head_rows:
        hpr = tn // HEAD_DIM
        out_spec = (pl.BlockSpec((None, tm, tn), lambda i, j: (j, i, 0)),
                    pl.BlockSpec((None, tm * hpr, HEAD_DIM), lambda i, j: (j, i, 0)))
        out_shape = (jax.ShapeDtypeStruct((n // tn, m, tn), out_dtype),
                     jax.ShapeDtypeStruct((n // tn, m * hpr, HEAD_DIM), out_dtype))
    else:
        out_spec = pl.BlockSpec((tm, tn), lambda i, j: (i, j))
        out_shape = jax.ShapeDtypeStruct((m, n), out_dtype)
    return pl.pallas_call(
        functools.partial(_norm_matmul_kernel, row_chunk=min(tm, 64), head_rows=head_rows),
        grid=(m // tm, n // tn),
        in_specs=[
            pl.BlockSpec((tm, D_MODEL), lambda i, j: (i, 0)),
            mod_spec, mod_spec,
            pl.BlockSpec((1, D_MODEL), lambda i, j: (0, 0)),
            pl.BlockSpec((None,) * len(w_lead) + (D_MODEL, tn), lambda i, j: (*w_lead, 0, j)),
        ],
        out_specs=out_spec,
        out_shape=out_shape,
        scratch_shapes=[pltpu.VMEM((tm, D_MODEL), BF16)],
        compiler_params=_params("parallel", "arbitrary"),
        name="norm_matmul",
    )(x, shift, scale, g_norm, w)


def _proj_residual_kernel(z_ref, x_ref, gt_ref, *refs, glu, gate_scale):
    z = z_ref[...]
    if glu:
        wa_ref, wg_ref, o_ref = refs
        a = jnp.dot(z, wa_ref[...].astype(BF16), preferred_element_type=F32)
        g = jnp.dot(z, wg_ref[...].astype(BF16), preferred_element_type=F32)
        y = a * jax.nn.sigmoid(g)
    else:
        w_ref, o_ref = refs
        y = jnp.dot(z, w_ref[...].astype(BF16), preferred_element_type=F32)
    gate = gt_ref[...] if gate_scale == 1.0 else gate_scale * gt_ref[...]
    o_ref[...] = x_ref[...] + gate * y


def _proj_residual_call(z, x, gate, w, w_lead, glu, tm, tn, gate_scale=1.0, name="proj_residual"):
    m = x.shape[0]
    kdim = z.shape[1]
    nb, r, _ = gate.shape
    tiles_per_seq = (m // nb) // tm if r == 1 else 1
    n_blocks = D_MODEL // tn
    w_block = (None,) * len(w_lead) + (kdim, tn)
    w_specs = [pl.BlockSpec(w_block, lambda i, j: (*w_lead, 0, j))]
    w_args = [w]
    if glu:
        w_specs.append(pl.BlockSpec(w_block, lambda i, j: (*w_lead, 0, j + n_blocks)))
        w_args.append(w)
    return pl.pallas_call(
        functools.partial(_proj_residual_kernel, glu=glu, gate_scale=gate_scale),
        grid=(m // tm, n_blocks),
        in_specs=[
            pl.BlockSpec((tm, kdim), lambda i, j: (i, 0)),
            pl.BlockSpec((tm, tn), lambda i, j: (i, j)),
            pl.BlockSpec((None, r, tn), lambda i, j: (i // tiles_per_seq, 0, j)),
        ] + w_specs,
        out_specs=pl.BlockSpec((tm, tn), lambda i, j: (i, j)),
        out_shape=jax.ShapeDtypeStruct((m, D_MODEL), F32),
        compiler_params=_params("parallel", "arbitrary"),
        name=name,
    )(z, x, gate, *w_args)


def _final_norm_kernel(x_ref, sh_ref, sc_ref, g_ref, o_ref, *, row_chunk):
    _norm_rows(x_ref, g_ref, sh_ref, sc_ref, o_ref, row_chunk)


def _final_norm_call(x, shift, scale, g_norm, tm):
    m = x.shape[0]
    nb, r, _ = shift.shape
    tiles_per_seq = (m // nb) // tm if r == 1 else 1
    mod_spec = pl.BlockSpec((None, r, D_MODEL), lambda i: (i // tiles_per_seq, 0, 0))
    return pl.pallas_call(
        functools.partial(_final_norm_kernel, row_chunk=min(tm, 64)),
        grid=(m // tm,),
        in_specs=[
            pl.BlockSpec((tm, D_MODEL), lambda i: (i, 0)),
            mod_spec, mod_spec,
            pl.BlockSpec((1, D_MODEL), lambda i: (0, 0)),
        ],
        out_specs=pl.BlockSpec((tm, D_MODEL), lambda i: (i, 0)),
        out_shape=jax.ShapeDtypeStruct((m, D_MODEL), F32),
        compiler_params=_params("parallel"),
        name="final_norm",
    )(x, shift, scale, g_norm)


def _s5_disc_kernel(lre_ref, lim_ref, ldt_ref, bre_ref, bim_ref, are_ref, aim_ref, bbre_ref, bbim_ref):
    lam_re = lre_ref[...]
    lam_im = lim_ref[...]
    dt = jnp.exp(ldt_ref[...])
    decay = jnp.exp(lam_re * dt)
    ab_re = decay * jnp.cos(lam_im * dt)
    ab_im = decay * jnp.sin(lam_im * dt)
    den = lam_re * lam_re + lam_im * lam_im
    f_re = ((ab_re - 1.0) * lam_re + ab_im * lam_im) / den
    f_im = (ab_im * lam_re - (ab_re - 1.0) * lam_im) / den
    b_re = bre_ref[...]
    b_im = bim_ref[...]
    are_ref[...] = ab_re
    aim_ref[...] = ab_im
    bbre_ref[...] = f_re * b_re - f_im * b_im
    bbim_ref[...] = f_re * b_im + f_im * b_re


def _s5_disc_call(lam_re, lam_im, log_dt, b_re, b_im):
    shape = jax.ShapeDtypeStruct(lam_re.shape, F32)
    return pl.pallas_call(
        _s5_disc_kernel,
        out_shape=(shape, shape, shape, shape),
        name="s5_discretise",
    )(lam_re, lam_im, log_dt, b_re, b_im)


def _s5_kernel(x_ref, sh_ref, sc_ref, g_ref, bbre_ref, bbim_ref, are_ref, aim_ref, cre_ref, cim_ref,
               d_ref, h0re_ref, h0im_ref, z_ref, hre_out, him_out,
               u_scr, sre_scr, sim_scr, y_scr, *, row_chunk):
    t = pl.program_id(0)
    nb, steps, _ = x_ref.shape
    rows = steps * nb
    lane = HEAD_DIM
    steps_per_tile = SUBLANES // nb

    @pl.when(t == 0)
    def _():
        hre_out[...] = h0re_ref[...]
        him_out[...] = h0im_ref[...]

    g = g_ref[...]
    for b in range(nb):
        def norm_chunk(c, carry, b=b):
            r0 = pl.multiple_of(c * row_chunk, row_chunk)
            u = _rms_mod(x_ref[b, pl.ds(r0, row_chunk), :], g, sh_ref[b], sc_ref[b])
            for j in range(D_MODEL // lane):
                u_scr[j, pl.ds(r0 * nb + b, row_chunk, stride=nb), :] = u[:, j * lane:(j + 1) * lane]
            return carry

        lax.fori_loop(0, steps // row_chunk, norm_chunk, 0)

    tiles_per_block = S5_CH_BLOCK // lane
    for cb in range(S5_N_BLOCKS):
        ch = slice(cb * S5_CH_BLOCK, (cb + 1) * S5_CH_BLOCK)
        st = slice(cb * S5_COL_BLOCK, (cb + 1) * S5_COL_BLOCK)
        u_blk = jnp.concatenate([u_scr[cb * tiles_per_block + j] for j in range(tiles_per_block)], axis=1)
        u_bf = u_blk.astype(BF16)
        sre_scr[...] = jnp.dot(u_bf, bbre_ref[cb], preferred_element_type=F32)
        sim_scr[...] = jnp.dot(u_bf, bbim_ref[cb], preferred_element_type=F32)

        a_re = jnp.broadcast_to(are_ref[cb], (nb, S5_COL_BLOCK))
        a_im = jnp.broadcast_to(aim_ref[cb], (nb, S5_COL_BLOCK))

        def tile_step(i, carry, a_re=a_re, a_im=a_im):
            h_re, h_im = carry
            r0 = pl.multiple_of(i * SUBLANES, SUBLANES)
            t_re = sre_scr[pl.ds(r0, SUBLANES), :]
            t_im = sim_scr[pl.ds(r0, SUBLANES), :]
            out_re, out_im = [], []
            for s in range(steps_per_tile):
                sl = slice(s * nb, (s + 1) * nb)
                h_re, h_im = (a_re * h_re - a_im * h_im + t_re[sl],
                              a_re * h_im + a_im * h_re + t_im[sl])
                out_re.append(h_re)
                out_im.append(h_im)
            sre_scr[pl.ds(r0, SUBLANES), :] = out_re[0] if steps_per_tile == 1 else jnp.concatenate(out_re, axis=0)
            sim_scr[pl.ds(r0, SUBLANES), :] = out_im[0] if steps_per_tile == 1 else jnp.concatenate(out_im, axis=0)
            return h_re, h_im

        n_tiles = rows // SUBLANES
        h_re, h_im = lax.fori_loop(0, n_tiles, tile_step, (hre_out[:, st], him_out[:, st]),
                                   unroll=min(n_tiles, 2 if steps_per_tile > 1 else 4))
        hre_out[:, st] = h_re
        him_out[:, st] = h_im

        y = (jnp.dot(sre_scr[...].astype(BF16), cre_ref[cb], preferred_element_type=F32)
             - jnp.dot(sim_scr[...].astype(BF16), cim_ref[cb], preferred_element_type=F32))
        zf = jax.nn.gelu(y + d_ref[:, ch] * u_blk)
        for j in range(tiles_per_block):
            y_scr[j] = zf[:, j * lane:(j + 1) * lane]
        for b in range(nb):
            for j in range(tiles_per_block):
                c0 = cb * S5_CH_BLOCK + j * lane
                z_ref[b, :, c0:c0 + lane] = y_scr[j, pl.ds(b, steps, stride=nb), :].astype(z_ref.dtype)


def _s5_call(x, shift, scale, g_norm, bb_re, bb_im, a_re, a_im, c_re, c_im, d_skip,
             h0_re, h0_im, steps_per_chunk, z_dtype):
    nb, seq, _ = x.shape
    assert SUBLANES % nb == 0 and seq % steps_per_chunk == 0
    tr = steps_per_chunk * nb
    row_chunk = min(steps_per_chunk, 64)
    const2 = lambda t: (0, 0)
    const3 = lambda t: (0, 0, 0)
    state_spec = pl.BlockSpec((nb, S5_NSTATE), const2)
    io_spec = pl.BlockSpec((nb, steps_per_chunk, D_MODEL), lambda t: (0, t, 0))
    return pl.pallas_call(
        functools.partial(_s5_kernel, row_chunk=row_chunk),
        grid=(seq // steps_per_chunk,),
        in_specs=[
            io_spec,
            pl.BlockSpec((nb, 1, D_MODEL), const3),
            pl.BlockSpec((nb, 1, D_MODEL), const3),
            pl.BlockSpec((1, D_MODEL), const2),
            pl.BlockSpec((S5_N_BLOCKS, S5_CH_BLOCK, S5_COL_BLOCK), const3, pipeline_mode=pl.Buffered(1)),
            pl.BlockSpec((S5_N_BLOCKS, S5_CH_BLOCK, S5_COL_BLOCK), const3, pipeline_mode=pl.Buffered(1)),
            pl.BlockSpec((S5_N_BLOCKS, 1, S5_COL_BLOCK), const3),
            pl.BlockSpec((S5_N_BLOCKS, 1, S5_COL_BLOCK), const3),
            pl.BlockSpec((S5_N_BLOCKS, S5_COL_BLOCK, S5_CH_BLOCK), const3, pipeline_mode=pl.Buffered(1)),
            pl.BlockSpec((S5_N_BLOCKS, S5_COL_BLOCK, S5_CH_BLOCK), const3, pipeline_mode=pl.Buffered(1)),
            pl.BlockSpec((1, D_MODEL), const2),
            state_spec, state_spec,
        ],
        out_specs=(io_spec, state_spec, state_spec),
        out_shape=(jax.ShapeDtypeStruct((nb, seq, D_MODEL), z_dtype),
                   jax.ShapeDtypeStruct((nb, S5_NSTATE), F32),
                   jax.ShapeDtypeStruct((nb, S5_NSTATE), F32)),
        scratch_shapes=[pltpu.VMEM((D_MODEL // HEAD_DIM, tr, HEAD_DIM), F32),
                        pltpu.VMEM((tr, S5_COL_BLOCK), F32),
                        pltpu.VMEM((tr, S5_COL_BLOCK), F32),
                        pltpu.VMEM((S5_CH_BLOCK // HEAD_DIM, tr, HEAD_DIM), F32)],
        compiler_params=_params("arbitrary"),
        name="s5_mixer",
    )(x, shift, scale, g_norm, bb_re, bb_im, a_re, a_im, c_re, c_im, d_skip, h0_re, h0_im)


def _softmax_pv(s, v_bf):
    m = jnp.max(s, axis=-1, keepdims=True)
    p = jnp.exp(s - m)
    l = jnp.sum(p, axis=-1, keepdims=True)
    o = jnp.dot(p.astype(BF16), v_bf, preferred_element_type=F32) / l
    return o, jnp.broadcast_to(m + jnp.log(l), o.shape)


def _qk(q_bf, k_bf):
    return lax.dot_general(q_bf, k_bf, (((1,), (1,)), ((), ())), preferred_element_type=F32) * ATTN_SCALE


def _attn_prompt_kernel(q0_ref, q1_ref, q2_ref, k0_ref, v0_ref, k1_ref, v1_ref, k2_ref, v2_ref, o_ref,
                        qf_scr, o1_scr, l1_scr, o2_scr, l2_scr, band_scr, first_scr):
    seq = q0_ref.shape[0]
    qb = Q_BLOCK
    row = lax.broadcasted_iota(jnp.int32, (REP * qb, 2 * qb), 0) & (qb - 1)
    col = lax.broadcasted_iota(jnp.int32, (REP * qb, 2 * qb), 1)
    band_scr[...] = jnp.where((col >= row) & (col <= row + qb), 0.0, -jnp.inf)
    row1 = lax.broadcasted_iota(jnp.int32, (REP * qb, qb), 0) & (qb - 1)
    col1 = lax.broadcasted_iota(jnp.int32, (REP * qb, qb), 1)
    first_scr[...] = jnp.where(col1 <= row1, 0.0, -jnp.inf)

    def rows_of(start, d):
        return pl.ds(start, qb) if d == 1 else pl.ds(start, qb, stride=d)

    def load_q(src_ref, start, d):
        if d == 1:
            parts = [src_ref[pl.ds(start, qb), e * HEAD_DIM:(e + 1) * HEAD_DIM] for e in range(REP)]
        else:
            parts = [src_ref[e, rows_of(start, d), :] for e in range(REP)]
        return jnp.concatenate(parts, axis=0).astype(BF16)

    def unit(src_q, k_ref, v_ref, start, d, first):
        q4 = load_q(src_q, start, d)
        k_cur = k_ref[rows_of(start, d), :]
        v_cur = v_ref[rows_of(start, d), :]
        if first:
            s = _qk(q4, k_cur.astype(BF16)) + first_scr[...]
            return _softmax_pv(s, v_cur.astype(BF16))
        prev = start - qb * d
        k_band = jnp.concatenate([k_ref[rows_of(prev, d), :], k_cur], axis=0)
        v_band = jnp.concatenate([v_ref[rows_of(prev, d), :], v_cur], axis=0)
        s = _qk(q4, k_band.astype(BF16)) + band_scr[...]
        return _softmax_pv(s, v_band.astype(BF16))

    def stage_group(q_ref, k_ref, v_ref, d, o_scr, l_scr):
        n_blocks = seq // (qb * d)

        def copy(c, carry):
            r0 = pl.multiple_of(c * 256, 256)
            for e in range(REP):
                qf_scr[e, pl.ds(r0, 256), :] = q_ref[pl.ds(r0, 256), e * HEAD_DIM:(e + 1) * HEAD_DIM].astype(F32)
            return carry

        lax.fori_loop(0, seq // 256, copy, 0)

        def store(start, o, lse):
            for e in range(REP):
                sl = slice(e * qb, (e + 1) * qb)
                o_scr[e, rows_of(start, d), :] = o[sl]
                l_scr[e, rows_of(start, d), :] = lse[sl]

        def per_class(r, carry):
            store(r, *unit(qf_scr, k_ref, v_ref, r, d, True))

            def per_block(jb, c2):
                start = jb * (qb * d) + r
                store(start, *unit(qf_scr, k_ref, v_ref, start, d, False))
                return c2

            if n_blocks > 1:
                lax.fori_loop(1, n_blocks, per_block, 0, unroll=True)
            return carry

        lax.fori_loop(0, d, per_class, 0, unroll=2 if n_blocks == 1 else 1)

    stage_group(q2_ref, k2_ref, v2_ref, DILATED_PATTERNS[2][1], o2_scr, l2_scr)
    stage_group(q1_ref, k1_ref, v1_ref, DILATED_PATTERNS[1][1], o1_scr, l1_scr)

    def merge(start, o0, lse0):
        for e in range(REP):
            sl = slice(e * qb, (e + 1) * qb)
            cols = slice(e * HEAD_DIM, (e + 1) * HEAD_DIM)
            la = lse0[sl]
            lb = l1_scr[e, pl.ds(start, qb), :]
            lc = l2_scr[e, pl.ds(start, qb), :]
            mx = jnp.maximum(jnp.maximum(la, lb), lc)
            wa = jnp.exp(la - mx)
            wb = jnp.exp(lb - mx)
            wc = jnp.exp(lc - mx)
            inv = 1.0 / (wa + wb + wc)
            acc = (wa * inv) * o0[sl] + (wb * inv) * o1_scr[e, pl.ds(start, qb), :] \
                + (wc * inv) * o2_scr[e, pl.ds(start, qb), :]
            o_ref[pl.ds(start, qb), cols] = acc.astype(o_ref.dtype)

    merge(0, *unit(q0_ref, k0_ref, v0_ref, 0, 1, True))

    def per_block0(jb, carry):
        start = pl.multiple_of(jb * qb, qb)
        merge(start, *unit(q0_ref, k0_ref, v0_ref, start, 1, False))
        return carry

    n_blocks0 = seq // qb
    lax.fori_loop(1, n_blocks0, per_block0, 0, unroll=3 if (n_blocks0 - 1) % 3 == 0 else 1)


def _attn_prompt_call(q, kv):
    b, seq, _ = q.shape
    qw = REP * HEAD_DIM
    q_specs = [pl.BlockSpec((None, seq, qw), functools.partial(lambda bi, h, g: (bi, 0, g * KV_HEADS + h), g=g))
               for g in range(N_DIL)]
    kv_specs = []
    for g in range(N_DIL):
        kv_specs.append(pl.BlockSpec((None, None, seq, HEAD_DIM),
                                     functools.partial(lambda bi, h, g: (g, bi, 0, h), g=g)))
        kv_specs.append(pl.BlockSpec((None, None, seq, HEAD_DIM),
                                     functools.partial(lambda bi, h, g: (g, bi, 0, KV_HEADS + h), g=g)))
    kv_args = [kv] * (2 * N_DIL)
    return pl.pallas_call(
        _attn_prompt_kernel,
        grid=(b, KV_HEADS),
        in_specs=q_specs + kv_specs,
        out_specs=pl.BlockSpec((None, seq, qw), lambda bi, h: (bi, 0, h)),
        out_shape=jax.ShapeDtypeStruct((b, seq, HEADS * HEAD_DIM), BF16),
        scratch_shapes=[pltpu.VMEM((REP, seq, HEAD_DIM), F32),
                        pltpu.VMEM((REP, seq, HEAD_DIM), F32), pltpu.VMEM((REP, seq, HEAD_DIM), F32),
                        pltpu.VMEM((REP, seq, HEAD_DIM), F32), pltpu.VMEM((REP, seq, HEAD_DIM), F32),
                        pltpu.VMEM((REP * Q_BLOCK, 2 * Q_BLOCK), F32), pltpu.VMEM((REP * Q_BLOCK, Q_BLOCK), F32)],
        compiler_params=_params("parallel", "arbitrary"),
        name="dilated_attention_prompt",
    )(q, q, q, *kv_args)


def _attn_sample_kernel(q_ref, kvn_ref, c0_ref, c1_ref, c2_ref, o_ref, *, n_tok):
    rows = REP * n_tok
    cache_refs = (c0_ref, c1_ref, c2_ref)
    kv_rows = 2 * KV_HEADS
    for h in range(KV_HEADS):
        outs, lses = [], []
        for g, (_, d) in enumerate(DILATED_PATTERNS):
            q = q_ref[g * KV_HEADS + h]
            lw = cache_refs[g].shape[0] // kv_rows
            k_c = cache_refs[g][pl.ds(h, lw, stride=kv_rows), :].astype(BF16)
            v_c = cache_refs[g][pl.ds(KV_HEADS + h, lw, stride=kv_rows), :].astype(BF16)
            k_n = kvn_ref[g, pl.ds(h, n_tok, stride=kv_rows), :].astype(BF16)
            v_n = kvn_ref[g, pl.ds(KV_HEADS + h, n_tok, stride=kv_rows), :].astype(BF16)
            tq_c = lax.broadcasted_iota(jnp.int32, (rows, lw), 0) & (n_tok - 1)
            idx_c = lax.broadcasted_iota(jnp.int32, (rows, lw), 1)
            ok_c = (idx_c >= tq_c) & (((idx_c - tq_c) & (d - 1)) == 0)
            tq_n = lax.broadcasted_iota(jnp.int32, (rows, n_tok), 0) & (n_tok - 1)
            idx_n = lax.broadcasted_iota(jnp.int32, (rows, n_tok), 1)
            ok_n = (idx_n <= tq_n) & (((tq_n - idx_n) & (d - 1)) == 0)
            s_c = jnp.where(ok_c, _qk(q, k_c), -jnp.inf)
            s_n = jnp.where(ok_n, _qk(q, k_n), -jnp.inf)
            m = jnp.maximum(jnp.max(s_c, axis=-1, keepdims=True), jnp.max(s_n, axis=-1, keepdims=True))
            p_c = jnp.exp(s_c - m)
            p_n = jnp.exp(s_n - m)
            l = jnp.sum(p_c, axis=-1, keepdims=True) + jnp.sum(p_n, axis=-1, keepdims=True)
            o = (jnp.dot(p_c.astype(BF16), v_c, preferred_element_type=F32)
                 + jnp.dot(p_n.astype(BF16), v_n, preferred_element_type=F32)) / l
            outs.append(o)
            lses.append(m + jnp.log(l))
        mx = jnp.maximum(jnp.maximum(lses[0], lses[1]), lses[2])
        w = [jnp.exp(x - mx) for x in lses]
        inv = 1.0 / (w[0] + w[1] + w[2])
        acc = (w[0] * inv) * outs[0] + (w[1] * inv) * outs[1] + (w[2] * inv) * outs[2]
        o_ref[h] = acc.astype(o_ref.dtype)


def _attn_sample_call(q_heads, kv_new, caches, n_tok):
    b = q_heads.shape[0]
    rows = REP * n_tok
    assert n_tok & (n_tok - 1) == 0
    in_specs = [pl.BlockSpec((None, N_DIL * KV_HEADS, rows, HEAD_DIM), lambda bi: (bi, 0, 0, 0)),
                pl.BlockSpec((N_DIL, None, kv_new.shape[2], HEAD_DIM), lambda bi: (0, bi, 0, 0))]
    in_specs += [pl.BlockSpec((None, c.shape[1], HEAD_DIM), lambda bi: (bi, 0, 0)) for c in caches]
    return pl.pallas_call(
        functools.partial(_attn_sample_kernel, n_tok=n_tok),
        grid=(b,),
        in_specs=in_specs,
        out_specs=pl.BlockSpec((None, KV_HEADS, rows, HEAD_DIM), lambda bi: (bi, 0, 0, 0)),
        out_shape=jax.ShapeDtypeStruct((b, KV_HEADS, rows, HEAD_DIM), BF16),
        compiler_params=_params("parallel"),
        name="dilated_attention_sample",
    )(q_heads, kv_new, *caches)


def _s5_weights(lam_re, lam_im, log_dt, b_re, b_im, c_re, c_im):
    g, p, c = S5_GROUPS, S5_STATE, S5_GROUP
    rep = lambda a: jnp.repeat(a, c, axis=1)
    a_re, a_im, bb_re, bb_im = _s5_disc_call(rep(lam_re), rep(lam_im), log_dt.reshape(g, 1),
                                             b_re.reshape(g, p * c), b_im.reshape(g, p * c))
    gl = S5_CH_BLOCK // c
    eye = jnp.eye(gl, dtype=F32)

    def in_blocks(bb):
        t = bb.reshape(S5_N_BLOCKS, gl, p, c).transpose(0, 1, 3, 2)
        return jnp.einsum('bgcp,gh->bgchp', t, eye).reshape(S5_N_BLOCKS, gl * c, gl * p).astype(BF16)

    def out_blocks(cm):
        t = cm.reshape(S5_N_BLOCKS, gl, c, p).transpose(0, 1, 3, 2)
        return jnp.einsum('bgpc,gh->bgphc', t, eye).reshape(S5_N_BLOCKS, gl * p, gl * c).astype(BF16)

    lam_bar = lambda a: a.reshape(g, p, c)[:, :, 0].reshape(S5_N_BLOCKS, 1, S5_COL_BLOCK)
    return (in_blocks(bb_re), in_blocks(bb_im), lam_bar(a_re), lam_bar(a_im),
            out_blocks(c_re), out_blocks(c_im))


def _trunk(x, mod, s5_state, kv_caches, wts, tiles):
    (norm_g, ffn_w_in, ffn_w_out, s5_mats, s5_d, s5_w_glu, kv_norm_g, w_kv,
     attn_w_q, attn_w_o, final_norm_g) = wts
    bn, seq, _ = x.shape
    m = bn * seq
    sample = kv_caches is not None
    tm = tiles["tm"]

    if sample:
        def mod_rows(k):
            return jnp.repeat(mod[:, k], seq, axis=0).reshape(1, m, D_MODEL)
    else:
        def mod_rows(k):
            return mod[:, k].reshape(bn, 1, D_MODEL)

    def norm_g_row(a):
        return a.reshape(1, D_MODEL)

    xf = x.reshape(m, D_MODEL)
    new_re, new_im = [], []
    kv = None
    kv_rows = 2 * KV_HEADS
    for layer in range(DEPTH):
        base = 9 * layer
        if layer == N_A_LAYERS:
            kv_flat, kv = _norm_matmul_call(xf, mod_rows(9 * DEPTH), mod_rows(9 * DEPTH + 1),
                                            norm_g_row(kv_norm_g), w_kv, (), F32, min(tm, tiles["tm_kv"]),
                                            kv_rows * HEAD_DIM, head_rows=True)
        xf = _ffn_call(xf, mod_rows(base), mod_rows(base + 1), mod_rows(base + 2), norm_g_row(norm_g[layer, 0]),
                       ffn_w_in, ffn_w_out, layer, 0, tm, tiles["tf"], tiles["tn_down"])
        if layer < N_A_LAYERS:
            if s5_state is None:
                h0_re = jnp.zeros((bn, S5_NSTATE), F32)
                h0_im = h0_re
            else:
                h0_re = s5_state[0][layer].reshape(bn, S5_NSTATE)
                h0_im = s5_state[1][layer].reshape(bn, S5_NSTATE)
            bb_re, bb_im, a_re, a_im, c_re, c_im = s5_mats[layer]
            z, h_re, h_im = _s5_call(xf.reshape(bn, seq, D_MODEL), mod[:, base + 3].reshape(bn, 1, D_MODEL),
                                     mod[:, base + 4].reshape(bn, 1, D_MODEL), norm_g_row(norm_g[layer, 1]),
                                     bb_re, bb_im, a_re, a_im, c_re, c_im, s5_d[layer].reshape(1, D_MODEL),
                                     h0_re, h0_im, min(seq, tiles["s5_steps"]), F32 if sample else BF16)
            new_re.append(h_re.reshape(bn, S5_GROUPS, S5_STATE))
            new_im.append(h_im.reshape(bn, S5_GROUPS, S5_STATE))
            xf = _proj_residual_call(z.reshape(m, D_MODEL).astype(BF16), xf, mod_rows(base + 5), s5_w_glu, (layer,),
                                     True, tm, 512, name="s5_glu")
        else:
            bl = layer - N_A_LAYERS
            q = _norm_matmul_call(xf, mod_rows(base + 3), mod_rows(base + 4), norm_g_row(norm_g[layer, 1]),
                                  attn_w_q, (bl,), BF16, tm, 1024)
            if sample:
                qh = q.reshape(bn, seq, N_DIL, KV_HEADS, REP, HEAD_DIM).transpose(0, 2, 3, 4, 1, 5)
                qh = qh.reshape(bn, N_DIL * KV_HEADS, REP * seq, HEAD_DIM)
                o = _attn_sample_call(qh, kv.reshape(N_DIL, bn, seq * kv_rows, HEAD_DIM), kv_caches, seq)
                o = o.reshape(bn, KV_HEADS, REP, seq, HEAD_DIM).transpose(0, 3, 1, 2, 4).reshape(m, D_MODEL)
            else:
                o = _attn_prompt_call(q.reshape(bn, seq, N_DIL * HEADS * HEAD_DIM),
                                      kv_flat.reshape(N_DIL, bn, seq, kv_rows * HEAD_DIM)).reshape(m, D_MODEL)
            xf = _proj_residual_call(o, xf, mod_rows(base + 5), attn_w_o, (bl,), False, tm, 512, name="attn_out")
        xf = _ffn_call(xf, mod_rows(base + 6), mod_rows(base + 7), mod_rows(base + 8), norm_g_row(norm_g[layer, 2]),
                       ffn_w_in, ffn_w_out, layer, 1, tm, tiles["tf"], tiles["tn_down"])
    y = _final_norm_call(xf, mod_rows(9 * DEPTH + 2), mod_rows(9 * DEPTH + 3), norm_g_row(final_norm_g), tm)
    kv = kv.reshape(N_DIL, bn, seq, 2, KV_HEADS, HEAD_DIM)
    return y.reshape(bn, seq, D_MODEL), jnp.stack(new_re, axis=0), jnp.stack(new_im, axis=0), kv


def kernel(x_prompt, x_sample, c_prompt, c_sample, state_s5_re, state_s5_im, cache_kv_g0, cache_kv_g1, cache_kv_g2, w_mod, b_mod, norm_g, ffn_w_in, ffn_w_out, s5_lambda_re, s5_lambda_im, s5_log_dt, s5_b_re, s5_b_im, s5_c_re, s5_c_im, s5_d, s5_w_glu, kv_norm_g, w_kv, attn_w_q, attn_w_o, final_norm_g):
    bp, seq_p, _ = x_prompt.shape
    bs, seq_s, _ = x_sample.shape
    caches = (cache_kv_g0, cache_kv_g1, cache_kv_g2)
    for (w, d), c in zip(DILATED_PATTERNS, caches):
        assert c.shape[1] == w and w == d * Q_BLOCK and seq_p % (d * Q_BLOCK) == 0
    assert bp <= SUBLANES and bs <= SUBLANES

    n_c = bp + bs
    c_all = jnp.pad(jnp.concatenate([c_prompt, c_sample], axis=0), ((0, (-n_c) % SUBLANES), (0, 0)))
    mod_all = _mod_call(c_all, w_mod, b_mod)
    mod_p = mod_all[:bp].reshape(bp, N_MOD, D_MODEL)
    mod_s = mod_all[bp:n_c].reshape(bs, N_MOD, D_MODEL)

    s5_mats = [_s5_weights(s5_lambda_re[l], s5_lambda_im[l], s5_log_dt[l], s5_b_re[l], s5_b_im[l],
                           s5_c_re[l], s5_c_im[l]) for l in range(N_A_LAYERS)]
    wts = (norm_g, ffn_w_in, ffn_w_out, s5_mats, s5_d, s5_w_glu, kv_norm_g, w_kv, attn_w_q, attn_w_o, final_norm_g)

    y_p, re_p, im_p, kv_p = _trunk(x_prompt, mod_p, None, None, wts,
                                   {"tm": 1024, "tm_kv": 512, "tf": 512, "tn_down": 256, "s5_steps": 128})
    caches_flat = tuple(c.reshape(bs, c.shape[1] * 2 * KV_HEADS, HEAD_DIM) for c in caches)
    y_s, re_s, im_s, kv_s = _trunk(x_sample, mod_s, (state_s5_re, state_s5_im), caches_flat, wts,
                                   {"tm": bs * seq_s, "tm_kv": bs * seq_s, "tf": 512, "tn_down": 256,
                                    "s5_steps": seq_s})

    kvp = [kv_p[g][:, seq_p - min(w, seq_p):] for g, (w, _) in enumerate(DILATED_PATTERNS)]
    return (y_p, y_s, re_p, im_p, kvp[0], kvp[1], kvp[2], re_s, im_s, kv_s[0], kv_s[1], kv_s[2])
```

```python
import functools

import jax
import jax.numpy as jnp
from jax import lax
from jax.experimental import pallas as pl
from jax.experimental.pallas import tpu as pltpu

F32 = jnp.float32
BF16 = jnp.bfloat16

D_MODEL = 2048
DEPTH = 4
N_A_LAYERS = DEPTH // 2
D_FF = 5632
S5_GROUP = 16
S5_GROUPS = D_MODEL // S5_GROUP
S5_STATE = 64
S5_NSTATE = S5_GROUPS * S5_STATE
HEAD_DIM = 128
HEADS = 16
KV_HEADS = 4
REP = HEADS // KV_HEADS
DILATED_PATTERNS = ((128, 1), (512, 4), (2048, 16))
N_DIL = len(DILATED_PATTERNS)
Q_BLOCK = 128
N_MOD = 9 * DEPTH + 4
EPS = 1e-6
ATTN_SCALE = HEAD_DIM ** -0.5

VMEM_LIMIT_BYTES = 56 * 1024 * 1024
SUBLANES = 8
S5_COL_BLOCK = 1024
S5_CH_BLOCK = S5_COL_BLOCK // S5_STATE * S5_GROUP
S5_N_BLOCKS = S5_NSTATE // S5_COL_BLOCK


def _params(*sem):
    return pltpu.CompilerParams(dimension_semantics=sem, vmem_limit_bytes=VMEM_LIMIT_BYTES)


def _rms_mod(x, g, shift, scale):
    xn = x * lax.rsqrt(jnp.mean(x * x, axis=-1, keepdims=True) + EPS)
    return xn * g * (1.0 + scale) + shift


def _norm_rows(x_ref, g_ref, sh_ref, sc_ref, h_ref, row_chunk):
    rows = x_ref.shape[0]
    g = g_ref[...]
    mod_rows = sh_ref.shape[0]

    def body(c, carry):
        r0 = pl.multiple_of(c * row_chunk, row_chunk)
        x = x_ref[pl.ds(r0, row_chunk), :]
        if mod_rows == rows and rows != row_chunk:
            sh = sh_ref[pl.ds(r0, row_chunk), :]
            sc = sc_ref[pl.ds(r0, row_chunk), :]
        else:
            sh = sh_ref[...]
            sc = sc_ref[...]
        h_ref[pl.ds(r0, row_chunk), :] = _rms_mod(x, g, sh, sc).astype(h_ref.dtype)
        return carry

    lax.fori_loop(0, rows // row_chunk, body, 0)


def _mod_kernel(c_ref, w_ref, b_ref, o_ref):
    c = c_ref[...]
    a = (c * jax.nn.sigmoid(c)).astype(BF16)
    o_ref[...] = jnp.dot(a, w_ref[...].astype(BF16), preferred_element_type=F32) + b_ref[...]


def _mod_call(c_all, w_mod, b_mod):
    rows = c_all.shape[0]
    n = w_mod.shape[1]
    tn = 1024
    return pl.pallas_call(
        _mod_kernel,
        grid=(n // tn,),
        in_specs=[
            pl.BlockSpec((rows, D_MODEL), lambda j: (0, 0)),
            pl.BlockSpec((D_MODEL, tn), lambda j: (0, j)),
            pl.BlockSpec((1, tn), lambda j: (0, j)),
        ],
        out_specs=pl.BlockSpec((rows, tn), lambda j: (0, j)),
        out_shape=jax.ShapeDtypeStruct((rows, n), F32),
        compiler_params=_params("arbitrary"),
        name="adaln_mod",
    )(c_all, w_mod, b_mod.reshape(1, n))


def _ffn_up_kernel(x_ref, sh_ref, sc_ref, g_ref, wg_ref, wu_ref, a_ref, h_ref, *, row_chunk):
    @pl.when(pl.program_id(1) == 0)
    def _():
        _norm_rows(x_ref, g_ref, sh_ref, sc_ref, h_ref, row_chunk)

    h = h_ref[...]
    g = jnp.dot(h, wg_ref[...].astype(BF16), preferred_element_type=F32)
    u = jnp.dot(h, wu_ref[...].astype(BF16), preferred_element_type=F32)
    a_ref[...] = (g * jax.nn.sigmoid(g) * u).astype(a_ref.dtype)


def _ffn_call(x, shift, scale, gate, g_norm, w_in, w_out, layer, which, tm, tf, tm_down, tn):
    m = x.shape[0]
    nb, r, _ = shift.shape
    tiles_per_seq = (m // nb) // tm if r == 1 else 1
    nk = D_FF // tf
    mod_spec = pl.BlockSpec((None, r, D_MODEL), lambda i, k: (i // tiles_per_seq, 0, 0))
    a = pl.pallas_call(
        functools.partial(_ffn_up_kernel, row_chunk=min(tm, 64)),
        grid=(m // tm, nk),
        in_specs=[
            pl.BlockSpec((tm, D_MODEL), lambda i, k: (i, 0)),
            mod_spec, mod_spec,
            pl.BlockSpec((1, D_MODEL), lambda i, k: (0, 0)),
            pl.BlockSpec((None, None, D_MODEL, tf), lambda i, k: (layer, which, 0, k)),
            pl.BlockSpec((None, None, D_MODEL, tf), lambda i, k: (layer, which, 0, k + nk)),
        ],
        out_specs=pl.BlockSpec((tm, tf), lambda i, k: (i, k)),
        out_shape=jax.ShapeDtypeStruct((m, D_FF), BF16),
        scratch_shapes=[pltpu.VMEM((tm, D_MODEL), BF16)],
        compiler_params=_params("parallel", "arbitrary"),
        name="ffn_up",
    )(x, shift, scale, g_norm, w_in, w_in)
    return _proj_residual_call(a, x, gate, w_out, (layer, which), False, tm_down, tn, gate_scale=0.5,
                               name="ffn_down", z_buffers=1 if tm_down > tm else 2)


def _norm_matmul_kernel(x_ref, sh_ref, sc_ref, g_ref, w_ref, o_ref, *rest, row_chunk, head_rows):
    h_ref = rest[-1]

    @pl.when(pl.program_id(1) == 0)
    def _():
        _norm_rows(x_ref, g_ref, sh_ref, sc_ref, h_ref, row_chunk)

    res = jnp.dot(h_ref[...], w_ref[...].astype(BF16), preferred_element_type=F32)
    o_ref[...] = res.astype(o_ref.dtype)
    if head_rows:
        heads_ref = rest[0]
        rows, cols = res.shape
        heads_per_tile = cols // HEAD_DIM
        first = (pl.program_id(1) % (head_rows // heads_per_tile)) * heads_per_tile
        for c in range(heads_per_tile):
            heads_ref[pl.ds(first + c, rows, stride=head_rows), :] = res[:, c * HEAD_DIM:(c + 1) * HEAD_DIM]


def _norm_matmul_call(x, shift, scale, g_norm, w, w_lead, out_dtype, tm, tn, head_rows=0):
    m = x.shape[0]
    nb, r, _ = shift.shape
    n = w.shape[-1]
    tiles_per_seq = (m // nb) // tm if r == 1 else 1
    mod_spec = pl.BlockSpec((None, r, D_MODEL), lambda i, j: (i // tiles_per_seq, 0, 0))
    if head_rows:
        slab = head_rows * HEAD_DIM
        tps = slab // tn
        out_spec = (pl.BlockSpec((None, tm, tn), lambda i, j: (j // tps, i, j % tps)),
                    pl.BlockSpec((None, tm * head_rows, HEAD_DIM), lambda i, j: (j // tps, i, 0)))
        out_shape = (jax.ShapeDtypeStruct((n // slab, m, slab), out_dtype),
                     jax.ShapeDtypeStruct((n // slab, m * head_rows, HEAD_DIM), out_dtype))
    else:
        out_spec = pl.BlockSpec((tm, tn), lambda i, j: (i, j))
        out_shape = jax.ShapeDtypeStruct((m, n), out_dtype)
    return pl.pallas_call(
        functools.partial(_norm_matmul_kernel, row_chunk=min(tm, 64), head_rows=head_rows),
        grid=(m // tm, n // tn),
        in_specs=[
            pl.BlockSpec((tm, D_MODEL), lambda i, j: (i, 0)),
            mod_spec, mod_spec,
            pl.BlockSpec((1, D_MODEL), lambda i, j: (0, 0)),
            pl.BlockSpec((None,) * len(w_lead) + (D_MODEL, tn), lambda i, j: (*w_lead, 0, j)),
        ],
        out_specs=out_spec,
        out_shape=out_shape,
        scratch_shapes=[pltpu.VMEM((tm, D_MODEL), BF16)],
        compiler_params=_params("parallel", "arbitrary"),
        name="norm_matmul",
    )(x, shift, scale, g_norm, w)


def _proj_residual_kernel(z_ref, x_ref, gt_ref, *refs, glu, gate_scale):
    z = z_ref[...]
    if glu:
        wa_ref, wg_ref, o_ref = refs
        a = jnp.dot(z, wa_ref[...].astype(BF16), preferred_element_type=F32)
        g = jnp.dot(z, wg_ref[...].astype(BF16), preferred_element_type=F32)
        y = a * jax.nn.sigmoid(g)
    else:
        w_ref, o_ref = refs
        y = jnp.dot(z, w_ref[...].astype(BF16), preferred_element_type=F32)
    gate = gt_ref[...] if gate_scale == 1.0 else gate_scale * gt_ref[...]
    o_ref[...] = x_ref[...] + gate * y


def _proj_residual_call(z, x, gate, w, w_lead, glu, tm, tn, gate_scale=1.0, name="proj_residual", z_buffers=2):
    m = x.shape[0]
    kdim = z.shape[1]
    nb, r, _ = gate.shape
    tiles_per_seq = (m // nb) // tm if r == 1 else 1
    n_blocks = D_MODEL // tn
    w_block = (None,) * len(w_lead) + (kdim, tn)
    w_specs = [pl.BlockSpec(w_block, lambda i, j: (*w_lead, 0, j))]
    w_args = [w]
    if glu:
        w_specs.append(pl.BlockSpec(w_block, lambda i, j: (*w_lead, 0, j + n_blocks)))
        w_args.append(w)
    return pl.pallas_call(
        functools.partial(_proj_residual_kernel, glu=glu, gate_scale=gate_scale),
        grid=(m // tm, n_blocks),
        in_specs=[
            (pl.BlockSpec((tm, kdim), lambda i, j: (i, 0)) if z_buffers == 2 else
             pl.BlockSpec((tm, kdim), lambda i, j: (i, 0), pipeline_mode=pl.Buffered(z_buffers))),
            pl.BlockSpec((tm, tn), lambda i, j: (i, j)),
            pl.BlockSpec((None, r, tn), lambda i, j: (i // tiles_per_seq, 0, j)),
        ] + w_specs,
        out_specs=pl.BlockSpec((tm, tn), lambda i, j: (i, j)),
        out_shape=jax.ShapeDtypeStruct((m, D_MODEL), F32),
        compiler_params=_params("parallel", "arbitrary"),
        name=name,
    )(z, x, gate, *w_args)


def _final_norm_kernel(x_ref, sh_ref, sc_ref, g_ref, o_ref, *, row_chunk):
    _norm_rows(x_ref, g_ref, sh_ref, sc_ref, o_ref, row_chunk)


def _final_norm_call(x, shift, scale, g_norm, tm):
    m = x.shape[0]
    nb, r, _ = shift.shape
    tiles_per_seq = (m // nb) // tm if r == 1 else 1
    mod_spec = pl.BlockSpec((None, r, D_MODEL), lambda i: (i // tiles_per_seq, 0, 0))
    return pl.pallas_call(
        functools.partial(_final_norm_kernel, row_chunk=min(tm, 64)),
        grid=(m // tm,),
        in_specs=[
            pl.BlockSpec((tm, D_MODEL), lambda i: (i, 0)),
            mod_spec, mod_spec,
            pl.BlockSpec((1, D_MODEL), lambda i: (0, 0)),
        ],
        out_specs=pl.BlockSpec((tm, D_MODEL), lambda i: (i, 0)),
        out_shape=jax.ShapeDtypeStruct((m, D_MODEL), F32),
        compiler_params=_params("parallel"),
        name="final_norm",
    )(x, shift, scale, g_norm)


def _s5_disc_kernel(lre_ref, lim_ref, ldt_ref, bre_ref, bim_ref, are_ref, aim_ref, bbre_ref, bbim_ref):
    lam_re = lre_ref[...]
    lam_im = lim_ref[...]
    dt = jnp.exp(ldt_ref[...])
    decay = jnp.exp(lam_re * dt)
    ab_re = decay * jnp.cos(lam_im * dt)
    ab_im = decay * jnp.sin(lam_im * dt)
    den = lam_re * lam_re + lam_im * lam_im
    f_re = ((ab_re - 1.0) * lam_re + ab_im * lam_im) / den
    f_im = (ab_im * lam_re - (ab_re - 1.0) * lam_im) / den
    b_re = bre_ref[...]
    b_im = bim_ref[...]
    are_ref[...] = ab_re
    aim_ref[...] = ab_im
    bbre_ref[...] = f_re * b_re - f_im * b_im
    bbim_ref[...] = f_re * b_im + f_im * b_re


def _s5_disc_call(lam_re, lam_im, log_dt, b_re, b_im):
    shape = jax.ShapeDtypeStruct(lam_re.shape, F32)
    return pl.pallas_call(
        _s5_disc_kernel,
        out_shape=(shape, shape, shape, shape),
        name="s5_discretise",
    )(lam_re, lam_im, log_dt, b_re, b_im)


def _s5_kernel(x_ref, sh_ref, sc_ref, g_ref, bbre_ref, bbim_ref, are_ref, aim_ref, cre_ref, cim_ref,
               d_ref, h0re_ref, h0im_ref, z_ref, hre_out, him_out,
               u_scr, sre_scr, sim_scr, y_scr, *, row_chunk):
    t = pl.program_id(0)
    nb, steps, _ = x_ref.shape
    rows = steps * nb
    lane = HEAD_DIM
    steps_per_tile = SUBLANES // nb

    @pl.when(t == 0)
    def _():
        hre_out[...] = h0re_ref[...]
        him_out[...] = h0im_ref[...]

    g = g_ref[...]
    for b in range(nb):
        def norm_chunk(c, carry, b=b):
            r0 = pl.multiple_of(c * row_chunk, row_chunk)
            u = _rms_mod(x_ref[b, pl.ds(r0, row_chunk), :], g, sh_ref[b], sc_ref[b])
            for j in range(D_MODEL // lane):
                u_scr[j, pl.ds(r0 * nb + b, row_chunk, stride=nb), :] = u[:, j * lane:(j + 1) * lane]
            return carry

        lax.fori_loop(0, steps // row_chunk, norm_chunk, 0)

    tiles_per_block = S5_CH_BLOCK // lane
    per_block_scratch = sre_scr.shape[0] == S5_N_BLOCKS
    for cb in range(S5_N_BLOCKS):
        ch = slice(cb * S5_CH_BLOCK, (cb + 1) * S5_CH_BLOCK)
        st = slice(cb * S5_COL_BLOCK, (cb + 1) * S5_COL_BLOCK)
        slab = cb if per_block_scratch else 0
        sre, sim, ys = sre_scr.at[slab], sim_scr.at[slab], y_scr.at[slab]
        u_blk = jnp.concatenate([u_scr[cb * tiles_per_block + j] for j in range(tiles_per_block)], axis=1)
        u_bf = u_blk.astype(BF16)
        sre[...] = jnp.dot(u_bf, bbre_ref[cb], preferred_element_type=F32)
        sim[...] = jnp.dot(u_bf, bbim_ref[cb], preferred_element_type=F32)

        a_re = jnp.broadcast_to(are_ref[cb], (nb, S5_COL_BLOCK))
        a_im = jnp.broadcast_to(aim_ref[cb], (nb, S5_COL_BLOCK))

        def tile_step(i, carry, a_re=a_re, a_im=a_im, sre=sre, sim=sim):
            h_re, h_im = carry
            r0 = i * SUBLANES if isinstance(i, int) else pl.multiple_of(i * SUBLANES, SUBLANES)
            t_re = sre[pl.ds(r0, SUBLANES), :]
            t_im = sim[pl.ds(r0, SUBLANES), :]
            out_re, out_im = [], []
            for s in range(steps_per_tile):
                sl = slice(s * nb, (s + 1) * nb)
                h_re, h_im = (a_re * h_re - a_im * h_im + t_re[sl],
                              a_re * h_im + a_im * h_re + t_im[sl])
                out_re.append(h_re)
                out_im.append(h_im)
            sre[pl.ds(r0, SUBLANES), :] = out_re[0] if steps_per_tile == 1 else jnp.concatenate(out_re, axis=0)
            sim[pl.ds(r0, SUBLANES), :] = out_im[0] if steps_per_tile == 1 else jnp.concatenate(out_im, axis=0)
            return h_re, h_im

        n_tiles = rows // SUBLANES
        carry = (hre_out[:, st], him_out[:, st])
        if per_block_scratch:
            for i in range(n_tiles):
                carry = tile_step(i, carry)
        else:
            carry = lax.fori_loop(0, n_tiles, tile_step, carry,
                                  unroll=min(n_tiles, 2 if steps_per_tile > 1 else 4))
        hre_out[:, st], him_out[:, st] = carry

        y = (jnp.dot(sre[...].astype(BF16), cre_ref[cb], preferred_element_type=F32)
             - jnp.dot(sim[...].astype(BF16), cim_ref[cb], preferred_element_type=F32))
        zf = jax.nn.gelu(y + d_ref[:, ch] * u_blk)
        for j in range(tiles_per_block):
            ys[j] = zf[:, j * lane:(j + 1) * lane]
        for b in range(nb):
            for j in range(tiles_per_block):
                c0 = cb * S5_CH_BLOCK + j * lane
                z_ref[b, :, c0:c0 + lane] = ys[j, pl.ds(b, steps, stride=nb), :].astype(z_ref.dtype)


def _s5_call(x, shift, scale, g_norm, bb_re, bb_im, a_re, a_im, c_re, c_im, d_skip,
             h0_re, h0_im, steps_per_chunk, z_dtype, per_block_scratch):
    nb, seq, _ = x.shape
    assert SUBLANES % nb == 0 and seq % steps_per_chunk == 0
    tr = steps_per_chunk * nb
    slabs = S5_N_BLOCKS if per_block_scratch else 1
    row_chunk = min(steps_per_chunk, 64)
    const2 = lambda t: (0, 0)
    const3 = lambda t: (0, 0, 0)
    state_spec = pl.BlockSpec((nb, S5_NSTATE), const2)
    io_spec = pl.BlockSpec((nb, steps_per_chunk, D_MODEL), lambda t: (0, t, 0))
    return pl.pallas_call(
        functools.partial(_s5_kernel, row_chunk=row_chunk),
        grid=(seq // steps_per_chunk,),
        in_specs=[
            io_spec,
            pl.BlockSpec((nb, 1, D_MODEL), const3),
            pl.BlockSpec((nb, 1, D_MODEL), const3),
            pl.BlockSpec((1, D_MODEL), const2),
            pl.BlockSpec((S5_N_BLOCKS, S5_CH_BLOCK, S5_COL_BLOCK), const3, pipeline_mode=pl.Buffered(1)),
            pl.BlockSpec((S5_N_BLOCKS, S5_CH_BLOCK, S5_COL_BLOCK), const3, pipeline_mode=pl.Buffered(1)),
            pl.BlockSpec((S5_N_BLOCKS, 1, S5_COL_BLOCK), const3),
            pl.BlockSpec((S5_N_BLOCKS, 1, S5_COL_BLOCK), const3),
            pl.BlockSpec((S5_N_BLOCKS, S5_COL_BLOCK, S5_CH_BLOCK), const3, pipeline_mode=pl.Buffered(1)),
            pl.BlockSpec((S5_N_BLOCKS, S5_COL_BLOCK, S5_CH_BLOCK), const3, pipeline_mode=pl.Buffered(1)),
            pl.BlockSpec((1, D_MODEL), const2),
            state_spec, state_spec,
        ],
        out_specs=(io_spec, state_spec, state_spec),
        out_shape=(jax.ShapeDtypeStruct((nb, seq, D_MODEL), z_dtype),
                   jax.ShapeDtypeStruct((nb, S5_NSTATE), F32),
                   jax.ShapeDtypeStruct((nb, S5_NSTATE), F32)),
        scratch_shapes=[pltpu.VMEM((D_MODEL // HEAD_DIM, tr, HEAD_DIM), F32),
                        pltpu.VMEM((slabs, tr, S5_COL_BLOCK), F32),
                        pltpu.VMEM((slabs, tr, S5_COL_BLOCK), F32),
                        pltpu.VMEM((slabs, S5_CH_BLOCK // HEAD_DIM, tr, HEAD_DIM), F32)],
        compiler_params=_params("arbitrary"),
        name="s5_mixer",
    )(x, shift, scale, g_norm, bb_re, bb_im, a_re, a_im, c_re, c_im, d_skip, h0_re, h0_im)


def _softmax_pv(s, v_bf):
    m = jnp.max(s, axis=-1, keepdims=True)
    p = jnp.exp(s - m)
    l = jnp.sum(p, axis=-1, keepdims=True)
    o = jnp.dot(p.astype(BF16), v_bf, preferred_element_type=F32) / l
    return o, jnp.broadcast_to(m + jnp.log(l), o.shape)


def _qk(q_bf, k_bf):
    return lax.dot_general(q_bf, k_bf, (((1,), (1,)), ((), ())), preferred_element_type=F32) * ATTN_SCALE


def _attn_prompt_kernel(q0_ref, q1_ref, q2_ref, k0_ref, v0_ref, k1_ref, v1_ref, k2_ref, v2_ref, o_ref,
                        qf_scr, o1_scr, l1_scr, o2_scr, l2_scr, band_scr, first_scr):
    seq = q0_ref.shape[0]
    qb = Q_BLOCK
    row = lax.broadcasted_iota(jnp.int32, (REP * qb, 2 * qb), 0) & (qb - 1)
    col = lax.broadcasted_iota(jnp.int32, (REP * qb, 2 * qb), 1)
    band_scr[...] = jnp.where((col >= row) & (col <= row + qb), 0.0, -jnp.inf)
    row1 = lax.broadcasted_iota(jnp.int32, (REP * qb, qb), 0) & (qb - 1)
    col1 = lax.broadcasted_iota(jnp.int32, (REP * qb, qb), 1)
    first_scr[...] = jnp.where(col1 <= row1, 0.0, -jnp.inf)

    def rows_of(start, d):
        return pl.ds(start, qb) if d == 1 else pl.ds(start, qb, stride=d)

    def load_q(src_ref, start, d):
        if d == 1:
            parts = [src_ref[pl.ds(start, qb), e * HEAD_DIM:(e + 1) * HEAD_DIM] for e in range(REP)]
        else:
            parts = [src_ref[e, rows_of(start, d), :] for e in range(REP)]
        return jnp.concatenate(parts, axis=0).astype(BF16)

    def unit(src_q, k_ref, v_ref, start, d, first):
        q4 = load_q(src_q, start, d)
        k_cur = k_ref[rows_of(start, d), :]
        v_cur = v_ref[rows_of(start, d), :]
        if first:
            s = _qk(q4, k_cur.astype(BF16)) + first_scr[...]
            return _softmax_pv(s, v_cur.astype(BF16))
        prev = start - qb * d
        k_band = jnp.concatenate([k_ref[rows_of(prev, d), :], k_cur], axis=0)
        v_band = jnp.concatenate([v_ref[rows_of(prev, d), :], v_cur], axis=0)
        s = _qk(q4, k_band.astype(BF16)) + band_scr[...]
        return _softmax_pv(s, v_band.astype(BF16))

    def stage_group(q_ref, k_ref, v_ref, d, o_scr, l_scr):
        n_blocks = seq // (qb * d)

        def copy(c, carry):
            r0 = pl.multiple_of(c * 256, 256)
            for e in range(REP):
                qf_scr[e, pl.ds(r0, 256), :] = q_ref[pl.ds(r0, 256), e * HEAD_DIM:(e + 1) * HEAD_DIM].astype(F32)
            return carry

        lax.fori_loop(0, seq // 256, copy, 0)

        def store(start, o, lse):
            for e in range(REP):
                sl = slice(e * qb, (e + 1) * qb)
                o_scr[e, rows_of(start, d), :] = o[sl]
                l_scr[e, rows_of(start, d), :] = lse[sl]

        def per_class(r, carry):
            store(r, *unit(qf_scr, k_ref, v_ref, r, d, True))

            def per_block(jb, c2):
                start = jb * (qb * d) + r
                store(start, *unit(qf_scr, k_ref, v_ref, start, d, False))
                return c2

            if n_blocks > 1:
                lax.fori_loop(1, n_blocks, per_block, 0, unroll=True)
            return carry

        lax.fori_loop(0, d, per_class, 0, unroll=2 if n_blocks == 1 else 1)

    stage_group(q2_ref, k2_ref, v2_ref, DILATED_PATTERNS[2][1], o2_scr, l2_scr)
    stage_group(q1_ref, k1_ref, v1_ref, DILATED_PATTERNS[1][1], o1_scr, l1_scr)

    def merge(start, o0, lse0):
        for e in range(REP):
            sl = slice(e * qb, (e + 1) * qb)
            cols = slice(e * HEAD_DIM, (e + 1) * HEAD_DIM)
            la = lse0[sl]
            lb = l1_scr[e, pl.ds(start, qb), :]
            lc = l2_scr[e, pl.ds(start, qb), :]
            mx = jnp.maximum(jnp.maximum(la, lb), lc)
            wa = jnp.exp(la - mx)
            wb = jnp.exp(lb - mx)
            wc = jnp.exp(lc - mx)
            inv = 1.0 / (wa + wb + wc)
            acc = (wa * inv) * o0[sl] + (wb * inv) * o1_scr[e, pl.ds(start, qb), :] \
                + (wc * inv) * o2_scr[e, pl.ds(start, qb), :]
            o_ref[pl.ds(start, qb), cols] = acc.astype(o_ref.dtype)

    merge(0, *unit(q0_ref, k0_ref, v0_ref, 0, 1, True))

    def per_block0(jb, carry):
        start = pl.multiple_of(jb * qb, qb)
        merge(start, *unit(q0_ref, k0_ref, v0_ref, start, 1, False))
        return carry

    n_blocks0 = seq // qb
    lax.fori_loop(1, n_blocks0, per_block0, 0, unroll=3 if (n_blocks0 - 1) % 3 == 0 else 1)


def _attn_prompt_call(q, kv):
    b, seq, _ = q.shape
    qw = REP * HEAD_DIM
    q_specs = [pl.BlockSpec((None, seq, qw), functools.partial(lambda bi, h, g: (bi, 0, g * KV_HEADS + h), g=g))
               for g in range(N_DIL)]
    kv_specs = []
    for g in range(N_DIL):
        kv_specs.append(pl.BlockSpec((None, None, seq, HEAD_DIM),
                                     functools.partial(lambda bi, h, g: (g, bi, 0, h), g=g)))
        kv_specs.append(pl.BlockSpec((None, None, seq, HEAD_DIM),
                                     functools.partial(lambda bi, h, g: (g, bi, 0, KV_HEADS + h), g=g)))
    kv_args = [kv] * (2 * N_DIL)
    return pl.pallas_call(
        _attn_prompt_kernel,
        grid=(b, KV_HEADS),
        in_specs=q_specs + kv_specs,
        out_specs=pl.BlockSpec((None, seq, qw), lambda bi, h: (bi, 0, h)),
        out_shape=jax.ShapeDtypeStruct((b, seq, HEADS * HEAD_DIM), BF16),
        scratch_shapes=[pltpu.VMEM((REP, seq, HEAD_DIM), F32),
                        pltpu.VMEM((REP, seq, HEAD_DIM), F32), pltpu.VMEM((REP, seq, HEAD_DIM), F32),
                        pltpu.VMEM((REP, seq, HEAD_DIM), F32), pltpu.VMEM((REP, seq, HEAD_DIM), F32),
                        pltpu.VMEM((REP * Q_BLOCK, 2 * Q_BLOCK), F32), pltpu.VMEM((REP * Q_BLOCK, Q_BLOCK), F32)],
        compiler_params=_params("parallel", "arbitrary"),
        name="dilated_attention_prompt",
    )(q, q, q, *kv_args)


def _attn_sample_kernel(q_ref, kvn_ref, c0_ref, c1_ref, c2_ref, o_ref, *, n_tok):
    rows = REP * n_tok
    cache_refs = (c0_ref, c1_ref, c2_ref)
    kv_rows = 2 * KV_HEADS
    for h in range(KV_HEADS):
        outs, lses = [], []
        for g, (_, d) in enumerate(DILATED_PATTERNS):
            q = q_ref[g * KV_HEADS + h]
            lw = cache_refs[g].shape[0] // kv_rows
            k_c = cache_refs[g][pl.ds(h, lw, stride=kv_rows), :].astype(BF16)
            v_c = cache_refs[g][pl.ds(KV_HEADS + h, lw, stride=kv_rows), :].astype(BF16)
            k_n = kvn_ref[g, pl.ds(h, n_tok, stride=kv_rows), :].astype(BF16)
            v_n = kvn_ref[g, pl.ds(KV_HEADS + h, n_tok, stride=kv_rows), :].astype(BF16)
            tq_c = lax.broadcasted_iota(jnp.int32, (rows, lw), 0) & (n_tok - 1)
            idx_c = lax.broadcasted_iota(jnp.int32, (rows, lw), 1)
            ok_c = (idx_c >= tq_c) & (((idx_c - tq_c) & (d - 1)) == 0)
            tq_n = lax.broadcasted_iota(jnp.int32, (rows, n_tok), 0) & (n_tok - 1)
            idx_n = lax.broadcasted_iota(jnp.int32, (rows, n_tok), 1)
            ok_n = (idx_n <= tq_n) & (((tq_n - idx_n) & (d - 1)) == 0)
            s_c = jnp.where(ok_c, _qk(q, k_c), -jnp.inf)
            s_n = jnp.where(ok_n, _qk(q, k_n), -jnp.inf)
            m = jnp.maximum(jnp.max(s_c, axis=-1, keepdims=True), jnp.max(s_n, axis=-1, keepdims=True))
            p_c = jnp.exp(s_c - m)
            p_n = jnp.exp(s_n - m)
            l = jnp.sum(p_c, axis=-1, keepdims=True) + jnp.sum(p_n, axis=-1, keepdims=True)
            o = (jnp.dot(p_c.astype(BF16), v_c, preferred_element_type=F32)
                 + jnp.dot(p_n.astype(BF16), v_n, preferred_element_type=F32)) / l
            outs.append(o)
            lses.append(m + jnp.log(l))
        mx = jnp.maximum(jnp.maximum(lses[0], lses[1]), lses[2])
        w = [jnp.exp(x - mx) for x in lses]
        inv = 1.0 / (w[0] + w[1] + w[2])
        acc = (w[0] * inv) * outs[0] + (w[1] * inv) * outs[1] + (w[2] * inv) * outs[2]
        o_ref[h] = acc.astype(o_ref.dtype)


def _attn_sample_call(q_heads, kv_new, caches, n_tok):
    b = q_heads.shape[0]
    rows = REP * n_tok
    assert n_tok & (n_tok - 1) == 0
    in_specs = [pl.BlockSpec((None, N_DIL * KV_HEADS, rows, HEAD_DIM), lambda bi: (bi, 0, 0, 0)),
                pl.BlockSpec((N_DIL, None, kv_new.shape[2], HEAD_DIM), lambda bi: (0, bi, 0, 0))]
    in_specs += [pl.BlockSpec((None, c.shape[1], HEAD_DIM), lambda bi: (bi, 0, 0)) for c in caches]
    return pl.pallas_call(
        functools.partial(_attn_sample_kernel, n_tok=n_tok),
        grid=(b,),
        in_specs=in_specs,
        out_specs=pl.BlockSpec((None, KV_HEADS, rows, HEAD_DIM), lambda bi: (bi, 0, 0, 0)),
        out_shape=jax.ShapeDtypeStruct((b, KV_HEADS, rows, HEAD_DIM), BF16),
        compiler_params=_params("parallel"),
        name="dilated_attention_sample",
    )(q_heads, kv_new, *caches)


def _s5_weights(lam_re, lam_im, log_dt, b_re, b_im, c_re, c_im):
    g, p, c = S5_GROUPS, S5_STATE, S5_GROUP
    rep = lambda a: jnp.repeat(a, c, axis=1)
    a_re, a_im, bb_re, bb_im = _s5_disc_call(rep(lam_re), rep(lam_im), log_dt.reshape(g, 1),
                                             b_re.reshape(g, p * c), b_im.reshape(g, p * c))
    gl = S5_CH_BLOCK // c
    eye = jnp.eye(gl, dtype=F32)

    def in_blocks(bb):
        t = bb.reshape(S5_N_BLOCKS, gl, p, c).transpose(0, 1, 3, 2)
        return jnp.einsum('bgcp,gh->bgchp', t, eye).reshape(S5_N_BLOCKS, gl * c, gl * p).astype(BF16)

    def out_blocks(cm):
        t = cm.reshape(S5_N_BLOCKS, gl, c, p).transpose(0, 1, 3, 2)
        return jnp.einsum('bgpc,gh->bgphc', t, eye).reshape(S5_N_BLOCKS, gl * p, gl * c).astype(BF16)

    lam_bar = lambda a: a.reshape(g, p, c)[:, :, 0].reshape(S5_N_BLOCKS, 1, S5_COL_BLOCK)
    return (in_blocks(bb_re), in_blocks(bb_im), lam_bar(a_re), lam_bar(a_im),
            out_blocks(c_re), out_blocks(c_im))


def _trunk(x, mod, s5_state, kv_caches, wts, tiles):
    (norm_g, ffn_w_in, ffn_w_out, s5_mats, s5_d, s5_w_glu, kv_norm_g, w_kv,
     attn_w_q, attn_w_o, final_norm_g) = wts
    bn, seq, _ = x.shape
    m = bn * seq
    sample = kv_caches is not None
    tm = tiles["tm"]

    if sample:
        def mod_rows(k):
            return jnp.repeat(mod[:, k], seq, axis=0).reshape(1, m, D_MODEL)
    else:
        def mod_rows(k):
            return mod[:, k].reshape(bn, 1, D_MODEL)

    def norm_g_row(a):
        return a.reshape(1, D_MODEL)

    xf = x.reshape(m, D_MODEL)
    new_re, new_im = [], []
    kv = None
    kv_rows = 2 * KV_HEADS
    for layer in range(DEPTH):
        base = 9 * layer
        if layer == N_A_LAYERS:
            kv_flat, kv = _norm_matmul_call(xf, mod_rows(9 * DEPTH), mod_rows(9 * DEPTH + 1),
                                            norm_g_row(kv_norm_g), w_kv, (), F32, tm,
                                            KV_HEADS * HEAD_DIM, head_rows=kv_rows)
        xf = _ffn_call(xf, mod_rows(base), mod_rows(base + 1), mod_rows(base + 2), norm_g_row(norm_g[layer, 0]),
                       ffn_w_in, ffn_w_out, layer, 0, tm, tiles["tf"], tiles["tm_down"], tiles["tn_down"])
        if layer < N_A_LAYERS:
            if s5_state is None:
                h0_re = jnp.zeros((bn, S5_NSTATE), F32)
                h0_im = h0_re
            else:
                h0_re = s5_state[0][layer].reshape(bn, S5_NSTATE)
                h0_im = s5_state[1][layer].reshape(bn, S5_NSTATE)
            bb_re, bb_im, a_re, a_im, c_re, c_im = s5_mats[layer]
            z, h_re, h_im = _s5_call(xf.reshape(bn, seq, D_MODEL), mod[:, base + 3].reshape(bn, 1, D_MODEL),
                                     mod[:, base + 4].reshape(bn, 1, D_MODEL), norm_g_row(norm_g[layer, 1]),
                                     bb_re, bb_im, a_re, a_im, c_re, c_im, s5_d[layer].reshape(1, D_MODEL),
                                     h0_re, h0_im, min(seq, tiles["s5_steps"]), F32 if sample else BF16,
                                     tiles["s5_block_scratch"])
            new_re.append(h_re.reshape(bn, S5_GROUPS, S5_STATE))
            new_im.append(h_im.reshape(bn, S5_GROUPS, S5_STATE))
            xf = _proj_residual_call(z.reshape(m, D_MODEL).astype(BF16), xf, mod_rows(base + 5), s5_w_glu, (layer,),
                                     True, tm, 512, name="s5_glu")
        else:
            bl = layer - N_A_LAYERS
            q = _norm_matmul_call(xf, mod_rows(base + 3), mod_rows(base + 4), norm_g_row(norm_g[layer, 1]),
                                  attn_w_q, (bl,), BF16, tm, 1024)
            if sample:
                qh = q.reshape(bn, seq, N_DIL, KV_HEADS, REP, HEAD_DIM).transpose(0, 2, 3, 4, 1, 5)
                qh = qh.reshape(bn, N_DIL * KV_HEADS, REP * seq, HEAD_DIM)
                o = _attn_sample_call(qh, kv.reshape(N_DIL, bn, seq * kv_rows, HEAD_DIM), kv_caches, seq)
                o = o.reshape(bn, KV_HEADS, REP, seq, HEAD_DIM).transpose(0, 3, 1, 2, 4).reshape(m, D_MODEL)
            else:
                o = _attn_prompt_call(q.reshape(bn, seq, N_DIL * HEADS * HEAD_DIM),
                                      kv_flat.reshape(N_DIL, bn, seq, kv_rows * HEAD_DIM)).reshape(m, D_MODEL)
            xf = _proj_residual_call(o, xf, mod_rows(base + 5), attn_w_o, (bl,), False, tm, 512, name="attn_out")
        xf = _ffn_call(xf, mod_rows(base + 6), mod_rows(base + 7), mod_rows(base + 8), norm_g_row(norm_g[layer, 2]),
                       ffn_w_in, ffn_w_out, layer, 1, tm, tiles["tf"], tiles["tm_down"], tiles["tn_down"])
    y = _final_norm_call(xf, mod_rows(9 * DEPTH + 2), mod_rows(9 * DEPTH + 3), norm_g_row(final_norm_g), tm)
    kv = kv.reshape(N_DIL, bn, seq, 2, KV_HEADS, HEAD_DIM)
    return y.reshape(bn, seq, D_MODEL), jnp.stack(new_re, axis=0), jnp.stack(new_im, axis=0), kv


def kernel(x_prompt, x_sample, c_prompt, c_sample, state_s5_re, state_s5_im, cache_kv_g0, cache_kv_g1, cache_kv_g2, w_mod, b_mod, norm_g, ffn_w_in, ffn_w_out, s5_lambda_re, s5_lambda_im, s5_log_dt, s5_b_re, s5_b_im, s5_c_re, s5_c_im, s5_d, s5_w_glu, kv_norm_g, w_kv, attn_w_q, attn_w_o, final_norm_g):
    bp, seq_p, _ = x_prompt.shape
    bs, seq_s, _ = x_sample.shape
    caches = (cache_kv_g0, cache_kv_g1, cache_kv_g2)
    for (w, d), c in zip(DILATED_PATTERNS, caches):
        assert c.shape[1] == w and w == d * Q_BLOCK and seq_p % (d * Q_BLOCK) == 0
    assert bp <= SUBLANES and bs <= SUBLANES

    n_c = bp + bs
    c_all = jnp.pad(jnp.concatenate([c_prompt, c_sample], axis=0), ((0, (-n_c) % SUBLANES), (0, 0)))
    mod_all = _mod_call(c_all, w_mod, b_mod)
    mod_p = mod_all[:bp].reshape(bp, N_MOD, D_MODEL)
    mod_s = mod_all[bp:n_c].reshape(bs, N_MOD, D_MODEL)

    s5_mats = [_s5_weights(s5_lambda_re[l], s5_lambda_im[l], s5_log_dt[l], s5_b_re[l], s5_b_im[l],
                           s5_c_re[l], s5_c_im[l]) for l in range(N_A_LAYERS)]
    wts = (norm_g, ffn_w_in, ffn_w_out, s5_mats, s5_d, s5_w_glu, kv_norm_g, w_kv, attn_w_q, attn_w_o, final_norm_g)

    y_p, re_p, im_p, kv_p = _trunk(x_prompt, mod_p, None, None, wts,
                                   {"tm": 1024, "tf": 512, "tm_down": 2048, "tn_down": 256, "s5_steps": 32,
                                    "s5_block_scratch": True})
    caches_flat = tuple(c.reshape(bs, c.shape[1] * 2 * KV_HEADS, HEAD_DIM) for c in caches)
    y_s, re_s, im_s, kv_s = _trunk(x_sample, mod_s, (state_s5_re, state_s5_im), caches_flat, wts,
                                   {"tm": bs * seq_s, "tf": 512, "tm_down": bs * seq_s, "tn_down": 256,
                                    "s5_steps": seq_s, "s5_block_scratch": False})

    kvp = [kv_p[g][:, seq_p - min(w, seq_p):] for g, (w, _) in enumerate(DILATED_PATTERNS)]
    return (y_p, y_s, re_p, im_p, kvp[0], kvp[1], kvp[2], re_s, im_s, kv_s[0], kv_s[1], kv_s[2])
```

```python
import functools

import jax
import jax.numpy as jnp
from jax import lax
from jax.experimental import pallas as pl
from jax.experimental.pallas import tpu as pltpu

F32 = jnp.float32
BF16 = jnp.bfloat16

D_MODEL = 2048
DEPTH = 4
N_A_LAYERS = DEPTH // 2
D_FF = 5632
S5_GROUP = 16
S5_GROUPS = D_MODEL // S5_GROUP
S5_STATE = 64
S5_NSTATE = S5_GROUPS * S5_STATE
HEAD_DIM = 128
HEADS = 16
KV_HEADS = 4
REP = HEADS // KV_HEADS
DILATED_PATTERNS = ((128, 1), (512, 4), (2048, 16))
N_DIL = len(DILATED_PATTERNS)
Q_BLOCK = 128
N_MOD = 9 * DEPTH + 4
EPS = 1e-6
ATTN_SCALE = HEAD_DIM ** -0.5

VMEM_LIMIT_BYTES = 56 * 1024 * 1024
SUBLANES = 8
S5_COL_BLOCK = 1024
S5_CH_BLOCK = S5_COL_BLOCK // S5_STATE * S5_GROUP
S5_N_BLOCKS = S5_NSTATE // S5_COL_BLOCK


def _params(*sem):
    return pltpu.CompilerParams(dimension_semantics=sem, vmem_limit_bytes=VMEM_LIMIT_BYTES)


def _rms_mod(x, g, shift, scale):
    xn = x * lax.rsqrt(jnp.mean(x * x, axis=-1, keepdims=True) + EPS)
    return xn * g * (1.0 + scale) + shift


def _norm_rows(x_ref, g_ref, sh_ref, sc_ref, h_ref, row_chunk):
    rows = x_ref.shape[0]
    g = g_ref[...]
    mod_rows = sh_ref.shape[0]

    def body(c, carry):
        r0 = pl.multiple_of(c * row_chunk, row_chunk)
        x = x_ref[pl.ds(r0, row_chunk), :]
        if mod_rows == rows and rows != row_chunk:
            sh = sh_ref[pl.ds(r0, row_chunk), :]
            sc = sc_ref[pl.ds(r0, row_chunk), :]
        else:
            sh = sh_ref[...]
            sc = sc_ref[...]
        h_ref[pl.ds(r0, row_chunk), :] = _rms_mod(x, g, sh, sc).astype(h_ref.dtype)
        return carry

    lax.fori_loop(0, rows // row_chunk, body, 0)


def _mod_kernel(c_ref, w_ref, b_ref, o_ref):
    c = c_ref[...]
    a = (c * jax.nn.sigmoid(c)).astype(BF16)
    o_ref[...] = jnp.dot(a, w_ref[...].astype(BF16), preferred_element_type=F32) + b_ref[...]


def _mod_call(c_all, w_mod, b_mod):
    rows = c_all.shape[0]
    n = w_mod.shape[1]
    tn = 1024
    return pl.pallas_call(
        _mod_kernel,
        grid=(n // tn,),
        in_specs=[
            pl.BlockSpec((rows, D_MODEL), lambda j: (0, 0)),
            pl.BlockSpec((D_MODEL, tn), lambda j: (0, j)),
            pl.BlockSpec((1, tn), lambda j: (0, j)),
        ],
        out_specs=pl.BlockSpec((rows, tn), lambda j: (0, j)),
        out_shape=jax.ShapeDtypeStruct((rows, n), F32),
        compiler_params=_params("arbitrary"),
        name="adaln_mod",
    )(c_all, w_mod, b_mod.reshape(1, n))


def _ffn_up_kernel(x_ref, sh_ref, sc_ref, g_ref, wg_ref, wu_ref, a_ref, h_ref, *, row_chunk):
    @pl.when(pl.program_id(1) == 0)
    def _():
        _norm_rows(x_ref, g_ref, sh_ref, sc_ref, h_ref, row_chunk)

    h = h_ref[...]
    g = jnp.dot(h, wg_ref[...].astype(BF16), preferred_element_type=F32)
    u = jnp.dot(h, wu_ref[...].astype(BF16), preferred_element_type=F32)
    a_ref[...] = (g * jax.nn.sigmoid(g) * u).astype(a_ref.dtype)


def _ffn_call(x, shift, scale, gate, g_norm, w_in, w_out, layer, which, tm, tf, tm_down, tn):
    m = x.shape[0]
    nb, r, _ = shift.shape
    tiles_per_seq = (m // nb) // tm if r == 1 else 1
    nk = D_FF // tf
    mod_spec = pl.BlockSpec((None, r, D_MODEL), lambda i, k: (i // tiles_per_seq, 0, 0))
    a = pl.pallas_call(
        functools.partial(_ffn_up_kernel, row_chunk=min(tm, 64)),
        grid=(m // tm, nk),
        in_specs=[
            pl.BlockSpec((tm, D_MODEL), lambda i, k: (i, 0)),
            mod_spec, mod_spec,
            pl.BlockSpec((1, D_MODEL), lambda i, k: (0, 0)),
            pl.BlockSpec((None, None, D_MODEL, tf), lambda i, k: (layer, which, 0, k)),
            pl.BlockSpec((None, None, D_MODEL, tf), lambda i, k: (layer, which, 0, k + nk)),
        ],
        out_specs=pl.BlockSpec((tm, tf), lambda i, k: (i, k)),
        out_shape=jax.ShapeDtypeStruct((m, D_FF), BF16),
        scratch_shapes=[pltpu.VMEM((tm, D_MODEL), BF16)],
        compiler_params=_params("parallel", "arbitrary"),
        name="ffn_up",
    )(x, shift, scale, g_norm, w_in, w_in)
    return _proj_residual_call(a, x, gate, w_out, (layer, which), False, tm_down, tn, gate_scale=0.5,
                               name="ffn_down", z_buffers=1 if tm_down > tm else 2)


def _norm_matmul_kernel(x_ref, sh_ref, sc_ref, g_ref, w_ref, o_ref, *rest, row_chunk, head_rows):
    h_ref = rest[-1]

    @pl.when(pl.program_id(1) == 0)
    def _():
        _norm_rows(x_ref, g_ref, sh_ref, sc_ref, h_ref, row_chunk)

    res = jnp.dot(h_ref[...], w_ref[...].astype(BF16), preferred_element_type=F32)
    o_ref[...] = res.astype(o_ref.dtype)
    if head_rows:
        heads_ref = rest[0]
        rows, cols = res.shape
        heads_per_tile = cols // HEAD_DIM
        first = (pl.program_id(1) % (head_rows // heads_per_tile)) * heads_per_tile
        for c in range(heads_per_tile):
            heads_ref[pl.ds(first + c, rows, stride=head_rows), :] = res[:, c * HEAD_DIM:(c + 1) * HEAD_DIM]


def _norm_matmul_call(x, shift, scale, g_norm, w, w_lead, out_dtype, tm, tn, head_rows=0):
    m = x.shape[0]
    nb, r, _ = shift.shape
    n = w.shape[-1]
    tiles_per_seq = (m // nb) // tm if r == 1 else 1
    mod_spec = pl.BlockSpec((None, r, D_MODEL), lambda i, j: (i // tiles_per_seq, 0, 0))
    if head_rows:
        slab = head_rows * HEAD_DIM
        tps = slab // tn
        out_spec = (pl.BlockSpec((None, tm, tn), lambda i, j: (j // tps, i, j % tps)),
                    pl.BlockSpec((None, tm * head_rows, HEAD_DIM), lambda i, j: (j // tps, i, 0)))
        out_shape = (jax.ShapeDtypeStruct((n // slab, m, slab), out_dtype),
                     jax.ShapeDtypeStruct((n // slab, m * head_rows, HEAD_DIM), out_dtype))
    else:
        out_spec = pl.BlockSpec((tm, tn), lambda i, j: (i, j))
        out_shape = jax.ShapeDtypeStruct((m, n), out_dtype)
    return pl.pallas_call(
        functools.partial(_norm_matmul_kernel, row_chunk=min(tm, 64), head_rows=head_rows),
        grid=(m // tm, n // tn),
        in_specs=[
            pl.BlockSpec((tm, D_MODEL), lambda i, j: (i, 0)),
            mod_spec, mod_spec,
            pl.BlockSpec((1, D_MODEL), lambda i, j: (0, 0)),
            pl.BlockSpec((None,) * len(w_lead) + (D_MODEL, tn), lambda i, j: (*w_lead, 0, j)),
        ],
        out_specs=out_spec,
        out_shape=out_shape,
        scratch_shapes=[pltpu.VMEM((tm, D_MODEL), BF16)],
        compiler_params=_params("parallel", "arbitrary"),
        name="norm_matmul",
    )(x, shift, scale, g_norm, w)


def _proj_residual_kernel(z_ref, x_ref, gt_ref, *refs, glu, gate_scale):
    z = z_ref[...]
    if glu:
        wa_ref, wg_ref, o_ref = refs
        a = jnp.dot(z, wa_ref[...].astype(BF16), preferred_element_type=F32)
        g = jnp.dot(z, wg_ref[...].astype(BF16), preferred_element_type=F32)
        y = a * jax.nn.sigmoid(g)
    else:
        w_ref, o_ref = refs
        y = jnp.dot(z, w_ref[...].astype(BF16), preferred_element_type=F32)
    gate = gt_ref[...] if gate_scale == 1.0 else gate_scale * gt_ref[...]
    o_ref[...] = x_ref[...] + gate * y


def _proj_residual_call(z, x, gate, w, w_lead, glu, tm, tn, gate_scale=1.0, name="proj_residual", z_buffers=2):
    m = x.shape[0]
    kdim = z.shape[1]
    nb, r, _ = gate.shape
    tiles_per_seq = (m // nb) // tm if r == 1 else 1
    n_blocks = D_MODEL // tn
    w_block = (None,) * len(w_lead) + (kdim, tn)
    w_specs = [pl.BlockSpec(w_block, lambda i, j: (*w_lead, 0, j))]
    w_args = [w]
    if glu:
        w_specs.append(pl.BlockSpec(w_block, lambda i, j: (*w_lead, 0, j + n_blocks)))
        w_args.append(w)
    return pl.pallas_call(
        functools.partial(_proj_residual_kernel, glu=glu, gate_scale=gate_scale),
        grid=(m // tm, n_blocks),
        in_specs=[
            (pl.BlockSpec((tm, kdim), lambda i, j: (i, 0)) if z_buffers == 2 else
             pl.BlockSpec((tm, kdim), lambda i, j: (i, 0), pipeline_mode=pl.Buffered(z_buffers))),
            pl.BlockSpec((tm, tn), lambda i, j: (i, j)),
            pl.BlockSpec((None, r, tn), lambda i, j: (i // tiles_per_seq, 0, j)),
        ] + w_specs,
        out_specs=pl.BlockSpec((tm, tn), lambda i, j: (i, j)),
        out_shape=jax.ShapeDtypeStruct((m, D_MODEL), F32),
        compiler_params=_params("parallel", "arbitrary"),
        name=name,
    )(z, x, gate, *w_args)


def _final_norm_kernel(x_ref, sh_ref, sc_ref, g_ref, o_ref, *, row_chunk):
    _norm_rows(x_ref, g_ref, sh_ref, sc_ref, o_ref, row_chunk)


def _final_norm_call(x, shift, scale, g_norm, tm):
    m = x.shape[0]
    nb, r, _ = shift.shape
    tiles_per_seq = (m // nb) // tm if r == 1 else 1
    mod_spec = pl.BlockSpec((None, r, D_MODEL), lambda i: (i // tiles_per_seq, 0, 0))
    return pl.pallas_call(
        functools.partial(_final_norm_kernel, row_chunk=min(tm, 64)),
        grid=(m // tm,),
        in_specs=[
            pl.BlockSpec((tm, D_MODEL), lambda i: (i, 0)),
            mod_spec, mod_spec,
            pl.BlockSpec((1, D_MODEL), lambda i: (0, 0)),
        ],
        out_specs=pl.BlockSpec((tm, D_MODEL), lambda i: (i, 0)),
        out_shape=jax.ShapeDtypeStruct((m, D_MODEL), F32),
        compiler_params=_params("parallel"),
        name="final_norm",
    )(x, shift, scale, g_norm)


def _s5_disc_kernel(lre_ref, lim_ref, ldt_ref, bre_ref, bim_ref, are_ref, aim_ref, bbre_ref, bbim_ref):
    lam_re = lre_ref[...]
    lam_im = lim_ref[...]
    dt = jnp.exp(ldt_ref[...])
    decay = jnp.exp(lam_re * dt)
    ab_re = decay * jnp.cos(lam_im * dt)
    ab_im = decay * jnp.sin(lam_im * dt)
    den = lam_re * lam_re + lam_im * lam_im
    f_re = ((ab_re - 1.0) * lam_re + ab_im * lam_im) / den
    f_im = (ab_im * lam_re - (ab_re - 1.0) * lam_im) / den
    b_re = bre_ref[...]
    b_im = bim_ref[...]
    are_ref[...] = ab_re
    aim_ref[...] = ab_im
    bbre_ref[...] = f_re * b_re - f_im * b_im
    bbim_ref[...] = f_re * b_im + f_im * b_re


def _s5_disc_call(lam_re, lam_im, log_dt, b_re, b_im):
    shape = jax.ShapeDtypeStruct(lam_re.shape, F32)
    return pl.pallas_call(
        _s5_disc_kernel,
        out_shape=(shape, shape, shape, shape),
        name="s5_discretise",
    )(lam_re, lam_im, log_dt, b_re, b_im)


def _s5_kernel(x_ref, sh_ref, sc_ref, g_ref, bbre_ref, bbim_ref, are_ref, aim_ref, cre_ref, cim_ref,
               d_ref, h0re_ref, h0im_ref, z_ref, hre_out, him_out,
               u_scr, sre_scr, sim_scr, y_scr, pwre_scr, pwim_scr, *, row_chunk):
    t = pl.program_id(0)
    nb, steps, _ = x_ref.shape
    rows = steps * nb
    lane = HEAD_DIM
    seg = SUBLANES // nb
    seg_len = steps // seg
    assert seg in (1, 2)

    @pl.when(t == 0)
    def _():
        hre_out[...] = h0re_ref[...]
        him_out[...] = h0im_ref[...]
        if seg > 1:
            zero = jnp.zeros((nb, S5_COL_BLOCK), F32)
            for cb in range(S5_N_BLOCKS):
                a_re = jnp.broadcast_to(are_ref[cb], (nb, S5_COL_BLOCK))
                a_im = jnp.broadcast_to(aim_ref[cb], (nb, S5_COL_BLOCK))
                p_re, p_im = a_re, a_im
                for i in range(seg_len):
                    pwre_scr[cb, i * SUBLANES:(i + 1) * SUBLANES, :] = jnp.concatenate([zero, p_re], axis=0)
                    pwim_scr[cb, i * SUBLANES:(i + 1) * SUBLANES, :] = jnp.concatenate([zero, p_im], axis=0)
                    p_re, p_im = p_re * a_re - p_im * a_im, p_re * a_im + p_im * a_re

    g = g_ref[...]
    for b in range(nb):
        for s in range(seg):
            def norm_chunk(c, carry, b=b, s=s):
                r0 = pl.multiple_of(c * row_chunk, row_chunk)
                u = _rms_mod(x_ref[b, pl.ds(s * seg_len + r0, row_chunk), :], g, sh_ref[b], sc_ref[b])
                for j in range(D_MODEL // lane):
                    u_scr[j, pl.ds(r0 * SUBLANES + s * nb + b, row_chunk, stride=SUBLANES), :] = \
                        u[:, j * lane:(j + 1) * lane]
                return carry

            lax.fori_loop(0, seg_len // row_chunk, norm_chunk, 0)

    tiles_per_block = S5_CH_BLOCK // lane
    per_block_scratch = sre_scr.shape[0] == S5_N_BLOCKS
    for cb in range(S5_N_BLOCKS):
        ch = slice(cb * S5_CH_BLOCK, (cb + 1) * S5_CH_BLOCK)
        st = slice(cb * S5_COL_BLOCK, (cb + 1) * S5_COL_BLOCK)
        slab = cb if per_block_scratch else 0
        sre, sim, ys = sre_scr.at[slab], sim_scr.at[slab], y_scr.at[slab]
        u_blk = jnp.concatenate([u_scr[cb * tiles_per_block + j] for j in range(tiles_per_block)], axis=1)
        u_bf = u_blk.astype(BF16)
        sre[...] = jnp.dot(u_bf, bbre_ref[cb], preferred_element_type=F32)
        sim[...] = jnp.dot(u_bf, bbim_ref[cb], preferred_element_type=F32)

        a_re = jnp.broadcast_to(are_ref[cb], (SUBLANES, S5_COL_BLOCK))
        a_im = jnp.broadcast_to(aim_ref[cb], (SUBLANES, S5_COL_BLOCK))

        def tile_step(i, carry, a_re=a_re, a_im=a_im, sre=sre, sim=sim):
            h_re, h_im = carry
            r0 = i * SUBLANES if isinstance(i, int) else pl.multiple_of(i * SUBLANES, SUBLANES)
            n_re = a_re * h_re - a_im * h_im + sre[pl.ds(r0, SUBLANES), :]
            n_im = a_re * h_im + a_im * h_re + sim[pl.ds(r0, SUBLANES), :]
            sre[pl.ds(r0, SUBLANES), :] = n_re
            sim[pl.ds(r0, SUBLANES), :] = n_im
            return n_re, n_im

        def first_rows(v):
            return jnp.concatenate([v[:nb]] * seg, axis=0)

        h_re, h_im = hre_out[:, st], him_out[:, st]
        if seg > 1:
            zero = jnp.zeros((SUBLANES - nb, S5_COL_BLOCK), F32)
            h_re, h_im = jnp.concatenate([h_re, zero], axis=0), jnp.concatenate([h_im, zero], axis=0)
        carry = (h_re, h_im)
        if per_block_scratch:
            for i in range(seg_len):
                carry = tile_step(i, carry)
        else:
            carry = lax.fori_loop(0, seg_len, tile_step, carry, unroll=min(seg_len, 4))
        h_re, h_im = carry
        if seg > 1:
            c_re, c_im = first_rows(h_re), first_rows(h_im)
            for i in range(seg_len):
                rows_i = slice(i * SUBLANES, (i + 1) * SUBLANES)
                p_re, p_im = pwre_scr[cb, rows_i, :], pwim_scr[cb, rows_i, :]
                f_re = sre[rows_i, :] + (p_re * c_re - p_im * c_im)
                f_im = sim[rows_i, :] + (p_re * c_im + p_im * c_re)
                sre[rows_i, :] = f_re
                sim[rows_i, :] = f_im
            h_re, h_im = f_re[SUBLANES - nb:], f_im[SUBLANES - nb:]
        hre_out[:, st] = h_re
        him_out[:, st] = h_im

        y = (jnp.dot(sre[...].astype(BF16), cre_ref[cb], preferred_element_type=F32)
             - jnp.dot(sim[...].astype(BF16), cim_ref[cb], preferred_element_type=F32))
        zf = jax.nn.gelu(y + d_ref[:, ch] * u_blk)
        for j in range(tiles_per_block):
            ys[j] = zf[:, j * lane:(j + 1) * lane]
        for b in range(nb):
            for s in range(seg):
                for j in range(tiles_per_block):
                    c0 = cb * S5_CH_BLOCK + j * lane
                    z_ref[b, s * seg_len:(s + 1) * seg_len, c0:c0 + lane] = \
                        ys[j, pl.ds(s * nb + b, seg_len, stride=SUBLANES), :].astype(z_ref.dtype)


def _s5_call(x, shift, scale, g_norm, bb_re, bb_im, a_re, a_im, c_re, c_im, d_skip,
             h0_re, h0_im, steps_per_chunk, z_dtype, per_block_scratch):
    nb, seq, _ = x.shape
    assert SUBLANES % nb == 0 and seq % steps_per_chunk == 0
    tr = steps_per_chunk * nb
    slabs = S5_N_BLOCKS if per_block_scratch else 1
    seg = SUBLANES // nb
    pw_rows = tr if seg > 1 else SUBLANES
    row_chunk = min(steps_per_chunk // (SUBLANES // nb), 64)
    const2 = lambda t: (0, 0)
    const3 = lambda t: (0, 0, 0)
    state_spec = pl.BlockSpec((nb, S5_NSTATE), const2)
    io_spec = pl.BlockSpec((nb, steps_per_chunk, D_MODEL), lambda t: (0, t, 0))
    return pl.pallas_call(
        functools.partial(_s5_kernel, row_chunk=row_chunk),
        grid=(seq // steps_per_chunk,),
        in_specs=[
            io_spec,
            pl.BlockSpec((nb, 1, D_MODEL), const3),
            pl.BlockSpec((nb, 1, D_MODEL), const3),
            pl.BlockSpec((1, D_MODEL), const2),
            pl.BlockSpec((S5_N_BLOCKS, S5_CH_BLOCK, S5_COL_BLOCK), const3, pipeline_mode=pl.Buffered(1)),
            pl.BlockSpec((S5_N_BLOCKS, S5_CH_BLOCK, S5_COL_BLOCK), const3, pipeline_mode=pl.Buffered(1)),
            pl.BlockSpec((S5_N_BLOCKS, 1, S5_COL_BLOCK), const3),
            pl.BlockSpec((S5_N_BLOCKS, 1, S5_COL_BLOCK), const3),
            pl.BlockSpec((S5_N_BLOCKS, S5_COL_BLOCK, S5_CH_BLOCK), const3, pipeline_mode=pl.Buffered(1)),
            pl.BlockSpec((S5_N_BLOCKS, S5_COL_BLOCK, S5_CH_BLOCK), const3, pipeline_mode=pl.Buffered(1)),
            pl.BlockSpec((1, D_MODEL), const2),
            state_spec, state_spec,
        ],
        out_specs=(io_spec, state_spec, state_spec),
        out_shape=(jax.ShapeDtypeStruct((nb, seq, D_MODEL), z_dtype),
                   jax.ShapeDtypeStruct((nb, S5_NSTATE), F32),
                   jax.ShapeDtypeStruct((nb, S5_NSTATE), F32)),
        scratch_shapes=[pltpu.VMEM((D_MODEL // HEAD_DIM, tr, HEAD_DIM), F32),
                        pltpu.VMEM((slabs, tr, S5_COL_BLOCK), F32),
                        pltpu.VMEM((slabs, tr, S5_COL_BLOCK), F32),
                        pltpu.VMEM((slabs, S5_CH_BLOCK // HEAD_DIM, tr, HEAD_DIM), F32),
                        pltpu.VMEM((S5_N_BLOCKS, pw_rows, S5_COL_BLOCK), F32),
                        pltpu.VMEM((S5_N_BLOCKS, pw_rows, S5_COL_BLOCK), F32)],
        compiler_params=_params("arbitrary"),
        name="s5_mixer",
    )(x, shift, scale, g_norm, bb_re, bb_im, a_re, a_im, c_re, c_im, d_skip, h0_re, h0_im)


def _softmax_pv(s, v_bf):
    m = jnp.max(s, axis=-1, keepdims=True)
    p = jnp.exp(s - m)
    l = jnp.sum(p, axis=-1, keepdims=True)
    o = jnp.dot(p.astype(BF16), v_bf, preferred_element_type=F32) / l
    return o, jnp.broadcast_to(m + jnp.log(l), o.shape)


def _qk(q_bf, k_bf):
    return lax.dot_general(q_bf, k_bf, (((1,), (1,)), ((), ())), preferred_element_type=F32) * ATTN_SCALE


def _attn_prompt_kernel(q0_ref, q1_ref, q2_ref, k0_ref, v0_ref, k1_ref, v1_ref, k2_ref, v2_ref, o_ref,
                        qf_scr, o1_scr, l1_scr, o2_scr, l2_scr, band_scr, first_scr):
    seq = q0_ref.shape[0]
    qb = Q_BLOCK
    row = lax.broadcasted_iota(jnp.int32, (REP * qb, 2 * qb), 0) & (qb - 1)
    col = lax.broadcasted_iota(jnp.int32, (REP * qb, 2 * qb), 1)
    band_scr[...] = jnp.where((col >= row) & (col <= row + qb), 0.0, -jnp.inf)
    row1 = lax.broadcasted_iota(jnp.int32, (REP * qb, qb), 0) & (qb - 1)
    col1 = lax.broadcasted_iota(jnp.int32, (REP * qb, qb), 1)
    first_scr[...] = jnp.where(col1 <= row1, 0.0, -jnp.inf)

    def rows_of(start, d):
        return pl.ds(start, qb) if d == 1 else pl.ds(start, qb, stride=d)

    def load_q(src_ref, start, d):
        if d == 1:
            parts = [src_ref[pl.ds(start, qb), e * HEAD_DIM:(e + 1) * HEAD_DIM] for e in range(REP)]
        else:
            parts = [src_ref[e, rows_of(start, d), :] for e in range(REP)]
        return jnp.concatenate(parts, axis=0).astype(BF16)

    def unit(src_q, k_ref, v_ref, start, d, first):
        q4 = load_q(src_q, start, d)
        k_cur = k_ref[rows_of(start, d), :]
        v_cur = v_ref[rows_of(start, d), :]
        if first:
            s = _qk(q4, k_cur.astype(BF16)) + first_scr[...]
            return _softmax_pv(s, v_cur.astype(BF16))
        prev = start - qb * d
        k_band = jnp.concatenate([k_ref[rows_of(prev, d), :], k_cur], axis=0)
        v_band = jnp.concatenate([v_ref[rows_of(prev, d), :], v_cur], axis=0)
        s = _qk(q4, k_band.astype(BF16)) + band_scr[...]
        return _softmax_pv(s, v_band.astype(BF16))

    def stage_group(q_ref, k_ref, v_ref, d, o_scr, l_scr):
        n_blocks = seq // (qb * d)

        def copy(c, carry):
            r0 = pl.multiple_of(c * 256, 256)
            for e in range(REP):
                qf_scr[e, pl.ds(r0, 256), :] = q_ref[pl.ds(r0, 256), e * HEAD_DIM:(e + 1) * HEAD_DIM].astype(F32)
            return carry

        lax.fori_loop(0, seq // 256, copy, 0)

        def store(start, o, lse):
            for e in range(REP):
                sl = slice(e * qb, (e + 1) * qb)
                o_scr[e, rows_of(start, d), :] = o[sl]
                l_scr[e, rows_of(start, d), :] = lse[sl]

        def per_class(r, carry):
            store(r, *unit(qf_scr, k_ref, v_ref, r, d, True))

            def per_block(jb, c2):
                start = jb * (qb * d) + r
                store(start, *unit(qf_scr, k_ref, v_ref, start, d, False))
                return c2

            if n_blocks > 1:
                lax.fori_loop(1, n_blocks, per_block, 0, unroll=True)
            return carry

        lax.fori_loop(0, d, per_class, 0, unroll=2 if n_blocks == 1 else 1)

    stage_group(q2_ref, k2_ref, v2_ref, DILATED_PATTERNS[2][1], o2_scr, l2_scr)
    stage_group(q1_ref, k1_ref, v1_ref, DILATED_PATTERNS[1][1], o1_scr, l1_scr)

    def merge(start, o0, lse0):
        for e in range(REP):
            sl = slice(e * qb, (e + 1) * qb)
            cols = slice(e * HEAD_DIM, (e + 1) * HEAD_DIM)
            la = lse0[sl]
            lb = l1_scr[e, pl.ds(start, qb), :]
            lc = l2_scr[e, pl.ds(start, qb), :]
            mx = jnp.maximum(jnp.maximum(la, lb), lc)
            wa = jnp.exp(la - mx)
            wb = jnp.exp(lb - mx)
            wc = jnp.exp(lc - mx)
            inv = 1.0 / (wa + wb + wc)
            acc = (wa * inv) * o0[sl] + (wb * inv) * o1_scr[e, pl.ds(start, qb), :] \
                + (wc * inv) * o2_scr[e, pl.ds(start, qb), :]
            o_ref[pl.ds(start, qb), cols] = acc.astype(o_ref.dtype)

    merge(0, *unit(q0_ref, k0_ref, v0_ref, 0, 1, True))

    def per_block0(jb, carry):
        start = pl.multiple_of(jb * qb, qb)
        merge(start, *unit(q0_ref, k0_ref, v0_ref, start, 1, False))
        return carry

    n_blocks0 = seq // qb
    lax.fori_loop(1, n_blocks0, per_block0, 0, unroll=3 if (n_blocks0 - 1) % 3 == 0 else 1)


def _attn_prompt_call(q, kv):
    b, seq, _ = q.shape
    qw = REP * HEAD_DIM
    q_specs = [pl.BlockSpec((None, seq, qw), functools.partial(lambda bi, h, g: (bi, 0, g * KV_HEADS + h), g=g))
               for g in range(N_DIL)]
    kv_specs = []
    for g in range(N_DIL):
        kv_specs.append(pl.BlockSpec((None, None, seq, HEAD_DIM),
                                     functools.partial(lambda bi, h, g: (g, bi, 0, h), g=g)))
        kv_specs.append(pl.BlockSpec((None, None, seq, HEAD_DIM),
                                     functools.partial(lambda bi, h, g: (g, bi, 0, KV_HEADS + h), g=g)))
    kv_args = [kv] * (2 * N_DIL)
    return pl.pallas_call(
        _attn_prompt_kernel,
        grid=(b, KV_HEADS),
        in_specs=q_specs + kv_specs,
        out_specs=pl.BlockSpec((None, seq, qw), lambda bi, h: (bi, 0, h)),
        out_shape=jax.ShapeDtypeStruct((b, seq, HEADS * HEAD_DIM), BF16),
        scratch_shapes=[pltpu.VMEM((REP, seq, HEAD_DIM), F32),
                        pltpu.VMEM((REP, seq, HEAD_DIM), F32), pltpu.VMEM((REP, seq, HEAD_DIM), F32),
                        pltpu.VMEM((REP, seq, HEAD_DIM), F32), pltpu.VMEM((REP, seq, HEAD_DIM), F32),
                        pltpu.VMEM((REP * Q_BLOCK, 2 * Q_BLOCK), F32), pltpu.VMEM((REP * Q_BLOCK, Q_BLOCK), F32)],
        compiler_params=_params("parallel", "arbitrary"),
        name="dilated_attention_prompt",
    )(q, q, q, *kv_args)


def _attn_sample_kernel(q_ref, kvn_ref, c0_ref, c1_ref, c2_ref, o_ref, *, n_tok):
    rows = REP * n_tok
    cache_refs = (c0_ref, c1_ref, c2_ref)
    kv_rows = 2 * KV_HEADS
    for h in range(KV_HEADS):
        outs, lses = [], []
        for g, (_, d) in enumerate(DILATED_PATTERNS):
            q = q_ref[g * KV_HEADS + h]
            lw = cache_refs[g].shape[0] // kv_rows
            k_c = cache_refs[g][pl.ds(h, lw, stride=kv_rows), :].astype(BF16)
            v_c = cache_refs[g][pl.ds(KV_HEADS + h, lw, stride=kv_rows), :].astype(BF16)
            k_n = kvn_ref[g, pl.ds(h, n_tok, stride=kv_rows), :].astype(BF16)
            v_n = kvn_ref[g, pl.ds(KV_HEADS + h, n_tok, stride=kv_rows), :].astype(BF16)
            tq_c = lax.broadcasted_iota(jnp.int32, (rows, lw), 0) & (n_tok - 1)
            idx_c = lax.broadcasted_iota(jnp.int32, (rows, lw), 1)
            ok_c = (idx_c >= tq_c) & (((idx_c - tq_c) & (d - 1)) == 0)
            tq_n = lax.broadcasted_iota(jnp.int32, (rows, n_tok), 0) & (n_tok - 1)
            idx_n = lax.broadcasted_iota(jnp.int32, (rows, n_tok), 1)
            ok_n = (idx_n <= tq_n) & (((tq_n - idx_n) & (d - 1)) == 0)
            s_c = jnp.where(ok_c, _qk(q, k_c), -jnp.inf)
            s_n = jnp.where(ok_n, _qk(q, k_n), -jnp.inf)
            m = jnp.maximum(jnp.max(s_c, axis=-1, keepdims=True), jnp.max(s_n, axis=-1, keepdims=True))
            p_c = jnp.exp(s_c - m)
            p_n = jnp.exp(s_n - m)
            l = jnp.sum(p_c, axis=-1, keepdims=True) + jnp.sum(p_n, axis=-1, keepdims=True)
            o = (jnp.dot(p_c.astype(BF16), v_c, preferred_element_type=F32)
                 + jnp.dot(p_n.astype(BF16), v_n, preferred_element_type=F32)) / l
            outs.append(o)
            lses.append(m + jnp.log(l))
        mx = jnp.maximum(jnp.maximum(lses[0], lses[1]), lses[2])
        w = [jnp.exp(x - mx) for x in lses]
        inv = 1.0 / (w[0] + w[1] + w[2])
        acc = (w[0] * inv) * outs[0] + (w[1] * inv) * outs[1] + (w[2] * inv) * outs[2]
        o_ref[h] = acc.astype(o_ref.dtype)


def _attn_sample_call(q_heads, kv_new, caches, n_tok):
    b = q_heads.shape[0]
    rows = REP * n_tok
    assert n_tok & (n_tok - 1) == 0
    in_specs = [pl.BlockSpec((None, N_DIL * KV_HEADS, rows, HEAD_DIM), lambda bi: (bi, 0, 0, 0)),
                pl.BlockSpec((N_DIL, None, kv_new.shape[2], HEAD_DIM), lambda bi: (0, bi, 0, 0))]
    in_specs += [pl.BlockSpec((None, c.shape[1], HEAD_DIM), lambda bi: (bi, 0, 0)) for c in caches]
    return pl.pallas_call(
        functools.partial(_attn_sample_kernel, n_tok=n_tok),
        grid=(b,),
        in_specs=in_specs,
        out_specs=pl.BlockSpec((None, KV_HEADS, rows, HEAD_DIM), lambda bi: (bi, 0, 0, 0)),
        out_shape=jax.ShapeDtypeStruct((b, KV_HEADS, rows, HEAD_DIM), BF16),
        compiler_params=_params("parallel"),
        name="dilated_attention_sample",
    )(q_heads, kv_new, *caches)


def _s5_weights(lam_re, lam_im, log_dt, b_re, b_im, c_re, c_im):
    g, p, c = S5_GROUPS, S5_STATE, S5_GROUP
    rep = lambda a: jnp.repeat(a, c, axis=1)
    a_re, a_im, bb_re, bb_im = _s5_disc_call(rep(lam_re), rep(lam_im), log_dt.reshape(g, 1),
                                             b_re.reshape(g, p * c), b_im.reshape(g, p * c))
    gl = S5_CH_BLOCK // c
    eye = jnp.eye(gl, dtype=F32)

    def in_blocks(bb):
        t = bb.reshape(S5_N_BLOCKS, gl, p, c).transpose(0, 1, 3, 2)
        return jnp.einsum('bgcp,gh->bgchp', t, eye).reshape(S5_N_BLOCKS, gl * c, gl * p).astype(BF16)

    def out_blocks(cm):
        t = cm.reshape(S5_N_BLOCKS, gl, c, p).transpose(0, 1, 3, 2)
        return jnp.einsum('bgpc,gh->bgphc', t, eye).reshape(S5_N_BLOCKS, gl * p, gl * c).astype(BF16)

    lam_bar = lambda a: a.reshape(g, p, c)[:, :, 0].reshape(S5_N_BLOCKS, 1, S5_COL_BLOCK)
    return (in_blocks(bb_re), in_blocks(bb_im), lam_bar(a_re), lam_bar(a_im),
            out_blocks(c_re), out_blocks(c_im))


def _trunk(x, mod, s5_state, kv_caches, wts, tiles):
    (norm_g, ffn_w_in, ffn_w_out, s5_mats, s5_d, s5_w_glu, kv_norm_g, w_kv,
     attn_w_q, attn_w_o, final_norm_g) = wts
    bn, seq, _ = x.shape
    m = bn * seq
    sample = kv_caches is not None
    tm = tiles["tm"]

    if sample:
        def mod_rows(k):
            return jnp.repeat(mod[:, k], seq, axis=0).reshape(1, m, D_MODEL)
    else:
        def mod_rows(k):
            return mod[:, k].reshape(bn, 1, D_MODEL)

    def norm_g_row(a):
        return a.reshape(1, D_MODEL)

    xf = x.reshape(m, D_MODEL)
    new_re, new_im = [], []
    kv = None
    kv_rows = 2 * KV_HEADS
    for layer in range(DEPTH):
        base = 9 * layer
        if layer == N_A_LAYERS:
            kv_flat, kv = _norm_matmul_call(xf, mod_rows(9 * DEPTH), mod_rows(9 * DEPTH + 1),
                                            norm_g_row(kv_norm_g), w_kv, (), F32, tm,
                                            KV_HEADS * HEAD_DIM, head_rows=kv_rows)
        xf = _ffn_call(xf, mod_rows(base), mod_rows(base + 1), mod_rows(base + 2), norm_g_row(norm_g[layer, 0]),
                       ffn_w_in, ffn_w_out, layer, 0, tm, tiles["tf"], tiles["tm_down"], tiles["tn_down"])
        if layer < N_A_LAYERS:
            if s5_state is None:
                h0_re = jnp.zeros((bn, S5_NSTATE), F32)
                h0_im = h0_re
            else:
                h0_re = s5_state[0][layer].reshape(bn, S5_NSTATE)
                h0_im = s5_state[1][layer].reshape(bn, S5_NSTATE)
            bb_re, bb_im, a_re, a_im, c_re, c_im = s5_mats[layer]
            z, h_re, h_im = _s5_call(xf.reshape(bn, seq, D_MODEL), mod[:, base + 3].reshape(bn, 1, D_MODEL),
                                     mod[:, base + 4].reshape(bn, 1, D_MODEL), norm_g_row(norm_g[layer, 1]),
                                     bb_re, bb_im, a_re, a_im, c_re, c_im, s5_d[layer].reshape(1, D_MODEL),
                                     h0_re, h0_im, min(seq, tiles["s5_steps"]), F32 if sample else BF16,
                                     tiles["s5_block_scratch"])
            new_re.append(h_re.reshape(bn, S5_GROUPS, S5_STATE))
            new_im.append(h_im.reshape(bn, S5_GROUPS, S5_STATE))
            xf = _proj_residual_call(z.reshape(m, D_MODEL).astype(BF16), xf, mod_rows(base + 5), s5_w_glu, (layer,),
                                     True, tm, 512, name="s5_glu")
        else:
            bl = layer - N_A_LAYERS
            q = _norm_matmul_call(xf, mod_rows(base + 3), mod_rows(base + 4), norm_g_row(norm_g[layer, 1]),
                                  attn_w_q, (bl,), BF16, tm, 1024)
            if sample:
                qh = q.reshape(bn, seq, N_DIL, KV_HEADS, REP, HEAD_DIM).transpose(0, 2, 3, 4, 1, 5)
                qh = qh.reshape(bn, N_DIL * KV_HEADS, REP * seq, HEAD_DIM)
                o = _attn_sample_call(qh, kv.reshape(N_DIL, bn, seq * kv_rows, HEAD_DIM), kv_caches, seq)
                o = o.reshape(bn, KV_HEADS, REP, seq, HEAD_DIM).transpose(0, 3, 1, 2, 4).reshape(m, D_MODEL)
            else:
                o = _attn_prompt_call(q.reshape(bn, seq, N_DIL * HEADS * HEAD_DIM),
                                      kv_flat.reshape(N_DIL, bn, seq, kv_rows * HEAD_DIM)).reshape(m, D_MODEL)
            xf = _proj_residual_call(o, xf, mod_rows(base + 5), attn_w_o, (bl,), False, tm, 512, name="attn_out")
        xf = _ffn_call(xf, mod_rows(base + 6), mod_rows(base + 7), mod_rows(base + 8), norm_g_row(norm_g[layer, 2]),
                       ffn_w_in, ffn_w_out, layer, 1, tm, tiles["tf"], tiles["tm_down"], tiles["tn_down"])
    y = _final_norm_call(xf, mod_rows(9 * DEPTH + 2), mod_rows(9 * DEPTH + 3), norm_g_row(final_norm_g), tm)
    kv = kv.reshape(N_DIL, bn, seq, 2, KV_HEADS, HEAD_DIM)
    return y.reshape(bn, seq, D_MODEL), jnp.stack(new_re, axis=0), jnp.stack(new_im, axis=0), kv


def kernel(x_prompt, x_sample, c_prompt, c_sample, state_s5_re, state_s5_im, cache_kv_g0, cache_kv_g1, cache_kv_g2, w_mod, b_mod, norm_g, ffn_w_in, ffn_w_out, s5_lambda_re, s5_lambda_im, s5_log_dt, s5_b_re, s5_b_im, s5_c_re, s5_c_im, s5_d, s5_w_glu, kv_norm_g, w_kv, attn_w_q, attn_w_o, final_norm_g):
    bp, seq_p, _ = x_prompt.shape
    bs, seq_s, _ = x_sample.shape
    caches = (cache_kv_g0, cache_kv_g1, cache_kv_g2)
    for (w, d), c in zip(DILATED_PATTERNS, caches):
        assert c.shape[1] == w and w == d * Q_BLOCK and seq_p % (d * Q_BLOCK) == 0
    assert bp <= SUBLANES and bs <= SUBLANES

    n_c = bp + bs
    c_all = jnp.pad(jnp.concatenate([c_prompt, c_sample], axis=0), ((0, (-n_c) % SUBLANES), (0, 0)))
    mod_all = _mod_call(c_all, w_mod, b_mod)
    mod_p = mod_all[:bp].reshape(bp, N_MOD, D_MODEL)
    mod_s = mod_all[bp:n_c].reshape(bs, N_MOD, D_MODEL)

    s5_mats = [_s5_weights(s5_lambda_re[l], s5_lambda_im[l], s5_log_dt[l], s5_b_re[l], s5_b_im[l],
                           s5_c_re[l], s5_c_im[l]) for l in range(N_A_LAYERS)]
    wts = (norm_g, ffn_w_in, ffn_w_out, s5_mats, s5_d, s5_w_glu, kv_norm_g, w_kv, attn_w_q, attn_w_o, final_norm_g)

    y_p, re_p, im_p, kv_p = _trunk(x_prompt, mod_p, None, None, wts,
                                   {"tm": 1024, "tf": 512, "tm_down": 1024, "tn_down": 256, "s5_steps": 32,
                                    "s5_block_scratch": True})
    caches_flat = tuple(c.reshape(bs, c.shape[1] * 2 * KV_HEADS, HEAD_DIM) for c in caches)
    y_s, re_s, im_s, kv_s = _trunk(x_sample, mod_s, (state_s5_re, state_s5_im), caches_flat, wts,
                                   {"tm": bs * seq_s, "tf": 512, "tm_down": bs * seq_s, "tn_down": 256,
                                    "s5_steps": seq_s, "s5_block_scratch": False})

    kvp = [kv_p[g][:, seq_p - min(w, seq_p):] for g, (w, _) in enumerate(DILATED_PATTERNS)]
    return (y_p, y_s, re_p, im_p, kvp[0], kvp[1], kvp[2], re_s, im_s, kv_s[0], kv_s[1], kv_s[2])
```

```python
import functools

import jax
import jax.numpy as jnp
from jax import lax
from jax.experimental import pallas as pl
from jax.experimental.pallas import tpu as pltpu

F32 = jnp.float32
BF16 = jnp.bfloat16

D_MODEL = 2048
DEPTH = 4
N_A_LAYERS = DEPTH // 2
D_FF = 5632
S5_GROUP = 16
S5_GROUPS = D_MODEL // S5_GROUP
S5_STATE = 64
S5_NSTATE = S5_GROUPS * S5_STATE
HEAD_DIM = 128
HEADS = 16
KV_HEADS = 4
REP = HEADS // KV_HEADS
DILATED_PATTERNS = ((128, 1), (512, 4), (2048, 16))
N_DIL = len(DILATED_PATTERNS)
Q_BLOCK = 128
N_MOD = 9 * DEPTH + 4
EPS = 1e-6
ATTN_SCALE = HEAD_DIM ** -0.5

VMEM_LIMIT_BYTES = 56 * 1024 * 1024
SUBLANES = 8
S5_COL_BLOCK = 1024
S5_CH_BLOCK = S5_COL_BLOCK // S5_STATE * S5_GROUP
S5_N_BLOCKS = S5_NSTATE // S5_COL_BLOCK


def _params(*sem):
    return pltpu.CompilerParams(dimension_semantics=sem, vmem_limit_bytes=VMEM_LIMIT_BYTES)


def _rms_mod(x, g, shift, scale):
    xn = x * lax.rsqrt(jnp.mean(x * x, axis=-1, keepdims=True) + EPS)
    return xn * g * (1.0 + scale) + shift


def _norm_rows(x_ref, g_ref, sh_ref, sc_ref, h_ref, row_chunk):
    rows = x_ref.shape[0]
    g = g_ref[...]
    mod_rows = sh_ref.shape[0]

    def body(c, carry):
        r0 = pl.multiple_of(c * row_chunk, row_chunk)
        x = x_ref[pl.ds(r0, row_chunk), :]
        if mod_rows == rows and rows != row_chunk:
            sh = sh_ref[pl.ds(r0, row_chunk), :]
            sc = sc_ref[pl.ds(r0, row_chunk), :]
        else:
            sh = sh_ref[...]
            sc = sc_ref[...]
        h_ref[pl.ds(r0, row_chunk), :] = _rms_mod(x, g, sh, sc).astype(h_ref.dtype)
        return carry

    lax.fori_loop(0, rows // row_chunk, body, 0)


def _mod_kernel(c_ref, w_ref, b_ref, o_ref):
    c = c_ref[...]
    a = (c * jax.nn.sigmoid(c)).astype(BF16)
    o_ref[...] = jnp.dot(a, w_ref[...].astype(BF16), preferred_element_type=F32) + b_ref[...]


def _mod_call(c_all, w_mod, b_mod):
    rows = c_all.shape[0]
    n = w_mod.shape[1]
    tn = 1024
    return pl.pallas_call(
        _mod_kernel,
        grid=(n // tn,),
        in_specs=[
            pl.BlockSpec((rows, D_MODEL), lambda j: (0, 0)),
            pl.BlockSpec((D_MODEL, tn), lambda j: (0, j)),
            pl.BlockSpec((1, tn), lambda j: (0, j)),
        ],
        out_specs=pl.BlockSpec((rows, tn), lambda j: (0, j)),
        out_shape=jax.ShapeDtypeStruct((rows, n), F32),
        compiler_params=_params("arbitrary"),
        name="adaln_mod",
    )(c_all, w_mod, b_mod.reshape(1, n))


def _norm_two_row_sets(x_ref, sh_ref, sc_ref, x2_ref, sh2_ref, sc2_ref, g_ref, h_ref, row_chunk):
    tm, r2 = x_ref.shape[0], x2_ref.shape[0]

    @pl.when(pl.program_id(1) == 0)
    def _():
        _norm_rows(x_ref, g_ref, sh_ref, sc_ref, h_ref.at[pl.ds(0, tm)], row_chunk)

    @pl.when((pl.program_id(0) == 0) & (pl.program_id(1) == 0))
    def _():
        _norm_rows(x2_ref, g_ref, sh2_ref, sc2_ref, h_ref.at[pl.ds(tm, r2)], r2)


def _ffn_up_kernel(x_ref, sh_ref, sc_ref, x2_ref, sh2_ref, sc2_ref, g_ref, wg_ref, wu_ref, a_ref, a2_ref, h_ref,
                   *, row_chunk):
    tm = x_ref.shape[0]
    _norm_two_row_sets(x_ref, sh_ref, sc_ref, x2_ref, sh2_ref, sc2_ref, g_ref, h_ref, row_chunk)
    wg = wg_ref[...].astype(BF16)
    wu = wu_ref[...].astype(BF16)

    def act(h):
        g = jnp.dot(h, wg, preferred_element_type=F32)
        u = jnp.dot(h, wu, preferred_element_type=F32)
        return (g * jax.nn.sigmoid(g) * u).astype(a_ref.dtype)

    @pl.when(pl.program_id(0) == 0)
    def _():
        a = act(h_ref[...])
        a_ref[...] = a[:tm]
        a2_ref[...] = a[tm:]

    @pl.when(pl.program_id(0) != 0)
    def _():
        a_ref[...] = act(h_ref[pl.ds(0, tm), :])


def _ffn_call(x, mods, x2, mods2, g_norm, w_in, w_out, layer, which, tm, tf, tn):
    m, r2 = x.shape[0], x2.shape[0]
    nb = mods[0].shape[0]
    tiles_per_seq = (m // nb) // tm
    nk = D_FF // tf
    mod_spec = pl.BlockSpec((None, 1, D_MODEL), lambda i, k: (i // tiles_per_seq, 0, 0))
    full2 = pl.BlockSpec((r2, D_MODEL), lambda i, k: (0, 0))
    a, a2 = pl.pallas_call(
        functools.partial(_ffn_up_kernel, row_chunk=min(tm, 64)),
        grid=(m // tm, nk),
        in_specs=[
            pl.BlockSpec((tm, D_MODEL), lambda i, k: (i, 0)),
            mod_spec, mod_spec,
            full2, full2, full2,
            pl.BlockSpec((1, D_MODEL), lambda i, k: (0, 0)),
            pl.BlockSpec((None, None, D_MODEL, tf), lambda i, k: (layer, which, 0, k)),
            pl.BlockSpec((None, None, D_MODEL, tf), lambda i, k: (layer, which, 0, k + nk)),
        ],
        out_specs=(pl.BlockSpec((tm, tf), lambda i, k: (i, k)),
                   pl.BlockSpec((r2, tf), lambda i, k: (0, jnp.where(i == 0, k, nk - 1)))),
        out_shape=(jax.ShapeDtypeStruct((m, D_FF), BF16), jax.ShapeDtypeStruct((r2, D_FF), BF16)),
        scratch_shapes=[pltpu.VMEM((tm + r2, D_MODEL), BF16)],
        compiler_params=_params("arbitrary", "arbitrary"),
        name="ffn_up",
    )(x, mods[0], mods[1], x2, mods2[0], mods2[1], g_norm, w_in, w_in)
    out = _proj_residual_call(a, x, mods[2], w_out, (layer, which), False, tm, tn, gate_scale=0.5, name="ffn_down")
    out2 = _proj_residual_call(a2, x2, mods2[2].reshape(1, r2, D_MODEL), w_out, (layer, which), False, r2, tn,
                               gate_scale=0.5, name="ffn_down")
    return out, out2


def _norm_matmul_kernel(x_ref, sh_ref, sc_ref, x2_ref, sh2_ref, sc2_ref, g_ref, w_ref, *rest, row_chunk, head_rows):
    h_ref = rest[-1]
    outs = rest[:-1]
    per_set = len(outs) // 2
    tm = x_ref.shape[0]
    _norm_two_row_sets(x_ref, sh_ref, sc_ref, x2_ref, sh2_ref, sc2_ref, g_ref, h_ref, row_chunk)
    w = w_ref[...].astype(BF16)

    def emit(res, refs):
        refs[0][...] = res.astype(refs[0].dtype)
        if head_rows:
            rows, cols = res.shape
            heads_per_tile = cols // HEAD_DIM
            first = (pl.program_id(1) % (head_rows // heads_per_tile)) * heads_per_tile
            for c in range(heads_per_tile):
                refs[1][pl.ds(first + c, rows, stride=head_rows), :] = res[:, c * HEAD_DIM:(c + 1) * HEAD_DIM]

    @pl.when(pl.program_id(0) == 0)
    def _():
        res = jnp.dot(h_ref[...], w, preferred_element_type=F32)
        emit(res[:tm], outs[:per_set])
        emit(res[tm:], outs[per_set:])

    @pl.when(pl.program_id(0) != 0)
    def _():
        emit(jnp.dot(h_ref[pl.ds(0, tm), :], w, preferred_element_type=F32), outs[:per_set])


def _norm_matmul_call(x, shift, scale, x2, shift2, scale2, g_norm, w, w_lead, out_dtype, tm, tn, head_rows=0):
    m, r2 = x.shape[0], x2.shape[0]
    nb = shift.shape[0]
    n = w.shape[-1]
    nj = n // tn
    tiles_per_seq = (m // nb) // tm
    mod_spec = pl.BlockSpec((None, 1, D_MODEL), lambda i, j: (i // tiles_per_seq, 0, 0))
    full2 = pl.BlockSpec((r2, D_MODEL), lambda i, j: (0, 0))

    def col2(i, j):
        return jnp.where(i == 0, j, nj - 1)

    if head_rows:
        slab = head_rows * HEAD_DIM
        tps = slab // tn
        out_spec = (pl.BlockSpec((None, tm, tn), lambda i, j: (j // tps, i, j % tps)),
                    pl.BlockSpec((None, tm * head_rows, HEAD_DIM), lambda i, j: (j // tps, i, 0)),
                    pl.BlockSpec((None, r2, tn), lambda i, j: (col2(i, j) // tps, 0, col2(i, j) % tps)),
                    pl.BlockSpec((None, r2 * head_rows, HEAD_DIM), lambda i, j: (col2(i, j) // tps, 0, 0)))
        out_shape = (jax.ShapeDtypeStruct((n // slab, m, slab), out_dtype),
                     jax.ShapeDtypeStruct((n // slab, m * head_rows, HEAD_DIM), out_dtype),
                     jax.ShapeDtypeStruct((n // slab, r2, slab), out_dtype),
                     jax.ShapeDtypeStruct((n // slab, r2 * head_rows, HEAD_DIM), out_dtype))
    else:
        out_spec = (pl.BlockSpec((tm, tn), lambda i, j: (i, j)),
                    pl.BlockSpec((r2, tn), lambda i, j: (0, col2(i, j))))
        out_shape = (jax.ShapeDtypeStruct((m, n), out_dtype), jax.ShapeDtypeStruct((r2, n), out_dtype))
    return pl.pallas_call(
        functools.partial(_norm_matmul_kernel, row_chunk=min(tm, 64), head_rows=head_rows),
        grid=(m // tm, nj),
        in_specs=[
            pl.BlockSpec((tm, D_MODEL), lambda i, j: (i, 0)),
            mod_spec, mod_spec,
            full2, full2, full2,
            pl.BlockSpec((1, D_MODEL), lambda i, j: (0, 0)),
            pl.BlockSpec((None,) * len(w_lead) + (D_MODEL, tn), lambda i, j: (*w_lead, 0, j)),
        ],
        out_specs=out_spec,
        out_shape=out_shape,
        scratch_shapes=[pltpu.VMEM((tm + r2, D_MODEL), BF16)],
        compiler_params=_params("arbitrary", "arbitrary"),
        name="norm_matmul",
    )(x, shift, scale, x2, shift2, scale2, g_norm, w)


def _proj_residual_kernel(z_ref, x_ref, gt_ref, *refs, glu, gate_scale):
    z = z_ref[...]
    if glu:
        wa_ref, wg_ref, o_ref = refs
        a = jnp.dot(z, wa_ref[...].astype(BF16), preferred_element_type=F32)
        g = jnp.dot(z, wg_ref[...].astype(BF16), preferred_element_type=F32)
        y = a * jax.nn.sigmoid(g)
    else:
        w_ref, o_ref = refs
        y = jnp.dot(z, w_ref[...].astype(BF16), preferred_element_type=F32)
    gate = gt_ref[...] if gate_scale == 1.0 else gate_scale * gt_ref[...]
    o_ref[...] = x_ref[...] + gate * y


def _proj_residual_call(z, x, gate, w, w_lead, glu, tm, tn, gate_scale=1.0, name="proj_residual", z_buffers=2):
    m = x.shape[0]
    kdim = z.shape[1]
    nb, r, _ = gate.shape
    tiles_per_seq = (m // nb) // tm if r == 1 else 1
    n_blocks = D_MODEL // tn
    w_block = (None,) * len(w_lead) + (kdim, tn)
    w_specs = [pl.BlockSpec(w_block, lambda i, j: (*w_lead, 0, j))]
    w_args = [w]
    if glu:
        w_specs.append(pl.BlockSpec(w_block, lambda i, j: (*w_lead, 0, j + n_blocks)))
        w_args.append(w)
    return pl.pallas_call(
        functools.partial(_proj_residual_kernel, glu=glu, gate_scale=gate_scale),
        grid=(m // tm, n_blocks),
        in_specs=[
            (pl.BlockSpec((tm, kdim), lambda i, j: (i, 0)) if z_buffers == 2 else
             pl.BlockSpec((tm, kdim), lambda i, j: (i, 0), pipeline_mode=pl.Buffered(z_buffers))),
            pl.BlockSpec((tm, tn), lambda i, j: (i, j)),
            pl.BlockSpec((None, r, tn), lambda i, j: (i // tiles_per_seq, 0, j)),
        ] + w_specs,
        out_specs=pl.BlockSpec((tm, tn), lambda i, j: (i, j)),
        out_shape=jax.ShapeDtypeStruct((m, D_MODEL), F32),
        compiler_params=_params("parallel", "arbitrary"),
        name=name,
    )(z, x, gate, *w_args)


def _final_norm_kernel(x_ref, sh_ref, sc_ref, g_ref, o_ref, *, row_chunk):
    _norm_rows(x_ref, g_ref, sh_ref, sc_ref, o_ref, row_chunk)


def _final_norm_call(x, shift, scale, g_norm, tm):
    m = x.shape[0]
    nb, r, _ = shift.shape
    tiles_per_seq = (m // nb) // tm if r == 1 else 1
    mod_spec = pl.BlockSpec((None, r, D_MODEL), lambda i: (i // tiles_per_seq, 0, 0))
    return pl.pallas_call(
        functools.partial(_final_norm_kernel, row_chunk=min(tm, 64)),
        grid=(m // tm,),
        in_specs=[
            pl.BlockSpec((tm, D_MODEL), lambda i: (i, 0)),
            mod_spec, mod_spec,
            pl.BlockSpec((1, D_MODEL), lambda i: (0, 0)),
        ],
        out_specs=pl.BlockSpec((tm, D_MODEL), lambda i: (i, 0)),
        out_shape=jax.ShapeDtypeStruct((m, D_MODEL), F32),
        compiler_params=_params("parallel"),
        name="final_norm",
    )(x, shift, scale, g_norm)


def _s5_disc_kernel(lre_ref, lim_ref, ldt_ref, bre_ref, bim_ref, are_ref, aim_ref, bbre_ref, bbim_ref):
    lam_re = lre_ref[...]
    lam_im = lim_ref[...]
    dt = jnp.exp(ldt_ref[...])
    decay = jnp.exp(lam_re * dt)
    ab_re = decay * jnp.cos(lam_im * dt)
    ab_im = decay * jnp.sin(lam_im * dt)
    den = lam_re * lam_re + lam_im * lam_im
    f_re = ((ab_re - 1.0) * lam_re + ab_im * lam_im) / den
    f_im = (ab_im * lam_re - (ab_re - 1.0) * lam_im) / den
    b_re = bre_ref[...]
    b_im = bim_ref[...]
    are_ref[...] = ab_re
    aim_ref[...] = ab_im
    bbre_ref[...] = f_re * b_re - f_im * b_im
    bbim_ref[...] = f_re * b_im + f_im * b_re


def _s5_disc_call(lam_re, lam_im, log_dt, b_re, b_im):
    shape = jax.ShapeDtypeStruct(lam_re.shape, F32)
    return pl.pallas_call(
        _s5_disc_kernel,
        out_shape=(shape, shape, shape, shape),
        name="s5_discretise",
    )(lam_re, lam_im, log_dt, b_re, b_im)


def _s5_kernel(x_ref, sh_ref, sc_ref, g_ref, bbre_ref, bbim_ref, are_ref, aim_ref, cre_ref, cim_ref,
               d_ref, h0re_ref, h0im_ref, z_ref, hre_out, him_out,
               u_scr, sre_scr, sim_scr, y_scr, pwre_scr, pwim_scr, *, row_chunk):
    t = pl.program_id(0)
    nb, steps, _ = x_ref.shape
    rows = steps * nb
    lane = HEAD_DIM
    seg = SUBLANES // nb
    seg_len = steps // seg
    assert seg in (1, 2)

    @pl.when(t == 0)
    def _():
        hre_out[...] = h0re_ref[...]
        him_out[...] = h0im_ref[...]
        if seg > 1:
            zero = jnp.zeros((nb, S5_COL_BLOCK), F32)
            for cb in range(S5_N_BLOCKS):
                a_re = jnp.broadcast_to(are_ref[cb], (nb, S5_COL_BLOCK))
                a_im = jnp.broadcast_to(aim_ref[cb], (nb, S5_COL_BLOCK))
                p_re, p_im = a_re, a_im
                for i in range(seg_len):
                    pwre_scr[cb, i * SUBLANES:(i + 1) * SUBLANES, :] = jnp.concatenate([zero, p_re], axis=0)
                    pwim_scr[cb, i * SUBLANES:(i + 1) * SUBLANES, :] = jnp.concatenate([zero, p_im], axis=0)
                    p_re, p_im = p_re * a_re - p_im * a_im, p_re * a_im + p_im * a_re

    g = g_ref[...]
    for b in range(nb):
        for s in range(seg):
            def norm_chunk(c, carry, b=b, s=s):
                r0 = pl.multiple_of(c * row_chunk, row_chunk)
                u = _rms_mod(x_ref[b, pl.ds(s * seg_len + r0, row_chunk), :], g, sh_ref[b], sc_ref[b])
                for j in range(D_MODEL // lane):
                    u_scr[j, pl.ds(r0 * SUBLANES + s * nb + b, row_chunk, stride=SUBLANES), :] = \
                        u[:, j * lane:(j + 1) * lane]
                return carry

            lax.fori_loop(0, seg_len // row_chunk, norm_chunk, 0)

    tiles_per_block = S5_CH_BLOCK // lane
    per_block_scratch = sre_scr.shape[0] == S5_N_BLOCKS
    for cb in range(S5_N_BLOCKS):
        ch = slice(cb * S5_CH_BLOCK, (cb + 1) * S5_CH_BLOCK)
        st = slice(cb * S5_COL_BLOCK, (cb + 1) * S5_COL_BLOCK)
        slab = cb if per_block_scratch else 0
        sre, sim, ys = sre_scr.at[slab], sim_scr.at[slab], y_scr.at[slab]
        u_blk = jnp.concatenate([u_scr[cb * tiles_per_block + j] for j in range(tiles_per_block)], axis=1)
        u_bf = u_blk.astype(BF16)
        sre[...] = jnp.dot(u_bf, bbre_ref[cb], preferred_element_type=F32)
        sim[...] = jnp.dot(u_bf, bbim_ref[cb], preferred_element_type=F32)

        a_re = jnp.broadcast_to(are_ref[cb], (SUBLANES, S5_COL_BLOCK))
        a_im = jnp.broadcast_to(aim_ref[cb], (SUBLANES, S5_COL_BLOCK))

        def tile_step(i, carry, a_re=a_re, a_im=a_im, sre=sre, sim=sim):
            h_re, h_im = carry
            r0 = i * SUBLANES if isinstance(i, int) else pl.multiple_of(i * SUBLANES, SUBLANES)
            n_re = a_re * h_re - a_im * h_im + sre[pl.ds(r0, SUBLANES), :]
            n_im = a_re * h_im + a_im * h_re + sim[pl.ds(r0, SUBLANES), :]
            sre[pl.ds(r0, SUBLANES), :] = n_re
            sim[pl.ds(r0, SUBLANES), :] = n_im
            return n_re, n_im

        def first_rows(v):
            return jnp.concatenate([v[:nb]] * seg, axis=0)

        h_re, h_im = hre_out[:, st], him_out[:, st]
        if seg > 1:
            zero = jnp.zeros((SUBLANES - nb, S5_COL_BLOCK), F32)
            h_re, h_im = jnp.concatenate([h_re, zero], axis=0), jnp.concatenate([h_im, zero], axis=0)
        carry = (h_re, h_im)
        if per_block_scratch:
            for i in range(seg_len):
                carry = tile_step(i, carry)
        else:
            carry = lax.fori_loop(0, seg_len, tile_step, carry, unroll=min(seg_len, 4))
        h_re, h_im = carry
        if seg > 1:
            c_re, c_im = first_rows(h_re), first_rows(h_im)
            for i in range(seg_len):
                rows_i = slice(i * SUBLANES, (i + 1) * SUBLANES)
                p_re, p_im = pwre_scr[cb, rows_i, :], pwim_scr[cb, rows_i, :]
                f_re = sre[rows_i, :] + (p_re * c_re - p_im * c_im)
                f_im = sim[rows_i, :] + (p_re * c_im + p_im * c_re)
                sre[rows_i, :] = f_re
                sim[rows_i, :] = f_im
            h_re, h_im = f_re[SUBLANES - nb:], f_im[SUBLANES - nb:]
        hre_out[:, st] = h_re
        him_out[:, st] = h_im

        y = (jnp.dot(sre[...].astype(BF16), cre_ref[cb], preferred_element_type=F32)
             - jnp.dot(sim[...].astype(BF16), cim_ref[cb], preferred_element_type=F32))
        zf = jax.nn.gelu(y + d_ref[:, ch] * u_blk)
        for j in range(tiles_per_block):
            ys[j] = zf[:, j * lane:(j + 1) * lane]
        for b in range(nb):
            for s in range(seg):
                for j in range(tiles_per_block):
                    c0 = cb * S5_CH_BLOCK + j * lane
                    z_ref[b, s * seg_len:(s + 1) * seg_len, c0:c0 + lane] = \
                        ys[j, pl.ds(s * nb + b, seg_len, stride=SUBLANES), :].astype(z_ref.dtype)


def _s5_call(x, shift, scale, g_norm, bb_re, bb_im, a_re, a_im, c_re, c_im, d_skip,
             h0_re, h0_im, steps_per_chunk, z_dtype, per_block_scratch):
    nb, seq, _ = x.shape
    assert SUBLANES % nb == 0 and seq % steps_per_chunk == 0
    tr = steps_per_chunk * nb
    slabs = S5_N_BLOCKS if per_block_scratch else 1
    seg = SUBLANES // nb
    pw_rows = tr if seg > 1 else SUBLANES
    row_chunk = min(steps_per_chunk // (SUBLANES // nb), 64)
    const2 = lambda t: (0, 0)
    const3 = lambda t: (0, 0, 0)
    state_spec = pl.BlockSpec((nb, S5_NSTATE), const2)
    io_spec = pl.BlockSpec((nb, steps_per_chunk, D_MODEL), lambda t: (0, t, 0))
    return pl.pallas_call(
        functools.partial(_s5_kernel, row_chunk=row_chunk),
        grid=(seq // steps_per_chunk,),
        in_specs=[
            io_spec,
            pl.BlockSpec((nb, 1, D_MODEL), const3),
            pl.BlockSpec((nb, 1, D_MODEL), const3),
            pl.BlockSpec((1, D_MODEL), const2),
            pl.BlockSpec((S5_N_BLOCKS, S5_CH_BLOCK, S5_COL_BLOCK), const3, pipeline_mode=pl.Buffered(1)),
            pl.BlockSpec((S5_N_BLOCKS, S5_CH_BLOCK, S5_COL_BLOCK), const3, pipeline_mode=pl.Buffered(1)),
            pl.BlockSpec((S5_N_BLOCKS, 1, S5_COL_BLOCK), const3),
            pl.BlockSpec((S5_N_BLOCKS, 1, S5_COL_BLOCK), const3),
            pl.BlockSpec((S5_N_BLOCKS, S5_COL_BLOCK, S5_CH_BLOCK), const3, pipeline_mode=pl.Buffered(1)),
            pl.BlockSpec((S5_N_BLOCKS, S5_COL_BLOCK, S5_CH_BLOCK), const3, pipeline_mode=pl.Buffered(1)),
            pl.BlockSpec((1, D_MODEL), const2),
            state_spec, state_spec,
        ],
        out_specs=(io_spec, state_spec, state_spec),
        out_shape=(jax.ShapeDtypeStruct((nb, seq, D_MODEL), z_dtype),
                   jax.ShapeDtypeStruct((nb, S5_NSTATE), F32),
                   jax.ShapeDtypeStruct((nb, S5_NSTATE), F32)),
        scratch_shapes=[pltpu.VMEM((D_MODEL // HEAD_DIM, tr, HEAD_DIM), F32),
                        pltpu.VMEM((slabs, tr, S5_COL_BLOCK), F32),
                        pltpu.VMEM((slabs, tr, S5_COL_BLOCK), F32),
                        pltpu.VMEM((slabs, S5_CH_BLOCK // HEAD_DIM, tr, HEAD_DIM), F32),
                        pltpu.VMEM((S5_N_BLOCKS, pw_rows, S5_COL_BLOCK), F32),
                        pltpu.VMEM((S5_N_BLOCKS, pw_rows, S5_COL_BLOCK), F32)],
        compiler_params=_params("arbitrary"),
        name="s5_mixer",
    )(x, shift, scale, g_norm, bb_re, bb_im, a_re, a_im, c_re, c_im, d_skip, h0_re, h0_im)


def _softmax_pv(s, v_bf):
    m = jnp.max(s, axis=-1, keepdims=True)
    p = jnp.exp(s - m)
    l = jnp.sum(p, axis=-1, keepdims=True)
    o = jnp.dot(p.astype(BF16), v_bf, preferred_element_type=F32) / l
    return o, jnp.broadcast_to(m + jnp.log(l), o.shape)


def _qk(q_bf, k_bf):
    return lax.dot_general(q_bf, k_bf, (((1,), (1,)), ((), ())), preferred_element_type=F32) * ATTN_SCALE


def _attn_prompt_kernel(q0_ref, q1_ref, q2_ref, k0_ref, v0_ref, k1_ref, v1_ref, k2_ref, v2_ref, o_ref,
                        qf_scr, o1_scr, l1_scr, o2_scr, l2_scr, band_scr, first_scr):
    seq = q0_ref.shape[0]
    qb = Q_BLOCK
    row = lax.broadcasted_iota(jnp.int32, (REP * qb, 2 * qb), 0) & (qb - 1)
    col = lax.broadcasted_iota(jnp.int32, (REP * qb, 2 * qb), 1)
    band_scr[...] = jnp.where((col >= row) & (col <= row + qb), 0.0, -jnp.inf)
    row1 = lax.broadcasted_iota(jnp.int32, (REP * qb, qb), 0) & (qb - 1)
    col1 = lax.broadcasted_iota(jnp.int32, (REP * qb, qb), 1)
    first_scr[...] = jnp.where(col1 <= row1, 0.0, -jnp.inf)

    def rows_of(start, d):
        return pl.ds(start, qb) if d == 1 else pl.ds(start, qb, stride=d)

    def load_q(src_ref, start, d):
        if d == 1:
            parts = [src_ref[pl.ds(start, qb), e * HEAD_DIM:(e + 1) * HEAD_DIM] for e in range(REP)]
        else:
            parts = [src_ref[e, rows_of(start, d), :] for e in range(REP)]
        return jnp.concatenate(parts, axis=0).astype(BF16)

    def unit(src_q, k_ref, v_ref, start, d, first):
        q4 = load_q(src_q, start, d)
        k_cur = k_ref[rows_of(start, d), :]
        v_cur = v_ref[rows_of(start, d), :]
        if first:
            s = _qk(q4, k_cur.astype(BF16)) + first_scr[...]
            return _softmax_pv(s, v_cur.astype(BF16))
        prev = start - qb * d
        k_band = jnp.concatenate([k_ref[rows_of(prev, d), :], k_cur], axis=0)
        v_band = jnp.concatenate([v_ref[rows_of(prev, d), :], v_cur], axis=0)
        s = _qk(q4, k_band.astype(BF16)) + band_scr[...]
        return _softmax_pv(s, v_band.astype(BF16))

    def stage_group(q_ref, k_ref, v_ref, d, o_scr, l_scr):
        n_blocks = seq // (qb * d)

        def copy(c, carry):
            r0 = pl.multiple_of(c * 256, 256)
            for e in range(REP):
                qf_scr[e, pl.ds(r0, 256), :] = q_ref[pl.ds(r0, 256), e * HEAD_DIM:(e + 1) * HEAD_DIM].astype(F32)
            return carry

        lax.fori_loop(0, seq // 256, copy, 0)

        def store(start, o, lse):
            for e in range(REP):
                sl = slice(e * qb, (e + 1) * qb)
                o_scr[e, rows_of(start, d), :] = o[sl]
                l_scr[e, rows_of(start, d), :] = lse[sl]

        def per_class(r, carry):
            store(r, *unit(qf_scr, k_ref, v_ref, r, d, True))

            def per_block(jb, c2):
                start = jb * (qb * d) + r
                store(start, *unit(qf_scr, k_ref, v_ref, start, d, False))
                return c2

            if n_blocks > 1:
                lax.fori_loop(1, n_blocks, per_block, 0, unroll=True)
            return carry

        lax.fori_loop(0, d, per_class, 0, unroll=2 if n_blocks == 1 else 1)

    stage_group(q2_ref, k2_ref, v2_ref, DILATED_PATTERNS[2][1], o2_scr, l2_scr)
    stage_group(q1_ref, k1_ref, v1_ref, DILATED_PATTERNS[1][1], o1_scr, l1_scr)

    def merge(start, o0, lse0):
        for e in range(REP):
            sl = slice(e * qb, (e + 1) * qb)
            cols = slice(e * HEAD_DIM, (e + 1) * HEAD_DIM)
            la = lse0[sl]
            lb = l1_scr[e, pl.ds(start, qb), :]
            lc = l2_scr[e, pl.ds(start, qb), :]
            mx = jnp.maximum(jnp.maximum(la, lb), lc)
            wa = jnp.exp(la - mx)
            wb = jnp.exp(lb - mx)
            wc = jnp.exp(lc - mx)
            inv = 1.0 / (wa + wb + wc)
            acc = (wa * inv) * o0[sl] + (wb * inv) * o1_scr[e, pl.ds(start, qb), :] \
                + (wc * inv) * o2_scr[e, pl.ds(start, qb), :]
            o_ref[pl.ds(start, qb), cols] = acc.astype(o_ref.dtype)

    merge(0, *unit(q0_ref, k0_ref, v0_ref, 0, 1, True))

    def per_block0(jb, carry):
        start = pl.multiple_of(jb * qb, qb)
        merge(start, *unit(q0_ref, k0_ref, v0_ref, start, 1, False))
        return carry

    n_blocks0 = seq // qb
    lax.fori_loop(1, n_blocks0, per_block0, 0, unroll=3 if (n_blocks0 - 1) % 3 == 0 else 1)


def _attn_prompt_call(q, kv):
    b, seq, _ = q.shape
    qw = REP * HEAD_DIM
    q_specs = [pl.BlockSpec((None, seq, qw), functools.partial(lambda bi, h, g: (bi, 0, g * KV_HEADS + h), g=g))
               for g in range(N_DIL)]
    kv_specs = []
    for g in range(N_DIL):
        kv_specs.append(pl.BlockSpec((None, None, seq, HEAD_DIM),
                                     functools.partial(lambda bi, h, g: (g, bi, 0, h), g=g)))
        kv_specs.append(pl.BlockSpec((None, None, seq, HEAD_DIM),
                                     functools.partial(lambda bi, h, g: (g, bi, 0, KV_HEADS + h), g=g)))
    kv_args = [kv] * (2 * N_DIL)
    return pl.pallas_call(
        _attn_prompt_kernel,
        grid=(b, KV_HEADS),
        in_specs=q_specs + kv_specs,
        out_specs=pl.BlockSpec((None, seq, qw), lambda bi, h: (bi, 0, h)),
        out_shape=jax.ShapeDtypeStruct((b, seq, HEADS * HEAD_DIM), BF16),
        scratch_shapes=[pltpu.VMEM((REP, seq, HEAD_DIM), F32),
                        pltpu.VMEM((REP, seq, HEAD_DIM), F32), pltpu.VMEM((REP, seq, HEAD_DIM), F32),
                        pltpu.VMEM((REP, seq, HEAD_DIM), F32), pltpu.VMEM((REP, seq, HEAD_DIM), F32),
                        pltpu.VMEM((REP * Q_BLOCK, 2 * Q_BLOCK), F32), pltpu.VMEM((REP * Q_BLOCK, Q_BLOCK), F32)],
        compiler_params=_params("parallel", "arbitrary"),
        name="dilated_attention_prompt",
    )(q, q, q, *kv_args)


def _attn_sample_kernel(q_ref, kvn_ref, c0_ref, c1_ref, c2_ref, o_ref, *, n_tok):
    rows = REP * n_tok
    cache_refs = (c0_ref, c1_ref, c2_ref)
    kv_rows = 2 * KV_HEADS
    for h in range(KV_HEADS):
        outs, lses = [], []
        for g, (_, d) in enumerate(DILATED_PATTERNS):
            q = q_ref[g * KV_HEADS + h]
            lw = cache_refs[g].shape[0] // kv_rows
            k_c = cache_refs[g][pl.ds(h, lw, stride=kv_rows), :].astype(BF16)
            v_c = cache_refs[g][pl.ds(KV_HEADS + h, lw, stride=kv_rows), :].astype(BF16)
            k_n = kvn_ref[g, pl.ds(h, n_tok, stride=kv_rows), :].astype(BF16)
            v_n = kvn_ref[g, pl.ds(KV_HEADS + h, n_tok, stride=kv_rows), :].astype(BF16)
            tq_c = lax.broadcasted_iota(jnp.int32, (rows, lw), 0) & (n_tok - 1)
            idx_c = lax.broadcasted_iota(jnp.int32, (rows, lw), 1)
            ok_c = (idx_c >= tq_c) & (((idx_c - tq_c) & (d - 1)) == 0)
            tq_n = lax.broadcasted_iota(jnp.int32, (rows, n_tok), 0) & (n_tok - 1)
            idx_n = lax.broadcasted_iota(jnp.int32, (rows, n_tok), 1)
            ok_n = (idx_n <= tq_n) & (((tq_n - idx_n) & (d - 1)) == 0)
            s_c = jnp.where(ok_c, _qk(q, k_c), -jnp.inf)
            s_n = jnp.where(ok_n, _qk(q, k_n), -jnp.inf)
            m = jnp.maximum(jnp.max(s_c, axis=-1, keepdims=True), jnp.max(s_n, axis=-1, keepdims=True))
            p_c = jnp.exp(s_c - m)
            p_n = jnp.exp(s_n - m)
            l = jnp.sum(p_c, axis=-1, keepdims=True) + jnp.sum(p_n, axis=-1, keepdims=True)
            o = (jnp.dot(p_c.astype(BF16), v_c, preferred_element_type=F32)
                 + jnp.dot(p_n.astype(BF16), v_n, preferred_element_type=F32)) / l
            outs.append(o)
            lses.append(m + jnp.log(l))
        mx = jnp.maximum(jnp.maximum(lses[0], lses[1]), lses[2])
        w = [jnp.exp(x - mx) for x in lses]
        inv = 1.0 / (w[0] + w[1] + w[2])
        acc = (w[0] * inv) * outs[0] + (w[1] * inv) * outs[1] + (w[2] * inv) * outs[2]
        o_ref[h] = acc.astype(o_ref.dtype)


def _attn_sample_call(q_heads, kv_new, caches, n_tok):
    b = q_heads.shape[0]
    rows = REP * n_tok
    assert n_tok & (n_tok - 1) == 0
    in_specs = [pl.BlockSpec((None, N_DIL * KV_HEADS, rows, HEAD_DIM), lambda bi: (bi, 0, 0, 0)),
                pl.BlockSpec((N_DIL, None, kv_new.shape[2], HEAD_DIM), lambda bi: (0, bi, 0, 0))]
    in_specs += [pl.BlockSpec((None, c.shape[1], HEAD_DIM), lambda bi: (bi, 0, 0)) for c in caches]
    return pl.pallas_call(
        functools.partial(_attn_sample_kernel, n_tok=n_tok),
        grid=(b,),
        in_specs=in_specs,
        out_specs=pl.BlockSpec((None, KV_HEADS, rows, HEAD_DIM), lambda bi: (bi, 0, 0, 0)),
        out_shape=jax.ShapeDtypeStruct((b, KV_HEADS, rows, HEAD_DIM), BF16),
        compiler_params=_params("parallel"),
        name="dilated_attention_sample",
    )(q_heads, kv_new, *caches)


def _s5_weights(lam_re, lam_im, log_dt, b_re, b_im, c_re, c_im):
    g, p, c = S5_GROUPS, S5_STATE, S5_GROUP
    rep = lambda a: jnp.repeat(a, c, axis=1)
    a_re, a_im, bb_re, bb_im = _s5_disc_call(rep(lam_re), rep(lam_im), log_dt.reshape(g, 1),
                                             b_re.reshape(g, p * c), b_im.reshape(g, p * c))
    gl = S5_CH_BLOCK // c
    eye = jnp.eye(gl, dtype=F32)

    def in_blocks(bb):
        t = bb.reshape(S5_N_BLOCKS, gl, p, c).transpose(0, 1, 3, 2)
        return jnp.einsum('bgcp,gh->bgchp', t, eye).reshape(S5_N_BLOCKS, gl * c, gl * p).astype(BF16)

    def out_blocks(cm):
        t = cm.reshape(S5_N_BLOCKS, gl, c, p).transpose(0, 1, 3, 2)
        return jnp.einsum('bgpc,gh->bgphc', t, eye).reshape(S5_N_BLOCKS, gl * p, gl * c).astype(BF16)

    lam_bar = lambda a: a.reshape(g, p, c)[:, :, 0].reshape(S5_N_BLOCKS, 1, S5_COL_BLOCK)
    return (in_blocks(bb_re), in_blocks(bb_im), lam_bar(a_re), lam_bar(a_im),
            out_blocks(c_re), out_blocks(c_im))


def _trunks(x_p, x_s, mod_p, mod_s, s5_state, kv_caches, wts, tiles):
    (norm_g, ffn_w_in, ffn_w_out, s5_mats, s5_d, s5_w_glu, kv_norm_g, w_kv,
     attn_w_q, attn_w_o, final_norm_g) = wts
    bp, lp, _ = x_p.shape
    bs, ls, _ = x_s.shape
    mp, ms = bp * lp, bs * ls
    tm = tiles["tm"]
    kv_rows = 2 * KV_HEADS

    def mods_p(*ks):
        return tuple(mod_p[:, k].reshape(bp, 1, D_MODEL) for k in ks)

    def mods_s(*ks):
        return tuple(jnp.repeat(mod_s[:, k], ls, axis=0) for k in ks)

    def as_block(a):
        return a.reshape(1, *a.shape)

    def norm_g_row(a):
        return a.reshape(1, D_MODEL)

    xp = x_p.reshape(mp, D_MODEL)
    xs = x_s.reshape(ms, D_MODEL)
    re_p, im_p, re_s, im_s = [], [], [], []
    for layer in range(DEPTH):
        base = 9 * layer
        if layer == N_A_LAYERS:
            k0 = 9 * DEPTH
            kvf_p, kv_p, _, kv_s = _norm_matmul_call(xp, *mods_p(k0, k0 + 1), xs, *mods_s(k0, k0 + 1),
                                                     norm_g_row(kv_norm_g), w_kv, (), F32, tm,
                                                     KV_HEADS * HEAD_DIM, head_rows=kv_rows)
        xp, xs = _ffn_call(xp, mods_p(base, base + 1, base + 2), xs, mods_s(base, base + 1, base + 2),
                           norm_g_row(norm_g[layer, 0]), ffn_w_in, ffn_w_out, layer, 0, tm, tiles["tf"],
                           tiles["tn_down"])
        g_mix = norm_g_row(norm_g[layer, 1])
        if layer < N_A_LAYERS:
            d_skip = s5_d[layer].reshape(1, D_MODEL)
            zero_state = jnp.zeros((bp, S5_NSTATE), F32)
            zp, h_re, h_im = _s5_call(xp.reshape(bp, lp, D_MODEL), *mods_p(base + 3, base + 4), g_mix,
                                      *s5_mats[layer], d_skip, zero_state, zero_state,
                                      min(lp, tiles["s5_steps"]), BF16, True)
            re_p.append(h_re.reshape(bp, S5_GROUPS, S5_STATE))
            im_p.append(h_im.reshape(bp, S5_GROUPS, S5_STATE))
            zs, h_re, h_im = _s5_call(xs.reshape(bs, ls, D_MODEL), mod_s[:, base + 3].reshape(bs, 1, D_MODEL),
                                      mod_s[:, base + 4].reshape(bs, 1, D_MODEL), g_mix, *s5_mats[layer], d_skip,
                                      s5_state[0][layer].reshape(bs, S5_NSTATE),
                                      s5_state[1][layer].reshape(bs, S5_NSTATE), ls, F32, False)
            re_s.append(h_re.reshape(bs, S5_GROUPS, S5_STATE))
            im_s.append(h_im.reshape(bs, S5_GROUPS, S5_STATE))
            xp = _proj_residual_call(zp.reshape(mp, D_MODEL), xp, *mods_p(base + 5), s5_w_glu, (layer,),
                                     True, tm, 512, name="s5_glu")
            xs = _proj_residual_call(zs.reshape(ms, D_MODEL).astype(BF16), xs, as_block(*mods_s(base + 5)),
                                     s5_w_glu, (layer,), True, ms, 512, name="s5_glu")
        else:
            bl = layer - N_A_LAYERS
            q_p, q_s = _norm_matmul_call(xp, *mods_p(base + 3, base + 4), xs, *mods_s(base + 3, base + 4), g_mix,
                                         attn_w_q, (bl,), BF16, tm, 1024)
            o_p = _attn_prompt_call(q_p.reshape(bp, lp, N_DIL * HEADS * HEAD_DIM),
                                    kvf_p.reshape(N_DIL, bp, lp, kv_rows * HEAD_DIM)).reshape(mp, D_MODEL)
            qh = q_s.reshape(bs, ls, N_DIL, KV_HEADS, REP, HEAD_DIM).transpose(0, 2, 3, 4, 1, 5)
            qh = qh.reshape(bs, N_DIL * KV_HEADS, REP * ls, HEAD_DIM)
            o_s = _attn_sample_call(qh, kv_s.reshape(N_DIL, bs, ls * kv_rows, HEAD_DIM), kv_caches, ls)
            o_s = o_s.reshape(bs, KV_HEADS, REP, ls, HEAD_DIM).transpose(0, 3, 1, 2, 4).reshape(ms, D_MODEL)
            xp = _proj_residual_call(o_p, xp, *mods_p(base + 5), attn_w_o, (bl,), False, tm, 512, name="attn_out")
            xs = _proj_residual_call(o_s, xs, as_block(*mods_s(base + 5)), attn_w_o, (bl,), False, ms, 512,
                                     name="attn_out")
        xp, xs = _ffn_call(xp, mods_p(base + 6, base + 7, base + 8), xs, mods_s(base + 6, base + 7, base + 8),
                           norm_g_row(norm_g[layer, 2]), ffn_w_in, ffn_w_out, layer, 1, tm, tiles["tf"],
                           tiles["tn_down"])
    k1 = 9 * DEPTH + 2
    y_p = _final_norm_call(xp, *mods_p(k1, k1 + 1), norm_g_row(final_norm_g), tm)
    y_s = _final_norm_call(xs, *(as_block(a) for a in mods_s(k1, k1 + 1)), norm_g_row(final_norm_g), ms)
    kv_p = kv_p.reshape(N_DIL, bp, lp, 2, KV_HEADS, HEAD_DIM)
    kv_s = kv_s.reshape(N_DIL, bs, ls, 2, KV_HEADS, HEAD_DIM)
    return ((y_p.reshape(bp, lp, D_MODEL), jnp.stack(re_p, axis=0), jnp.stack(im_p, axis=0), kv_p),
            (y_s.reshape(bs, ls, D_MODEL), jnp.stack(re_s, axis=0), jnp.stack(im_s, axis=0), kv_s))


def kernel(x_prompt, x_sample, c_prompt, c_sample, state_s5_re, state_s5_im, cache_kv_g0, cache_kv_g1, cache_kv_g2, w_mod, b_mod, norm_g, ffn_w_in, ffn_w_out, s5_lambda_re, s5_lambda_im, s5_log_dt, s5_b_re, s5_b_im, s5_c_re, s5_c_im, s5_d, s5_w_glu, kv_norm_g, w_kv, attn_w_q, attn_w_o, final_norm_g):
    bp, seq_p, _ = x_prompt.shape
    bs, seq_s, _ = x_sample.shape
    caches = (cache_kv_g0, cache_kv_g1, cache_kv_g2)
    for (w, d), c in zip(DILATED_PATTERNS, caches):
        assert c.shape[1] == w and w == d * Q_BLOCK and seq_p % (d * Q_BLOCK) == 0
    assert bp <= SUBLANES and bs <= SUBLANES

    n_c = bp + bs
    c_all = jnp.pad(jnp.concatenate([c_prompt, c_sample], axis=0), ((0, (-n_c) % SUBLANES), (0, 0)))
    mod_all = _mod_call(c_all, w_mod, b_mod)
    mod_p = mod_all[:bp].reshape(bp, N_MOD, D_MODEL)
    mod_s = mod_all[bp:n_c].reshape(bs, N_MOD, D_MODEL)

    s5_mats = [_s5_weights(s5_lambda_re[l], s5_lambda_im[l], s5_log_dt[l], s5_b_re[l], s5_b_im[l],
                           s5_c_re[l], s5_c_im[l]) for l in range(N_A_LAYERS)]
    wts = (norm_g, ffn_w_in, ffn_w_out, s5_mats, s5_d, s5_w_glu, kv_norm_g, w_kv, attn_w_q, attn_w_o, final_norm_g)

    caches_flat = tuple(c.reshape(bs, c.shape[1] * 2 * KV_HEADS, HEAD_DIM) for c in caches)
    (y_p, re_p, im_p, kv_p), (y_s, re_s, im_s, kv_s) = _trunks(
        x_prompt, x_sample, mod_p, mod_s, (state_s5_re, state_s5_im), caches_flat, wts,
        {"tm": 1024, "tf": 512, "tn_down": 256, "s5_steps": 32})

    kvp = [kv_p[g][:, seq_p - min(w, seq_p):] for g, (w, _) in enumerate(DILATED_PATTERNS)]
    return (y_p, y_s, re_p, im_p, kvp[0], kvp[1], kvp[2], re_s, im_s, kv_s[0], kv_s[1], kv_s[2])
```

```python
import functools

import jax
import jax.numpy as jnp
from jax import lax
from jax.experimental import pallas as pl
from jax.experimental.pallas import tpu as pltpu

F32 = jnp.float32
BF16 = jnp.bfloat16

D_MODEL = 2048
DEPTH = 4
N_A_LAYERS = DEPTH // 2
D_FF = 5632
S5_GROUP = 16
S5_GROUPS = D_MODEL // S5_GROUP
S5_STATE = 64
S5_NSTATE = S5_GROUPS * S5_STATE
HEAD_DIM = 128
HEADS = 16
KV_HEADS = 4
REP = HEADS // KV_HEADS
DILATED_PATTERNS = ((128, 1), (512, 4), (2048, 16))
N_DIL = len(DILATED_PATTERNS)
Q_BLOCK = 128
N_MOD = 9 * DEPTH + 4
EPS = 1e-6
ATTN_SCALE = HEAD_DIM ** -0.5
ATTN_SCALE_LOG2 = ATTN_SCALE * 1.4426950408889634

VMEM_LIMIT_BYTES = 56 * 1024 * 1024
SUBLANES = 8
S5_COL_BLOCK = 1024
S5_CH_BLOCK = S5_COL_BLOCK // S5_STATE * S5_GROUP
S5_N_BLOCKS = S5_NSTATE // S5_COL_BLOCK


def _params(*sem):
    return pltpu.CompilerParams(dimension_semantics=sem, vmem_limit_bytes=VMEM_LIMIT_BYTES)


def _rms_mod(x, g, shift, scale):
    xn = x * lax.rsqrt(jnp.mean(x * x, axis=-1, keepdims=True) + EPS)
    return xn * g * (1.0 + scale) + shift


def _norm_rows(x_ref, g_ref, sh_ref, sc_ref, h_ref, row_chunk):
    rows = x_ref.shape[0]
    g = g_ref[...]
    mod_rows = sh_ref.shape[0]
    if mod_rows == 1:
        gs = g * (1.0 + sc_ref[...])
        sh = sh_ref[...]
        row_chunk = min(row_chunk, 16)

        def body(c, carry):
            r0 = pl.multiple_of(c * row_chunk, row_chunk)
            x = x_ref[pl.ds(r0, row_chunk), :]
            xn = x * lax.rsqrt(jnp.mean(x * x, axis=-1, keepdims=True) + EPS)
            h_ref[pl.ds(r0, row_chunk), :] = (xn * gs + sh).astype(h_ref.dtype)
            return carry

        n_chunks = rows // row_chunk
        lax.fori_loop(0, n_chunks, body, 0, unroll=4 if n_chunks % 4 == 0 else 1)
        return

    def body(c, carry):
        r0 = pl.multiple_of(c * row_chunk, row_chunk)
        x = x_ref[pl.ds(r0, row_chunk), :]
        if mod_rows == rows and rows != row_chunk:
            sh = sh_ref[pl.ds(r0, row_chunk), :]
            sc = sc_ref[pl.ds(r0, row_chunk), :]
        else:
            sh = sh_ref[...]
            sc = sc_ref[...]
        h_ref[pl.ds(r0, row_chunk), :] = _rms_mod(x, g, sh, sc).astype(h_ref.dtype)
        return carry

    lax.fori_loop(0, rows // row_chunk, body, 0)


def _mod_kernel(c_ref, w_ref, b_ref, o_ref):
    c = c_ref[...]
    a = (c * jax.nn.sigmoid(c)).astype(BF16)
    o_ref[...] = jnp.dot(a, w_ref[...].astype(BF16), preferred_element_type=F32) + b_ref[...]


def _mod_call(c_all, w_mod, b_mod):
    rows = c_all.shape[0]
    n = w_mod.shape[1]
    tn = 1024
    return pl.pallas_call(
        _mod_kernel,
        grid=(n // tn,),
        in_specs=[
            pl.BlockSpec((rows, D_MODEL), lambda j: (0, 0)),
            pl.BlockSpec((D_MODEL, tn), lambda j: (0, j)),
            pl.BlockSpec((1, tn), lambda j: (0, j)),
        ],
        out_specs=pl.BlockSpec((rows, tn), lambda j: (0, j)),
        out_shape=jax.ShapeDtypeStruct((rows, n), F32),
        compiler_params=_params("arbitrary"),
        name="adaln_mod",
    )(c_all, w_mod, b_mod.reshape(1, n))


def _norm_two_row_sets(x_ref, sh_ref, sc_ref, x2_ref, sh2_ref, sc2_ref, g_ref, h_ref, row_chunk):
    tm, r2 = x_ref.shape[0], x2_ref.shape[0]

    @pl.when(pl.program_id(1) == 0)
    def _():
        _norm_rows(x_ref, g_ref, sh_ref, sc_ref, h_ref.at[pl.ds(0, tm)], row_chunk)

    @pl.when((pl.program_id(0) == 0) & (pl.program_id(1) == 0))
    def _():
        _norm_rows(x2_ref, g_ref, sh2_ref, sc2_ref, h_ref.at[pl.ds(tm, r2)], r2)


def _ffn_up_kernel(x_ref, sh_ref, sc_ref, x2_ref, sh2_ref, sc2_ref, g_ref, wg_ref, wu_ref, a_ref, a2_ref, h_ref,
                   *, row_chunk):
    tm = x_ref.shape[0]
    _norm_two_row_sets(x_ref, sh_ref, sc_ref, x2_ref, sh2_ref, sc2_ref, g_ref, h_ref, row_chunk)
    def act(h):
        g = jnp.dot(h, wg_ref[...].astype(BF16), preferred_element_type=F32)
        u = jnp.dot(h, wu_ref[...].astype(BF16), preferred_element_type=F32)
        return (g * jax.nn.sigmoid(g) * u).astype(a_ref.dtype)

    @pl.when(pl.program_id(0) == 0)
    def _():
        a = act(h_ref[...])
        a_ref[...] = a[:tm]
        a2_ref[...] = a[tm:]

    @pl.when(pl.program_id(0) != 0)
    def _():
        a_ref[...] = act(h_ref[pl.ds(0, tm), :])


def _ffn_call(x, mods, x2, mods2, g_norm, w_in, w_out, layer, which, tm, tf, tn):
    m, r2 = x.shape[0], x2.shape[0]
    nb = mods[0].shape[0]
    tiles_per_seq = (m // nb) // tm
    nk = D_FF // tf
    mod_spec = pl.BlockSpec((None, 1, D_MODEL), lambda i, k: (i // tiles_per_seq, 0, 0))
    full2 = pl.BlockSpec((r2, D_MODEL), lambda i, k: (0, 0))
    a, a2 = pl.pallas_call(
        functools.partial(_ffn_up_kernel, row_chunk=min(tm, 64)),
        grid=(m // tm, nk),
        in_specs=[
            pl.BlockSpec((tm, D_MODEL), lambda i, k: (i, 0)),
            mod_spec, mod_spec,
            full2, full2, full2,
            pl.BlockSpec((1, D_MODEL), lambda i, k: (0, 0)),
            pl.BlockSpec((None, None, D_MODEL, tf), lambda i, k: (layer, which, 0, k)),
            pl.BlockSpec((None, None, D_MODEL, tf), lambda i, k: (layer, which, 0, k + nk)),
        ],
        out_specs=(pl.BlockSpec((tm, tf), lambda i, k: (i, k)),
                   pl.BlockSpec((r2, tf), lambda i, k: (0, jnp.where(i == 0, k, nk - 1)))),
        out_shape=(jax.ShapeDtypeStruct((m, D_FF), BF16), jax.ShapeDtypeStruct((r2, D_FF), BF16)),
        scratch_shapes=[pltpu.VMEM((tm + r2, D_MODEL), BF16)],
        compiler_params=_params("arbitrary", "arbitrary"),
        name="ffn_up",
    )(x, mods[0], mods[1], x2, mods2[0], mods2[1], g_norm, w_in, w_in)
    out = _proj_residual_call(a, x, mods[2], w_out, (layer, which), False, tm, tn, gate_scale=0.5, name="ffn_down")
    out2 = _proj_residual_call(a2, x2, mods2[2].reshape(1, r2, D_MODEL), w_out, (layer, which), False, r2, tn,
                               gate_scale=0.5, name="ffn_down")
    return out, out2


def _norm_matmul_kernel(x_ref, sh_ref, sc_ref, x2_ref, sh2_ref, sc2_ref, g_ref, w_ref, *rest, row_chunk, head_rows):
    h_ref = rest[-1]
    outs = rest[:-1]
    per_set = len(outs) // 2
    tm = x_ref.shape[0]
    _norm_two_row_sets(x_ref, sh_ref, sc_ref, x2_ref, sh2_ref, sc2_ref, g_ref, h_ref, row_chunk)
    def emit(res, refs):
        refs[0][...] = res.astype(refs[0].dtype)
        if head_rows:
            rows, cols = res.shape
            heads_per_tile = cols // HEAD_DIM
            first = (pl.program_id(1) % (head_rows // heads_per_tile)) * heads_per_tile
            for c in range(heads_per_tile):
                refs[1][pl.ds(first + c, rows, stride=head_rows), :] = res[:, c * HEAD_DIM:(c + 1) * HEAD_DIM]

    @pl.when(pl.program_id(0) == 0)
    def _():
        res = jnp.dot(h_ref[...], w_ref[...].astype(BF16), preferred_element_type=F32)
        emit(res[:tm], outs[:per_set])
        emit(res[tm:], outs[per_set:])

    @pl.when(pl.program_id(0) != 0)
    def _():
        emit(jnp.dot(h_ref[pl.ds(0, tm), :], w_ref[...].astype(BF16), preferred_element_type=F32),
             outs[:per_set])


def _norm_matmul_call(x, shift, scale, x2, shift2, scale2, g_norm, w, w_lead, out_dtype, tm, tn, head_rows=0):
    m, r2 = x.shape[0], x2.shape[0]
    nb = shift.shape[0]
    n = w.shape[-1]
    nj = n // tn
    tiles_per_seq = (m // nb) // tm
    mod_spec = pl.BlockSpec((None, 1, D_MODEL), lambda i, j: (i // tiles_per_seq, 0, 0))
    full2 = pl.BlockSpec((r2, D_MODEL), lambda i, j: (0, 0))

    def col2(i, j):
        return jnp.where(i == 0, j, nj - 1)

    if head_rows:
        slab = head_rows * HEAD_DIM
        tps = slab // tn
        out_spec = (pl.BlockSpec((None, tm, tn), lambda i, j: (j // tps, i, j % tps)),
                    pl.BlockSpec((None, tm * head_rows, HEAD_DIM), lambda i, j: (j // tps, i, 0)),
                    pl.BlockSpec((None, r2, tn), lambda i, j: (col2(i, j) // tps, 0, col2(i, j) % tps)),
                    pl.BlockSpec((None, r2 * head_rows, HEAD_DIM), lambda i, j: (col2(i, j) // tps, 0, 0)))
        out_shape = (jax.ShapeDtypeStruct((n // slab, m, slab), out_dtype),
                     jax.ShapeDtypeStruct((n // slab, m * head_rows, HEAD_DIM), out_dtype),
                     jax.ShapeDtypeStruct((n // slab, r2, slab), out_dtype),
                     jax.ShapeDtypeStruct((n // slab, r2 * head_rows, HEAD_DIM), out_dtype))
    else:
        out_spec = (pl.BlockSpec((tm, tn), lambda i, j: (i, j)),
                    pl.BlockSpec((r2, tn), lambda i, j: (0, col2(i, j))))
        out_shape = (jax.ShapeDtypeStruct((m, n), out_dtype), jax.ShapeDtypeStruct((r2, n), out_dtype))
    return pl.pallas_call(
        functools.partial(_norm_matmul_kernel, row_chunk=min(tm, 64), head_rows=head_rows),
        grid=(m // tm, nj),
        in_specs=[
            pl.BlockSpec((tm, D_MODEL), lambda i, j: (i, 0)),
            mod_spec, mod_spec,
            full2, full2, full2,
            pl.BlockSpec((1, D_MODEL), lambda i, j: (0, 0)),
            pl.BlockSpec((None,) * len(w_lead) + (D_MODEL, tn), lambda i, j: (*w_lead, 0, j)),
        ],
        out_specs=out_spec,
        out_shape=out_shape,
        scratch_shapes=[pltpu.VMEM((tm + r2, D_MODEL), BF16)],
        compiler_params=_params("arbitrary", "arbitrary"),
        name="norm_matmul",
    )(x, shift, scale, x2, shift2, scale2, g_norm, w)


def _proj_residual_kernel(z_ref, x_ref, gt_ref, *refs, glu, gate_scale):
    z = z_ref[...]
    if glu:
        wa_ref, wg_ref, o_ref = refs
        a = jnp.dot(z, wa_ref[...].astype(BF16), preferred_element_type=F32)
        g = jnp.dot(z, wg_ref[...].astype(BF16), preferred_element_type=F32)
        y = a * jax.nn.sigmoid(g)
    else:
        w_ref, o_ref = refs
        y = jnp.dot(z, w_ref[...].astype(BF16), preferred_element_type=F32)
    gate = gt_ref[...] if gate_scale == 1.0 else gate_scale * gt_ref[...]
    o_ref[...] = x_ref[...] + gate * y


def _proj_residual_call(z, x, gate, w, w_lead, glu, tm, tn, gate_scale=1.0, name="proj_residual", z_buffers=2):
    m = x.shape[0]
    kdim = z.shape[1]
    nb, r, _ = gate.shape
    tiles_per_seq = (m // nb) // tm if r == 1 else 1
    n_blocks = D_MODEL // tn
    w_block = (None,) * len(w_lead) + (kdim, tn)
    w_specs = [pl.BlockSpec(w_block, lambda i, j: (*w_lead, 0, j))]
    w_args = [w]
    if glu:
        w_specs.append(pl.BlockSpec(w_block, lambda i, j: (*w_lead, 0, j + n_blocks)))
        w_args.append(w)
    return pl.pallas_call(
        functools.partial(_proj_residual_kernel, glu=glu, gate_scale=gate_scale),
        grid=(m // tm, n_blocks),
        in_specs=[
            (pl.BlockSpec((tm, kdim), lambda i, j: (i, 0)) if z_buffers == 2 else
             pl.BlockSpec((tm, kdim), lambda i, j: (i, 0), pipeline_mode=pl.Buffered(z_buffers))),
            pl.BlockSpec((tm, tn), lambda i, j: (i, j)),
            pl.BlockSpec((None, r, tn), lambda i, j: (i // tiles_per_seq, 0, j)),
        ] + w_specs,
        out_specs=pl.BlockSpec((tm, tn), lambda i, j: (i, j)),
        out_shape=jax.ShapeDtypeStruct((m, D_MODEL), F32),
        compiler_params=_params("parallel", "arbitrary"),
        name=name,
    )(z, x, gate, *w_args)


def _final_norm_kernel(x_ref, sh_ref, sc_ref, g_ref, o_ref, *, row_chunk):
    _norm_rows(x_ref, g_ref, sh_ref, sc_ref, o_ref, row_chunk)


def _final_norm_call(x, shift, scale, g_norm, tm):
    m = x.shape[0]
    nb, r, _ = shift.shape
    tiles_per_seq = (m // nb) // tm if r == 1 else 1
    mod_spec = pl.BlockSpec((None, r, D_MODEL), lambda i: (i // tiles_per_seq, 0, 0))
    return pl.pallas_call(
        functools.partial(_final_norm_kernel, row_chunk=min(tm, 64)),
        grid=(m // tm,),
        in_specs=[
            pl.BlockSpec((tm, D_MODEL), lambda i: (i, 0)),
            mod_spec, mod_spec,
            pl.BlockSpec((1, D_MODEL), lambda i: (0, 0)),
        ],
        out_specs=pl.BlockSpec((tm, D_MODEL), lambda i: (i, 0)),
        out_shape=jax.ShapeDtypeStruct((m, D_MODEL), F32),
        compiler_params=_params("parallel"),
        name="final_norm",
    )(x, shift, scale, g_norm)


def _s5_disc_kernel(lre_ref, lim_ref, ldt_ref, bre_ref, bim_ref, are_ref, aim_ref, bbre_ref, bbim_ref):
    lam_re = lre_ref[...]
    lam_im = lim_ref[...]
    dt = jnp.exp(ldt_ref[...])
    decay = jnp.exp(lam_re * dt)
    ab_re = decay * jnp.cos(lam_im * dt)
    ab_im = decay * jnp.sin(lam_im * dt)
    den = lam_re * lam_re + lam_im * lam_im
    f_re = ((ab_re - 1.0) * lam_re + ab_im * lam_im) / den
    f_im = (ab_im * lam_re - (ab_re - 1.0) * lam_im) / den
    b_re = bre_ref[...]
    b_im = bim_ref[...]
    are_ref[...] = ab_re
    aim_ref[...] = ab_im
    bbre_ref[...] = f_re * b_re - f_im * b_im
    bbim_ref[...] = f_re * b_im + f_im * b_re


def _s5_disc_call(lam_re, lam_im, log_dt, b_re, b_im):
    shape = jax.ShapeDtypeStruct(lam_re.shape, F32)
    return pl.pallas_call(
        _s5_disc_kernel,
        out_shape=(shape, shape, shape, shape),
        name="s5_discretise",
    )(lam_re, lam_im, log_dt, b_re, b_im)


def _s5_kernel(x_ref, sh_ref, sc_ref, g_ref, bbre_ref, bbim_ref, are_ref, aim_ref, cre_ref, cim_ref,
               d_ref, h0re_ref, h0im_ref, z_ref, hre_out, him_out,
               u_scr, sre_scr, sim_scr, y_scr, pwre_scr, pwim_scr, *, row_chunk):
    t = pl.program_id(0)
    nb, steps, _ = x_ref.shape
    rows = steps * nb
    lane = HEAD_DIM
    seg = SUBLANES // nb
    seg_len = steps // seg
    assert seg in (1, 2)

    @pl.when(t == 0)
    def _():
        hre_out[...] = h0re_ref[...]
        him_out[...] = h0im_ref[...]
        if seg > 1:
            zero = jnp.zeros((nb, S5_COL_BLOCK), F32)
            for cb in range(S5_N_BLOCKS):
                a_re = jnp.broadcast_to(are_ref[cb], (nb, S5_COL_BLOCK))
                a_im = jnp.broadcast_to(aim_ref[cb], (nb, S5_COL_BLOCK))
                p_re, p_im = a_re, a_im
                for i in range(seg_len):
                    pwre_scr[cb, i * SUBLANES:(i + 1) * SUBLANES, :] = jnp.concatenate([zero, p_re], axis=0)
                    pwim_scr[cb, i * SUBLANES:(i + 1) * SUBLANES, :] = jnp.concatenate([zero, p_im], axis=0)
                    p_re, p_im = p_re * a_re - p_im * a_im, p_re * a_im + p_im * a_re

    g = g_ref[...]
    for b in range(nb):
        for s in range(seg):
            def norm_chunk(c, carry, b=b, s=s):
                r0 = pl.multiple_of(c * row_chunk, row_chunk)
                u = _rms_mod(x_ref[b, pl.ds(s * seg_len + r0, row_chunk), :], g, sh_ref[b], sc_ref[b])
                for j in range(D_MODEL // lane):
                    u_scr[j, pl.ds(r0 * SUBLANES + s * nb + b, row_chunk, stride=SUBLANES), :] = \
                        u[:, j * lane:(j + 1) * lane]
                return carry

            lax.fori_loop(0, seg_len // row_chunk, norm_chunk, 0)

    tiles_per_block = S5_CH_BLOCK // lane
    per_block_scratch = sre_scr.shape[0] == S5_N_BLOCKS
    for cb in range(S5_N_BLOCKS):
        ch = slice(cb * S5_CH_BLOCK, (cb + 1) * S5_CH_BLOCK)
        st = slice(cb * S5_COL_BLOCK, (cb + 1) * S5_COL_BLOCK)
        slab = cb if per_block_scratch else 0
        sre, sim, ys = sre_scr.at[slab], sim_scr.at[slab], y_scr.at[slab]
        u_blk = jnp.concatenate([u_scr[cb * tiles_per_block + j] for j in range(tiles_per_block)], axis=1)
        u_bf = u_blk.astype(BF16)
        sre[...] = jnp.dot(u_bf, bbre_ref[cb], preferred_element_type=F32)
        sim[...] = jnp.dot(u_bf, bbim_ref[cb], preferred_element_type=F32)

        a_re = jnp.broadcast_to(are_ref[cb], (SUBLANES, S5_COL_BLOCK))
        a_im = jnp.broadcast_to(aim_ref[cb], (SUBLANES, S5_COL_BLOCK))

        def tile_step(i, carry, a_re=a_re, a_im=a_im, sre=sre, sim=sim):
            h_re, h_im = carry
            r0 = i * SUBLANES if isinstance(i, int) else pl.multiple_of(i * SUBLANES, SUBLANES)
            n_re = a_re * h_re - a_im * h_im + sre[pl.ds(r0, SUBLANES), :]
            n_im = a_re * h_im + a_im * h_re + sim[pl.ds(r0, SUBLANES), :]
            sre[pl.ds(r0, SUBLANES), :] = n_re
            sim[pl.ds(r0, SUBLANES), :] = n_im
            return n_re, n_im

        def first_rows(v):
            return jnp.concatenate([v[:nb]] * seg, axis=0)

        h_re, h_im = hre_out[:, st], him_out[:, st]
        if seg > 1:
            zero = jnp.zeros((SUBLANES - nb, S5_COL_BLOCK), F32)
            h_re, h_im = jnp.concatenate([h_re, zero], axis=0), jnp.concatenate([h_im, zero], axis=0)
        carry = (h_re, h_im)
        if per_block_scratch:
            for i in range(seg_len):
                carry = tile_step(i, carry)
        else:
            carry = lax.fori_loop(0, seg_len, tile_step, carry, unroll=min(seg_len, 4))
        h_re, h_im = carry
        if seg > 1:
            c_re, c_im = first_rows(h_re), first_rows(h_im)
            for i in range(seg_len):
                rows_i = slice(i * SUBLANES, (i + 1) * SUBLANES)
                p_re, p_im = pwre_scr[cb, rows_i, :], pwim_scr[cb, rows_i, :]
                f_re = sre[rows_i, :] + (p_re * c_re - p_im * c_im)
                f_im = sim[rows_i, :] + (p_re * c_im + p_im * c_re)
                sre[rows_i, :] = f_re
                sim[rows_i, :] = f_im
            h_re, h_im = f_re[SUBLANES - nb:], f_im[SUBLANES - nb:]
        hre_out[:, st] = h_re
        him_out[:, st] = h_im

        y = (jnp.dot(sre[...].astype(BF16), cre_ref[cb], preferred_element_type=F32)
             - jnp.dot(sim[...].astype(BF16), cim_ref[cb], preferred_element_type=F32))
        zf = jax.nn.gelu(y + d_ref[:, ch] * u_blk)
        for j in range(tiles_per_block):
            ys[j] = zf[:, j * lane:(j + 1) * lane]
        for b in range(nb):
            for s in range(seg):
                for j in range(tiles_per_block):
                    c0 = cb * S5_CH_BLOCK + j * lane
                    z_ref[b, s * seg_len:(s + 1) * seg_len, c0:c0 + lane] = \
                        ys[j, pl.ds(s * nb + b, seg_len, stride=SUBLANES), :].astype(z_ref.dtype)


def _s5_call(x, shift, scale, g_norm, bb_re, bb_im, a_re, a_im, c_re, c_im, d_skip,
             h0_re, h0_im, steps_per_chunk, z_dtype, per_block_scratch):
    nb, seq, _ = x.shape
    assert SUBLANES % nb == 0 and seq % steps_per_chunk == 0
    tr = steps_per_chunk * nb
    slabs = S5_N_BLOCKS if per_block_scratch else 1
    seg = SUBLANES // nb
    pw_rows = tr if seg > 1 else SUBLANES
    row_chunk = min(steps_per_chunk // (SUBLANES // nb), 64)
    const2 = lambda t: (0, 0)
    const3 = lambda t: (0, 0, 0)
    state_spec = pl.BlockSpec((nb, S5_NSTATE), const2)
    io_spec = pl.BlockSpec((nb, steps_per_chunk, D_MODEL), lambda t: (0, t, 0))
    return pl.pallas_call(
        functools.partial(_s5_kernel, row_chunk=row_chunk),
        grid=(seq // steps_per_chunk,),
        in_specs=[
            io_spec,
            pl.BlockSpec((nb, 1, D_MODEL), const3),
            pl.BlockSpec((nb, 1, D_MODEL), const3),
            pl.BlockSpec((1, D_MODEL), const2),
            pl.BlockSpec((S5_N_BLOCKS, S5_CH_BLOCK, S5_COL_BLOCK), const3, pipeline_mode=pl.Buffered(1)),
            pl.BlockSpec((S5_N_BLOCKS, S5_CH_BLOCK, S5_COL_BLOCK), const3, pipeline_mode=pl.Buffered(1)),
            pl.BlockSpec((S5_N_BLOCKS, 1, S5_COL_BLOCK), const3),
            pl.BlockSpec((S5_N_BLOCKS, 1, S5_COL_BLOCK), const3),
            pl.BlockSpec((S5_N_BLOCKS, S5_COL_BLOCK, S5_CH_BLOCK), const3, pipeline_mode=pl.Buffered(1)),
            pl.BlockSpec((S5_N_BLOCKS, S5_COL_BLOCK, S5_CH_BLOCK), const3, pipeline_mode=pl.Buffered(1)),
            pl.BlockSpec((1, D_MODEL), const2),
            state_spec, state_spec,
        ],
        out_specs=(io_spec, state_spec, state_spec),
        out_shape=(jax.ShapeDtypeStruct((nb, seq, D_MODEL), z_dtype),
                   jax.ShapeDtypeStruct((nb, S5_NSTATE), F32),
                   jax.ShapeDtypeStruct((nb, S5_NSTATE), F32)),
        scratch_shapes=[pltpu.VMEM((D_MODEL // HEAD_DIM, tr, HEAD_DIM), F32),
                        pltpu.VMEM((slabs, tr, S5_COL_BLOCK), F32),
                        pltpu.VMEM((slabs, tr, S5_COL_BLOCK), F32),
                        pltpu.VMEM((slabs, S5_CH_BLOCK // HEAD_DIM, tr, HEAD_DIM), F32),
                        pltpu.VMEM((S5_N_BLOCKS, pw_rows, S5_COL_BLOCK), F32),
                        pltpu.VMEM((S5_N_BLOCKS, pw_rows, S5_COL_BLOCK), F32)],
        compiler_params=_params("arbitrary"),
        name="s5_mixer",
    )(x, shift, scale, g_norm, bb_re, bb_im, a_re, a_im, c_re, c_im, d_skip, h0_re, h0_im)


def _softmax_pv(s2, v_bf):
    m = jnp.max(s2, axis=-1, keepdims=True)
    p = jnp.exp2(s2 - m)
    l = jnp.sum(p, axis=-1, keepdims=True)
    o = jnp.dot(p.astype(BF16), v_bf, preferred_element_type=F32) / l
    return o, jnp.broadcast_to(m + jnp.log2(l), o.shape)


def _qk(q_bf, k_bf, scale=ATTN_SCALE):
    return lax.dot_general(q_bf, k_bf, (((1,), (1,)), ((), ())), preferred_element_type=F32) * scale


def _attn_prompt_kernel(q0_ref, q1_ref, q2_ref, k0_ref, v0_ref, k1_ref, v1_ref, k2_ref, v2_ref, o_ref,
                        qf_scr, o1_scr, l1_scr, o2_scr, l2_scr, band_scr, first_scr):
    seq = q0_ref.shape[0]
    qb = Q_BLOCK
    row = lax.broadcasted_iota(jnp.int32, (REP * qb, 2 * qb), 0) & (qb - 1)
    col = lax.broadcasted_iota(jnp.int32, (REP * qb, 2 * qb), 1)
    band_scr[...] = jnp.where((col >= row) & (col <= row + qb), 0.0, -jnp.inf)
    row1 = lax.broadcasted_iota(jnp.int32, (REP * qb, qb), 0) & (qb - 1)
    col1 = lax.broadcasted_iota(jnp.int32, (REP * qb, qb), 1)
    first_scr[...] = jnp.where(col1 <= row1, 0.0, -jnp.inf)

    def rows_of(start, d):
        return pl.ds(start, qb) if d == 1 else pl.ds(start, qb, stride=d)

    def load_q(src_ref, start, d):
        if d == 1:
            parts = [src_ref[pl.ds(start, qb), e * HEAD_DIM:(e + 1) * HEAD_DIM] for e in range(REP)]
        else:
            parts = [src_ref[e, rows_of(start, d), :] for e in range(REP)]
        return jnp.concatenate(parts, axis=0).astype(BF16)

    def unit(src_q, k_ref, v_ref, start, d, first):
        q4 = load_q(src_q, start, d)
        k_cur = k_ref[rows_of(start, d), :]
        v_cur = v_ref[rows_of(start, d), :]
        if first:
            s = _qk(q4, k_cur.astype(BF16), ATTN_SCALE_LOG2) + first_scr[...]
            return _softmax_pv(s, v_cur.astype(BF16))
        prev = start - qb * d
        k_band = jnp.concatenate([k_ref[rows_of(prev, d), :], k_cur], axis=0)
        v_band = jnp.concatenate([v_ref[rows_of(prev, d), :], v_cur], axis=0)
        s = _qk(q4, k_band.astype(BF16), ATTN_SCALE_LOG2) + band_scr[...]
        return _softmax_pv(s, v_band.astype(BF16))

    def stage_group(q_ref, k_ref, v_ref, d, o_scr, l_scr):
        n_blocks = seq // (qb * d)

        def copy(c, carry):
            r0 = pl.multiple_of(c * 256, 256)
            for e in range(REP):
                qf_scr[e, pl.ds(r0, 256), :] = q_ref[pl.ds(r0, 256), e * HEAD_DIM:(e + 1) * HEAD_DIM].astype(F32)
            return carry

        lax.fori_loop(0, seq // 256, copy, 0)

        def store(start, o, lse):
            for e in range(REP):
                sl = slice(e * qb, (e + 1) * qb)
                o_scr[e, rows_of(start, d), :] = o[sl]
                l_scr[e, rows_of(start, d), :] = lse[sl]

        def per_class(r, carry):
            store(r, *unit(qf_scr, k_ref, v_ref, r, d, True))

            def per_block(jb, c2):
                start = jb * (qb * d) + r
                store(start, *unit(qf_scr, k_ref, v_ref, start, d, False))
                return c2

            if n_blocks > 1:
                lax.fori_loop(1, n_blocks, per_block, 0, unroll=True)
            return carry

        lax.fori_loop(0, d, per_class, 0, unroll=2 if n_blocks == 1 else 1)

    stage_group(q2_ref, k2_ref, v2_ref, DILATED_PATTERNS[2][1], o2_scr, l2_scr)
    stage_group(q1_ref, k1_ref, v1_ref, DILATED_PATTERNS[1][1], o1_scr, l1_scr)

    def merge(start, o0, lse0):
        for e in range(REP):
            sl = slice(e * qb, (e + 1) * qb)
            cols = slice(e * HEAD_DIM, (e + 1) * HEAD_DIM)
            la = lse0[sl]
            lb = l1_scr[e, pl.ds(start, qb), :]
            lc = l2_scr[e, pl.ds(start, qb), :]
            mx = jnp.maximum(jnp.maximum(la, lb), lc)
            wa = jnp.exp2(la - mx)
            wb = jnp.exp2(lb - mx)
            wc = jnp.exp2(lc - mx)
            inv = 1.0 / (wa + wb + wc)
            acc = (wa * inv) * o0[sl] + (wb * inv) * o1_scr[e, pl.ds(start, qb), :] \
                + (wc * inv) * o2_scr[e, pl.ds(start, qb), :]
            o_ref[pl.ds(start, qb), cols] = acc.astype(o_ref.dtype)

    merge(0, *unit(q0_ref, k0_ref, v0_ref, 0, 1, True))

    def per_block0(jb, carry):
        start = pl.multiple_of(jb * qb, qb)
        merge(start, *unit(q0_ref, k0_ref, v0_ref, start, 1, False))
        return carry

    n_blocks0 = seq // qb
    lax.fori_loop(1, n_blocks0, per_block0, 0, unroll=3 if (n_blocks0 - 1) % 3 == 0 else 1)


def _attn_prompt_call(q, kv):
    b, seq, _ = q.shape
    qw = REP * HEAD_DIM
    q_specs = [pl.BlockSpec((None, seq, qw), functools.partial(lambda bi, h, g: (bi, 0, g * KV_HEADS + h), g=g))
               for g in range(N_DIL)]
    kv_specs = []
    for g in range(N_DIL):
        kv_specs.append(pl.BlockSpec((None, None, seq, HEAD_DIM),
                                     functools.partial(lambda bi, h, g: (g, bi, 0, h), g=g)))
        kv_specs.append(pl.BlockSpec((None, None, seq, HEAD_DIM),
                                     functools.partial(lambda bi, h, g: (g, bi, 0, KV_HEADS + h), g=g)))
    kv_args = [kv] * (2 * N_DIL)
    return pl.pallas_call(
        _attn_prompt_kernel,
        grid=(b, KV_HEADS),
        in_specs=q_specs + kv_specs,
        out_specs=pl.BlockSpec((None, seq, qw), lambda bi, h: (bi, 0, h)),
        out_shape=jax.ShapeDtypeStruct((b, seq, HEADS * HEAD_DIM), BF16),
        scratch_shapes=[pltpu.VMEM((REP, seq, HEAD_DIM), F32),
                        pltpu.VMEM((REP, seq, HEAD_DIM), F32), pltpu.VMEM((REP, seq, HEAD_DIM), F32),
                        pltpu.VMEM((REP, seq, HEAD_DIM), F32), pltpu.VMEM((REP, seq, HEAD_DIM), F32),
                        pltpu.VMEM((REP * Q_BLOCK, 2 * Q_BLOCK), F32), pltpu.VMEM((REP * Q_BLOCK, Q_BLOCK), F32)],
        compiler_params=_params("parallel", "arbitrary"),
        name="dilated_attention_prompt",
    )(q, q, q, *kv_args)


def _attn_sample_kernel(q_ref, kvn_ref, c0_ref, c1_ref, c2_ref, o_ref, *, n_tok):
    rows = REP * n_tok
    cache_refs = (c0_ref, c1_ref, c2_ref)
    kv_rows = 2 * KV_HEADS
    for h in range(KV_HEADS):
        outs, lses = [], []
        for g, (_, d) in enumerate(DILATED_PATTERNS):
            q = q_ref[g * KV_HEADS + h]
            lw = cache_refs[g].shape[0] // kv_rows
            k_c = cache_refs[g][pl.ds(h, lw, stride=kv_rows), :].astype(BF16)
            v_c = cache_refs[g][pl.ds(KV_HEADS + h, lw, stride=kv_rows), :].astype(BF16)
            k_n = kvn_ref[g, pl.ds(h, n_tok, stride=kv_rows), :].astype(BF16)
            v_n = kvn_ref[g, pl.ds(KV_HEADS + h, n_tok, stride=kv_rows), :].astype(BF16)
            tq_c = lax.broadcasted_iota(jnp.int32, (rows, lw), 0) & (n_tok - 1)
            idx_c = lax.broadcasted_iota(jnp.int32, (rows, lw), 1)
            ok_c = (idx_c >= tq_c) & (((idx_c - tq_c) & (d - 1)) == 0)
            tq_n = lax.broadcasted_iota(jnp.int32, (rows, n_tok), 0) & (n_tok - 1)
            idx_n = lax.broadcasted_iota(jnp.int32, (rows, n_tok), 1)
            ok_n = (idx_n <= tq_n) & (((tq_n - idx_n) & (d - 1)) == 0)
            s_c = jnp.where(ok_c, _qk(q, k_c), -jnp.inf)
            s_n = jnp.where(ok_n, _qk(q, k_n), -jnp.inf)
            m = jnp.maximum(jnp.max(s_c, axis=-1, keepdims=True), jnp.max(s_n, axis=-1, keepdims=True))
            p_c = jnp.exp(s_c - m)
            p_n = jnp.exp(s_n - m)
            l = jnp.sum(p_c, axis=-1, keepdims=True) + jnp.sum(p_n, axis=-1, keepdims=True)
            o = (jnp.dot(p_c.astype(BF16), v_c, preferred_element_type=F32)
                 + jnp.dot(p_n.astype(BF16), v_n, preferred_element_type=F32)) / l
            outs.append(o)
            lses.append(m + jnp.log(l))
        mx = jnp.maximum(jnp.maximum(lses[0], lses[1]), lses[2])
        w = [jnp.exp(x - mx) for x in lses]
        inv = 1.0 / (w[0] + w[1] + w[2])
        acc = (w[0] * inv) * outs[0] + (w[1] * inv) * outs[1] + (w[2] * inv) * outs[2]
        o_ref[h] = acc.astype(o_ref.dtype)


def _attn_sample_call(q_heads, kv_new, caches, n_tok):
    b = q_heads.shape[0]
    rows = REP * n_tok
    assert n_tok & (n_tok - 1) == 0
    in_specs = [pl.BlockSpec((None, N_DIL * KV_HEADS, rows, HEAD_DIM), lambda bi: (bi, 0, 0, 0)),
                pl.BlockSpec((N_DIL, None, kv_new.shape[2], HEAD_DIM), lambda bi: (0, bi, 0, 0))]
    in_specs += [pl.BlockSpec((None, c.shape[1], HEAD_DIM), lambda bi: (bi, 0, 0)) for c in caches]
    return pl.pallas_call(
        functools.partial(_attn_sample_kernel, n_tok=n_tok),
        grid=(b,),
        in_specs=in_specs,
        out_specs=pl.BlockSpec((None, KV_HEADS, rows, HEAD_DIM), lambda bi: (bi, 0, 0, 0)),
        out_shape=jax.ShapeDtypeStruct((b, KV_HEADS, rows, HEAD_DIM), BF16),
        compiler_params=_params("parallel"),
        name="dilated_attention_sample",
    )(q_heads, kv_new, *caches)


def _s5_weights(lam_re, lam_im, log_dt, b_re, b_im, c_re, c_im):
    g, p, c = S5_GROUPS, S5_STATE, S5_GROUP
    rep = lambda a: jnp.repeat(a, c, axis=1)
    a_re, a_im, bb_re, bb_im = _s5_disc_call(rep(lam_re), rep(lam_im), log_dt.reshape(g, 1),
                                             b_re.reshape(g, p * c), b_im.reshape(g, p * c))
    gl = S5_CH_BLOCK // c

    def block_diag(t, rows_per_group, cols_per_group):
        tiled = jnp.tile(t, (1, 1, gl))
        row_g = lax.broadcasted_iota(jnp.int32, tiled.shape, 1) // rows_per_group
        col_g = lax.broadcasted_iota(jnp.int32, tiled.shape, 2) // cols_per_group
        return jnp.where(row_g == col_g, tiled, 0.0).astype(BF16)

    def in_blocks(bb):
        t = bb.reshape(S5_N_BLOCKS, gl, p, c).transpose(0, 1, 3, 2)
        return block_diag(t.reshape(S5_N_BLOCKS, gl * c, p), c, p)

    def out_blocks(cm):
        t = cm.reshape(S5_N_BLOCKS, gl, c, p).transpose(0, 1, 3, 2)
        return block_diag(t.reshape(S5_N_BLOCKS, gl * p, c), p, c)

    lam_bar = lambda a: a.reshape(g, p, c)[:, :, 0].reshape(S5_N_BLOCKS, 1, S5_COL_BLOCK)
    return (in_blocks(bb_re), in_blocks(bb_im), lam_bar(a_re), lam_bar(a_im),
            out_blocks(c_re), out_blocks(c_im))


def _trunks(x_p, x_s, mod_p, mod_s, s5_state, kv_caches, wts, tiles):
    (norm_g, ffn_w_in, ffn_w_out, s5_mats, s5_d, s5_w_glu, kv_norm_g, w_kv,
     attn_w_q, attn_w_o, final_norm_g) = wts
    bp, lp, _ = x_p.shape
    bs, ls, _ = x_s.shape
    mp, ms = bp * lp, bs * ls
    tm = tiles["tm"]
    kv_rows = 2 * KV_HEADS

    def mods_p(*ks):
        return tuple(mod_p[:, k].reshape(bp, 1, D_MODEL) for k in ks)

    def mods_s(*ks):
        return tuple(jnp.repeat(mod_s[:, k], ls, axis=0) for k in ks)

    def as_block(a):
        return a.reshape(1, *a.shape)

    def norm_g_row(a):
        return a.reshape(1, D_MODEL)

    xp = x_p.reshape(mp, D_MODEL)
    xs = x_s.reshape(ms, D_MODEL)
    re_p, im_p, re_s, im_s = [], [], [], []
    for layer in range(DEPTH):
        base = 9 * layer
        if layer == N_A_LAYERS:
            k0 = 9 * DEPTH
            kvf_p, kv_p, _, kv_s = _norm_matmul_call(xp, *mods_p(k0, k0 + 1), xs, *mods_s(k0, k0 + 1),
                                                     norm_g_row(kv_norm_g), w_kv, (), F32, tm,
                                                     KV_HEADS * HEAD_DIM, head_rows=kv_rows)
        xp, xs = _ffn_call(xp, mods_p(base, base + 1, base + 2), xs, mods_s(base, base + 1, base + 2),
                           norm_g_row(norm_g[layer, 0]), ffn_w_in, ffn_w_out, layer, 0, tm, tiles["tf"],
                           tiles["tn_down"])
        g_mix = norm_g_row(norm_g[layer, 1])
        if layer < N_A_LAYERS:
            d_skip = s5_d[layer].reshape(1, D_MODEL)
            zero_state = jnp.zeros((bp, S5_NSTATE), F32)
            zp, h_re, h_im = _s5_call(xp.reshape(bp, lp, D_MODEL), *mods_p(base + 3, base + 4), g_mix,
                                      *s5_mats[layer], d_skip, zero_state, zero_state,
                                      min(lp, tiles["s5_steps"]), BF16, True)
            re_p.append(h_re.reshape(bp, S5_GROUPS, S5_STATE))
            im_p.append(h_im.reshape(bp, S5_GROUPS, S5_STATE))
            zs, h_re, h_im = _s5_call(xs.reshape(bs, ls, D_MODEL), mod_s[:, base + 3].reshape(bs, 1, D_MODEL),
                                      mod_s[:, base + 4].reshape(bs, 1, D_MODEL), g_mix, *s5_mats[layer], d_skip,
                                      s5_state[0][layer].reshape(bs, S5_NSTATE),
                                      s5_state[1][layer].reshape(bs, S5_NSTATE), ls, F32, False)
            re_s.append(h_re.reshape(bs, S5_GROUPS, S5_STATE))
            im_s.append(h_im.reshape(bs, S5_GROUPS, S5_STATE))
            xp = _proj_residual_call(zp.reshape(mp, D_MODEL), xp, *mods_p(base + 5), s5_w_glu, (layer,),
                                     True, tm, 512, name="s5_glu")
            xs = _proj_residual_call(zs.reshape(ms, D_MODEL).astype(BF16), xs, as_block(*mods_s(base + 5)),
                                     s5_w_glu, (layer,), True, ms, 512, name="s5_glu")
        else:
            bl = layer - N_A_LAYERS
            q_p, q_s = _norm_matmul_call(xp, *mods_p(base + 3, base + 4), xs, *mods_s(base + 3, base + 4), g_mix,
                                         attn_w_q, (bl,), BF16, tm, 1024)
            o_p = _attn_prompt_call(q_p.reshape(bp, lp, N_DIL * HEADS * HEAD_DIM),
                                    kvf_p.reshape(N_DIL, bp, lp, kv_rows * HEAD_DIM)).reshape(mp, D_MODEL)
            qh = q_s.reshape(bs, ls, N_DIL, KV_HEADS, REP, HEAD_DIM).transpose(0, 2, 3, 4, 1, 5)
            qh = qh.reshape(bs, N_DIL * KV_HEADS, REP * ls, HEAD_DIM)
            o_s = _attn_sample_call(qh, kv_s.reshape(N_DIL, bs, ls * kv_rows, HEAD_DIM), kv_caches, ls)
            o_s = o_s.reshape(bs, KV_HEADS, REP, ls, HEAD_DIM).transpose(0, 3, 1, 2, 4).reshape(ms, D_MODEL)
            xp = _proj_residual_call(o_p, xp, *mods_p(base + 5), attn_w_o, (bl,), False, tm, 512, name="attn_out")
            xs = _proj_residual_call(o_s, xs, as_block(*mods_s(base + 5)), attn_w_o, (bl,), False, ms, 512,
                                     name="attn_out")
        xp, xs = _ffn_call(xp, mods_p(base + 6, base + 7, base + 8), xs, mods_s(base + 6, base + 7, base + 8),
                           norm_g_row(norm_g[layer, 2]), ffn_w_in, ffn_w_out, layer, 1, tm, tiles["tf"],
                           tiles["tn_down"])
    k1 = 9 * DEPTH + 2
    y_p = _final_norm_call(xp, *mods_p(k1, k1 + 1), norm_g_row(final_norm_g), tm)
    y_s = _final_norm_call(xs, *(as_block(a) for a in mods_s(k1, k1 + 1)), norm_g_row(final_norm_g), ms)
    kv_p = kv_p.reshape(N_DIL, bp, lp, 2, KV_HEADS, HEAD_DIM)
    kv_s = kv_s.reshape(N_DIL, bs, ls, 2, KV_HEADS, HEAD_DIM)
    return ((y_p.reshape(bp, lp, D_MODEL), jnp.stack(re_p, axis=0), jnp.stack(im_p, axis=0), kv_p),
            (y_s.reshape(bs, ls, D_MODEL), jnp.stack(re_s, axis=0), jnp.stack(im_s, axis=0), kv_s))


def kernel(x_prompt, x_sample, c_prompt, c_sample, state_s5_re, state_s5_im, cache_kv_g0, cache_kv_g1, cache_kv_g2, w_mod, b_mod, norm_g, ffn_w_in, ffn_w_out, s5_lambda_re, s5_lambda_im, s5_log_dt, s5_b_re, s5_b_im, s5_c_re, s5_c_im, s5_d, s5_w_glu, kv_norm_g, w_kv, attn_w_q, attn_w_o, final_norm_g):
    bp, seq_p, _ = x_prompt.shape
    bs, seq_s, _ = x_sample.shape
    caches = (cache_kv_g0, cache_kv_g1, cache_kv_g2)
    for (w, d), c in zip(DILATED_PATTERNS, caches):
        assert c.shape[1] == w and w == d * Q_BLOCK and seq_p % (d * Q_BLOCK) == 0
    assert bp <= SUBLANES and bs <= SUBLANES

    n_c = bp + bs
    c_all = jnp.pad(jnp.concatenate([c_prompt, c_sample], axis=0), ((0, (-n_c) % SUBLANES), (0, 0)))
    mod_all = _mod_call(c_all, w_mod, b_mod)
    mod_p = mod_all[:bp].reshape(bp, N_MOD, D_MODEL)
    mod_s = mod_all[bp:n_c].reshape(bs, N_MOD, D_MODEL)

    s5_mats = [_s5_weights(s5_lambda_re[l], s5_lambda_im[l], s5_log_dt[l], s5_b_re[l], s5_b_im[l],
                           s5_c_re[l], s5_c_im[l]) for l in range(N_A_LAYERS)]
    wts = (norm_g, ffn_w_in, ffn_w_out, s5_mats, s5_d, s5_w_glu, kv_norm_g, w_kv, attn_w_q, attn_w_o, final_norm_g)

    caches_flat = tuple(c.reshape(bs, c.shape[1] * 2 * KV_HEADS, HEAD_DIM) for c in caches)
    (y_p, re_p, im_p, kv_p), (y_s, re_s, im_s, kv_s) = _trunks(
        x_prompt, x_sample, mod_p, mod_s, (state_s5_re, state_s5_im), caches_flat, wts,
        {"tm": 1024, "tf": 512, "tn_down": 256, "s5_steps": 32})

    kvp = [kv_p[g][:, seq_p - min(w, seq_p):] for g, (w, _) in enumerate(DILATED_PATTERNS)]
    return (y_p, y_s, re_p, im_p, kvp[0], kvp[1], kvp[2], re_s, im_s, kv_s[0], kv_s[1], kv_s[2])
```

```python
import functools

import jax
import jax.numpy as jnp
from jax import lax
from jax.experimental import pallas as pl
from jax.experimental.pallas import tpu as pltpu

F32 = jnp.float32
BF16 = jnp.bfloat16

D_MODEL = 2048
DEPTH = 4
N_A_LAYERS = DEPTH // 2
D_FF = 5632
S5_GROUP = 16
S5_GROUPS = D_MODEL // S5_GROUP
S5_STATE = 64
S5_NSTATE = S5_GROUPS * S5_STATE
HEAD_DIM = 128
HEADS = 16
KV_HEADS = 4
REP = HEADS // KV_HEADS
DILATED_PATTERNS = ((128, 1), (512, 4), (2048, 16))
N_DIL = len(DILATED_PATTERNS)
Q_BLOCK = 128
N_MOD = 9 * DEPTH + 4
EPS = 1e-6
ATTN_SCALE = HEAD_DIM ** -0.5
ATTN_SCALE_LOG2 = ATTN_SCALE * 1.4426950408889634

VMEM_LIMIT_BYTES = 56 * 1024 * 1024
SUBLANES = 8
S5_COL_BLOCK = 1024
S5_CH_BLOCK = S5_COL_BLOCK // S5_STATE * S5_GROUP
S5_N_BLOCKS = S5_NSTATE // S5_COL_BLOCK


def _params(*sem):
    return pltpu.CompilerParams(dimension_semantics=sem, vmem_limit_bytes=VMEM_LIMIT_BYTES)


def _rms_mod(x, g, shift, scale):
    xn = x * lax.rsqrt(jnp.mean(x * x, axis=-1, keepdims=True) + EPS)
    return xn * g * (1.0 + scale) + shift


def _norm_rows(x_ref, g_ref, sh_ref, sc_ref, h_ref, row_chunk):
    rows = x_ref.shape[0]
    g = g_ref[...]
    mod_rows = sh_ref.shape[0]
    if mod_rows == 1:
        gs = g * (1.0 + sc_ref[...])
        sh = sh_ref[...]
        row_chunk = min(row_chunk, 16)

        def body(c, carry):
            r0 = pl.multiple_of(c * row_chunk, row_chunk)
            x = x_ref[pl.ds(r0, row_chunk), :]
            xn = x * lax.rsqrt(jnp.mean(x * x, axis=-1, keepdims=True) + EPS)
            h_ref[pl.ds(r0, row_chunk), :] = (xn * gs + sh).astype(h_ref.dtype)
            return carry

        n_chunks = rows // row_chunk
        lax.fori_loop(0, n_chunks, body, 0, unroll=4 if n_chunks % 4 == 0 else 1)
        return

    def body(c, carry):
        r0 = pl.multiple_of(c * row_chunk, row_chunk)
        x = x_ref[pl.ds(r0, row_chunk), :]
        if mod_rows == rows and rows != row_chunk:
            sh = sh_ref[pl.ds(r0, row_chunk), :]
            sc = sc_ref[pl.ds(r0, row_chunk), :]
        else:
            sh = sh_ref[...]
            sc = sc_ref[...]
        h_ref[pl.ds(r0, row_chunk), :] = _rms_mod(x, g, sh, sc).astype(h_ref.dtype)
        return carry

    lax.fori_loop(0, rows // row_chunk, body, 0)


def _mod_kernel(c_ref, w_ref, b_ref, o_ref):
    c = c_ref[...]
    a = (c * jax.nn.sigmoid(c)).astype(BF16)
    o_ref[...] = jnp.dot(a, w_ref[...].astype(BF16), preferred_element_type=F32) + b_ref[...]


def _mod_call(c_all, w_mod, b_mod):
    rows = c_all.shape[0]
    n = w_mod.shape[1]
    tn = 1024
    return pl.pallas_call(
        _mod_kernel,
        grid=(n // tn,),
        in_specs=[
            pl.BlockSpec((rows, D_MODEL), lambda j: (0, 0)),
            pl.BlockSpec((D_MODEL, tn), lambda j: (0, j)),
            pl.BlockSpec((1, tn), lambda j: (0, j)),
        ],
        out_specs=pl.BlockSpec((rows, tn), lambda j: (0, j)),
        out_shape=jax.ShapeDtypeStruct((rows, n), F32),
        compiler_params=_params("arbitrary"),
        name="adaln_mod",
    )(c_all, w_mod, b_mod.reshape(1, n))


def _norm_two_row_sets(x_ref, sh_ref, sc_ref, x2_ref, sh2_ref, sc2_ref, g_ref, h_ref, row_chunk):
    tm, r2 = x_ref.shape[0], x2_ref.shape[0]

    @pl.when(pl.program_id(1) == 0)
    def _():
        _norm_rows(x_ref, g_ref, sh_ref, sc_ref, h_ref.at[pl.ds(0, tm)], row_chunk)

    @pl.when((pl.program_id(0) == 0) & (pl.program_id(1) == 0))
    def _():
        _norm_rows(x2_ref, g_ref, sh2_ref, sc2_ref, h_ref.at[pl.ds(tm, r2)], r2)


def _ffn_up_kernel(x_ref, sh_ref, sc_ref, x2_ref, sh2_ref, sc2_ref, g_ref, wg_ref, wu_ref, a_ref, a2_ref, h_ref,
                   *, row_chunk):
    tm = x_ref.shape[0]
    _norm_two_row_sets(x_ref, sh_ref, sc_ref, x2_ref, sh2_ref, sc2_ref, g_ref, h_ref, row_chunk)
    def act(h):
        g = jnp.dot(h, wg_ref[...].astype(BF16), preferred_element_type=F32)
        u = jnp.dot(h, wu_ref[...].astype(BF16), preferred_element_type=F32)
        return (g * jax.nn.sigmoid(g) * u).astype(a_ref.dtype)

    @pl.when(pl.program_id(0) == 0)
    def _():
        a = act(h_ref[...])
        a_ref[...] = a[:tm]
        a2_ref[...] = a[tm:]

    @pl.when(pl.program_id(0) != 0)
    def _():
        a_ref[...] = act(h_ref[pl.ds(0, tm), :])


def _ffn_call(x, mods, x2, mods2, g_norm, w_in, w_out, layer, which, tm, tf, tn):
    m, r2 = x.shape[0], x2.shape[0]
    nb = mods[0].shape[0]
    tiles_per_seq = (m // nb) // tm
    nk = D_FF // tf
    mod_spec = pl.BlockSpec((None, 1, D_MODEL), lambda i, k: (i // tiles_per_seq, 0, 0))
    full2 = pl.BlockSpec((r2, D_MODEL), lambda i, k: (0, 0))
    a, a2 = pl.pallas_call(
        functools.partial(_ffn_up_kernel, row_chunk=min(tm, 64)),
        grid=(m // tm, nk),
        in_specs=[
            pl.BlockSpec((tm, D_MODEL), lambda i, k: (i, 0)),
            mod_spec, mod_spec,
            full2, full2, full2,
            pl.BlockSpec((1, D_MODEL), lambda i, k: (0, 0)),
            pl.BlockSpec((None, None, D_MODEL, tf), lambda i, k: (layer, which, 0, k)),
            pl.BlockSpec((None, None, D_MODEL, tf), lambda i, k: (layer, which, 0, k + nk)),
        ],
        out_specs=(pl.BlockSpec((tm, tf), lambda i, k: (i, k)),
                   pl.BlockSpec((r2, tf), lambda i, k: (0, jnp.where(i == 0, k, nk - 1)))),
        out_shape=(jax.ShapeDtypeStruct((m, D_FF), BF16), jax.ShapeDtypeStruct((r2, D_FF), BF16)),
        scratch_shapes=[pltpu.VMEM((tm + r2, D_MODEL), BF16)],
        compiler_params=_params("arbitrary", "arbitrary"),
        name="ffn_up",
    )(x, mods[0], mods[1], x2, mods2[0], mods2[1], g_norm, w_in, w_in)
    out = _proj_residual_call(a, x, mods[2], w_out, (layer, which), False, tm, tn, gate_scale=0.5, name="ffn_down")
    out2 = _proj_residual_call(a2, x2, mods2[2].reshape(1, r2, D_MODEL), w_out, (layer, which), False, r2, tn,
                               gate_scale=0.5, name="ffn_down")
    return out, out2


def _norm_matmul_kernel(x_ref, sh_ref, sc_ref, x2_ref, sh2_ref, sc2_ref, g_ref, w_ref, *rest, row_chunk, head_rows):
    h_ref = rest[-1]
    outs = rest[:-1]
    per_set = len(outs) // 2
    tm = x_ref.shape[0]
    _norm_two_row_sets(x_ref, sh_ref, sc_ref, x2_ref, sh2_ref, sc2_ref, g_ref, h_ref, row_chunk)
    def emit(res, refs):
        refs[0][...] = res.astype(refs[0].dtype)
        if head_rows:
            rows, cols = res.shape
            heads_per_tile = cols // HEAD_DIM
            first = (pl.program_id(1) % (head_rows // heads_per_tile)) * heads_per_tile
            for c in range(heads_per_tile):
                refs[1][pl.ds(first + c, rows, stride=head_rows), :] = res[:, c * HEAD_DIM:(c + 1) * HEAD_DIM]

    @pl.when(pl.program_id(0) == 0)
    def _():
        res = jnp.dot(h_ref[...], w_ref[...].astype(BF16), preferred_element_type=F32)
        emit(res[:tm], outs[:per_set])
        emit(res[tm:], outs[per_set:])

    @pl.when(pl.program_id(0) != 0)
    def _():
        emit(jnp.dot(h_ref[pl.ds(0, tm), :], w_ref[...].astype(BF16), preferred_element_type=F32),
             outs[:per_set])


def _norm_matmul_call(x, shift, scale, x2, shift2, scale2, g_norm, w, w_lead, out_dtype, tm, tn, head_rows=0):
    m, r2 = x.shape[0], x2.shape[0]
    nb = shift.shape[0]
    n = w.shape[-1]
    nj = n // tn
    tiles_per_seq = (m // nb) // tm
    mod_spec = pl.BlockSpec((None, 1, D_MODEL), lambda i, j: (i // tiles_per_seq, 0, 0))
    full2 = pl.BlockSpec((r2, D_MODEL), lambda i, j: (0, 0))

    def col2(i, j):
        return jnp.where(i == 0, j, nj - 1)

    if head_rows:
        slab = head_rows * HEAD_DIM
        tps = slab // tn
        out_spec = (pl.BlockSpec((None, tm, tn), lambda i, j: (j // tps, i, j % tps)),
                    pl.BlockSpec((None, tm * head_rows, HEAD_DIM), lambda i, j: (j // tps, i, 0)),
                    pl.BlockSpec((None, r2, tn), lambda i, j: (col2(i, j) // tps, 0, col2(i, j) % tps)),
                    pl.BlockSpec((None, r2 * head_rows, HEAD_DIM), lambda i, j: (col2(i, j) // tps, 0, 0)))
        out_shape = (jax.ShapeDtypeStruct((n // slab, m, slab), out_dtype),
                     jax.ShapeDtypeStruct((n // slab, m * head_rows, HEAD_DIM), out_dtype),
                     jax.ShapeDtypeStruct((n // slab, r2, slab), out_dtype),
                     jax.ShapeDtypeStruct((n // slab, r2 * head_rows, HEAD_DIM), out_dtype))
    else:
        out_spec = (pl.BlockSpec((tm, tn), lambda i, j: (i, j)),
                    pl.BlockSpec((r2, tn), lambda i, j: (0, col2(i, j))))
        out_shape = (jax.ShapeDtypeStruct((m, n), out_dtype), jax.ShapeDtypeStruct((r2, n), out_dtype))
    return pl.pallas_call(
        functools.partial(_norm_matmul_kernel, row_chunk=min(tm, 64), head_rows=head_rows),
        grid=(m // tm, nj),
        in_specs=[
            pl.BlockSpec((tm, D_MODEL), lambda i, j: (i, 0)),
            mod_spec, mod_spec,
            full2, full2, full2,
            pl.BlockSpec((1, D_MODEL), lambda i, j: (0, 0)),
            pl.BlockSpec((None,) * len(w_lead) + (D_MODEL, tn), lambda i, j: (*w_lead, 0, j)),
        ],
        out_specs=out_spec,
        out_shape=out_shape,
        scratch_shapes=[pltpu.VMEM((tm + r2, D_MODEL), BF16)],
        compiler_params=_params("arbitrary", "arbitrary"),
        name="norm_matmul",
    )(x, shift, scale, x2, shift2, scale2, g_norm, w)


def _proj_residual_kernel(z_ref, x_ref, gt_ref, *refs, glu, gate_scale):
    z = z_ref[...]
    if glu:
        wa_ref, wg_ref, o_ref = refs
        a = jnp.dot(z, wa_ref[...].astype(BF16), preferred_element_type=F32)
        g = jnp.dot(z, wg_ref[...].astype(BF16), preferred_element_type=F32)
        y = a * jax.nn.sigmoid(g)
    else:
        w_ref, o_ref = refs
        y = jnp.dot(z, w_ref[...].astype(BF16), preferred_element_type=F32)
    gate = gt_ref[...] if gate_scale == 1.0 else gate_scale * gt_ref[...]
    o_ref[...] = x_ref[...] + gate * y


def _proj_residual_call(z, x, gate, w, w_lead, glu, tm, tn, gate_scale=1.0, name="proj_residual", z_buffers=2):
    m = x.shape[0]
    kdim = z.shape[1]
    nb, r, _ = gate.shape
    tiles_per_seq = (m // nb) // tm if r == 1 else 1
    n_blocks = D_MODEL // tn
    w_block = (None,) * len(w_lead) + (kdim, tn)
    w_specs = [pl.BlockSpec(w_block, lambda i, j: (*w_lead, 0, j))]
    w_args = [w]
    if glu:
        w_specs.append(pl.BlockSpec(w_block, lambda i, j: (*w_lead, 0, j + n_blocks)))
        w_args.append(w)
    return pl.pallas_call(
        functools.partial(_proj_residual_kernel, glu=glu, gate_scale=gate_scale),
        grid=(m // tm, n_blocks),
        in_specs=[
            (pl.BlockSpec((tm, kdim), lambda i, j: (i, 0)) if z_buffers == 2 else
             pl.BlockSpec((tm, kdim), lambda i, j: (i, 0), pipeline_mode=pl.Buffered(z_buffers))),
            pl.BlockSpec((tm, tn), lambda i, j: (i, j)),
            pl.BlockSpec((None, r, tn), lambda i, j: (i // tiles_per_seq, 0, j)),
        ] + w_specs,
        out_specs=pl.BlockSpec((tm, tn), lambda i, j: (i, j)),
        out_shape=jax.ShapeDtypeStruct((m, D_MODEL), F32),
        compiler_params=_params("parallel", "arbitrary"),
        name=name,
    )(z, x, gate, *w_args)


def _final_norm_kernel(x_ref, sh_ref, sc_ref, g_ref, o_ref, *, row_chunk):
    _norm_rows(x_ref, g_ref, sh_ref, sc_ref, o_ref, row_chunk)


def _final_norm_call(x, shift, scale, g_norm, tm):
    m = x.shape[0]
    nb, r, _ = shift.shape
    tiles_per_seq = (m // nb) // tm if r == 1 else 1
    mod_spec = pl.BlockSpec((None, r, D_MODEL), lambda i: (i // tiles_per_seq, 0, 0))
    return pl.pallas_call(
        functools.partial(_final_norm_kernel, row_chunk=min(tm, 64)),
        grid=(m // tm,),
        in_specs=[
            pl.BlockSpec((tm, D_MODEL), lambda i: (i, 0)),
            mod_spec, mod_spec,
            pl.BlockSpec((1, D_MODEL), lambda i: (0, 0)),
        ],
        out_specs=pl.BlockSpec((tm, D_MODEL), lambda i: (i, 0)),
        out_shape=jax.ShapeDtypeStruct((m, D_MODEL), F32),
        compiler_params=_params("parallel"),
        name="final_norm",
    )(x, shift, scale, g_norm)


def _s5_disc_kernel(lre_ref, lim_ref, ldt_ref, bre_ref, bim_ref, are_ref, aim_ref, bbre_ref, bbim_ref):
    lam_re = lre_ref[...]
    lam_im = lim_ref[...]
    dt = jnp.exp(ldt_ref[...])
    decay = jnp.exp(lam_re * dt)
    ab_re = decay * jnp.cos(lam_im * dt)
    ab_im = decay * jnp.sin(lam_im * dt)
    den = lam_re * lam_re + lam_im * lam_im
    f_re = ((ab_re - 1.0) * lam_re + ab_im * lam_im) / den
    f_im = (ab_im * lam_re - (ab_re - 1.0) * lam_im) / den
    b_re = bre_ref[...]
    b_im = bim_ref[...]
    are_ref[...] = ab_re
    aim_ref[...] = ab_im
    bbre_ref[...] = f_re * b_re - f_im * b_im
    bbim_ref[...] = f_re * b_im + f_im * b_re


def _s5_disc_call(lam_re, lam_im, log_dt, b_re, b_im):
    shape = jax.ShapeDtypeStruct(lam_re.shape, F32)
    return pl.pallas_call(
        _s5_disc_kernel,
        out_shape=(shape, shape, shape, shape),
        name="s5_discretise",
    )(lam_re, lam_im, log_dt, b_re, b_im)


def _s5_kernel(x_ref, sh_ref, sc_ref, g_ref, bbre_ref, bbim_ref, are_ref, aim_ref, cre_ref, cim_ref,
               d_ref, h0re_ref, h0im_ref, z_ref, hre_out, him_out,
               u_scr, sre_scr, sim_scr, y_scr, pwre_scr, pwim_scr, *, row_chunk):
    t = pl.program_id(0)
    nb, steps, _ = x_ref.shape
    rows = steps * nb
    lane = HEAD_DIM
    seg = SUBLANES // nb
    seg_len = steps // seg
    assert seg in (1, 2)

    @pl.when(t == 0)
    def _():
        hre_out[...] = h0re_ref[...]
        him_out[...] = h0im_ref[...]
        if seg > 1:
            zero = jnp.zeros((nb, S5_COL_BLOCK), F32)
            for cb in range(S5_N_BLOCKS):
                a_re = jnp.broadcast_to(are_ref[cb], (nb, S5_COL_BLOCK))
                a_im = jnp.broadcast_to(aim_ref[cb], (nb, S5_COL_BLOCK))
                p_re, p_im = a_re, a_im
                for i in range(seg_len):
                    pwre_scr[cb, i * SUBLANES:(i + 1) * SUBLANES, :] = jnp.concatenate([zero, p_re], axis=0)
                    pwim_scr[cb, i * SUBLANES:(i + 1) * SUBLANES, :] = jnp.concatenate([zero, p_im], axis=0)
                    p_re, p_im = p_re * a_re - p_im * a_im, p_re * a_im + p_im * a_re

    g = g_ref[...]
    for b in range(nb):
        for s in range(seg):
            def norm_chunk(c, carry, b=b, s=s):
                r0 = pl.multiple_of(c * row_chunk, row_chunk)
                u = _rms_mod(x_ref[b, pl.ds(s * seg_len + r0, row_chunk), :], g, sh_ref[b], sc_ref[b])
                for j in range(D_MODEL // lane):
                    u_scr[j, pl.ds(r0 * SUBLANES + s * nb + b, row_chunk, stride=SUBLANES), :] = \
                        u[:, j * lane:(j + 1) * lane]
                return carry

            lax.fori_loop(0, seg_len // row_chunk, norm_chunk, 0)

    tiles_per_block = S5_CH_BLOCK // lane
    per_block_scratch = sre_scr.shape[0] == S5_N_BLOCKS
    for cb in range(S5_N_BLOCKS):
        ch = slice(cb * S5_CH_BLOCK, (cb + 1) * S5_CH_BLOCK)
        st = slice(cb * S5_COL_BLOCK, (cb + 1) * S5_COL_BLOCK)
        slab = cb if per_block_scratch else 0
        sre, sim, ys = sre_scr.at[slab], sim_scr.at[slab], y_scr.at[slab]
        u_blk = jnp.concatenate([u_scr[cb * tiles_per_block + j] for j in range(tiles_per_block)], axis=1)
        u_bf = u_blk.astype(BF16)
        sre[...] = jnp.dot(u_bf, bbre_ref[cb], preferred_element_type=F32)
        sim[...] = jnp.dot(u_bf, bbim_ref[cb], preferred_element_type=F32)

        a_re = jnp.broadcast_to(are_ref[cb], (SUBLANES, S5_COL_BLOCK))
        a_im = jnp.broadcast_to(aim_ref[cb], (SUBLANES, S5_COL_BLOCK))

        def tile_step(i, carry, a_re=a_re, a_im=a_im, sre=sre, sim=sim):
            h_re, h_im = carry
            r0 = i * SUBLANES if isinstance(i, int) else pl.multiple_of(i * SUBLANES, SUBLANES)
            n_re = a_re * h_re - a_im * h_im + sre[pl.ds(r0, SUBLANES), :]
            n_im = a_re * h_im + a_im * h_re + sim[pl.ds(r0, SUBLANES), :]
            sre[pl.ds(r0, SUBLANES), :] = n_re
            sim[pl.ds(r0, SUBLANES), :] = n_im
            return n_re, n_im

        def first_rows(v):
            return jnp.concatenate([v[:nb]] * seg, axis=0)

        h_re, h_im = hre_out[:, st], him_out[:, st]
        if seg > 1:
            zero = jnp.zeros((SUBLANES - nb, S5_COL_BLOCK), F32)
            h_re, h_im = jnp.concatenate([h_re, zero], axis=0), jnp.concatenate([h_im, zero], axis=0)
        carry = (h_re, h_im)
        if per_block_scratch:
            for i in range(seg_len):
                carry = tile_step(i, carry)
        else:
            carry = lax.fori_loop(0, seg_len, tile_step, carry, unroll=min(seg_len, 4))
        h_re, h_im = carry
        if seg > 1:
            c_re, c_im = first_rows(h_re), first_rows(h_im)
            for i in range(seg_len):
                rows_i = slice(i * SUBLANES, (i + 1) * SUBLANES)
                p_re, p_im = pwre_scr[cb, rows_i, :], pwim_scr[cb, rows_i, :]
                f_re = sre[rows_i, :] + (p_re * c_re - p_im * c_im)
                f_im = sim[rows_i, :] + (p_re * c_im + p_im * c_re)
                sre[rows_i, :] = f_re
                sim[rows_i, :] = f_im
            h_re, h_im = f_re[SUBLANES - nb:], f_im[SUBLANES - nb:]
        hre_out[:, st] = h_re
        him_out[:, st] = h_im

        y = (jnp.dot(sre[...].astype(BF16), cre_ref[cb], preferred_element_type=F32)
             - jnp.dot(sim[...].astype(BF16), cim_ref[cb], preferred_element_type=F32))
        zf = jax.nn.gelu(y + d_ref[:, ch] * u_blk)
        for j in range(tiles_per_block):
            ys[j] = zf[:, j * lane:(j + 1) * lane]
        for b in range(nb):
            for s in range(seg):
                for j in range(tiles_per_block):
                    c0 = cb * S5_CH_BLOCK + j * lane
                    z_ref[b, s * seg_len:(s + 1) * seg_len, c0:c0 + lane] = \
                        ys[j, pl.ds(s * nb + b, seg_len, stride=SUBLANES), :].astype(z_ref.dtype)


def _s5_call(x, shift, scale, g_norm, bb_re, bb_im, a_re, a_im, c_re, c_im, d_skip,
             h0_re, h0_im, steps_per_chunk, z_dtype, per_block_scratch):
    nb, seq, _ = x.shape
    assert SUBLANES % nb == 0 and seq % steps_per_chunk == 0
    tr = steps_per_chunk * nb
    slabs = S5_N_BLOCKS if per_block_scratch else 1
    seg = SUBLANES // nb
    pw_rows = tr if seg > 1 else SUBLANES
    row_chunk = min(steps_per_chunk // (SUBLANES // nb), 64)
    const2 = lambda t: (0, 0)
    const3 = lambda t: (0, 0, 0)
    state_spec = pl.BlockSpec((nb, S5_NSTATE), const2)
    io_spec = pl.BlockSpec((nb, steps_per_chunk, D_MODEL), lambda t: (0, t, 0))
    return pl.pallas_call(
        functools.partial(_s5_kernel, row_chunk=row_chunk),
        grid=(seq // steps_per_chunk,),
        in_specs=[
            io_spec,
            pl.BlockSpec((nb, 1, D_MODEL), const3),
            pl.BlockSpec((nb, 1, D_MODEL), const3),
            pl.BlockSpec((1, D_MODEL), const2),
            pl.BlockSpec((S5_N_BLOCKS, S5_CH_BLOCK, S5_COL_BLOCK), const3, pipeline_mode=pl.Buffered(1)),
            pl.BlockSpec((S5_N_BLOCKS, S5_CH_BLOCK, S5_COL_BLOCK), const3, pipeline_mode=pl.Buffered(1)),
            pl.BlockSpec((S5_N_BLOCKS, 1, S5_COL_BLOCK), const3),
            pl.BlockSpec((S5_N_BLOCKS, 1, S5_COL_BLOCK), const3),
            pl.BlockSpec((S5_N_BLOCKS, S5_COL_BLOCK, S5_CH_BLOCK), const3, pipeline_mode=pl.Buffered(1)),
            pl.BlockSpec((S5_N_BLOCKS, S5_COL_BLOCK, S5_CH_BLOCK), const3, pipeline_mode=pl.Buffered(1)),
            pl.BlockSpec((1, D_MODEL), const2),
            state_spec, state_spec,
        ],
        out_specs=(io_spec, state_spec, state_spec),
        out_shape=(jax.ShapeDtypeStruct((nb, seq, D_MODEL), z_dtype),
                   jax.ShapeDtypeStruct((nb, S5_NSTATE), F32),
                   jax.ShapeDtypeStruct((nb, S5_NSTATE), F32)),
        scratch_shapes=[pltpu.VMEM((D_MODEL // HEAD_DIM, tr, HEAD_DIM), F32),
                        pltpu.VMEM((slabs, tr, S5_COL_BLOCK), F32),
                        pltpu.VMEM((slabs, tr, S5_COL_BLOCK), F32),
                        pltpu.VMEM((slabs, S5_CH_BLOCK // HEAD_DIM, tr, HEAD_DIM), F32),
                        pltpu.VMEM((S5_N_BLOCKS, pw_rows, S5_COL_BLOCK), F32),
                        pltpu.VMEM((S5_N_BLOCKS, pw_rows, S5_COL_BLOCK), F32)],
        compiler_params=_params("arbitrary"),
        name="s5_mixer",
    )(x, shift, scale, g_norm, bb_re, bb_im, a_re, a_im, c_re, c_im, d_skip, h0_re, h0_im)


def _softmax_pv(s2, v_bf):
    m = jnp.max(s2, axis=-1, keepdims=True)
    p = jnp.exp2(s2 - m)
    l = jnp.sum(p, axis=-1, keepdims=True)
    o = jnp.dot(p.astype(BF16), v_bf, preferred_element_type=F32) / l
    return o, jnp.broadcast_to(m + jnp.log2(l), o.shape)


def _qk(q_bf, k_bf, scale=ATTN_SCALE):
    return lax.dot_general(q_bf, k_bf, (((1,), (1,)), ((), ())), preferred_element_type=F32) * scale


def _attn_prompt_kernel(q0_ref, q1_ref, q2_ref, k0_ref, v0_ref, k1_ref, v1_ref, k2_ref, v2_ref, o_ref,
                        qf_scr, o1_scr, l1_scr, o2_scr, l2_scr, band_scr, first_scr):
    seq = q0_ref.shape[0]
    qb = Q_BLOCK
    row = lax.broadcasted_iota(jnp.int32, (REP * qb, 2 * qb), 0) & (qb - 1)
    col = lax.broadcasted_iota(jnp.int32, (REP * qb, 2 * qb), 1)
    band_scr[...] = jnp.where((col >= row) & (col <= row + qb), 0.0, -jnp.inf)
    row1 = lax.broadcasted_iota(jnp.int32, (REP * qb, qb), 0) & (qb - 1)
    col1 = lax.broadcasted_iota(jnp.int32, (REP * qb, qb), 1)
    first_scr[...] = jnp.where(col1 <= row1, 0.0, -jnp.inf)

    def rows_of(start, d):
        return pl.ds(start, qb) if d == 1 else pl.ds(start, qb, stride=d)

    def load_q(src_ref, start, d):
        if d == 1:
            parts = [src_ref[pl.ds(start, qb), e * HEAD_DIM:(e + 1) * HEAD_DIM] for e in range(REP)]
        else:
            parts = [src_ref[e, rows_of(start, d), :] for e in range(REP)]
        return jnp.concatenate(parts, axis=0).astype(BF16)

    def unit(src_q, k_ref, v_ref, start, d, first):
        q4 = load_q(src_q, start, d)
        k_cur = k_ref[rows_of(start, d), :]
        v_cur = v_ref[rows_of(start, d), :]
        if first:
            s = _qk(q4, k_cur.astype(BF16), ATTN_SCALE_LOG2) + first_scr[...]
            return _softmax_pv(s, v_cur.astype(BF16))
        prev = start - qb * d
        k_band = jnp.concatenate([k_ref[rows_of(prev, d), :], k_cur], axis=0)
        v_band = jnp.concatenate([v_ref[rows_of(prev, d), :], v_cur], axis=0)
        s = _qk(q4, k_band.astype(BF16), ATTN_SCALE_LOG2) + band_scr[...]
        return _softmax_pv(s, v_band.astype(BF16))

    def stage_group(q_ref, k_ref, v_ref, d, o_scr, l_scr):
        n_blocks = seq // (qb * d)

        def copy(c, carry):
            r0 = pl.multiple_of(c * 256, 256)
            for e in range(REP):
                qf_scr[e, pl.ds(r0, 256), :] = q_ref[pl.ds(r0, 256), e * HEAD_DIM:(e + 1) * HEAD_DIM].astype(F32)
            return carry

        lax.fori_loop(0, seq // 256, copy, 0)

        def store(start, o, lse):
            for e in range(REP):
                sl = slice(e * qb, (e + 1) * qb)
                o_scr[e, rows_of(start, d), :] = o[sl]
                l_scr[e, rows_of(start, d), :] = lse[sl]

        def per_class(r, carry):
            store(r, *unit(qf_scr, k_ref, v_ref, r, d, True))

            def per_block(jb, c2):
                start = jb * (qb * d) + r
                store(start, *unit(qf_scr, k_ref, v_ref, start, d, False))
                return c2

            if n_blocks > 1:
                lax.fori_loop(1, n_blocks, per_block, 0, unroll=True)
            return carry

        lax.fori_loop(0, d, per_class, 0, unroll=4 if n_blocks == 1 else 2)

    stage_group(q2_ref, k2_ref, v2_ref, DILATED_PATTERNS[2][1], o2_scr, l2_scr)
    stage_group(q1_ref, k1_ref, v1_ref, DILATED_PATTERNS[1][1], o1_scr, l1_scr)

    def merge(start, o0, lse0):
        for e in range(REP):
            sl = slice(e * qb, (e + 1) * qb)
            cols = slice(e * HEAD_DIM, (e + 1) * HEAD_DIM)
            la = lse0[sl]
            lb = l1_scr[e, pl.ds(start, qb), :]
            lc = l2_scr[e, pl.ds(start, qb), :]
            mx = jnp.maximum(jnp.maximum(la, lb), lc)
            wa = jnp.exp2(la - mx)
            wb = jnp.exp2(lb - mx)
            wc = jnp.exp2(lc - mx)
            inv = 1.0 / (wa + wb + wc)
            acc = (wa * o0[sl] + wb * o1_scr[e, pl.ds(start, qb), :] + wc * o2_scr[e, pl.ds(start, qb), :]) * inv
            o_ref[pl.ds(start, qb), cols] = acc.astype(o_ref.dtype)

    merge(0, *unit(q0_ref, k0_ref, v0_ref, 0, 1, True))

    def per_block0(jb, carry):
        start = pl.multiple_of(jb * qb, qb)
        merge(start, *unit(q0_ref, k0_ref, v0_ref, start, 1, False))
        return carry

    n_blocks0 = seq // qb
    lax.fori_loop(1, n_blocks0, per_block0, 0, unroll=3 if (n_blocks0 - 1) % 3 == 0 else 1)


def _attn_prompt_call(q, kv):
    b, seq, _ = q.shape
    qw = REP * HEAD_DIM
    q_specs = [pl.BlockSpec((None, seq, qw), functools.partial(lambda bi, h, g: (bi, 0, g * KV_HEADS + h), g=g))
               for g in range(N_DIL)]
    kv_specs = []
    for g in range(N_DIL):
        kv_specs.append(pl.BlockSpec((None, None, seq, HEAD_DIM),
                                     functools.partial(lambda bi, h, g: (g, bi, 0, h), g=g)))
        kv_specs.append(pl.BlockSpec((None, None, seq, HEAD_DIM),
                                     functools.partial(lambda bi, h, g: (g, bi, 0, KV_HEADS + h), g=g)))
    kv_args = [kv] * (2 * N_DIL)
    return pl.pallas_call(
        _attn_prompt_kernel,
        grid=(b, KV_HEADS),
        in_specs=q_specs + kv_specs,
        out_specs=pl.BlockSpec((None, seq, qw), lambda bi, h: (bi, 0, h)),
        out_shape=jax.ShapeDtypeStruct((b, seq, HEADS * HEAD_DIM), BF16),
        scratch_shapes=[pltpu.VMEM((REP, seq, HEAD_DIM), F32),
                        pltpu.VMEM((REP, seq, HEAD_DIM), F32), pltpu.VMEM((REP, seq, HEAD_DIM), F32),
                        pltpu.VMEM((REP, seq, HEAD_DIM), F32), pltpu.VMEM((REP, seq, HEAD_DIM), F32),
                        pltpu.VMEM((REP * Q_BLOCK, 2 * Q_BLOCK), F32), pltpu.VMEM((REP * Q_BLOCK, Q_BLOCK), F32)],
        compiler_params=_params("parallel", "arbitrary"),
        name="dilated_attention_prompt",
    )(q, q, q, *kv_args)


def _attn_sample_kernel(q_ref, kvn_ref, c0_ref, c1_ref, c2_ref, o_ref, *, n_tok):
    rows = REP * n_tok
    cache_refs = (c0_ref, c1_ref, c2_ref)
    kv_rows = 2 * KV_HEADS
    for h in range(KV_HEADS):
        outs, lses = [], []
        for g, (_, d) in enumerate(DILATED_PATTERNS):
            q = q_ref[g * KV_HEADS + h]
            lw = cache_refs[g].shape[0] // kv_rows
            k_c = cache_refs[g][pl.ds(h, lw, stride=kv_rows), :].astype(BF16)
            v_c = cache_refs[g][pl.ds(KV_HEADS + h, lw, stride=kv_rows), :].astype(BF16)
            k_n = kvn_ref[g, pl.ds(h, n_tok, stride=kv_rows), :].astype(BF16)
            v_n = kvn_ref[g, pl.ds(KV_HEADS + h, n_tok, stride=kv_rows), :].astype(BF16)
            tq_c = lax.broadcasted_iota(jnp.int32, (rows, lw), 0) & (n_tok - 1)
            idx_c = lax.broadcasted_iota(jnp.int32, (rows, lw), 1)
            ok_c = (idx_c >= tq_c) & (((idx_c - tq_c) & (d - 1)) == 0)
            tq_n = lax.broadcasted_iota(jnp.int32, (rows, n_tok), 0) & (n_tok - 1)
            idx_n = lax.broadcasted_iota(jnp.int32, (rows, n_tok), 1)
            ok_n = (idx_n <= tq_n) & (((tq_n - idx_n) & (d - 1)) == 0)
            s_c = jnp.where(ok_c, _qk(q, k_c), -jnp.inf)
            s_n = jnp.where(ok_n, _qk(q, k_n), -jnp.inf)
            m = jnp.maximum(jnp.max(s_c, axis=-1, keepdims=True), jnp.max(s_n, axis=-1, keepdims=True))
            p_c = jnp.exp(s_c - m)
            p_n = jnp.exp(s_n - m)
            l = jnp.sum(p_c, axis=-1, keepdims=True) + jnp.sum(p_n, axis=-1, keepdims=True)
            o = (jnp.dot(p_c.astype(BF16), v_c, preferred_element_type=F32)
                 + jnp.dot(p_n.astype(BF16), v_n, preferred_element_type=F32)) / l
            outs.append(o)
            lses.append(m + jnp.log(l))
        mx = jnp.maximum(jnp.maximum(lses[0], lses[1]), lses[2])
        w = [jnp.exp(x - mx) for x in lses]
        inv = 1.0 / (w[0] + w[1] + w[2])
        acc = (w[0] * inv) * outs[0] + (w[1] * inv) * outs[1] + (w[2] * inv) * outs[2]
        o_ref[h] = acc.astype(o_ref.dtype)


def _attn_sample_call(q_heads, kv_new, caches, n_tok):
    b = q_heads.shape[0]
    rows = REP * n_tok
    assert n_tok & (n_tok - 1) == 0
    in_specs = [pl.BlockSpec((None, N_DIL * KV_HEADS, rows, HEAD_DIM), lambda bi: (bi, 0, 0, 0)),
                pl.BlockSpec((N_DIL, None, kv_new.shape[2], HEAD_DIM), lambda bi: (0, bi, 0, 0))]
    in_specs += [pl.BlockSpec((None, c.shape[1], HEAD_DIM), lambda bi: (bi, 0, 0)) for c in caches]
    return pl.pallas_call(
        functools.partial(_attn_sample_kernel, n_tok=n_tok),
        grid=(b,),
        in_specs=in_specs,
        out_specs=pl.BlockSpec((None, KV_HEADS, rows, HEAD_DIM), lambda bi: (bi, 0, 0, 0)),
        out_shape=jax.ShapeDtypeStruct((b, KV_HEADS, rows, HEAD_DIM), BF16),
        compiler_params=_params("parallel"),
        name="dilated_attention_sample",
    )(q_heads, kv_new, *caches)


def _s5_weights(lam_re, lam_im, log_dt, b_re, b_im, c_re, c_im):
    g, p, c = S5_GROUPS, S5_STATE, S5_GROUP
    rep = lambda a: jnp.repeat(a, c, axis=1)
    a_re, a_im, bb_re, bb_im = _s5_disc_call(rep(lam_re), rep(lam_im), log_dt.reshape(g, 1),
                                             b_re.reshape(g, p * c), b_im.reshape(g, p * c))
    gl = S5_CH_BLOCK // c

    def block_diag(t, rows_per_group, cols_per_group):
        tiled = jnp.tile(t, (1, 1, gl))
        row_g = lax.broadcasted_iota(jnp.int32, tiled.shape, 1) // rows_per_group
        col_g = lax.broadcasted_iota(jnp.int32, tiled.shape, 2) // cols_per_group
        return jnp.where(row_g == col_g, tiled, 0.0).astype(BF16)

    def in_blocks(bb):
        t = bb.reshape(S5_N_BLOCKS, gl, p, c).transpose(0, 1, 3, 2)
        return block_diag(t.reshape(S5_N_BLOCKS, gl * c, p), c, p)

    def out_blocks(cm):
        t = cm.reshape(S5_N_BLOCKS, gl, c, p).transpose(0, 1, 3, 2)
        return block_diag(t.reshape(S5_N_BLOCKS, gl * p, c), p, c)

    lam_bar = lambda a: a.reshape(g, p, c)[:, :, 0].reshape(S5_N_BLOCKS, 1, S5_COL_BLOCK)
    return (in_blocks(bb_re), in_blocks(bb_im), lam_bar(a_re), lam_bar(a_im),
            out_blocks(c_re), out_blocks(c_im))


def _trunks(x_p, x_s, mod_p, mod_s, s5_state, kv_caches, wts, tiles):
    (norm_g, ffn_w_in, ffn_w_out, s5_mats, s5_d, s5_w_glu, kv_norm_g, w_kv,
     attn_w_q, attn_w_o, final_norm_g) = wts
    bp, lp, _ = x_p.shape
    bs, ls, _ = x_s.shape
    mp, ms = bp * lp, bs * ls
    tm = tiles["tm"]
    kv_rows = 2 * KV_HEADS

    def mods_p(*ks):
        return tuple(mod_p[:, k].reshape(bp, 1, D_MODEL) for k in ks)

    def mods_s(*ks):
        return tuple(jnp.repeat(mod_s[:, k], ls, axis=0) for k in ks)

    def as_block(a):
        return a.reshape(1, *a.shape)

    def norm_g_row(a):
        return a.reshape(1, D_MODEL)

    xp = x_p.reshape(mp, D_MODEL)
    xs = x_s.reshape(ms, D_MODEL)
    re_p, im_p, re_s, im_s = [], [], [], []
    for layer in range(DEPTH):
        base = 9 * layer
        if layer == N_A_LAYERS:
            k0 = 9 * DEPTH
            kvf_p, kv_p, _, kv_s = _norm_matmul_call(xp, *mods_p(k0, k0 + 1), xs, *mods_s(k0, k0 + 1),
                                                     norm_g_row(kv_norm_g), w_kv, (), F32, tm,
                                                     KV_HEADS * HEAD_DIM, head_rows=kv_rows)
        xp, xs = _ffn_call(xp, mods_p(base, base + 1, base + 2), xs, mods_s(base, base + 1, base + 2),
                           norm_g_row(norm_g[layer, 0]), ffn_w_in, ffn_w_out, layer, 0, tm, tiles["tf"],
                           tiles["tn_down"])
        g_mix = norm_g_row(norm_g[layer, 1])
        if layer < N_A_LAYERS:
            d_skip = s5_d[layer].reshape(1, D_MODEL)
            zero_state = jnp.zeros((bp, S5_NSTATE), F32)
            zp, h_re, h_im = _s5_call(xp.reshape(bp, lp, D_MODEL), *mods_p(base + 3, base + 4), g_mix,
                                      *s5_mats[layer], d_skip, zero_state, zero_state,
                                      min(lp, tiles["s5_steps"]), BF16, True)
            re_p.append(h_re.reshape(bp, S5_GROUPS, S5_STATE))
            im_p.append(h_im.reshape(bp, S5_GROUPS, S5_STATE))
            zs, h_re, h_im = _s5_call(xs.reshape(bs, ls, D_MODEL), mod_s[:, base + 3].reshape(bs, 1, D_MODEL),
                                      mod_s[:, base + 4].reshape(bs, 1, D_MODEL), g_mix, *s5_mats[layer], d_skip,
                                      s5_state[0][layer].reshape(bs, S5_NSTATE),
                                      s5_state[1][layer].reshape(bs, S5_NSTATE), ls, F32, False)
            re_s.append(h_re.reshape(bs, S5_GROUPS, S5_STATE))
            im_s.append(h_im.reshape(bs, S5_GROUPS, S5_STATE))
            xp = _proj_residual_call(zp.reshape(mp, D_MODEL), xp, *mods_p(base + 5), s5_w_glu, (layer,),
                                     True, tm, 512, name="s5_glu")
            xs = _proj_residual_call(zs.reshape(ms, D_MODEL).astype(BF16), xs, as_block(*mods_s(base + 5)),
                                     s5_w_glu, (layer,), True, ms, 512, name="s5_glu")
        else:
            bl = layer - N_A_LAYERS
            q_p, q_s = _norm_matmul_call(xp, *mods_p(base + 3, base + 4), xs, *mods_s(base + 3, base + 4), g_mix,
                                         attn_w_q, (bl,), BF16, tm, 1024)
            o_p = _attn_prompt_call(q_p.reshape(bp, lp, N_DIL * HEADS * HEAD_DIM),
                                    kvf_p.reshape(N_DIL, bp, lp, kv_rows * HEAD_DIM)).reshape(mp, D_MODEL)
            qh = q_s.reshape(bs, ls, N_DIL, KV_HEADS, REP, HEAD_DIM).transpose(0, 2, 3, 4, 1, 5)
            qh = qh.reshape(bs, N_DIL * KV_HEADS, REP * ls, HEAD_DIM)
            o_s = _attn_sample_call(qh, kv_s.reshape(N_DIL, bs, ls * kv_rows, HEAD_DIM), kv_caches, ls)
            o_s = o_s.reshape(bs, KV_HEADS, REP, ls, HEAD_DIM).transpose(0, 3, 1, 2, 4).reshape(ms, D_MODEL)
            xp = _proj_residual_call(o_p, xp, *mods_p(base + 5), attn_w_o, (bl,), False, tiles["tm_out"], 512,
                                     name="attn_out")
            xs = _proj_residual_call(o_s, xs, as_block(*mods_s(base + 5)), attn_w_o, (bl,), False, ms, 512,
                                     name="attn_out")
        xp, xs = _ffn_call(xp, mods_p(base + 6, base + 7, base + 8), xs, mods_s(base + 6, base + 7, base + 8),
                           norm_g_row(norm_g[layer, 2]), ffn_w_in, ffn_w_out, layer, 1, tm, tiles["tf"],
                           tiles["tn_down"])
    k1 = 9 * DEPTH + 2
    y_p = _final_norm_call(xp, *mods_p(k1, k1 + 1), norm_g_row(final_norm_g), tm)
    y_s = _final_norm_call(xs, *(as_block(a) for a in mods_s(k1, k1 + 1)), norm_g_row(final_norm_g), ms)
    kv_p = kv_p.reshape(N_DIL, bp, lp, 2, KV_HEADS, HEAD_DIM)
    kv_s = kv_s.reshape(N_DIL, bs, ls, 2, KV_HEADS, HEAD_DIM)
    return ((y_p.reshape(bp, lp, D_MODEL), jnp.stack(re_p, axis=0), jnp.stack(im_p, axis=0), kv_p),
            (y_s.reshape(bs, ls, D_MODEL), jnp.stack(re_s, axis=0), jnp.stack(im_s, axis=0), kv_s))


def kernel(x_prompt, x_sample, c_prompt, c_sample, state_s5_re, state_s5_im, cache_kv_g0, cache_kv_g1, cache_kv_g2, w_mod, b_mod, norm_g, ffn_w_in, ffn_w_out, s5_lambda_re, s5_lambda_im, s5_log_dt, s5_b_re, s5_b_im, s5_c_re, s5_c_im, s5_d, s5_w_glu, kv_norm_g, w_kv, attn_w_q, attn_w_o, final_norm_g):
    bp, seq_p, _ = x_prompt.shape
    bs, seq_s, _ = x_sample.shape
    caches = (cache_kv_g0, cache_kv_g1, cache_kv_g2)
    for (w, d), c in zip(DILATED_PATTERNS, caches):
        assert c.shape[1] == w and w == d * Q_BLOCK and seq_p % (d * Q_BLOCK) == 0
    assert bp <= SUBLANES and bs <= SUBLANES

    n_c = bp + bs
    c_all = jnp.pad(jnp.concatenate([c_prompt, c_sample], axis=0), ((0, (-n_c) % SUBLANES), (0, 0)))
    mod_all = _mod_call(c_all, w_mod, b_mod)
    mod_p = mod_all[:bp].reshape(bp, N_MOD, D_MODEL)
    mod_s = mod_all[bp:n_c].reshape(bs, N_MOD, D_MODEL)

    s5_mats = [_s5_weights(s5_lambda_re[l], s5_lambda_im[l], s5_log_dt[l], s5_b_re[l], s5_b_im[l],
                           s5_c_re[l], s5_c_im[l]) for l in range(N_A_LAYERS)]
    wts = (norm_g, ffn_w_in, ffn_w_out, s5_mats, s5_d, s5_w_glu, kv_norm_g, w_kv, attn_w_q, attn_w_o, final_norm_g)

    caches_flat = tuple(c.reshape(bs, c.shape[1] * 2 * KV_HEADS, HEAD_DIM) for c in caches)
    (y_p, re_p, im_p, kv_p), (y_s, re_s, im_s, kv_s) = _trunks(
        x_prompt, x_sample, mod_p, mod_s, (state_s5_re, state_s5_im), caches_flat, wts,
        {"tm": 1024, "tf": 512, "tn_down": 256, "tm_out": 2048, "s5_steps": 32})

    kvp = [kv_p[g][:, seq_p - min(w, seq_p):] for g, (w, _) in enumerate(DILATED_PATTERNS)]
    return (y_p, y_s, re_p, im_p, kvp[0], kvp[1], kvp[2], re_s, im_s, kv_s[0], kv_s[1], kv_s[2])
```

```python
import functools

import jax
import jax.numpy as jnp
from jax import lax
from jax.experimental import pallas as pl
from jax.experimental.pallas import tpu as pltpu

F32 = jnp.float32
BF16 = jnp.bfloat16

D_MODEL = 2048
DEPTH = 4
N_A_LAYERS = DEPTH // 2
D_FF = 5632
S5_GROUP = 16
S5_GROUPS = D_MODEL // S5_GROUP
S5_STATE = 64
S5_NSTATE = S5_GROUPS * S5_STATE
HEAD_DIM = 128
HEADS = 16
KV_HEADS = 4
REP = HEADS // KV_HEADS
DILATED_PATTERNS = ((128, 1), (512, 4), (2048, 16))
N_DIL = len(DILATED_PATTERNS)
Q_BLOCK = 128
N_MOD = 9 * DEPTH + 4
EPS = 1e-6
ATTN_SCALE = HEAD_DIM ** -0.5
ATTN_SCALE_LOG2 = ATTN_SCALE * 1.4426950408889634

VMEM_LIMIT_BYTES = 56 * 1024 * 1024
SUBLANES = 8
S5_COL_BLOCK = 1024
S5_CH_BLOCK = S5_COL_BLOCK // S5_STATE * S5_GROUP
S5_N_BLOCKS = S5_NSTATE // S5_COL_BLOCK


def _params(*sem):
    return pltpu.CompilerParams(dimension_semantics=sem, vmem_limit_bytes=VMEM_LIMIT_BYTES)


def _rms_mod(x, g, shift, scale):
    xn = x * lax.rsqrt(jnp.mean(x * x, axis=-1, keepdims=True) + EPS)
    return xn * g * (1.0 + scale) + shift


def _norm_rows(x_ref, g_ref, sh_ref, sc_ref, h_ref, row_chunk):
    rows = x_ref.shape[0]
    g = g_ref[...]
    mod_rows = sh_ref.shape[0]
    if mod_rows == 1:
        gs = g * (1.0 + sc_ref[...])
        sh = sh_ref[...]
        row_chunk = min(row_chunk, 16)

        def body(c, carry):
            r0 = pl.multiple_of(c * row_chunk, row_chunk)
            x = x_ref[pl.ds(r0, row_chunk), :]
            xn = x * lax.rsqrt(jnp.mean(x * x, axis=-1, keepdims=True) + EPS)
            h_ref[pl.ds(r0, row_chunk), :] = (xn * gs + sh).astype(h_ref.dtype)
            return carry

        n_chunks = rows // row_chunk
        lax.fori_loop(0, n_chunks, body, 0, unroll=4 if n_chunks % 4 == 0 else 1)
        return

    def body(c, carry):
        r0 = pl.multiple_of(c * row_chunk, row_chunk)
        x = x_ref[pl.ds(r0, row_chunk), :]
        if mod_rows == rows and rows != row_chunk:
            sh = sh_ref[pl.ds(r0, row_chunk), :]
            sc = sc_ref[pl.ds(r0, row_chunk), :]
        else:
            sh = sh_ref[...]
            sc = sc_ref[...]
        h_ref[pl.ds(r0, row_chunk), :] = _rms_mod(x, g, sh, sc).astype(h_ref.dtype)
        return carry

    lax.fori_loop(0, rows // row_chunk, body, 0)


def _mod_kernel(c_ref, w_ref, b_ref, o_ref):
    c = c_ref[...]
    a = (c * jax.nn.sigmoid(c)).astype(BF16)
    o_ref[...] = jnp.dot(a, w_ref[...].astype(BF16), preferred_element_type=F32) + b_ref[...]


def _mod_call(c_all, w_mod, b_mod):
    rows = c_all.shape[0]
    n = w_mod.shape[1]
    tn = 1024
    return pl.pallas_call(
        _mod_kernel,
        grid=(n // tn,),
        in_specs=[
            pl.BlockSpec((rows, D_MODEL), lambda j: (0, 0)),
            pl.BlockSpec((D_MODEL, tn), lambda j: (0, j)),
            pl.BlockSpec((1, tn), lambda j: (0, j)),
        ],
        out_specs=pl.BlockSpec((rows, tn), lambda j: (0, j)),
        out_shape=jax.ShapeDtypeStruct((rows, n), F32),
        compiler_params=_params("arbitrary"),
        name="adaln_mod",
    )(c_all, w_mod, b_mod.reshape(1, n))


def _norm_two_row_sets(x_ref, sh_ref, sc_ref, x2_ref, sh2_ref, sc2_ref, g_ref, h_ref, row_chunk):
    tm, r2 = x_ref.shape[0], x2_ref.shape[0]

    @pl.when(pl.program_id(1) == 0)
    def _():
        _norm_rows(x_ref, g_ref, sh_ref, sc_ref, h_ref.at[pl.ds(0, tm)], row_chunk)

    @pl.when((pl.program_id(0) == 0) & (pl.program_id(1) == 0))
    def _():
        _norm_rows(x2_ref, g_ref, sh2_ref, sc2_ref, h_ref.at[pl.ds(tm, r2)], r2)


def _ffn_up_kernel(x_ref, sh_ref, sc_ref, x2_ref, sh2_ref, sc2_ref, g_ref, wg_ref, wu_ref, wo_ref,
                   a_ref, a2_ref, wob_ref, h_ref, *, row_chunk):
    tm = x_ref.shape[0]
    _norm_two_row_sets(x_ref, sh_ref, sc_ref, x2_ref, sh2_ref, sc2_ref, g_ref, h_ref, row_chunk)

    def act(h):
        wob_ref[...] = wo_ref[...].astype(wob_ref.dtype)
        g = jnp.dot(h, wg_ref[...].astype(BF16), preferred_element_type=F32)
        u = jnp.dot(h, wu_ref[...].astype(BF16), preferred_element_type=F32)
        return (g * jax.nn.sigmoid(g) * u).astype(a_ref.dtype)

    @pl.when(pl.program_id(0) == 0)
    def _():
        a = act(h_ref[...])
        a_ref[...] = a[:tm]
        a2_ref[...] = a[tm:]

    @pl.when(pl.program_id(0) != 0)
    def _():
        a_ref[...] = act(h_ref[pl.ds(0, tm), :])


def _ffn_call(x, mods, x2, mods2, g_norm, w_in, w_out, layer, which, tm, tf, tn):
    m, r2 = x.shape[0], x2.shape[0]
    nb = mods[0].shape[0]
    tiles_per_seq = (m // nb) // tm
    nk = D_FF // tf
    n_steps = (m // tm) * nk
    assert D_FF % n_steps == 0 and (D_FF // n_steps) % 16 == 0
    wo_rows = D_FF // n_steps
    mod_spec = pl.BlockSpec((None, 1, D_MODEL), lambda i, k: (i // tiles_per_seq, 0, 0))
    full2 = pl.BlockSpec((r2, D_MODEL), lambda i, k: (0, 0))
    a, a2, w_out_bf = pl.pallas_call(
        functools.partial(_ffn_up_kernel, row_chunk=min(tm, 64)),
        grid=(m // tm, nk),
        in_specs=[
            pl.BlockSpec((tm, D_MODEL), lambda i, k: (i, 0)),
            mod_spec, mod_spec,
            full2, full2, full2,
            pl.BlockSpec((1, D_MODEL), lambda i, k: (0, 0)),
            pl.BlockSpec((None, None, D_MODEL, tf), lambda i, k: (layer, which, 0, k)),
            pl.BlockSpec((None, None, D_MODEL, tf), lambda i, k: (layer, which, 0, k + nk)),
            pl.BlockSpec((None, None, wo_rows, D_MODEL), lambda i, k: (layer, which, i * nk + k, 0)),
        ],
        out_specs=(pl.BlockSpec((tm, tf), lambda i, k: (i, k)),
                   pl.BlockSpec((r2, tf), lambda i, k: (0, jnp.where(i == 0, k, nk - 1))),
                   pl.BlockSpec((wo_rows, D_MODEL), lambda i, k: (i * nk + k, 0))),
        out_shape=(jax.ShapeDtypeStruct((m, D_FF), BF16), jax.ShapeDtypeStruct((r2, D_FF), BF16),
                   jax.ShapeDtypeStruct((D_FF, D_MODEL), BF16)),
        scratch_shapes=[pltpu.VMEM((tm + r2, D_MODEL), BF16)],
        compiler_params=_params("arbitrary", "arbitrary"),
        name="ffn_up",
    )(x, mods[0], mods[1], x2, mods2[0], mods2[1], g_norm, w_in, w_in, w_out)
    out = _proj_residual_call(a, x, mods[2], w_out_bf, (), False, tm, tn, gate_scale=0.5, name="ffn_down")
    out2 = _proj_residual_call(a2, x2, mods2[2].reshape(1, r2, D_MODEL), w_out_bf, (), False, r2, tn,
                               gate_scale=0.5, name="ffn_down")
    return out, out2


def _norm_matmul_kernel(x_ref, sh_ref, sc_ref, x2_ref, sh2_ref, sc2_ref, g_ref, w_ref, *rest, row_chunk, head_rows):
    h_ref = rest[-1]
    outs = rest[:-1]
    per_set = len(outs) // 2
    tm = x_ref.shape[0]
    _norm_two_row_sets(x_ref, sh_ref, sc_ref, x2_ref, sh2_ref, sc2_ref, g_ref, h_ref, row_chunk)
    def emit(res, refs):
        refs[0][...] = res.astype(refs[0].dtype)
        if head_rows:
            rows, cols = res.shape
            heads_per_tile = cols // HEAD_DIM
            first = (pl.program_id(1) % (head_rows // heads_per_tile)) * heads_per_tile
            for c in range(heads_per_tile):
                refs[1][pl.ds(first + c, rows, stride=head_rows), :] = res[:, c * HEAD_DIM:(c + 1) * HEAD_DIM]

    @pl.when(pl.program_id(0) == 0)
    def _():
        res = jnp.dot(h_ref[...], w_ref[...].astype(BF16), preferred_element_type=F32)
        emit(res[:tm], outs[:per_set])
        emit(res[tm:], outs[per_set:])

    @pl.when(pl.program_id(0) != 0)
    def _():
        emit(jnp.dot(h_ref[pl.ds(0, tm), :], w_ref[...].astype(BF16), preferred_element_type=F32),
             outs[:per_set])


def _norm_matmul_call(x, shift, scale, x2, shift2, scale2, g_norm, w, w_lead, out_dtype, tm, tn, head_rows=0):
    m, r2 = x.shape[0], x2.shape[0]
    nb = shift.shape[0]
    n = w.shape[-1]
    nj = n // tn
    tiles_per_seq = (m // nb) // tm
    mod_spec = pl.BlockSpec((None, 1, D_MODEL), lambda i, j: (i // tiles_per_seq, 0, 0))
    full2 = pl.BlockSpec((r2, D_MODEL), lambda i, j: (0, 0))

    def col2(i, j):
        return jnp.where(i == 0, j, nj - 1)

    if head_rows:
        slab = head_rows * HEAD_DIM
        tps = slab // tn
        out_spec = (pl.BlockSpec((None, tm, tn), lambda i, j: (j // tps, i, j % tps)),
                    pl.BlockSpec((None, tm * head_rows, HEAD_DIM), lambda i, j: (j // tps, i, 0)),
                    pl.BlockSpec((None, r2, tn), lambda i, j: (col2(i, j) // tps, 0, col2(i, j) % tps)),
                    pl.BlockSpec((None, r2 * head_rows, HEAD_DIM), lambda i, j: (col2(i, j) // tps, 0, 0)))
        out_shape = (jax.ShapeDtypeStruct((n // slab, m, slab), out_dtype),
                     jax.ShapeDtypeStruct((n // slab, m * head_rows, HEAD_DIM), out_dtype),
                     jax.ShapeDtypeStruct((n // slab, r2, slab), out_dtype),
                     jax.ShapeDtypeStruct((n // slab, r2 * head_rows, HEAD_DIM), out_dtype))
    else:
        out_spec = (pl.BlockSpec((tm, tn), lambda i, j: (i, j)),
                    pl.BlockSpec((r2, tn), lambda i, j: (0, col2(i, j))))
        out_shape = (jax.ShapeDtypeStruct((m, n), out_dtype), jax.ShapeDtypeStruct((r2, n), out_dtype))
    return pl.pallas_call(
        functools.partial(_norm_matmul_kernel, row_chunk=min(tm, 64), head_rows=head_rows),
        grid=(m // tm, nj),
        in_specs=[
            pl.BlockSpec((tm, D_MODEL), lambda i, j: (i, 0)),
            mod_spec, mod_spec,
            full2, full2, full2,
            pl.BlockSpec((1, D_MODEL), lambda i, j: (0, 0)),
            pl.BlockSpec((None,) * len(w_lead) + (D_MODEL, tn), lambda i, j: (*w_lead, 0, j)),
        ],
        out_specs=out_spec,
        out_shape=out_shape,
        scratch_shapes=[pltpu.VMEM((tm + r2, D_MODEL), BF16)],
        compiler_params=_params("arbitrary", "arbitrary"),
        name="norm_matmul",
    )(x, shift, scale, x2, shift2, scale2, g_norm, w)


def _proj_residual_kernel(z_ref, x_ref, gt_ref, *refs, glu, gate_scale):
    z = z_ref[...]
    if glu:
        wa_ref, wg_ref, o_ref = refs
        a = jnp.dot(z, wa_ref[...].astype(BF16), preferred_element_type=F32)
        g = jnp.dot(z, wg_ref[...].astype(BF16), preferred_element_type=F32)
        y = a * jax.nn.sigmoid(g)
    else:
        w_ref, o_ref = refs
        y = jnp.dot(z, w_ref[...].astype(BF16), preferred_element_type=F32)
    gate = gt_ref[...] if gate_scale == 1.0 else gate_scale * gt_ref[...]
    o_ref[...] = x_ref[...] + gate * y


def _proj_residual_call(z, x, gate, w, w_lead, glu, tm, tn, gate_scale=1.0, name="proj_residual", z_buffers=2):
    m = x.shape[0]
    kdim = z.shape[1]
    nb, r, _ = gate.shape
    tiles_per_seq = (m // nb) // tm if r == 1 else 1
    n_blocks = D_MODEL // tn
    w_block = (None,) * len(w_lead) + (kdim, tn)
    w_specs = [pl.BlockSpec(w_block, lambda i, j: (*w_lead, 0, j))]
    w_args = [w]
    if glu:
        w_specs.append(pl.BlockSpec(w_block, lambda i, j: (*w_lead, 0, j + n_blocks)))
        w_args.append(w)
    return pl.pallas_call(
        functools.partial(_proj_residual_kernel, glu=glu, gate_scale=gate_scale),
        grid=(m // tm, n_blocks),
        in_specs=[
            (pl.BlockSpec((tm, kdim), lambda i, j: (i, 0)) if z_buffers == 2 else
             pl.BlockSpec((tm, kdim), lambda i, j: (i, 0), pipeline_mode=pl.Buffered(z_buffers))),
            pl.BlockSpec((tm, tn), lambda i, j: (i, j)),
            pl.BlockSpec((None, r, tn), lambda i, j: (i // tiles_per_seq, 0, j)),
        ] + w_specs,
        out_specs=pl.BlockSpec((tm, tn), lambda i, j: (i, j)),
        out_shape=jax.ShapeDtypeStruct((m, D_MODEL), F32),
        compiler_params=_params("parallel", "arbitrary"),
        name=name,
    )(z, x, gate, *w_args)


def _final_norm_kernel(x_ref, sh_ref, sc_ref, g_ref, o_ref, *, row_chunk):
    _norm_rows(x_ref, g_ref, sh_ref, sc_ref, o_ref, row_chunk)


def _final_norm_call(x, shift, scale, g_norm, tm):
    m = x.shape[0]
    nb, r, _ = shift.shape
    tiles_per_seq = (m // nb) // tm if r == 1 else 1
    mod_spec = pl.BlockSpec((None, r, D_MODEL), lambda i: (i // tiles_per_seq, 0, 0))
    return pl.pallas_call(
        functools.partial(_final_norm_kernel, row_chunk=min(tm, 64)),
        grid=(m // tm,),
        in_specs=[
            pl.BlockSpec((tm, D_MODEL), lambda i: (i, 0)),
            mod_spec, mod_spec,
            pl.BlockSpec((1, D_MODEL), lambda i: (0, 0)),
        ],
        out_specs=pl.BlockSpec((tm, D_MODEL), lambda i: (i, 0)),
        out_shape=jax.ShapeDtypeStruct((m, D_MODEL), F32),
        compiler_params=_params("parallel"),
        name="final_norm",
    )(x, shift, scale, g_norm)


def _s5_disc_kernel(lre_ref, lim_ref, ldt_ref, bre_ref, bim_ref, are_ref, aim_ref, bbre_ref, bbim_ref):
    lam_re = lre_ref[...]
    lam_im = lim_ref[...]
    dt = jnp.exp(ldt_ref[...])
    decay = jnp.exp(lam_re * dt)
    ab_re = decay * jnp.cos(lam_im * dt)
    ab_im = decay * jnp.sin(lam_im * dt)
    den = lam_re * lam_re + lam_im * lam_im
    f_re = ((ab_re - 1.0) * lam_re + ab_im * lam_im) / den
    f_im = (ab_im * lam_re - (ab_re - 1.0) * lam_im) / den
    b_re = bre_ref[...]
    b_im = bim_ref[...]
    are_ref[...] = ab_re
    aim_ref[...] = ab_im
    bbre_ref[...] = f_re * b_re - f_im * b_im
    bbim_ref[...] = f_re * b_im + f_im * b_re


def _s5_disc_call(lam_re, lam_im, log_dt, b_re, b_im):
    shape = jax.ShapeDtypeStruct(lam_re.shape, F32)
    return pl.pallas_call(
        _s5_disc_kernel,
        out_shape=(shape, shape, shape, shape),
        name="s5_discretise",
    )(lam_re, lam_im, log_dt, b_re, b_im)


def _s5_kernel(x_ref, sh_ref, sc_ref, g_ref, bbre_ref, bbim_ref, are_ref, aim_ref, cre_ref, cim_ref,
               d_ref, h0re_ref, h0im_ref, z_ref, hre_out, him_out,
               u_scr, sre_scr, sim_scr, y_scr, pwre_scr, pwim_scr, *, row_chunk):
    t = pl.program_id(0)
    nb, steps, _ = x_ref.shape
    rows = steps * nb
    lane = HEAD_DIM
    seg = SUBLANES // nb
    seg_len = steps // seg
    assert seg in (1, 2)

    @pl.when(t == 0)
    def _():
        hre_out[...] = h0re_ref[...]
        him_out[...] = h0im_ref[...]
        if seg > 1:
            zero = jnp.zeros((nb, S5_COL_BLOCK), F32)
            for cb in range(S5_N_BLOCKS):
                a_re = jnp.broadcast_to(are_ref[cb], (nb, S5_COL_BLOCK))
                a_im = jnp.broadcast_to(aim_ref[cb], (nb, S5_COL_BLOCK))
                p_re, p_im = a_re, a_im
                for i in range(seg_len):
                    pwre_scr[cb, i * SUBLANES:(i + 1) * SUBLANES, :] = jnp.concatenate([zero, p_re], axis=0)
                    pwim_scr[cb, i * SUBLANES:(i + 1) * SUBLANES, :] = jnp.concatenate([zero, p_im], axis=0)
                    p_re, p_im = p_re * a_re - p_im * a_im, p_re * a_im + p_im * a_re

    g = g_ref[...]
    for b in range(nb):
        for s in range(seg):
            def norm_chunk(c, carry, b=b, s=s):
                r0 = pl.multiple_of(c * row_chunk, row_chunk)
                u = _rms_mod(x_ref[b, pl.ds(s * seg_len + r0, row_chunk), :], g, sh_ref[b], sc_ref[b])
                for j in range(D_MODEL // lane):
                    u_scr[j, pl.ds(r0 * SUBLANES + s * nb + b, row_chunk, stride=SUBLANES), :] = \
                        u[:, j * lane:(j + 1) * lane]
                return carry

            lax.fori_loop(0, seg_len // row_chunk, norm_chunk, 0)

    tiles_per_block = S5_CH_BLOCK // lane
    per_block_scratch = sre_scr.shape[0] == S5_N_BLOCKS
    for cb in range(S5_N_BLOCKS):
        ch = slice(cb * S5_CH_BLOCK, (cb + 1) * S5_CH_BLOCK)
        st = slice(cb * S5_COL_BLOCK, (cb + 1) * S5_COL_BLOCK)
        slab = cb if per_block_scratch else 0
        sre, sim, ys = sre_scr.at[slab], sim_scr.at[slab], y_scr.at[slab]
        u_blk = jnp.concatenate([u_scr[cb * tiles_per_block + j] for j in range(tiles_per_block)], axis=1)
        u_bf = u_blk.astype(BF16)
        sre[...] = jnp.dot(u_bf, bbre_ref[cb], preferred_element_type=F32)
        sim[...] = jnp.dot(u_bf, bbim_ref[cb], preferred_element_type=F32)

        a_re = jnp.broadcast_to(are_ref[cb], (SUBLANES, S5_COL_BLOCK))
        a_im = jnp.broadcast_to(aim_ref[cb], (SUBLANES, S5_COL_BLOCK))

        def tile_step(i, carry, a_re=a_re, a_im=a_im, sre=sre, sim=sim):
            h_re, h_im = carry
            r0 = i * SUBLANES if isinstance(i, int) else pl.multiple_of(i * SUBLANES, SUBLANES)
            n_re = a_re * h_re - a_im * h_im + sre[pl.ds(r0, SUBLANES), :]
            n_im = a_re * h_im + a_im * h_re + sim[pl.ds(r0, SUBLANES), :]
            sre[pl.ds(r0, SUBLANES), :] = n_re
            sim[pl.ds(r0, SUBLANES), :] = n_im
            return n_re, n_im

        def first_rows(v):
            return jnp.concatenate([v[:nb]] * seg, axis=0)

        h_re, h_im = hre_out[:, st], him_out[:, st]
        if seg > 1:
            zero = jnp.zeros((SUBLANES - nb, S5_COL_BLOCK), F32)
            h_re, h_im = jnp.concatenate([h_re, zero], axis=0), jnp.concatenate([h_im, zero], axis=0)
        carry = (h_re, h_im)
        if per_block_scratch:
            for i in range(seg_len):
                carry = tile_step(i, carry)
        else:
            carry = lax.fori_loop(0, seg_len, tile_step, carry, unroll=min(seg_len, 4))
        h_re, h_im = carry
        if seg > 1:
            c_re, c_im = first_rows(h_re), first_rows(h_im)
            for i in range(seg_len):
                rows_i = slice(i * SUBLANES, (i + 1) * SUBLANES)
                p_re, p_im = pwre_scr[cb, rows_i, :], pwim_scr[cb, rows_i, :]
                f_re = sre[rows_i, :] + (p_re * c_re - p_im * c_im)
                f_im = sim[rows_i, :] + (p_re * c_im + p_im * c_re)
                sre[rows_i, :] = f_re
                sim[rows_i, :] = f_im
            h_re, h_im = f_re[SUBLANES - nb:], f_im[SUBLANES - nb:]
        hre_out[:, st] = h_re
        him_out[:, st] = h_im

        y = (jnp.dot(sre[...].astype(BF16), cre_ref[cb], preferred_element_type=F32)
             - jnp.dot(sim[...].astype(BF16), cim_ref[cb], preferred_element_type=F32))
        zf = jax.nn.gelu(y + d_ref[:, ch] * u_blk)
        for j in range(tiles_per_block):
            ys[j] = zf[:, j * lane:(j + 1) * lane]
        for b in range(nb):
            for s in range(seg):
                for j in range(tiles_per_block):
                    c0 = cb * S5_CH_BLOCK + j * lane
                    z_ref[b, s * seg_len:(s + 1) * seg_len, c0:c0 + lane] = \
                        ys[j, pl.ds(s * nb + b, seg_len, stride=SUBLANES), :].astype(z_ref.dtype)


def _s5_call(x, shift, scale, g_norm, bb_re, bb_im, a_re, a_im, c_re, c_im, d_skip,
             h0_re, h0_im, steps_per_chunk, z_dtype, per_block_scratch):
    nb, seq, _ = x.shape
    assert SUBLANES % nb == 0 and seq % steps_per_chunk == 0
    tr = steps_per_chunk * nb
    slabs = S5_N_BLOCKS if per_block_scratch else 1
    seg = SUBLANES // nb
    pw_rows = tr if seg > 1 else SUBLANES
    row_chunk = min(steps_per_chunk // (SUBLANES // nb), 64)
    const2 = lambda t: (0, 0)
    const3 = lambda t: (0, 0, 0)
    state_spec = pl.BlockSpec((nb, S5_NSTATE), const2)
    io_spec = pl.BlockSpec((nb, steps_per_chunk, D_MODEL), lambda t: (0, t, 0))
    return pl.pallas_call(
        functools.partial(_s5_kernel, row_chunk=row_chunk),
        grid=(seq // steps_per_chunk,),
        in_specs=[
            io_spec,
            pl.BlockSpec((nb, 1, D_MODEL), const3),
            pl.BlockSpec((nb, 1, D_MODEL), const3),
            pl.BlockSpec((1, D_MODEL), const2),
            pl.BlockSpec((S5_N_BLOCKS, S5_CH_BLOCK, S5_COL_BLOCK), const3, pipeline_mode=pl.Buffered(1)),
            pl.BlockSpec((S5_N_BLOCKS, S5_CH_BLOCK, S5_COL_BLOCK), const3, pipeline_mode=pl.Buffered(1)),
            pl.BlockSpec((S5_N_BLOCKS, 1, S5_COL_BLOCK), const3),
            pl.BlockSpec((S5_N_BLOCKS, 1, S5_COL_BLOCK), const3),
            pl.BlockSpec((S5_N_BLOCKS, S5_COL_BLOCK, S5_CH_BLOCK), const3, pipeline_mode=pl.Buffered(1)),
            pl.BlockSpec((S5_N_BLOCKS, S5_COL_BLOCK, S5_CH_BLOCK), const3, pipeline_mode=pl.Buffered(1)),
            pl.BlockSpec((1, D_MODEL), const2),
            state_spec, state_spec,
        ],
        out_specs=(io_spec, state_spec, state_spec),
        out_shape=(jax.ShapeDtypeStruct((nb, seq, D_MODEL), z_dtype),
                   jax.ShapeDtypeStruct((nb, S5_NSTATE), F32),
                   jax.ShapeDtypeStruct((nb, S5_NSTATE), F32)),
        scratch_shapes=[pltpu.VMEM((D_MODEL // HEAD_DIM, tr, HEAD_DIM), F32),
                        pltpu.VMEM((slabs, tr, S5_COL_BLOCK), F32),
                        pltpu.VMEM((slabs, tr, S5_COL_BLOCK), F32),
                        pltpu.VMEM((slabs, S5_CH_BLOCK // HEAD_DIM, tr, HEAD_DIM), F32),
                        pltpu.VMEM((S5_N_BLOCKS, pw_rows, S5_COL_BLOCK), F32),
                        pltpu.VMEM((S5_N_BLOCKS, pw_rows, S5_COL_BLOCK), F32)],
        compiler_params=_params("arbitrary"),
        name="s5_mixer",
    )(x, shift, scale, g_norm, bb_re, bb_im, a_re, a_im, c_re, c_im, d_skip, h0_re, h0_im)


def _softmax_pv(s2, v_bf):
    m = jnp.max(s2, axis=-1, keepdims=True)
    p = jnp.exp2(s2 - m)
    l = jnp.sum(p, axis=-1, keepdims=True)
    o = jnp.dot(p.astype(BF16), v_bf, preferred_element_type=F32) / l
    return o, jnp.broadcast_to(m + jnp.log2(l), o.shape)


def _qk(q_bf, k_bf, scale=ATTN_SCALE):
    return lax.dot_general(q_bf, k_bf, (((1,), (1,)), ((), ())), preferred_element_type=F32) * scale


def _attn_prompt_kernel(q0_ref, q1_ref, q2_ref, k0_ref, v0_ref, k1_ref, v1_ref, k2_ref, v2_ref, o_ref,
                        qf_scr, o1_scr, l1_scr, o2_scr, l2_scr, band_scr, first_scr):
    seq = q0_ref.shape[0]
    qb = Q_BLOCK
    row = lax.broadcasted_iota(jnp.int32, (REP * qb, 2 * qb), 0) & (qb - 1)
    col = lax.broadcasted_iota(jnp.int32, (REP * qb, 2 * qb), 1)
    band_scr[...] = jnp.where((col >= row) & (col <= row + qb), 0.0, -jnp.inf)
    row1 = lax.broadcasted_iota(jnp.int32, (REP * qb, qb), 0) & (qb - 1)
    col1 = lax.broadcasted_iota(jnp.int32, (REP * qb, qb), 1)
    first_scr[...] = jnp.where(col1 <= row1, 0.0, -jnp.inf)

    def rows_of(start, d):
        return pl.ds(start, qb) if d == 1 else pl.ds(start, qb, stride=d)

    def load_q(src_ref, start, d):
        if d == 1:
            parts = [src_ref[pl.ds(start, qb), e * HEAD_DIM:(e + 1) * HEAD_DIM] for e in range(REP)]
        else:
            parts = [src_ref[e, rows_of(start, d), :] for e in range(REP)]
        return jnp.concatenate(parts, axis=0).astype(BF16)

    def unit(src_q, k_ref, v_ref, start, d, first):
        q4 = load_q(src_q, start, d)
        k_cur = k_ref[rows_of(start, d), :]
        v_cur = v_ref[rows_of(start, d), :]
        if first:
            s = _qk(q4, k_cur.astype(BF16), ATTN_SCALE_LOG2) + first_scr[...]
            return _softmax_pv(s, v_cur.astype(BF16))
        prev = start - qb * d
        k_band = jnp.concatenate([k_ref[rows_of(prev, d), :], k_cur], axis=0)
        v_band = jnp.concatenate([v_ref[rows_of(prev, d), :], v_cur], axis=0)
        s = _qk(q4, k_band.astype(BF16), ATTN_SCALE_LOG2) + band_scr[...]
        return _softmax_pv(s, v_band.astype(BF16))

    def stage_group(q_ref, k_ref, v_ref, d, o_scr, l_scr):
        n_blocks = seq // (qb * d)

        def copy(c, carry):
            r0 = pl.multiple_of(c * 256, 256)
            for e in range(REP):
                qf_scr[e, pl.ds(r0, 256), :] = q_ref[pl.ds(r0, 256), e * HEAD_DIM:(e + 1) * HEAD_DIM].astype(F32)
            return carry

        lax.fori_loop(0, seq // 256, copy, 0)

        def store(start, o, lse):
            for e in range(REP):
                sl = slice(e * qb, (e + 1) * qb)
                o_scr[e, rows_of(start, d), :] = o[sl]
                l_scr[e, rows_of(start, d), :] = lse[sl]

        def per_class(r, carry):
            store(r, *unit(qf_scr, k_ref, v_ref, r, d, True))

            def per_block(jb, c2):
                start = jb * (qb * d) + r
                store(start, *unit(qf_scr, k_ref, v_ref, start, d, False))
                return c2

            if n_blocks > 1:
                lax.fori_loop(1, n_blocks, per_block, 0, unroll=True)
            return carry

        lax.fori_loop(0, d, per_class, 0, unroll=4 if n_blocks == 1 else 2)

    stage_group(q2_ref, k2_ref, v2_ref, DILATED_PATTERNS[2][1], o2_scr, l2_scr)
    stage_group(q1_ref, k1_ref, v1_ref, DILATED_PATTERNS[1][1], o1_scr, l1_scr)

    def merge(start, o0, lse0):
        for e in range(REP):
            sl = slice(e * qb, (e + 1) * qb)
            cols = slice(e * HEAD_DIM, (e + 1) * HEAD_DIM)
            la = lse0[sl]
            lb = l1_scr[e, pl.ds(start, qb), :]
            lc = l2_scr[e, pl.ds(start, qb), :]
            mx = jnp.maximum(jnp.maximum(la, lb), lc)
            wa = jnp.exp2(la - mx)
            wb = jnp.exp2(lb - mx)
            wc = jnp.exp2(lc - mx)
            inv = 1.0 / (wa + wb + wc)
            acc = (wa * o0[sl] + wb * o1_scr[e, pl.ds(start, qb), :] + wc * o2_scr[e, pl.ds(start, qb), :]) * inv
            o_ref[pl.ds(start, qb), cols] = acc.astype(o_ref.dtype)

    merge(0, *unit(q0_ref, k0_ref, v0_ref, 0, 1, True))

    def per_block0(jb, carry):
        start = pl.multiple_of(jb * qb, qb)
        merge(start, *unit(q0_ref, k0_ref, v0_ref, start, 1, False))
        return carry

    n_blocks0 = seq // qb
    lax.fori_loop(1, n_blocks0, per_block0, 0, unroll=3 if (n_blocks0 - 1) % 3 == 0 else 1)


def _attn_prompt_call(q, kv):
    b, seq, _ = q.shape
    qw = REP * HEAD_DIM
    q_specs = [pl.BlockSpec((None, seq, qw), functools.partial(lambda bi, h, g: (bi, 0, g * KV_HEADS + h), g=g))
               for g in range(N_DIL)]
    kv_specs = []
    for g in range(N_DIL):
        kv_specs.append(pl.BlockSpec((None, None, seq, HEAD_DIM),
                                     functools.partial(lambda bi, h, g: (g, bi, 0, h), g=g)))
        kv_specs.append(pl.BlockSpec((None, None, seq, HEAD_DIM),
                                     functools.partial(lambda bi, h, g: (g, bi, 0, KV_HEADS + h), g=g)))
    kv_args = [kv] * (2 * N_DIL)
    return pl.pallas_call(
        _attn_prompt_kernel,
        grid=(b, KV_HEADS),
        in_specs=q_specs + kv_specs,
        out_specs=pl.BlockSpec((None, seq, qw), lambda bi, h: (bi, 0, h)),
        out_shape=jax.ShapeDtypeStruct((b, seq, HEADS * HEAD_DIM), BF16),
        scratch_shapes=[pltpu.VMEM((REP, seq, HEAD_DIM), F32),
                        pltpu.VMEM((REP, seq, HEAD_DIM), F32), pltpu.VMEM((REP, seq, HEAD_DIM), F32),
                        pltpu.VMEM((REP, seq, HEAD_DIM), F32), pltpu.VMEM((REP, seq, HEAD_DIM), F32),
                        pltpu.VMEM((REP * Q_BLOCK, 2 * Q_BLOCK), F32), pltpu.VMEM((REP * Q_BLOCK, Q_BLOCK), F32)],
        compiler_params=_params("parallel", "arbitrary"),
        name="dilated_attention_prompt",
    )(q, q, q, *kv_args)


def _attn_sample_kernel(q_ref, kvn_ref, c0_ref, c1_ref, c2_ref, o_ref, *, n_tok):
    rows = REP * n_tok
    cache_refs = (c0_ref, c1_ref, c2_ref)
    kv_rows = 2 * KV_HEADS
    for h in range(KV_HEADS):
        outs, lses = [], []
        for g, (_, d) in enumerate(DILATED_PATTERNS):
            q = q_ref[g * KV_HEADS + h]
            lw = cache_refs[g].shape[0] // kv_rows
            k_c = cache_refs[g][pl.ds(h, lw, stride=kv_rows), :].astype(BF16)
            v_c = cache_refs[g][pl.ds(KV_HEADS + h, lw, stride=kv_rows), :].astype(BF16)
            k_n = kvn_ref[g, pl.ds(h, n_tok, stride=kv_rows), :].astype(BF16)
            v_n = kvn_ref[g, pl.ds(KV_HEADS + h, n_tok, stride=kv_rows), :].astype(BF16)
            tq_c = lax.broadcasted_iota(jnp.int32, (rows, lw), 0) & (n_tok - 1)
            idx_c = lax.broadcasted_iota(jnp.int32, (rows, lw), 1)
            ok_c = (idx_c >= tq_c) & (((idx_c - tq_c) & (d - 1)) == 0)
            tq_n = lax.broadcasted_iota(jnp.int32, (rows, n_tok), 0) & (n_tok - 1)
            idx_n = lax.broadcasted_iota(jnp.int32, (rows, n_tok), 1)
            ok_n = (idx_n <= tq_n) & (((tq_n - idx_n) & (d - 1)) == 0)
            s_c = jnp.where(ok_c, _qk(q, k_c), -jnp.inf)
            s_n = jnp.where(ok_n, _qk(q, k_n), -jnp.inf)
            m = jnp.maximum(jnp.max(s_c, axis=-1, keepdims=True), jnp.max(s_n, axis=-1, keepdims=True))
            p_c = jnp.exp(s_c - m)
            p_n = jnp.exp(s_n - m)
            l = jnp.sum(p_c, axis=-1, keepdims=True) + jnp.sum(p_n, axis=-1, keepdims=True)
            o = (jnp.dot(p_c.astype(BF16), v_c, preferred_element_type=F32)
                 + jnp.dot(p_n.astype(BF16), v_n, preferred_element_type=F32)) / l
            outs.append(o)
            lses.append(m + jnp.log(l))
        mx = jnp.maximum(jnp.maximum(lses[0], lses[1]), lses[2])
        w = [jnp.exp(x - mx) for x in lses]
        inv = 1.0 / (w[0] + w[1] + w[2])
        acc = (w[0] * inv) * outs[0] + (w[1] * inv) * outs[1] + (w[2] * inv) * outs[2]
        o_ref[h] = acc.astype(o_ref.dtype)


def _attn_sample_call(q_heads, kv_new, caches, n_tok):
    b = q_heads.shape[0]
    rows = REP * n_tok
    assert n_tok & (n_tok - 1) == 0
    in_specs = [pl.BlockSpec((None, N_DIL * KV_HEADS, rows, HEAD_DIM), lambda bi: (bi, 0, 0, 0)),
                pl.BlockSpec((N_DIL, None, kv_new.shape[2], HEAD_DIM), lambda bi: (0, bi, 0, 0))]
    in_specs += [pl.BlockSpec((None, c.shape[1], HEAD_DIM), lambda bi: (bi, 0, 0)) for c in caches]
    return pl.pallas_call(
        functools.partial(_attn_sample_kernel, n_tok=n_tok),
        grid=(b,),
        in_specs=in_specs,
        out_specs=pl.BlockSpec((None, KV_HEADS, rows, HEAD_DIM), lambda bi: (bi, 0, 0, 0)),
        out_shape=jax.ShapeDtypeStruct((b, KV_HEADS, rows, HEAD_DIM), BF16),
        compiler_params=_params("parallel"),
        name="dilated_attention_sample",
    )(q_heads, kv_new, *caches)


def _s5_weights(lam_re, lam_im, log_dt, b_re, b_im, c_re, c_im):
    g, p, c = S5_GROUPS, S5_STATE, S5_GROUP
    rep = lambda a: jnp.repeat(a, c, axis=1)
    a_re, a_im, bb_re, bb_im = _s5_disc_call(rep(lam_re), rep(lam_im), log_dt.reshape(g, 1),
                                             b_re.reshape(g, p * c), b_im.reshape(g, p * c))
    gl = S5_CH_BLOCK // c

    def block_diag(t, rows_per_group, cols_per_group):
        tiled = jnp.tile(t, (1, 1, gl))
        row_g = lax.broadcasted_iota(jnp.int32, tiled.shape, 1) // rows_per_group
        col_g = lax.broadcasted_iota(jnp.int32, tiled.shape, 2) // cols_per_group
        return jnp.where(row_g == col_g, tiled, 0.0).astype(BF16)

    def in_blocks(bb):
        t = bb.reshape(S5_N_BLOCKS, gl, p, c).transpose(0, 1, 3, 2)
        return block_diag(t.reshape(S5_N_BLOCKS, gl * c, p), c, p)

    def out_blocks(cm):
        t = cm.reshape(S5_N_BLOCKS, gl, c, p).transpose(0, 1, 3, 2)
        return block_diag(t.reshape(S5_N_BLOCKS, gl * p, c), p, c)

    lam_bar = lambda a: a.reshape(g, p, c)[:, :, 0].reshape(S5_N_BLOCKS, 1, S5_COL_BLOCK)
    return (in_blocks(bb_re), in_blocks(bb_im), lam_bar(a_re), lam_bar(a_im),
            out_blocks(c_re), out_blocks(c_im))


def _trunks(x_p, x_s, mod_p, mod_s, s5_state, kv_caches, wts, tiles):
    (norm_g, ffn_w_in, ffn_w_out, s5_mats, s5_d, s5_w_glu, kv_norm_g, w_kv,
     attn_w_q, attn_w_o, final_norm_g) = wts
    bp, lp, _ = x_p.shape
    bs, ls, _ = x_s.shape
    mp, ms = bp * lp, bs * ls
    tm = tiles["tm"]
    kv_rows = 2 * KV_HEADS

    def mods_p(*ks):
        return tuple(mod_p[:, k].reshape(bp, 1, D_MODEL) for k in ks)

    def mods_s(*ks):
        return tuple(jnp.repeat(mod_s[:, k], ls, axis=0) for k in ks)

    def as_block(a):
        return a.reshape(1, *a.shape)

    def norm_g_row(a):
        return a.reshape(1, D_MODEL)

    xp = x_p.reshape(mp, D_MODEL)
    xs = x_s.reshape(ms, D_MODEL)
    re_p, im_p, re_s, im_s = [], [], [], []
    for layer in range(DEPTH):
        base = 9 * layer
        if layer == N_A_LAYERS:
            k0 = 9 * DEPTH
            kvf_p, kv_p, _, kv_s = _norm_matmul_call(xp, *mods_p(k0, k0 + 1), xs, *mods_s(k0, k0 + 1),
                                                     norm_g_row(kv_norm_g), w_kv, (), F32, tm,
                                                     KV_HEADS * HEAD_DIM, head_rows=kv_rows)
        xp, xs = _ffn_call(xp, mods_p(base, base + 1, base + 2), xs, mods_s(base, base + 1, base + 2),
                           norm_g_row(norm_g[layer, 0]), ffn_w_in, ffn_w_out, layer, 0, tm, tiles["tf"],
                           tiles["tn_down"])
        g_mix = norm_g_row(norm_g[layer, 1])
        if layer < N_A_LAYERS:
            d_skip = s5_d[layer].reshape(1, D_MODEL)
            zero_state = jnp.zeros((bp, S5_NSTATE), F32)
            zp, h_re, h_im = _s5_call(xp.reshape(bp, lp, D_MODEL), *mods_p(base + 3, base + 4), g_mix,
                                      *s5_mats[layer], d_skip, zero_state, zero_state,
                                      min(lp, tiles["s5_steps"]), BF16, True)
            re_p.append(h_re.reshape(bp, S5_GROUPS, S5_STATE))
            im_p.append(h_im.reshape(bp, S5_GROUPS, S5_STATE))
            zs, h_re, h_im = _s5_call(xs.reshape(bs, ls, D_MODEL), mod_s[:, base + 3].reshape(bs, 1, D_MODEL),
                                      mod_s[:, base + 4].reshape(bs, 1, D_MODEL), g_mix, *s5_mats[layer], d_skip,
                                      s5_state[0][layer].reshape(bs, S5_NSTATE),
                                      s5_state[1][layer].reshape(bs, S5_NSTATE), ls, F32, False)
            re_s.append(h_re.reshape(bs, S5_GROUPS, S5_STATE))
            im_s.append(h_im.reshape(bs, S5_GROUPS, S5_STATE))
            xp = _proj_residual_call(zp.reshape(mp, D_MODEL), xp, *mods_p(base + 5), s5_w_glu, (layer,),
                                     True, tm, 512, name="s5_glu")
            xs = _proj_residual_call(zs.reshape(ms, D_MODEL).astype(BF16), xs, as_block(*mods_s(base + 5)),
                                     s5_w_glu, (layer,), True, ms, 512, name="s5_glu")
        else:
            bl = layer - N_A_LAYERS
            q_p, q_s = _norm_matmul_call(xp, *mods_p(base + 3, base + 4), xs, *mods_s(base + 3, base + 4), g_mix,
                                         attn_w_q, (bl,), BF16, tm, 1024)
            o_p = _attn_prompt_call(q_p.reshape(bp, lp, N_DIL * HEADS * HEAD_DIM),
                                    kvf_p.reshape(N_DIL, bp, lp, kv_rows * HEAD_DIM)).reshape(mp, D_MODEL)
            qh = q_s.reshape(bs, ls, N_DIL, KV_HEADS, REP, HEAD_DIM).transpose(0, 2, 3, 4, 1, 5)
            qh = qh.reshape(bs, N_DIL * KV_HEADS, REP * ls, HEAD_DIM)
            o_s = _attn_sample_call(qh, kv_s.reshape(N_DIL, bs, ls * kv_rows, HEAD_DIM), kv_caches, ls)
            o_s = o_s.reshape(bs, KV_HEADS, REP, ls, HEAD_DIM).transpose(0, 3, 1, 2, 4).reshape(ms, D_MODEL)
            xp = _proj_residual_call(o_p, xp, *mods_p(base + 5), attn_w_o, (bl,), False, tiles["tm_out"], 512,
                                     name="attn_out")
            xs = _proj_residual_call(o_s, xs, as_block(*mods_s(base + 5)), attn_w_o, (bl,), False, ms, 512,
                                     name="attn_out")
        xp, xs = _ffn_call(xp, mods_p(base + 6, base + 7, base + 8), xs, mods_s(base + 6, base + 7, base + 8),
                           norm_g_row(norm_g[layer, 2]), ffn_w_in, ffn_w_out, layer, 1, tm, tiles["tf"],
                           tiles["tn_down"])
    k1 = 9 * DEPTH + 2
    y_p = _final_norm_call(xp, *mods_p(k1, k1 + 1), norm_g_row(final_norm_g), tm)
    y_s = _final_norm_call(xs, *(as_block(a) for a in mods_s(k1, k1 + 1)), norm_g_row(final_norm_g), ms)
    kv_p = kv_p.reshape(N_DIL, bp, lp, 2, KV_HEADS, HEAD_DIM)
    kv_s = kv_s.reshape(N_DIL, bs, ls, 2, KV_HEADS, HEAD_DIM)
    return ((y_p.reshape(bp, lp, D_MODEL), jnp.stack(re_p, axis=0), jnp.stack(im_p, axis=0), kv_p),
            (y_s.reshape(bs, ls, D_MODEL), jnp.stack(re_s, axis=0), jnp.stack(im_s, axis=0), kv_s))


def kernel(x_prompt, x_sample, c_prompt, c_sample, state_s5_re, state_s5_im, cache_kv_g0, cache_kv_g1, cache_kv_g2, w_mod, b_mod, norm_g, ffn_w_in, ffn_w_out, s5_lambda_re, s5_lambda_im, s5_log_dt, s5_b_re, s5_b_im, s5_c_re, s5_c_im, s5_d, s5_w_glu, kv_norm_g, w_kv, attn_w_q, attn_w_o, final_norm_g):
    bp, seq_p, _ = x_prompt.shape
    bs, seq_s, _ = x_sample.shape
    caches = (cache_kv_g0, cache_kv_g1, cache_kv_g2)
    for (w, d), c in zip(DILATED_PATTERNS, caches):
        assert c.shape[1] == w and w == d * Q_BLOCK and seq_p % (d * Q_BLOCK) == 0
    assert bp <= SUBLANES and bs <= SUBLANES

    n_c = bp + bs
    c_all = jnp.pad(jnp.concatenate([c_prompt, c_sample], axis=0), ((0, (-n_c) % SUBLANES), (0, 0)))
    mod_all = _mod_call(c_all, w_mod, b_mod)
    mod_p = mod_all[:bp].reshape(bp, N_MOD, D_MODEL)
    mod_s = mod_all[bp:n_c].reshape(bs, N_MOD, D_MODEL)

    s5_mats = [_s5_weights(s5_lambda_re[l], s5_lambda_im[l], s5_log_dt[l], s5_b_re[l], s5_b_im[l],
                           s5_c_re[l], s5_c_im[l]) for l in range(N_A_LAYERS)]
    wts = (norm_g, ffn_w_in, ffn_w_out, s5_mats, s5_d, s5_w_glu, kv_norm_g, w_kv, attn_w_q, attn_w_o, final_norm_g)

    caches_flat = tuple(c.reshape(bs, c.shape[1] * 2 * KV_HEADS, HEAD_DIM) for c in caches)
    (y_p, re_p, im_p, kv_p), (y_s, re_s, im_s, kv_s) = _trunks(
        x_prompt, x_sample, mod_p, mod_s, (state_s5_re, state_s5_im), caches_flat, wts,
        {"tm": 1024, "tf": 512, "tn_down": 512, "tm_out": 2048, "s5_steps": 32})

    kvp = [kv_p[g][:, seq_p - min(w, seq_p):] for g, (w, _) in enumerate(DILATED_PATTERNS)]
    return (y_p, y_s, re_p, im_p, kvp[0], kvp[1], kvp[2], re_s, im_s, kv_s[0], kv_s[1], kv_s[2])
```

```python
import functools

import jax
import jax.numpy as jnp
from jax import lax
from jax.experimental import pallas as pl
from jax.experimental.pallas import tpu as pltpu

F32 = jnp.float32
BF16 = jnp.bfloat16

D_MODEL = 2048
DEPTH = 4
N_A_LAYERS = DEPTH // 2
D_FF = 5632
S5_GROUP = 16
S5_GROUPS = D_MODEL // S5_GROUP
S5_STATE = 64
S5_NSTATE = S5_GROUPS * S5_STATE
HEAD_DIM = 128
HEADS = 16
KV_HEADS = 4
REP = HEADS // KV_HEADS
DILATED_PATTERNS = ((128, 1), (512, 4), (2048, 16))
N_DIL = len(DILATED_PATTERNS)
Q_BLOCK = 128
N_MOD = 9 * DEPTH + 4
EPS = 1e-6
ATTN_SCALE = HEAD_DIM ** -0.5
ATTN_SCALE_LOG2 = ATTN_SCALE * 1.4426950408889634

VMEM_LIMIT_BYTES = 56 * 1024 * 1024
SUBLANES = 8
S5_COL_BLOCK = 1024
S5_CH_BLOCK = S5_COL_BLOCK // S5_STATE * S5_GROUP
S5_N_BLOCKS = S5_NSTATE // S5_COL_BLOCK


def _params(*sem):
    return pltpu.CompilerParams(dimension_semantics=sem, vmem_limit_bytes=VMEM_LIMIT_BYTES)


def _rms_mod(x, g, shift, scale):
    xn = x * lax.rsqrt(jnp.mean(x * x, axis=-1, keepdims=True) + EPS)
    return xn * g * (1.0 + scale) + shift


def _norm_rows(x_ref, g_ref, sh_ref, sc_ref, h_ref, row_chunk):
    rows = x_ref.shape[0]
    g = g_ref[...]
    mod_rows = sh_ref.shape[0]
    if mod_rows == 1:
        gs = g * (1.0 + sc_ref[...])
        sh = sh_ref[...]
        row_chunk = min(row_chunk, 16)

        def body(c, carry):
            r0 = pl.multiple_of(c * row_chunk, row_chunk)
            x = x_ref[pl.ds(r0, row_chunk), :]
            xn = x * lax.rsqrt(jnp.mean(x * x, axis=-1, keepdims=True) + EPS)
            h_ref[pl.ds(r0, row_chunk), :] = (xn * gs + sh).astype(h_ref.dtype)
            return carry

        n_chunks = rows // row_chunk
        lax.fori_loop(0, n_chunks, body, 0, unroll=4 if n_chunks % 4 == 0 else 1)
        return

    def body(c, carry):
        r0 = pl.multiple_of(c * row_chunk, row_chunk)
        x = x_ref[pl.ds(r0, row_chunk), :]
        if mod_rows == rows and rows != row_chunk:
            sh = sh_ref[pl.ds(r0, row_chunk), :]
            sc = sc_ref[pl.ds(r0, row_chunk), :]
        else:
            sh = sh_ref[...]
            sc = sc_ref[...]
        h_ref[pl.ds(r0, row_chunk), :] = _rms_mod(x, g, sh, sc).astype(h_ref.dtype)
        return carry

    lax.fori_loop(0, rows // row_chunk, body, 0)


def _mod_kernel(c_ref, w_ref, b_ref, o_ref):
    c = c_ref[...]
    a = (c * jax.nn.sigmoid(c)).astype(BF16)
    o_ref[...] = jnp.dot(a, w_ref[...].astype(BF16), preferred_element_type=F32) + b_ref[...]


def _mod_call(c_all, w_mod, b_mod):
    rows = c_all.shape[0]
    n = w_mod.shape[1]
    tn = 1024
    return pl.pallas_call(
        _mod_kernel,
        grid=(n // tn,),
        in_specs=[
            pl.BlockSpec((rows, D_MODEL), lambda j: (0, 0)),
            pl.BlockSpec((D_MODEL, tn), lambda j: (0, j)),
            pl.BlockSpec((1, tn), lambda j: (0, j)),
        ],
        out_specs=pl.BlockSpec((rows, tn), lambda j: (0, j)),
        out_shape=jax.ShapeDtypeStruct((rows, n), F32),
        compiler_params=_params("arbitrary"),
        name="adaln_mod",
    )(c_all, w_mod, b_mod.reshape(1, n))


def _norm_two_row_sets(x_ref, sh_ref, sc_ref, x2_ref, sh2_ref, sc2_ref, g_ref, h_ref, row_chunk):
    tm, r2 = x_ref.shape[0], x2_ref.shape[0]

    @pl.when(pl.program_id(1) == 0)
    def _():
        _norm_rows(x_ref, g_ref, sh_ref, sc_ref, h_ref.at[pl.ds(0, tm)], row_chunk)

    @pl.when((pl.program_id(0) == 0) & (pl.program_id(1) == 0))
    def _():
        _norm_rows(x2_ref, g_ref, sh2_ref, sc2_ref, h_ref.at[pl.ds(tm, r2)], r2)


def _ffn_up_kernel(x_ref, sh_ref, sc_ref, x2_ref, sh2_ref, sc2_ref, g_ref, wg_ref, wu_ref, *rest, row_chunk):
    n_side = (len(rest) - 3) // 2
    side_in, (a_ref, a2_ref), side_out, h_ref = (rest[:n_side], rest[n_side:n_side + 2],
                                                  rest[n_side + 2:2 * n_side + 2], rest[-1])
    tm = x_ref.shape[0]
    _norm_two_row_sets(x_ref, sh_ref, sc_ref, x2_ref, sh2_ref, sc2_ref, g_ref, h_ref, row_chunk)

    def act(h):
        for src, dst in zip(side_in, side_out):
            dst[...] = src[...].astype(dst.dtype)
        g = jnp.dot(h, wg_ref[...].astype(BF16), preferred_element_type=F32)
        u = jnp.dot(h, wu_ref[...].astype(BF16), preferred_element_type=F32)
        return (g * jax.nn.sigmoid(g) * u).astype(a_ref.dtype)

    @pl.when(pl.program_id(0) == 0)
    def _():
        a = act(h_ref[...])
        a_ref[...] = a[:tm]
        a2_ref[...] = a[tm:]

    @pl.when(pl.program_id(0) != 0)
    def _():
        a_ref[...] = act(h_ref[pl.ds(0, tm), :])


def _ffn_call(x, mods, x2, mods2, g_norm, w_in, w_out, layer, which, tm, tf, tn, round_also=()):
    m, r2 = x.shape[0], x2.shape[0]
    nb = mods[0].shape[0]
    tiles_per_seq = (m // nb) // tm
    nk = D_FF // tf
    n_steps = (m // tm) * nk
    mod_spec = pl.BlockSpec((None, 1, D_MODEL), lambda i, k: (i // tiles_per_seq, 0, 0))
    full2 = pl.BlockSpec((r2, D_MODEL), lambda i, k: (0, 0))

    side_in_specs, side_out_specs, side_out_shapes, side_args = [], [], [], []
    for w_side, lead in (((w_out, (layer, which)),) + tuple(round_also)):
        rows, cols = w_side.shape[-2:]
        rps = next(r for r in range(16, rows + 1, 16) if rows % r == 0 and r * n_steps >= rows)
        n_blk = rows // rps

        def blk(i, k, n_blk=n_blk):
            return jnp.minimum(i * nk + k, n_blk - 1)

        side_in_specs.append(pl.BlockSpec((None,) * len(lead) + (rps, cols),
                                          functools.partial(lambda i, k, lead, blk: (*lead, blk(i, k), 0),
                                                            lead=lead, blk=blk)))
        side_out_specs.append(pl.BlockSpec((rps, cols), functools.partial(lambda i, k, blk: (blk(i, k), 0), blk=blk)))
        side_out_shapes.append(jax.ShapeDtypeStruct((rows, cols), BF16))
        side_args.append(w_side)

    a, a2, w_out_bf, *rounded = pl.pallas_call(
        functools.partial(_ffn_up_kernel, row_chunk=min(tm, 64)),
        grid=(m // tm, nk),
        in_specs=[
            pl.BlockSpec((tm, D_MODEL), lambda i, k: (i, 0)),
            mod_spec, mod_spec,
            full2, full2, full2,
            pl.BlockSpec((1, D_MODEL), lambda i, k: (0, 0)),
            pl.BlockSpec((None, None, D_MODEL, tf), lambda i, k: (layer, which, 0, k)),
            pl.BlockSpec((None, None, D_MODEL, tf), lambda i, k: (layer, which, 0, k + nk)),
        ] + side_in_specs,
        out_specs=[pl.BlockSpec((tm, tf), lambda i, k: (i, k)),
                   pl.BlockSpec((r2, tf), lambda i, k: (0, jnp.where(i == 0, k, nk - 1)))] + side_out_specs,
        out_shape=[jax.ShapeDtypeStruct((m, D_FF), BF16), jax.ShapeDtypeStruct((r2, D_FF), BF16)] + side_out_shapes,
        scratch_shapes=[pltpu.VMEM((tm + r2, D_MODEL), BF16)],
        compiler_params=_params("arbitrary", "arbitrary"),
        name="ffn_up",
    )(x, mods[0], mods[1], x2, mods2[0], mods2[1], g_norm, w_in, w_in, *side_args)
    out = _proj_residual_call(a, x, mods[2], w_out_bf, (), False, tm, tn, gate_scale=0.5, name="ffn_down")
    out2 = _proj_residual_call(a2, x2, mods2[2].reshape(1, r2, D_MODEL), w_out_bf, (), False, r2, tn,
                               gate_scale=0.5, name="ffn_down")
    return out, out2, rounded


def _norm_matmul_kernel(x_ref, sh_ref, sc_ref, x2_ref, sh2_ref, sc2_ref, g_ref, w_ref, *rest, row_chunk, head_rows):
    h_ref = rest[-1]
    outs = rest[:-1]
    per_set = len(outs) // 2
    tm = x_ref.shape[0]
    _norm_two_row_sets(x_ref, sh_ref, sc_ref, x2_ref, sh2_ref, sc2_ref, g_ref, h_ref, row_chunk)
    def emit(res, refs):
        refs[0][...] = res.astype(refs[0].dtype)
        if head_rows:
            rows, cols = res.shape
            heads_per_tile = cols // HEAD_DIM
            first = (pl.program_id(1) % (head_rows // heads_per_tile)) * heads_per_tile
            for c in range(heads_per_tile):
                refs[1][pl.ds(first + c, rows, stride=head_rows), :] = res[:, c * HEAD_DIM:(c + 1) * HEAD_DIM]

    @pl.when(pl.program_id(0) == 0)
    def _():
        res = jnp.dot(h_ref[...], w_ref[...].astype(BF16), preferred_element_type=F32)
        emit(res[:tm], outs[:per_set])
        emit(res[tm:], outs[per_set:])

    @pl.when(pl.program_id(0) != 0)
    def _():
        emit(jnp.dot(h_ref[pl.ds(0, tm), :], w_ref[...].astype(BF16), preferred_element_type=F32),
             outs[:per_set])


def _norm_matmul_call(x, shift, scale, x2, shift2, scale2, g_norm, w, w_lead, out_dtype, tm, tn, head_rows=0):
    m, r2 = x.shape[0], x2.shape[0]
    nb = shift.shape[0]
    n = w.shape[-1]
    nj = n // tn
    tiles_per_seq = (m // nb) // tm
    mod_spec = pl.BlockSpec((None, 1, D_MODEL), lambda i, j: (i // tiles_per_seq, 0, 0))
    full2 = pl.BlockSpec((r2, D_MODEL), lambda i, j: (0, 0))

    def col2(i, j):
        return jnp.where(i == 0, j, nj - 1)

    if head_rows:
        slab = head_rows * HEAD_DIM
        tps = slab // tn
        out_spec = (pl.BlockSpec((None, tm, tn), lambda i, j: (j // tps, i, j % tps)),
                    pl.BlockSpec((None, tm * head_rows, HEAD_DIM), lambda i, j: (j // tps, i, 0)),
                    pl.BlockSpec((None, r2, tn), lambda i, j: (col2(i, j) // tps, 0, col2(i, j) % tps)),
                    pl.BlockSpec((None, r2 * head_rows, HEAD_DIM), lambda i, j: (col2(i, j) // tps, 0, 0)))
        out_shape = (jax.ShapeDtypeStruct((n // slab, m, slab), out_dtype),
                     jax.ShapeDtypeStruct((n // slab, m * head_rows, HEAD_DIM), out_dtype),
                     jax.ShapeDtypeStruct((n // slab, r2, slab), out_dtype),
                     jax.ShapeDtypeStruct((n // slab, r2 * head_rows, HEAD_DIM), out_dtype))
    else:
        out_spec = (pl.BlockSpec((tm, tn), lambda i, j: (i, j)),
                    pl.BlockSpec((r2, tn), lambda i, j: (0, col2(i, j))))
        out_shape = (jax.ShapeDtypeStruct((m, n), out_dtype), jax.ShapeDtypeStruct((r2, n), out_dtype))
    return pl.pallas_call(
        functools.partial(_norm_matmul_kernel, row_chunk=min(tm, 64), head_rows=head_rows),
        grid=(m // tm, nj),
        in_specs=[
            pl.BlockSpec((tm, D_MODEL), lambda i, j: (i, 0)),
            mod_spec, mod_spec,
            full2, full2, full2,
            pl.BlockSpec((1, D_MODEL), lambda i, j: (0, 0)),
            pl.BlockSpec((None,) * len(w_lead) + (D_MODEL, tn), lambda i, j: (*w_lead, 0, j)),
        ],
        out_specs=out_spec,
        out_shape=out_shape,
        scratch_shapes=[pltpu.VMEM((tm + r2, D_MODEL), BF16)],
        compiler_params=_params("arbitrary", "arbitrary"),
        name="norm_matmul",
    )(x, shift, scale, x2, shift2, scale2, g_norm, w)


def _proj_residual_kernel(z_ref, x_ref, gt_ref, *refs, glu, gate_scale):
    z = z_ref[...]
    if glu:
        wa_ref, wg_ref, o_ref = refs
        a = jnp.dot(z, wa_ref[...].astype(BF16), preferred_element_type=F32)
        g = jnp.dot(z, wg_ref[...].astype(BF16), preferred_element_type=F32)
        y = a * jax.nn.sigmoid(g)
    else:
        w_ref, o_ref = refs
        y = jnp.dot(z, w_ref[...].astype(BF16), preferred_element_type=F32)
    gate = gt_ref[...] if gate_scale == 1.0 else gate_scale * gt_ref[...]
    o_ref[...] = x_ref[...] + gate * y


def _proj_residual_call(z, x, gate, w, w_lead, glu, tm, tn, gate_scale=1.0, name="proj_residual", z_buffers=2):
    m = x.shape[0]
    kdim = z.shape[1]
    nb, r, _ = gate.shape
    tiles_per_seq = (m // nb) // tm if r == 1 else 1
    n_blocks = D_MODEL // tn
    w_block = (None,) * len(w_lead) + (kdim, tn)
    w_specs = [pl.BlockSpec(w_block, lambda i, j: (*w_lead, 0, j))]
    w_args = [w]
    if glu:
        w_specs.append(pl.BlockSpec(w_block, lambda i, j: (*w_lead, 0, j + n_blocks)))
        w_args.append(w)
    return pl.pallas_call(
        functools.partial(_proj_residual_kernel, glu=glu, gate_scale=gate_scale),
        grid=(m // tm, n_blocks),
        in_specs=[
            (pl.BlockSpec((tm, kdim), lambda i, j: (i, 0)) if z_buffers == 2 else
             pl.BlockSpec((tm, kdim), lambda i, j: (i, 0), pipeline_mode=pl.Buffered(z_buffers))),
            pl.BlockSpec((tm, tn), lambda i, j: (i, j)),
            pl.BlockSpec((None, r, tn), lambda i, j: (i // tiles_per_seq, 0, j)),
        ] + w_specs,
        out_specs=pl.BlockSpec((tm, tn), lambda i, j: (i, j)),
        out_shape=jax.ShapeDtypeStruct((m, D_MODEL), F32),
        compiler_params=_params("parallel", "arbitrary"),
        name=name,
    )(z, x, gate, *w_args)


def _final_norm_kernel(x_ref, sh_ref, sc_ref, g_ref, o_ref, *, row_chunk):
    _norm_rows(x_ref, g_ref, sh_ref, sc_ref, o_ref, row_chunk)


def _final_norm_call(x, shift, scale, g_norm, tm):
    m = x.shape[0]
    nb, r, _ = shift.shape
    tiles_per_seq = (m // nb) // tm if r == 1 else 1
    mod_spec = pl.BlockSpec((None, r, D_MODEL), lambda i: (i // tiles_per_seq, 0, 0))
    return pl.pallas_call(
        functools.partial(_final_norm_kernel, row_chunk=min(tm, 64)),
        grid=(m // tm,),
        in_specs=[
            pl.BlockSpec((tm, D_MODEL), lambda i: (i, 0)),
            mod_spec, mod_spec,
            pl.BlockSpec((1, D_MODEL), lambda i: (0, 0)),
        ],
        out_specs=pl.BlockSpec((tm, D_MODEL), lambda i: (i, 0)),
        out_shape=jax.ShapeDtypeStruct((m, D_MODEL), F32),
        compiler_params=_params("parallel"),
        name="final_norm",
    )(x, shift, scale, g_norm)


def _s5_disc_kernel(lre_ref, lim_ref, ldt_ref, bre_ref, bim_ref, are_ref, aim_ref, bbre_ref, bbim_ref):
    lam_re = lre_ref[...]
    lam_im = lim_ref[...]
    dt = jnp.exp(ldt_ref[...])
    decay = jnp.exp(lam_re * dt)
    ab_re = decay * jnp.cos(lam_im * dt)
    ab_im = decay * jnp.sin(lam_im * dt)
    den = lam_re * lam_re + lam_im * lam_im
    f_re = ((ab_re - 1.0) * lam_re + ab_im * lam_im) / den
    f_im = (ab_im * lam_re - (ab_re - 1.0) * lam_im) / den
    b_re = bre_ref[...]
    b_im = bim_ref[...]
    are_ref[...] = ab_re
    aim_ref[...] = ab_im
    bbre_ref[...] = f_re * b_re - f_im * b_im
    bbim_ref[...] = f_re * b_im + f_im * b_re


def _s5_disc_call(lam_re, lam_im, log_dt, b_re, b_im):
    shape = jax.ShapeDtypeStruct(lam_re.shape, F32)
    return pl.pallas_call(
        _s5_disc_kernel,
        out_shape=(shape, shape, shape, shape),
        name="s5_discretise",
    )(lam_re, lam_im, log_dt, b_re, b_im)


def _s5_kernel(x_ref, sh_ref, sc_ref, g_ref, bbre_ref, bbim_ref, are_ref, aim_ref, cre_ref, cim_ref,
               d_ref, h0re_ref, h0im_ref, z_ref, hre_out, him_out,
               u_scr, sre_scr, sim_scr, y_scr, pwre_scr, pwim_scr, *, row_chunk):
    t = pl.program_id(0)
    nb, steps, _ = x_ref.shape
    rows = steps * nb
    lane = HEAD_DIM
    seg = SUBLANES // nb
    seg_len = steps // seg
    assert seg in (1, 2)

    @pl.when(t == 0)
    def _():
        hre_out[...] = h0re_ref[...]
        him_out[...] = h0im_ref[...]
        if seg > 1:
            zero = jnp.zeros((nb, S5_COL_BLOCK), F32)
            for cb in range(S5_N_BLOCKS):
                a_re = jnp.broadcast_to(are_ref[cb], (nb, S5_COL_BLOCK))
                a_im = jnp.broadcast_to(aim_ref[cb], (nb, S5_COL_BLOCK))
                p_re, p_im = a_re, a_im
                for i in range(seg_len):
                    pwre_scr[cb, i * SUBLANES:(i + 1) * SUBLANES, :] = jnp.concatenate([zero, p_re], axis=0)
                    pwim_scr[cb, i * SUBLANES:(i + 1) * SUBLANES, :] = jnp.concatenate([zero, p_im], axis=0)
                    p_re, p_im = p_re * a_re - p_im * a_im, p_re * a_im + p_im * a_re

    g = g_ref[...]
    for b in range(nb):
        for s in range(seg):
            def norm_chunk(c, carry, b=b, s=s):
                r0 = pl.multiple_of(c * row_chunk, row_chunk)
                u = _rms_mod(x_ref[b, pl.ds(s * seg_len + r0, row_chunk), :], g, sh_ref[b], sc_ref[b])
                for j in range(D_MODEL // lane):
                    u_scr[j, pl.ds(r0 * SUBLANES + s * nb + b, row_chunk, stride=SUBLANES), :] = \
                        u[:, j * lane:(j + 1) * lane]
                return carry

            lax.fori_loop(0, seg_len // row_chunk, norm_chunk, 0)

    tiles_per_block = S5_CH_BLOCK // lane
    per_block_scratch = sre_scr.shape[0] == S5_N_BLOCKS
    for cb in range(S5_N_BLOCKS):
        ch = slice(cb * S5_CH_BLOCK, (cb + 1) * S5_CH_BLOCK)
        st = slice(cb * S5_COL_BLOCK, (cb + 1) * S5_COL_BLOCK)
        slab = cb if per_block_scratch else 0
        sre, sim, ys = sre_scr.at[slab], sim_scr.at[slab], y_scr.at[slab]
        u_blk = jnp.concatenate([u_scr[cb * tiles_per_block + j] for j in range(tiles_per_block)], axis=1)
        u_bf = u_blk.astype(BF16)
        sre[...] = jnp.dot(u_bf, bbre_ref[cb], preferred_element_type=F32)
        sim[...] = jnp.dot(u_bf, bbim_ref[cb], preferred_element_type=F32)

        a_re = jnp.broadcast_to(are_ref[cb], (SUBLANES, S5_COL_BLOCK))
        a_im = jnp.broadcast_to(aim_ref[cb], (SUBLANES, S5_COL_BLOCK))

        def tile_step(i, carry, a_re=a_re, a_im=a_im, sre=sre, sim=sim):
            h_re, h_im = carry
            r0 = i * SUBLANES if isinstance(i, int) else pl.multiple_of(i * SUBLANES, SUBLANES)
            n_re = a_re * h_re - a_im * h_im + sre[pl.ds(r0, SUBLANES), :]
            n_im = a_re * h_im + a_im * h_re + sim[pl.ds(r0, SUBLANES), :]
            sre[pl.ds(r0, SUBLANES), :] = n_re
            sim[pl.ds(r0, SUBLANES), :] = n_im
            return n_re, n_im

        def first_rows(v):
            return jnp.concatenate([v[:nb]] * seg, axis=0)

        h_re, h_im = hre_out[:, st], him_out[:, st]
        if seg > 1:
            zero = jnp.zeros((SUBLANES - nb, S5_COL_BLOCK), F32)
            h_re, h_im = jnp.concatenate([h_re, zero], axis=0), jnp.concatenate([h_im, zero], axis=0)
        carry = (h_re, h_im)
        if per_block_scratch:
            for i in range(seg_len):
                carry = tile_step(i, carry)
        else:
            carry = lax.fori_loop(0, seg_len, tile_step, carry, unroll=min(seg_len, 4))
        h_re, h_im = carry
        if seg > 1:
            c_re, c_im = first_rows(h_re), first_rows(h_im)
            for i in range(seg_len):
                rows_i = slice(i * SUBLANES, (i + 1) * SUBLANES)
                p_re, p_im = pwre_scr[cb, rows_i, :], pwim_scr[cb, rows_i, :]
                f_re = sre[rows_i, :] + (p_re * c_re - p_im * c_im)
                f_im = sim[rows_i, :] + (p_re * c_im + p_im * c_re)
                sre[rows_i, :] = f_re
                sim[rows_i, :] = f_im
            h_re, h_im = f_re[SUBLANES - nb:], f_im[SUBLANES - nb:]
        hre_out[:, st] = h_re
        him_out[:, st] = h_im

        y = (jnp.dot(sre[...].astype(BF16), cre_ref[cb], preferred_element_type=F32)
             - jnp.dot(sim[...].astype(BF16), cim_ref[cb], preferred_element_type=F32))
        zf = jax.nn.gelu(y + d_ref[:, ch] * u_blk)
        for j in range(tiles_per_block):
            ys[j] = zf[:, j * lane:(j + 1) * lane]
        for b in range(nb):
            for s in range(seg):
                for j in range(tiles_per_block):
                    c0 = cb * S5_CH_BLOCK + j * lane
                    z_ref[b, s * seg_len:(s + 1) * seg_len, c0:c0 + lane] = \
                        ys[j, pl.ds(s * nb + b, seg_len, stride=SUBLANES), :].astype(z_ref.dtype)


def _s5_call(x, shift, scale, g_norm, bb_re, bb_im, a_re, a_im, c_re, c_im, d_skip,
             h0_re, h0_im, steps_per_chunk, z_dtype, per_block_scratch):
    nb, seq, _ = x.shape
    assert SUBLANES % nb == 0 and seq % steps_per_chunk == 0
    tr = steps_per_chunk * nb
    slabs = S5_N_BLOCKS if per_block_scratch else 1
    seg = SUBLANES // nb
    pw_rows = tr if seg > 1 else SUBLANES
    row_chunk = min(steps_per_chunk // (SUBLANES // nb), 64)
    const2 = lambda t: (0, 0)
    const3 = lambda t: (0, 0, 0)
    state_spec = pl.BlockSpec((nb, S5_NSTATE), const2)
    io_spec = pl.BlockSpec((nb, steps_per_chunk, D_MODEL), lambda t: (0, t, 0))
    return pl.pallas_call(
        functools.partial(_s5_kernel, row_chunk=row_chunk),
        grid=(seq // steps_per_chunk,),
        in_specs=[
            io_spec,
            pl.BlockSpec((nb, 1, D_MODEL), const3),
            pl.BlockSpec((nb, 1, D_MODEL), const3),
            pl.BlockSpec((1, D_MODEL), const2),
            pl.BlockSpec((S5_N_BLOCKS, S5_CH_BLOCK, S5_COL_BLOCK), const3, pipeline_mode=pl.Buffered(1)),
            pl.BlockSpec((S5_N_BLOCKS, S5_CH_BLOCK, S5_COL_BLOCK), const3, pipeline_mode=pl.Buffered(1)),
            pl.BlockSpec((S5_N_BLOCKS, 1, S5_COL_BLOCK), const3),
            pl.BlockSpec((S5_N_BLOCKS, 1, S5_COL_BLOCK), const3),
            pl.BlockSpec((S5_N_BLOCKS, S5_COL_BLOCK, S5_CH_BLOCK), const3, pipeline_mode=pl.Buffered(1)),
            pl.BlockSpec((S5_N_BLOCKS, S5_COL_BLOCK, S5_CH_BLOCK), const3, pipeline_mode=pl.Buffered(1)),
            pl.BlockSpec((1, D_MODEL), const2),
            state_spec, state_spec,
        ],
        out_specs=(io_spec, state_spec, state_spec),
        out_shape=(jax.ShapeDtypeStruct((nb, seq, D_MODEL), z_dtype),
                   jax.ShapeDtypeStruct((nb, S5_NSTATE), F32),
                   jax.ShapeDtypeStruct((nb, S5_NSTATE), F32)),
        scratch_shapes=[pltpu.VMEM((D_MODEL // HEAD_DIM, tr, HEAD_DIM), F32),
                        pltpu.VMEM((slabs, tr, S5_COL_BLOCK), F32),
                        pltpu.VMEM((slabs, tr, S5_COL_BLOCK), F32),
                        pltpu.VMEM((slabs, S5_CH_BLOCK // HEAD_DIM, tr, HEAD_DIM), F32),
                        pltpu.VMEM((S5_N_BLOCKS, pw_rows, S5_COL_BLOCK), F32),
                        pltpu.VMEM((S5_N_BLOCKS, pw_rows, S5_COL_BLOCK), F32)],
        compiler_params=_params("arbitrary"),
        name="s5_mixer",
    )(x, shift, scale, g_norm, bb_re, bb_im, a_re, a_im, c_re, c_im, d_skip, h0_re, h0_im)


def _softmax_pv(s2, v_bf):
    m = jnp.max(s2, axis=-1, keepdims=True)
    p = jnp.exp2(s2 - m)
    l = jnp.sum(p, axis=-1, keepdims=True)
    o = jnp.dot(p.astype(BF16), v_bf, preferred_element_type=F32) / l
    return o, jnp.broadcast_to(m + jnp.log2(l), o.shape)


def _qk(q_bf, k_bf, scale=ATTN_SCALE):
    return lax.dot_general(q_bf, k_bf, (((1,), (1,)), ((), ())), preferred_element_type=F32) * scale


def _attn_prompt_kernel(q0_ref, q1_ref, q2_ref, k0_ref, v0_ref, k1_ref, v1_ref, k2_ref, v2_ref, o_ref,
                        qf_scr, o1_scr, l1_scr, o2_scr, l2_scr, band_scr, first_scr):
    seq = q0_ref.shape[0]
    qb = Q_BLOCK
    row = lax.broadcasted_iota(jnp.int32, (REP * qb, 2 * qb), 0) & (qb - 1)
    col = lax.broadcasted_iota(jnp.int32, (REP * qb, 2 * qb), 1)
    band_scr[...] = jnp.where((col >= row) & (col <= row + qb), 0.0, -jnp.inf)
    row1 = lax.broadcasted_iota(jnp.int32, (REP * qb, qb), 0) & (qb - 1)
    col1 = lax.broadcasted_iota(jnp.int32, (REP * qb, qb), 1)
    first_scr[...] = jnp.where(col1 <= row1, 0.0, -jnp.inf)

    def rows_of(start, d):
        return pl.ds(start, qb) if d == 1 else pl.ds(start, qb, stride=d)

    def load_q(src_ref, start, d):
        if d == 1:
            parts = [src_ref[pl.ds(start, qb), e * HEAD_DIM:(e + 1) * HEAD_DIM] for e in range(REP)]
        else:
            parts = [src_ref[e, rows_of(start, d), :] for e in range(REP)]
        return jnp.concatenate(parts, axis=0).astype(BF16)

    def unit(src_q, k_ref, v_ref, start, d, first):
        q4 = load_q(src_q, start, d)
        k_cur = k_ref[rows_of(start, d), :]
        v_cur = v_ref[rows_of(start, d), :]
        if first:
            s = _qk(q4, k_cur.astype(BF16), ATTN_SCALE_LOG2) + first_scr[...]
            return _softmax_pv(s, v_cur.astype(BF16))
        prev = start - qb * d
        k_band = jnp.concatenate([k_ref[rows_of(prev, d), :], k_cur], axis=0)
        v_band = jnp.concatenate([v_ref[rows_of(prev, d), :], v_cur], axis=0)
        s = _qk(q4, k_band.astype(BF16), ATTN_SCALE_LOG2) + band_scr[...]
        return _softmax_pv(s, v_band.astype(BF16))

    def stage_group(q_ref, k_ref, v_ref, d, o_scr, l_scr):
        n_blocks = seq // (qb * d)

        def copy(c, carry):
            r0 = pl.multiple_of(c * 256, 256)
            for e in range(REP):
                qf_scr[e, pl.ds(r0, 256), :] = q_ref[pl.ds(r0, 256), e * HEAD_DIM:(e + 1) * HEAD_DIM].astype(F32)
            return carry

        lax.fori_loop(0, seq // 256, copy, 0)

        def store(start, o, lse):
            for e in range(REP):
                sl = slice(e * qb, (e + 1) * qb)
                o_scr[e, rows_of(start, d), :] = o[sl]
                l_scr[e, rows_of(start, d), :] = lse[sl]

        def per_class(r, carry):
            store(r, *unit(qf_scr, k_ref, v_ref, r, d, True))

            def per_block(jb, c2):
                start = jb * (qb * d) + r
                store(start, *unit(qf_scr, k_ref, v_ref, start, d, False))
                return c2

            if n_blocks > 1:
                lax.fori_loop(1, n_blocks, per_block, 0, unroll=True)
            return carry

        lax.fori_loop(0, d, per_class, 0, unroll=4 if n_blocks == 1 else 2)

    stage_group(q2_ref, k2_ref, v2_ref, DILATED_PATTERNS[2][1], o2_scr, l2_scr)
    stage_group(q1_ref, k1_ref, v1_ref, DILATED_PATTERNS[1][1], o1_scr, l1_scr)

    def merge(start, o0, lse0):
        for e in range(REP):
            sl = slice(e * qb, (e + 1) * qb)
            cols = slice(e * HEAD_DIM, (e + 1) * HEAD_DIM)
            la = lse0[sl]
            lb = l1_scr[e, pl.ds(start, qb), :]
            lc = l2_scr[e, pl.ds(start, qb), :]
            mx = jnp.maximum(jnp.maximum(la, lb), lc)
            wa = jnp.exp2(la - mx)
            wb = jnp.exp2(lb - mx)
            wc = jnp.exp2(lc - mx)
            inv = 1.0 / (wa + wb + wc)
            acc = (wa * o0[sl] + wb * o1_scr[e, pl.ds(start, qb), :] + wc * o2_scr[e, pl.ds(start, qb), :]) * inv
            o_ref[pl.ds(start, qb), cols] = acc.astype(o_ref.dtype)

    merge(0, *unit(q0_ref, k0_ref, v0_ref, 0, 1, True))

    def per_block0(jb, carry):
        start = pl.multiple_of(jb * qb, qb)
        merge(start, *unit(q0_ref, k0_ref, v0_ref, start, 1, False))
        return carry

    n_blocks0 = seq // qb
    lax.fori_loop(1, n_blocks0, per_block0, 0, unroll=5 if (n_blocks0 - 1) % 5 == 0 else 1)


def _attn_prompt_call(q, kv):
    b, seq, _ = q.shape
    qw = REP * HEAD_DIM
    q_specs = [pl.BlockSpec((None, seq, qw), functools.partial(lambda bi, h, g: (bi, 0, g * KV_HEADS + h), g=g))
               for g in range(N_DIL)]
    kv_specs = []
    for g in range(N_DIL):
        kv_specs.append(pl.BlockSpec((None, None, seq, HEAD_DIM),
                                     functools.partial(lambda bi, h, g: (g, bi, 0, h), g=g)))
        kv_specs.append(pl.BlockSpec((None, None, seq, HEAD_DIM),
                                     functools.partial(lambda bi, h, g: (g, bi, 0, KV_HEADS + h), g=g)))
    kv_args = [kv] * (2 * N_DIL)
    return pl.pallas_call(
        _attn_prompt_kernel,
        grid=(b, KV_HEADS),
        in_specs=q_specs + kv_specs,
        out_specs=pl.BlockSpec((None, seq, qw), lambda bi, h: (bi, 0, h)),
        out_shape=jax.ShapeDtypeStruct((b, seq, HEADS * HEAD_DIM), BF16),
        scratch_shapes=[pltpu.VMEM((REP, seq, HEAD_DIM), F32),
                        pltpu.VMEM((REP, seq, HEAD_DIM), F32), pltpu.VMEM((REP, seq, HEAD_DIM), F32),
                        pltpu.VMEM((REP, seq, HEAD_DIM), F32), pltpu.VMEM((REP, seq, HEAD_DIM), F32),
                        pltpu.VMEM((REP * Q_BLOCK, 2 * Q_BLOCK), F32), pltpu.VMEM((REP * Q_BLOCK, Q_BLOCK), F32)],
        compiler_params=_params("parallel", "arbitrary"),
        name="dilated_attention_prompt",
    )(q, q, q, *kv_args)


def _attn_sample_kernel(q_ref, kvn_ref, c0_ref, c1_ref, c2_ref, o_ref, *, n_tok):
    rows = REP * n_tok
    cache_refs = (c0_ref, c1_ref, c2_ref)
    kv_rows = 2 * KV_HEADS
    for h in range(KV_HEADS):
        outs, lses = [], []
        for g, (_, d) in enumerate(DILATED_PATTERNS):
            q = q_ref[g * KV_HEADS + h]
            lw = cache_refs[g].shape[0] // kv_rows
            k_c = cache_refs[g][pl.ds(h, lw, stride=kv_rows), :].astype(BF16)
            v_c = cache_refs[g][pl.ds(KV_HEADS + h, lw, stride=kv_rows), :].astype(BF16)
            k_n = kvn_ref[g, pl.ds(h, n_tok, stride=kv_rows), :].astype(BF16)
            v_n = kvn_ref[g, pl.ds(KV_HEADS + h, n_tok, stride=kv_rows), :].astype(BF16)
            tq_c = lax.broadcasted_iota(jnp.int32, (rows, lw), 0) & (n_tok - 1)
            idx_c = lax.broadcasted_iota(jnp.int32, (rows, lw), 1)
            ok_c = (idx_c >= tq_c) & (((idx_c - tq_c) & (d - 1)) == 0)
            tq_n = lax.broadcasted_iota(jnp.int32, (rows, n_tok), 0) & (n_tok - 1)
            idx_n = lax.broadcasted_iota(jnp.int32, (rows, n_tok), 1)
            ok_n = (idx_n <= tq_n) & (((tq_n - idx_n) & (d - 1)) == 0)
            s_c = jnp.where(ok_c, _qk(q, k_c), -jnp.inf)
            s_n = jnp.where(ok_n, _qk(q, k_n), -jnp.inf)
            m = jnp.maximum(jnp.max(s_c, axis=-1, keepdims=True), jnp.max(s_n, axis=-1, keepdims=True))
            p_c = jnp.exp(s_c - m)
            p_n = jnp.exp(s_n - m)
            l = jnp.sum(p_c, axis=-1, keepdims=True) + jnp.sum(p_n, axis=-1, keepdims=True)
            o = (jnp.dot(p_c.astype(BF16), v_c, preferred_element_type=F32)
                 + jnp.dot(p_n.astype(BF16), v_n, preferred_element_type=F32)) / l
            outs.append(o)
            lses.append(m + jnp.log(l))
        mx = jnp.maximum(jnp.maximum(lses[0], lses[1]), lses[2])
        w = [jnp.exp(x - mx) for x in lses]
        inv = 1.0 / (w[0] + w[1] + w[2])
        acc = (w[0] * inv) * outs[0] + (w[1] * inv) * outs[1] + (w[2] * inv) * outs[2]
        o_ref[h] = acc.astype(o_ref.dtype)


def _attn_sample_call(q_heads, kv_new, caches, n_tok):
    b = q_heads.shape[0]
    rows = REP * n_tok
    assert n_tok & (n_tok - 1) == 0
    in_specs = [pl.BlockSpec((None, N_DIL * KV_HEADS, rows, HEAD_DIM), lambda bi: (bi, 0, 0, 0)),
                pl.BlockSpec((N_DIL, None, kv_new.shape[2], HEAD_DIM), lambda bi: (0, bi, 0, 0))]
    in_specs += [pl.BlockSpec((None, c.shape[1], HEAD_DIM), lambda bi: (bi, 0, 0)) for c in caches]
    return pl.pallas_call(
        functools.partial(_attn_sample_kernel, n_tok=n_tok),
        grid=(b,),
        in_specs=in_specs,
        out_specs=pl.BlockSpec((None, KV_HEADS, rows, HEAD_DIM), lambda bi: (bi, 0, 0, 0)),
        out_shape=jax.ShapeDtypeStruct((b, KV_HEADS, rows, HEAD_DIM), BF16),
        compiler_params=_params("parallel"),
        name="dilated_attention_sample",
    )(q_heads, kv_new, *caches)


def _s5_weights(lam_re, lam_im, log_dt, b_re, b_im, c_re, c_im):
    g, p, c = S5_GROUPS, S5_STATE, S5_GROUP
    rep = lambda a: jnp.repeat(a, c, axis=1)
    a_re, a_im, bb_re, bb_im = _s5_disc_call(rep(lam_re), rep(lam_im), log_dt.reshape(g, 1),
                                             b_re.reshape(g, p * c), b_im.reshape(g, p * c))
    gl = S5_CH_BLOCK // c

    def block_diag(t, rows_per_group, cols_per_group):
        tiled = jnp.tile(t, (1, 1, gl))
        row_g = lax.broadcasted_iota(jnp.int32, tiled.shape, 1) // rows_per_group
        col_g = lax.broadcasted_iota(jnp.int32, tiled.shape, 2) // cols_per_group
        return jnp.where(row_g == col_g, tiled, 0.0).astype(BF16)

    def in_blocks(bb):
        t = bb.reshape(S5_N_BLOCKS, gl, p, c).transpose(0, 1, 3, 2)
        return block_diag(t.reshape(S5_N_BLOCKS, gl * c, p), c, p)

    def out_blocks(cm):
        t = cm.reshape(S5_N_BLOCKS, gl, c, p).transpose(0, 1, 3, 2)
        return block_diag(t.reshape(S5_N_BLOCKS, gl * p, c), p, c)

    lam_bar = lambda a: a.reshape(g, p, c)[:, :, 0].reshape(S5_N_BLOCKS, 1, S5_COL_BLOCK)
    return (in_blocks(bb_re), in_blocks(bb_im), lam_bar(a_re), lam_bar(a_im),
            out_blocks(c_re), out_blocks(c_im))


def _trunks(x_p, x_s, mod_p, mod_s, s5_state, kv_caches, wts, tiles):
    (norm_g, ffn_w_in, ffn_w_out, s5_mats, s5_d, s5_w_glu, kv_norm_g, w_kv,
     attn_w_q, attn_w_o, final_norm_g) = wts
    bp, lp, _ = x_p.shape
    bs, ls, _ = x_s.shape
    mp, ms = bp * lp, bs * ls
    tm = tiles["tm"]
    kv_rows = 2 * KV_HEADS

    def mods_p(*ks):
        return tuple(mod_p[:, k].reshape(bp, 1, D_MODEL) for k in ks)

    def mods_s(*ks):
        return tuple(jnp.repeat(mod_s[:, k], ls, axis=0) for k in ks)

    def as_block(a):
        return a.reshape(1, *a.shape)

    def norm_g_row(a):
        return a.reshape(1, D_MODEL)

    xp = x_p.reshape(mp, D_MODEL)
    xs = x_s.reshape(ms, D_MODEL)
    re_p, im_p, re_s, im_s = [], [], [], []
    for layer in range(DEPTH):
        base = 9 * layer
        if layer == N_A_LAYERS:
            k0 = 9 * DEPTH
            kvf_p, kv_p, _, kv_s = _norm_matmul_call(xp, *mods_p(k0, k0 + 1), xs, *mods_s(k0, k0 + 1),
                                                     norm_g_row(kv_norm_g), w_kv_bf, (), F32, tm,
                                                     KV_HEADS * HEAD_DIM, head_rows=kv_rows)
        later = () if layer < N_A_LAYERS else ((attn_w_q, (layer - N_A_LAYERS,)), (attn_w_o, (layer - N_A_LAYERS,)))
        xp, xs, rounded = _ffn_call(xp, mods_p(base, base + 1, base + 2), xs, mods_s(base, base + 1, base + 2),
                                    norm_g_row(norm_g[layer, 0]), ffn_w_in, ffn_w_out, layer, 0, tm, tiles["tf"],
                                    tiles["tn_down"], round_also=later)
        g_mix = norm_g_row(norm_g[layer, 1])
        if layer < N_A_LAYERS:
            d_skip = s5_d[layer].reshape(1, D_MODEL)
            zero_state = jnp.zeros((bp, S5_NSTATE), F32)
            zp, h_re, h_im = _s5_call(xp.reshape(bp, lp, D_MODEL), *mods_p(base + 3, base + 4), g_mix,
                                      *s5_mats[layer], d_skip, zero_state, zero_state,
                                      min(lp, tiles["s5_steps"]), BF16, True)
            re_p.append(h_re.reshape(bp, S5_GROUPS, S5_STATE))
            im_p.append(h_im.reshape(bp, S5_GROUPS, S5_STATE))
            zs, h_re, h_im = _s5_call(xs.reshape(bs, ls, D_MODEL), mod_s[:, base + 3].reshape(bs, 1, D_MODEL),
                                      mod_s[:, base + 4].reshape(bs, 1, D_MODEL), g_mix, *s5_mats[layer], d_skip,
                                      s5_state[0][layer].reshape(bs, S5_NSTATE),
                                      s5_state[1][layer].reshape(bs, S5_NSTATE), ls, F32, False)
            re_s.append(h_re.reshape(bs, S5_GROUPS, S5_STATE))
            im_s.append(h_im.reshape(bs, S5_GROUPS, S5_STATE))
            xp = _proj_residual_call(zp.reshape(mp, D_MODEL), xp, *mods_p(base + 5), s5_w_glu, (layer,),
                                     True, tm, 512, name="s5_glu")
            xs = _proj_residual_call(zs.reshape(ms, D_MODEL).astype(BF16), xs, as_block(*mods_s(base + 5)),
                                     s5_w_glu, (layer,), True, ms, 512, name="s5_glu")
        else:
            w_q_bf, w_o_bf = rounded
            q_p, q_s = _norm_matmul_call(xp, *mods_p(base + 3, base + 4), xs, *mods_s(base + 3, base + 4), g_mix,
                                         w_q_bf, (), BF16, tm, 1024)
            o_p = _attn_prompt_call(q_p.reshape(bp, lp, N_DIL * HEADS * HEAD_DIM),
                                    kvf_p.reshape(N_DIL, bp, lp, kv_rows * HEAD_DIM)).reshape(mp, D_MODEL)
            qh = q_s.reshape(bs, ls, N_DIL, KV_HEADS, REP, HEAD_DIM).transpose(0, 2, 3, 4, 1, 5)
            qh = qh.reshape(bs, N_DIL * KV_HEADS, REP * ls, HEAD_DIM)
            o_s = _attn_sample_call(qh, kv_s.reshape(N_DIL, bs, ls * kv_rows, HEAD_DIM), kv_caches, ls)
            o_s = o_s.reshape(bs, KV_HEADS, REP, ls, HEAD_DIM).transpose(0, 3, 1, 2, 4).reshape(ms, D_MODEL)
            xp = _proj_residual_call(o_p, xp, *mods_p(base + 5), w_o_bf, (), False, tiles["tm_out"], 512,
                                     name="attn_out")
            xs = _proj_residual_call(o_s, xs, as_block(*mods_s(base + 5)), w_o_bf, (), False, ms, 512,
                                     name="attn_out")
        later = ((w_kv, ()),) if layer == N_A_LAYERS - 1 else ()
        xp, xs, rounded = _ffn_call(xp, mods_p(base + 6, base + 7, base + 8), xs, mods_s(base + 6, base + 7, base + 8),
                                    norm_g_row(norm_g[layer, 2]), ffn_w_in, ffn_w_out, layer, 1, tm, tiles["tf"],
                                    tiles["tn_down"], round_also=later)
        if later:
            (w_kv_bf,) = rounded
    k1 = 9 * DEPTH + 2
    y_p = _final_norm_call(xp, *mods_p(k1, k1 + 1), norm_g_row(final_norm_g), tm)
    y_s = _final_norm_call(xs, *(as_block(a) for a in mods_s(k1, k1 + 1)), norm_g_row(final_norm_g), ms)
    kv_p = kv_p.reshape(N_DIL, bp, lp, 2, KV_HEADS, HEAD_DIM)
    kv_s = kv_s.reshape(N_DIL, bs, ls, 2, KV_HEADS, HEAD_DIM)
    return ((y_p.reshape(bp, lp, D_MODEL), jnp.stack(re_p, axis=0), jnp.stack(im_p, axis=0), kv_p),
            (y_s.reshape(bs, ls, D_MODEL), jnp.stack(re_s, axis=0), jnp.stack(im_s, axis=0), kv_s))


def kernel(x_prompt, x_sample, c_prompt, c_sample, state_s5_re, state_s5_im, cache_kv_g0, cache_kv_g1, cache_kv_g2, w_mod, b_mod, norm_g, ffn_w_in, ffn_w_out, s5_lambda_re, s5_lambda_im, s5_log_dt, s5_b_re, s5_b_im, s5_c_re, s5_c_im, s5_d, s5_w_glu, kv_norm_g, w_kv, attn_w_q, attn_w_o, final_norm_g):
    bp, seq_p, _ = x_prompt.shape
    bs, seq_s, _ = x_sample.shape
    caches = (cache_kv_g0, cache_kv_g1, cache_kv_g2)
    for (w, d), c in zip(DILATED_PATTERNS, caches):
        assert c.shape[1] == w and w == d * Q_BLOCK and seq_p % (d * Q_BLOCK) == 0
    assert bp <= SUBLANES and bs <= SUBLANES

    n_c = bp + bs
    c_all = jnp.pad(jnp.concatenate([c_prompt, c_sample], axis=0), ((0, (-n_c) % SUBLANES), (0, 0)))
    mod_all = _mod_call(c_all, w_mod, b_mod)
    mod_p = mod_all[:bp].reshape(bp, N_MOD, D_MODEL)
    mod_s = mod_all[bp:n_c].reshape(bs, N_MOD, D_MODEL)

    s5_mats = [_s5_weights(s5_lambda_re[l], s5_lambda_im[l], s5_log_dt[l], s5_b_re[l], s5_b_im[l],
                           s5_c_re[l], s5_c_im[l]) for l in range(N_A_LAYERS)]
    wts = (norm_g, ffn_w_in, ffn_w_out, s5_mats, s5_d, s5_w_glu, kv_norm_g, w_kv, attn_w_q, attn_w_o, final_norm_g)

    caches_flat = tuple(c.reshape(bs, c.shape[1] * 2 * KV_HEADS, HEAD_DIM) for c in caches)
    (y_p, re_p, im_p, kv_p), (y_s, re_s, im_s, kv_s) = _trunks(
        x_prompt, x_sample, mod_p, mod_s, (state_s5_re, state_s5_im), caches_flat, wts,
        {"tm": 1024, "tf": 512, "tn_down": 512, "tm_out": 2048, "s5_steps": 32})

    kvp = [kv_p[g][:, seq_p - min(w, seq_p):] for g, (w, _) in enumerate(DILATED_PATTERNS)]
    return (y_p, y_s, re_p, im_p, kvp[0], kvp[1], kvp[2], re_s, im_s, kv_s[0], kv_s[1], kv_s[2])
```

```python
import functools

import jax
import jax.numpy as jnp
from jax import lax
from jax.experimental import pallas as pl
from jax.experimental.pallas import tpu as pltpu

F32 = jnp.float32
BF16 = jnp.bfloat16

D_MODEL = 2048
DEPTH = 4
N_A_LAYERS = DEPTH // 2
D_FF = 5632
S5_GROUP = 16
S5_GROUPS = D_MODEL // S5_GROUP
S5_STATE = 64
S5_NSTATE = S5_GROUPS * S5_STATE
HEAD_DIM = 128
HEADS = 16
KV_HEADS = 4
REP = HEADS // KV_HEADS
DILATED_PATTERNS = ((128, 1), (512, 4), (2048, 16))
N_DIL = len(DILATED_PATTERNS)
Q_BLOCK = 128
N_MOD = 9 * DEPTH + 4
EPS = 1e-6
ATTN_SCALE = HEAD_DIM ** -0.5
ATTN_SCALE_LOG2 = ATTN_SCALE * 1.4426950408889634

VMEM_LIMIT_BYTES = 56 * 1024 * 1024
SUBLANES = 8
BF16_SUBLANES = 16
NORM_ROWS = 64
STAGE_ROWS = 256
S5_COL_BLOCK = 1024
S5_CH_BLOCK = S5_COL_BLOCK // S5_STATE * S5_GROUP
S5_N_BLOCKS = S5_NSTATE // S5_COL_BLOCK


_TILES = {"tm": 1024, "tf": 512, "tn_down": 512, "tm_out": 2048, "tn_mix": 512, "tn_q": 1024, "tn_mod": 1024,
          "s5_steps": 32}


def _params(*sem):
    return pltpu.CompilerParams(dimension_semantics=sem, vmem_limit_bytes=VMEM_LIMIT_BYTES)


def _rms_mod(x, g, shift, scale):
    xn = x * lax.rsqrt(jnp.mean(x * x, axis=-1, keepdims=True) + EPS)
    return xn * g * (1.0 + scale) + shift


def _norm_rows(x_ref, g_ref, sh_ref, sc_ref, h_ref, row_chunk):
    rows = x_ref.shape[0]
    g = g_ref[...]
    mod_rows = sh_ref.shape[0]
    if mod_rows == 1:
        gs = g * (1.0 + sc_ref[...])
        sh = sh_ref[...]
        row_chunk = min(row_chunk, BF16_SUBLANES)

        def body(c, carry):
            r0 = pl.multiple_of(c * row_chunk, row_chunk)
            x = x_ref[pl.ds(r0, row_chunk), :]
            xn = x * lax.rsqrt(jnp.mean(x * x, axis=-1, keepdims=True) + EPS)
            h_ref[pl.ds(r0, row_chunk), :] = (xn * gs + sh).astype(h_ref.dtype)
            return carry

        n_chunks = rows // row_chunk
        lax.fori_loop(0, n_chunks, body, 0, unroll=4 if n_chunks % 4 == 0 else 1)
        return

    def body(c, carry):
        r0 = pl.multiple_of(c * row_chunk, row_chunk)
        x = x_ref[pl.ds(r0, row_chunk), :]
        if mod_rows == rows and rows != row_chunk:
            sh = sh_ref[pl.ds(r0, row_chunk), :]
            sc = sc_ref[pl.ds(r0, row_chunk), :]
        else:
            sh = sh_ref[...]
            sc = sc_ref[...]
        h_ref[pl.ds(r0, row_chunk), :] = _rms_mod(x, g, sh, sc).astype(h_ref.dtype)
        return carry

    lax.fori_loop(0, rows // row_chunk, body, 0)


def _mod_kernel(c_ref, w_ref, b_ref, o_ref):
    c = c_ref[...]
    a = (c * jax.nn.sigmoid(c)).astype(BF16)
    o_ref[...] = jnp.dot(a, w_ref[...].astype(BF16), preferred_element_type=F32) + b_ref[...]


def _mod_call(c_all, w_mod, b_mod):
    rows = c_all.shape[0]
    n = w_mod.shape[1]
    tn = _TILES["tn_mod"]
    return pl.pallas_call(
        _mod_kernel,
        grid=(n // tn,),
        in_specs=[
            pl.BlockSpec((rows, D_MODEL), lambda j: (0, 0)),
            pl.BlockSpec((D_MODEL, tn), lambda j: (0, j)),
            pl.BlockSpec((1, tn), lambda j: (0, j)),
        ],
        out_specs=pl.BlockSpec((rows, tn), lambda j: (0, j)),
        out_shape=jax.ShapeDtypeStruct((rows, n), F32),
        compiler_params=_params("arbitrary"),
        name="adaln_mod",
    )(c_all, w_mod, b_mod.reshape(1, n))


def _norm_two_row_sets(x_ref, sh_ref, sc_ref, x2_ref, sh2_ref, sc2_ref, g_ref, h_ref, row_chunk):
    tm, r2 = x_ref.shape[0], x2_ref.shape[0]

    @pl.when(pl.program_id(1) == 0)
    def _():
        _norm_rows(x_ref, g_ref, sh_ref, sc_ref, h_ref.at[pl.ds(0, tm)], row_chunk)

    @pl.when((pl.program_id(0) == 0) & (pl.program_id(1) == 0))
    def _():
        _norm_rows(x2_ref, g_ref, sh2_ref, sc2_ref, h_ref.at[pl.ds(tm, r2)], r2)


def _ffn_up_kernel(x_ref, sh_ref, sc_ref, x2_ref, sh2_ref, sc2_ref, g_ref, wg_ref, wu_ref, *rest, row_chunk):
    n_side = (len(rest) - 3) // 2
    side_in, (a_ref, a2_ref), side_out, h_ref = (rest[:n_side], rest[n_side:n_side + 2],
                                                  rest[n_side + 2:2 * n_side + 2], rest[-1])
    tm = x_ref.shape[0]
    _norm_two_row_sets(x_ref, sh_ref, sc_ref, x2_ref, sh2_ref, sc2_ref, g_ref, h_ref, row_chunk)

    def act(h):
        for src, dst in zip(side_in, side_out):
            dst[...] = src[...].astype(dst.dtype)
        g = jnp.dot(h, wg_ref[...].astype(BF16), preferred_element_type=F32)
        u = jnp.dot(h, wu_ref[...].astype(BF16), preferred_element_type=F32)
        return (g * jax.nn.sigmoid(g) * u).astype(a_ref.dtype)

    @pl.when(pl.program_id(0) == 0)
    def _():
        a = act(h_ref[...])
        a_ref[...] = a[:tm]
        a2_ref[...] = a[tm:]

    @pl.when(pl.program_id(0) != 0)
    def _():
        a_ref[...] = act(h_ref[pl.ds(0, tm), :])


def _ffn_call(x, mods, x2, mods2, g_norm, w_in, w_out, layer, which, tm, tf, tn, round_also=()):
    m, r2 = x.shape[0], x2.shape[0]
    nb = mods[0].shape[0]
    tiles_per_seq = (m // nb) // tm
    nk = D_FF // tf
    n_steps = (m // tm) * nk
    mod_spec = pl.BlockSpec((None, 1, D_MODEL), lambda i, k: (i // tiles_per_seq, 0, 0))
    full2 = pl.BlockSpec((r2, D_MODEL), lambda i, k: (0, 0))

    side_in_specs, side_out_specs, side_out_shapes, side_args = [], [], [], []
    for w_side, lead in (((w_out, (layer, which)),) + tuple(round_also)):
        rows, cols = w_side.shape[-2:]
        rps = next(r for r in range(BF16_SUBLANES, rows + 1, BF16_SUBLANES)
                   if rows % r == 0 and r * n_steps >= rows)
        n_blk = rows // rps

        def blk(i, k, n_blk=n_blk):
            return jnp.minimum(i * nk + k, n_blk - 1)

        side_in_specs.append(pl.BlockSpec((None,) * len(lead) + (rps, cols),
                                          functools.partial(lambda i, k, lead, blk: (*lead, blk(i, k), 0),
                                                            lead=lead, blk=blk)))
        side_out_specs.append(pl.BlockSpec((rps, cols), functools.partial(lambda i, k, blk: (blk(i, k), 0), blk=blk)))
        side_out_shapes.append(jax.ShapeDtypeStruct((rows, cols), BF16))
        side_args.append(w_side)

    a, a2, w_out_bf, *rounded = pl.pallas_call(
        functools.partial(_ffn_up_kernel, row_chunk=min(tm, NORM_ROWS)),
        grid=(m // tm, nk),
        in_specs=[
            pl.BlockSpec((tm, D_MODEL), lambda i, k: (i, 0)),
            mod_spec, mod_spec,
            full2, full2, full2,
            pl.BlockSpec((1, D_MODEL), lambda i, k: (0, 0)),
            pl.BlockSpec((None, None, D_MODEL, tf), lambda i, k: (layer, which, 0, k)),
            pl.BlockSpec((None, None, D_MODEL, tf), lambda i, k: (layer, which, 0, k + nk)),
        ] + side_in_specs,
        out_specs=[pl.BlockSpec((tm, tf), lambda i, k: (i, k)),
                   pl.BlockSpec((r2, tf), lambda i, k: (0, jnp.where(i == 0, k, nk - 1)))] + side_out_specs,
        out_shape=[jax.ShapeDtypeStruct((m, D_FF), BF16), jax.ShapeDtypeStruct((r2, D_FF), BF16)] + side_out_shapes,
        scratch_shapes=[pltpu.VMEM((tm + r2, D_MODEL), BF16)],
        compiler_params=_params("arbitrary", "arbitrary"),
        name="ffn_up",
    )(x, mods[0], mods[1], x2, mods2[0], mods2[1], g_norm, w_in, w_in, *side_args)
    out = _proj_residual_call(a, x, mods[2], w_out_bf, (), False, tm, tn, gate_scale=0.5, name="ffn_down")
    out2 = _proj_residual_call(a2, x2, mods2[2].reshape(1, r2, D_MODEL), w_out_bf, (), False, r2, tn,
                               gate_scale=0.5, name="ffn_down")
    return out, out2, rounded


def _norm_matmul_kernel(x_ref, sh_ref, sc_ref, x2_ref, sh2_ref, sc2_ref, g_ref, w_ref, *rest, row_chunk, head_rows):
    h_ref = rest[-1]
    outs = rest[:-1]
    per_set = len(outs) // 2
    tm = x_ref.shape[0]
    _norm_two_row_sets(x_ref, sh_ref, sc_ref, x2_ref, sh2_ref, sc2_ref, g_ref, h_ref, row_chunk)
    def emit(res, refs):
        refs[0][...] = res.astype(refs[0].dtype)
        if head_rows:
            rows, cols = res.shape
            heads_per_tile = cols // HEAD_DIM
            first = (pl.program_id(1) % (head_rows // heads_per_tile)) * heads_per_tile
            for c in range(heads_per_tile):
                refs[1][pl.ds(first + c, rows, stride=head_rows), :] = res[:, c * HEAD_DIM:(c + 1) * HEAD_DIM]

    @pl.when(pl.program_id(0) == 0)
    def _():
        res = jnp.dot(h_ref[...], w_ref[...].astype(BF16), preferred_element_type=F32)
        emit(res[:tm], outs[:per_set])
        emit(res[tm:], outs[per_set:])

    @pl.when(pl.program_id(0) != 0)
    def _():
        emit(jnp.dot(h_ref[pl.ds(0, tm), :], w_ref[...].astype(BF16), preferred_element_type=F32),
             outs[:per_set])


def _norm_matmul_call(x, shift, scale, x2, shift2, scale2, g_norm, w, w_lead, out_dtype, tm, tn, head_rows=0):
    m, r2 = x.shape[0], x2.shape[0]
    nb = shift.shape[0]
    n = w.shape[-1]
    nj = n // tn
    tiles_per_seq = (m // nb) // tm
    mod_spec = pl.BlockSpec((None, 1, D_MODEL), lambda i, j: (i // tiles_per_seq, 0, 0))
    full2 = pl.BlockSpec((r2, D_MODEL), lambda i, j: (0, 0))

    def col2(i, j):
        return jnp.where(i == 0, j, nj - 1)

    if head_rows:
        slab = head_rows * HEAD_DIM
        tps = slab // tn
        out_spec = (pl.BlockSpec((None, tm, tn), lambda i, j: (j // tps, i, j % tps)),
                    pl.BlockSpec((None, tm * head_rows, HEAD_DIM), lambda i, j: (j // tps, i, 0)),
                    pl.BlockSpec((None, r2, tn), lambda i, j: (col2(i, j) // tps, 0, col2(i, j) % tps)),
                    pl.BlockSpec((None, r2 * head_rows, HEAD_DIM), lambda i, j: (col2(i, j) // tps, 0, 0)))
        out_shape = (jax.ShapeDtypeStruct((n // slab, m, slab), out_dtype),
                     jax.ShapeDtypeStruct((n // slab, m * head_rows, HEAD_DIM), out_dtype),
                     jax.ShapeDtypeStruct((n // slab, r2, slab), out_dtype),
                     jax.ShapeDtypeStruct((n // slab, r2 * head_rows, HEAD_DIM), out_dtype))
    else:
        out_spec = (pl.BlockSpec((tm, tn), lambda i, j: (i, j)),
                    pl.BlockSpec((r2, tn), lambda i, j: (0, col2(i, j))))
        out_shape = (jax.ShapeDtypeStruct((m, n), out_dtype), jax.ShapeDtypeStruct((r2, n), out_dtype))
    return pl.pallas_call(
        functools.partial(_norm_matmul_kernel, row_chunk=min(tm, NORM_ROWS), head_rows=head_rows),
        grid=(m // tm, nj),
        in_specs=[
            pl.BlockSpec((tm, D_MODEL), lambda i, j: (i, 0)),
            mod_spec, mod_spec,
            full2, full2, full2,
            pl.BlockSpec((1, D_MODEL), lambda i, j: (0, 0)),
            pl.BlockSpec((None,) * len(w_lead) + (D_MODEL, tn), lambda i, j: (*w_lead, 0, j)),
        ],
        out_specs=out_spec,
        out_shape=out_shape,
        scratch_shapes=[pltpu.VMEM((tm + r2, D_MODEL), BF16)],
        compiler_params=_params("arbitrary", "arbitrary"),
        name="norm_matmul",
    )(x, shift, scale, x2, shift2, scale2, g_norm, w)


def _proj_residual_kernel(z_ref, x_ref, gt_ref, *refs, glu, gate_scale):
    z = z_ref[...]
    if glu:
        wa_ref, wg_ref, o_ref = refs
        a = jnp.dot(z, wa_ref[...].astype(BF16), preferred_element_type=F32)
        g = jnp.dot(z, wg_ref[...].astype(BF16), preferred_element_type=F32)
        y = a * jax.nn.sigmoid(g)
    else:
        w_ref, o_ref = refs
        y = jnp.dot(z, w_ref[...].astype(BF16), preferred_element_type=F32)
    gate = gt_ref[...] if gate_scale == 1.0 else gate_scale * gt_ref[...]
    o_ref[...] = x_ref[...] + gate * y


def _proj_residual_call(z, x, gate, w, w_lead, glu, tm, tn, gate_scale=1.0, name="proj_residual", z_buffers=2):
    m = x.shape[0]
    kdim = z.shape[1]
    nb, r, _ = gate.shape
    tiles_per_seq = (m // nb) // tm if r == 1 else 1
    n_blocks = D_MODEL // tn
    w_block = (None,) * len(w_lead) + (kdim, tn)
    w_specs = [pl.BlockSpec(w_block, lambda i, j: (*w_lead, 0, j))]
    w_args = [w]
    if glu:
        w_specs.append(pl.BlockSpec(w_block, lambda i, j: (*w_lead, 0, j + n_blocks)))
        w_args.append(w)
    return pl.pallas_call(
        functools.partial(_proj_residual_kernel, glu=glu, gate_scale=gate_scale),
        grid=(m // tm, n_blocks),
        in_specs=[
            (pl.BlockSpec((tm, kdim), lambda i, j: (i, 0)) if z_buffers == 2 else
             pl.BlockSpec((tm, kdim), lambda i, j: (i, 0), pipeline_mode=pl.Buffered(z_buffers))),
            pl.BlockSpec((tm, tn), lambda i, j: (i, j)),
            pl.BlockSpec((None, r, tn), lambda i, j: (i // tiles_per_seq, 0, j)),
        ] + w_specs,
        out_specs=pl.BlockSpec((tm, tn), lambda i, j: (i, j)),
        out_shape=jax.ShapeDtypeStruct((m, D_MODEL), F32),
        compiler_params=_params("parallel", "arbitrary"),
        name=name,
    )(z, x, gate, *w_args)


def _final_norm_kernel(x_ref, sh_ref, sc_ref, g_ref, o_ref, *, row_chunk):
    _norm_rows(x_ref, g_ref, sh_ref, sc_ref, o_ref, row_chunk)


def _final_norm_call(x, shift, scale, g_norm, tm):
    m = x.shape[0]
    nb, r, _ = shift.shape
    tiles_per_seq = (m // nb) // tm if r == 1 else 1
    mod_spec = pl.BlockSpec((None, r, D_MODEL), lambda i: (i // tiles_per_seq, 0, 0))
    return pl.pallas_call(
        functools.partial(_final_norm_kernel, row_chunk=min(tm, NORM_ROWS)),
        grid=(m // tm,),
        in_specs=[
            pl.BlockSpec((tm, D_MODEL), lambda i: (i, 0)),
            mod_spec, mod_spec,
            pl.BlockSpec((1, D_MODEL), lambda i: (0, 0)),
        ],
        out_specs=pl.BlockSpec((tm, D_MODEL), lambda i: (i, 0)),
        out_shape=jax.ShapeDtypeStruct((m, D_MODEL), F32),
        compiler_params=_params("parallel"),
        name="final_norm",
    )(x, shift, scale, g_norm)


def _s5_disc_kernel(lre_ref, lim_ref, ldt_ref, bre_ref, bim_ref, are_ref, aim_ref, bbre_ref, bbim_ref):
    lam_re = lre_ref[...]
    lam_im = lim_ref[...]
    dt = jnp.exp(ldt_ref[...])
    decay = jnp.exp(lam_re * dt)
    ab_re = decay * jnp.cos(lam_im * dt)
    ab_im = decay * jnp.sin(lam_im * dt)
    den = lam_re * lam_re + lam_im * lam_im
    f_re = ((ab_re - 1.0) * lam_re + ab_im * lam_im) / den
    f_im = (ab_im * lam_re - (ab_re - 1.0) * lam_im) / den
    b_re = bre_ref[...]
    b_im = bim_ref[...]
    are_ref[...] = ab_re
    aim_ref[...] = ab_im
    bbre_ref[...] = f_re * b_re - f_im * b_im
    bbim_ref[...] = f_re * b_im + f_im * b_re


def _s5_disc_call(lam_re, lam_im, log_dt, b_re, b_im):
    shape = jax.ShapeDtypeStruct(lam_re.shape, F32)
    return pl.pallas_call(
        _s5_disc_kernel,
        out_shape=(shape, shape, shape, shape),
        name="s5_discretise",
    )(lam_re, lam_im, log_dt, b_re, b_im)


def _s5_kernel(x_ref, sh_ref, sc_ref, g_ref, bbre_ref, bbim_ref, are_ref, aim_ref, cre_ref, cim_ref,
               d_ref, h0re_ref, h0im_ref, z_ref, hre_out, him_out,
               u_scr, sre_scr, sim_scr, y_scr, pwre_scr, pwim_scr, *, row_chunk):
    t = pl.program_id(0)
    nb, steps, _ = x_ref.shape
    rows = steps * nb
    lane = HEAD_DIM
    seg = SUBLANES // nb
    seg_len = steps // seg
    assert seg in (1, 2)

    @pl.when(t == 0)
    def _():
        hre_out[...] = h0re_ref[...]
        him_out[...] = h0im_ref[...]
        if seg > 1:
            zero = jnp.zeros((nb, S5_COL_BLOCK), F32)
            for cb in range(S5_N_BLOCKS):
                a_re = jnp.broadcast_to(are_ref[cb], (nb, S5_COL_BLOCK))
                a_im = jnp.broadcast_to(aim_ref[cb], (nb, S5_COL_BLOCK))
                p_re, p_im = a_re, a_im
                for i in range(seg_len):
                    pwre_scr[cb, i * SUBLANES:(i + 1) * SUBLANES, :] = jnp.concatenate([zero, p_re], axis=0)
                    pwim_scr[cb, i * SUBLANES:(i + 1) * SUBLANES, :] = jnp.concatenate([zero, p_im], axis=0)
                    p_re, p_im = p_re * a_re - p_im * a_im, p_re * a_im + p_im * a_re

    g = g_ref[...]
    for b in range(nb):
        for s in range(seg):
            def norm_chunk(c, carry, b=b, s=s):
                r0 = pl.multiple_of(c * row_chunk, row_chunk)
                u = _rms_mod(x_ref[b, pl.ds(s * seg_len + r0, row_chunk), :], g, sh_ref[b], sc_ref[b])
                for j in range(D_MODEL // lane):
                    u_scr[j, pl.ds(r0 * SUBLANES + s * nb + b, row_chunk, stride=SUBLANES), :] = \
                        u[:, j * lane:(j + 1) * lane]
                return carry

            lax.fori_loop(0, seg_len // row_chunk, norm_chunk, 0)

    tiles_per_block = S5_CH_BLOCK // lane
    per_block_scratch = sre_scr.shape[0] == S5_N_BLOCKS
    for cb in range(S5_N_BLOCKS):
        ch = slice(cb * S5_CH_BLOCK, (cb + 1) * S5_CH_BLOCK)
        st = slice(cb * S5_COL_BLOCK, (cb + 1) * S5_COL_BLOCK)
        slab = cb if per_block_scratch else 0
        sre, sim, ys = sre_scr.at[slab], sim_scr.at[slab], y_scr.at[slab]
        u_blk = jnp.concatenate([u_scr[cb * tiles_per_block + j] for j in range(tiles_per_block)], axis=1)
        u_bf = u_blk.astype(BF16)
        sre[...] = jnp.dot(u_bf, bbre_ref[cb], preferred_element_type=F32)
        sim[...] = jnp.dot(u_bf, bbim_ref[cb], preferred_element_type=F32)

        a_re = jnp.broadcast_to(are_ref[cb], (SUBLANES, S5_COL_BLOCK))
        a_im = jnp.broadcast_to(aim_ref[cb], (SUBLANES, S5_COL_BLOCK))

        def tile_step(i, carry, a_re=a_re, a_im=a_im, sre=sre, sim=sim):
            h_re, h_im = carry
            r0 = i * SUBLANES if isinstance(i, int) else pl.multiple_of(i * SUBLANES, SUBLANES)
            n_re = a_re * h_re - a_im * h_im + sre[pl.ds(r0, SUBLANES), :]
            n_im = a_re * h_im + a_im * h_re + sim[pl.ds(r0, SUBLANES), :]
            sre[pl.ds(r0, SUBLANES), :] = n_re
            sim[pl.ds(r0, SUBLANES), :] = n_im
            return n_re, n_im

        def first_rows(v):
            return jnp.concatenate([v[:nb]] * seg, axis=0)

        h_re, h_im = hre_out[:, st], him_out[:, st]
        if seg > 1:
            zero = jnp.zeros((SUBLANES - nb, S5_COL_BLOCK), F32)
            h_re, h_im = jnp.concatenate([h_re, zero], axis=0), jnp.concatenate([h_im, zero], axis=0)
        carry = (h_re, h_im)
        if per_block_scratch:
            for i in range(seg_len):
                carry = tile_step(i, carry)
        else:
            carry = lax.fori_loop(0, seg_len, tile_step, carry, unroll=min(seg_len, 4))
        h_re, h_im = carry
        if seg > 1:
            c_re, c_im = first_rows(h_re), first_rows(h_im)
            for i in range(seg_len):
                rows_i = slice(i * SUBLANES, (i + 1) * SUBLANES)
                p_re, p_im = pwre_scr[cb, rows_i, :], pwim_scr[cb, rows_i, :]
                f_re = sre[rows_i, :] + (p_re * c_re - p_im * c_im)
                f_im = sim[rows_i, :] + (p_re * c_im + p_im * c_re)
                sre[rows_i, :] = f_re
                sim[rows_i, :] = f_im
            h_re, h_im = f_re[SUBLANES - nb:], f_im[SUBLANES - nb:]
        hre_out[:, st] = h_re
        him_out[:, st] = h_im

        y = (jnp.dot(sre[...].astype(BF16), cre_ref[cb], preferred_element_type=F32)
             - jnp.dot(sim[...].astype(BF16), cim_ref[cb], preferred_element_type=F32))
        zf = jax.nn.gelu(y + d_ref[:, ch] * u_blk)
        for j in range(tiles_per_block):
            ys[j] = zf[:, j * lane:(j + 1) * lane]
        for b in range(nb):
            for s in range(seg):
                for j in range(tiles_per_block):
                    c0 = cb * S5_CH_BLOCK + j * lane
                    z_ref[b, s * seg_len:(s + 1) * seg_len, c0:c0 + lane] = \
                        ys[j, pl.ds(s * nb + b, seg_len, stride=SUBLANES), :].astype(z_ref.dtype)


def _s5_call(x, shift, scale, g_norm, bb_re, bb_im, a_re, a_im, c_re, c_im, d_skip,
             h0_re, h0_im, steps_per_chunk, z_dtype, per_block_scratch):
    nb, seq, _ = x.shape
    assert SUBLANES % nb == 0 and seq % steps_per_chunk == 0
    tr = steps_per_chunk * nb
    slabs = S5_N_BLOCKS if per_block_scratch else 1
    seg = SUBLANES // nb
    pw_rows = tr if seg > 1 else SUBLANES
    row_chunk = min(steps_per_chunk // (SUBLANES // nb), NORM_ROWS)
    const2 = lambda t: (0, 0)
    const3 = lambda t: (0, 0, 0)
    state_spec = pl.BlockSpec((nb, S5_NSTATE), const2)
    io_spec = pl.BlockSpec((nb, steps_per_chunk, D_MODEL), lambda t: (0, t, 0))
    return pl.pallas_call(
        functools.partial(_s5_kernel, row_chunk=row_chunk),
        grid=(seq // steps_per_chunk,),
        in_specs=[
            io_spec,
            pl.BlockSpec((nb, 1, D_MODEL), const3),
            pl.BlockSpec((nb, 1, D_MODEL), const3),
            pl.BlockSpec((1, D_MODEL), const2),
            pl.BlockSpec((S5_N_BLOCKS, S5_CH_BLOCK, S5_COL_BLOCK), const3, pipeline_mode=pl.Buffered(1)),
            pl.BlockSpec((S5_N_BLOCKS, S5_CH_BLOCK, S5_COL_BLOCK), const3, pipeline_mode=pl.Buffered(1)),
            pl.BlockSpec((S5_N_BLOCKS, 1, S5_COL_BLOCK), const3),
            pl.BlockSpec((S5_N_BLOCKS, 1, S5_COL_BLOCK), const3),
            pl.BlockSpec((S5_N_BLOCKS, S5_COL_BLOCK, S5_CH_BLOCK), const3, pipeline_mode=pl.Buffered(1)),
            pl.BlockSpec((S5_N_BLOCKS, S5_COL_BLOCK, S5_CH_BLOCK), const3, pipeline_mode=pl.Buffered(1)),
            pl.BlockSpec((1, D_MODEL), const2),
            state_spec, state_spec,
        ],
        out_specs=(io_spec, state_spec, state_spec),
        out_shape=(jax.ShapeDtypeStruct((nb, seq, D_MODEL), z_dtype),
                   jax.ShapeDtypeStruct((nb, S5_NSTATE), F32),
                   jax.ShapeDtypeStruct((nb, S5_NSTATE), F32)),
        scratch_shapes=[pltpu.VMEM((D_MODEL // HEAD_DIM, tr, HEAD_DIM), F32),
                        pltpu.VMEM((slabs, tr, S5_COL_BLOCK), F32),
                        pltpu.VMEM((slabs, tr, S5_COL_BLOCK), F32),
                        pltpu.VMEM((slabs, S5_CH_BLOCK // HEAD_DIM, tr, HEAD_DIM), F32),
                        pltpu.VMEM((S5_N_BLOCKS, pw_rows, S5_COL_BLOCK), F32),
                        pltpu.VMEM((S5_N_BLOCKS, pw_rows, S5_COL_BLOCK), F32)],
        compiler_params=_params("arbitrary"),
        name="s5_mixer",
    )(x, shift, scale, g_norm, bb_re, bb_im, a_re, a_im, c_re, c_im, d_skip, h0_re, h0_im)


def _softmax_pv(s2, v_bf):
    m = jnp.max(s2, axis=-1, keepdims=True)
    p = jnp.exp2(s2 - m)
    l = jnp.sum(p, axis=-1, keepdims=True)
    o = jnp.dot(p.astype(BF16), v_bf, preferred_element_type=F32) / l
    return o, jnp.broadcast_to(m + jnp.log2(l), o.shape)


def _qk(q_bf, k_bf, scale=ATTN_SCALE):
    return lax.dot_general(q_bf, k_bf, (((1,), (1,)), ((), ())), preferred_element_type=F32) * scale


def _attn_prompt_kernel(q0_ref, q1_ref, q2_ref, k0_ref, v0_ref, k1_ref, v1_ref, k2_ref, v2_ref, o_ref,
                        qf_scr, o1_scr, l1_scr, o2_scr, l2_scr, band_scr, first_scr):
    seq = q0_ref.shape[0]
    qb = Q_BLOCK
    row = lax.broadcasted_iota(jnp.int32, (REP * qb, 2 * qb), 0) & (qb - 1)
    col = lax.broadcasted_iota(jnp.int32, (REP * qb, 2 * qb), 1)
    band_scr[...] = jnp.where((col >= row) & (col <= row + qb), 0.0, -jnp.inf)
    row1 = lax.broadcasted_iota(jnp.int32, (REP * qb, qb), 0) & (qb - 1)
    col1 = lax.broadcasted_iota(jnp.int32, (REP * qb, qb), 1)
    first_scr[...] = jnp.where(col1 <= row1, 0.0, -jnp.inf)

    def rows_of(start, d):
        return pl.ds(start, qb) if d == 1 else pl.ds(start, qb, stride=d)

    def load_q(src_ref, start, d):
        if d == 1:
            parts = [src_ref[pl.ds(start, qb), e * HEAD_DIM:(e + 1) * HEAD_DIM] for e in range(REP)]
        else:
            parts = [src_ref[e, rows_of(start, d), :] for e in range(REP)]
        return jnp.concatenate(parts, axis=0).astype(BF16)

    def unit(src_q, k_ref, v_ref, start, d, first):
        q4 = load_q(src_q, start, d)
        k_cur = k_ref[rows_of(start, d), :]
        v_cur = v_ref[rows_of(start, d), :]
        if first:
            s = _qk(q4, k_cur.astype(BF16), ATTN_SCALE_LOG2) + first_scr[...]
            return _softmax_pv(s, v_cur.astype(BF16))
        prev = start - qb * d
        k_band = jnp.concatenate([k_ref[rows_of(prev, d), :], k_cur], axis=0)
        v_band = jnp.concatenate([v_ref[rows_of(prev, d), :], v_cur], axis=0)
        s = _qk(q4, k_band.astype(BF16), ATTN_SCALE_LOG2) + band_scr[...]
        return _softmax_pv(s, v_band.astype(BF16))

    def stage_group(q_ref, k_ref, v_ref, d, o_scr, l_scr):
        n_blocks = seq // (qb * d)

        def copy(c, carry):
            r0 = pl.multiple_of(c * STAGE_ROWS, STAGE_ROWS)
            for e in range(REP):
                qf_scr[e, pl.ds(r0, STAGE_ROWS), :] = \
                    q_ref[pl.ds(r0, STAGE_ROWS), e * HEAD_DIM:(e + 1) * HEAD_DIM].astype(F32)
            return carry

        lax.fori_loop(0, seq // STAGE_ROWS, copy, 0)

        def store(start, o, lse):
            for e in range(REP):
                sl = slice(e * qb, (e + 1) * qb)
                o_scr[e, rows_of(start, d), :] = o[sl]
                l_scr[e, rows_of(start, d), :] = lse[sl]

        def per_class(r, carry):
            store(r, *unit(qf_scr, k_ref, v_ref, r, d, True))

            def per_block(jb, c2):
                start = jb * (qb * d) + r
                store(start, *unit(qf_scr, k_ref, v_ref, start, d, False))
                return c2

            if n_blocks > 1:
                lax.fori_loop(1, n_blocks, per_block, 0, unroll=True)
            return carry

        lax.fori_loop(0, d, per_class, 0, unroll=4 if n_blocks == 1 else 2)

    stage_group(q2_ref, k2_ref, v2_ref, DILATED_PATTERNS[2][1], o2_scr, l2_scr)
    stage_group(q1_ref, k1_ref, v1_ref, DILATED_PATTERNS[1][1], o1_scr, l1_scr)

    def merge(start, o0, lse0):
        for e in range(REP):
            sl = slice(e * qb, (e + 1) * qb)
            cols = slice(e * HEAD_DIM, (e + 1) * HEAD_DIM)
            la = lse0[sl]
            lb = l1_scr[e, pl.ds(start, qb), :]
            lc = l2_scr[e, pl.ds(start, qb), :]
            mx = jnp.maximum(jnp.maximum(la, lb), lc)
            wa = jnp.exp2(la - mx)
            wb = jnp.exp2(lb - mx)
            wc = jnp.exp2(lc - mx)
            inv = 1.0 / (wa + wb + wc)
            acc = (wa * o0[sl] + wb * o1_scr[e, pl.ds(start, qb), :] + wc * o2_scr[e, pl.ds(start, qb), :]) * inv
            o_ref[pl.ds(start, qb), cols] = acc.astype(o_ref.dtype)

    merge(0, *unit(q0_ref, k0_ref, v0_ref, 0, 1, True))

    def per_block0(jb, carry):
        start = pl.multiple_of(jb * qb, qb)
        merge(start, *unit(q0_ref, k0_ref, v0_ref, start, 1, False))
        return carry

    n_blocks0 = seq // qb
    lax.fori_loop(1, n_blocks0, per_block0, 0, unroll=5 if (n_blocks0 - 1) % 5 == 0 else 1)


def _attn_prompt_call(q, kv):
    b, seq, _ = q.shape
    qw = REP * HEAD_DIM
    q_specs = [pl.BlockSpec((None, seq, qw), functools.partial(lambda bi, h, g: (bi, 0, g * KV_HEADS + h), g=g))
               for g in range(N_DIL)]
    kv_specs = []
    for g in range(N_DIL):
        kv_specs.append(pl.BlockSpec((None, None, seq, HEAD_DIM),
                                     functools.partial(lambda bi, h, g: (g, bi, 0, h), g=g)))
        kv_specs.append(pl.BlockSpec((None, None, seq, HEAD_DIM),
                                     functools.partial(lambda bi, h, g: (g, bi, 0, KV_HEADS + h), g=g)))
    kv_args = [kv] * (2 * N_DIL)
    return pl.pallas_call(
        _attn_prompt_kernel,
        grid=(b, KV_HEADS),
        in_specs=q_specs + kv_specs,
        out_specs=pl.BlockSpec((None, seq, qw), lambda bi, h: (bi, 0, h)),
        out_shape=jax.ShapeDtypeStruct((b, seq, HEADS * HEAD_DIM), BF16),
        scratch_shapes=[pltpu.VMEM((REP, seq, HEAD_DIM), F32),
                        pltpu.VMEM((REP, seq, HEAD_DIM), F32), pltpu.VMEM((REP, seq, HEAD_DIM), F32),
                        pltpu.VMEM((REP, seq, HEAD_DIM), F32), pltpu.VMEM((REP, seq, HEAD_DIM), F32),
                        pltpu.VMEM((REP * Q_BLOCK, 2 * Q_BLOCK), F32), pltpu.VMEM((REP * Q_BLOCK, Q_BLOCK), F32)],
        compiler_params=_params("parallel", "arbitrary"),
        name="dilated_attention_prompt",
    )(q, q, q, *kv_args)


def _attn_sample_kernel(q_ref, kvn_ref, c0_ref, c1_ref, c2_ref, o_ref, *, n_tok):
    rows = REP * n_tok
    cache_refs = (c0_ref, c1_ref, c2_ref)
    kv_rows = 2 * KV_HEADS
    for h in range(KV_HEADS):
        outs, lses = [], []
        for g, (_, d) in enumerate(DILATED_PATTERNS):
            q = q_ref[g * KV_HEADS + h]
            lw = cache_refs[g].shape[0] // kv_rows
            k_c = cache_refs[g][pl.ds(h, lw, stride=kv_rows), :].astype(BF16)
            v_c = cache_refs[g][pl.ds(KV_HEADS + h, lw, stride=kv_rows), :].astype(BF16)
            k_n = kvn_ref[g, pl.ds(h, n_tok, stride=kv_rows), :].astype(BF16)
            v_n = kvn_ref[g, pl.ds(KV_HEADS + h, n_tok, stride=kv_rows), :].astype(BF16)
            tq_c = lax.broadcasted_iota(jnp.int32, (rows, lw), 0) & (n_tok - 1)
            idx_c = lax.broadcasted_iota(jnp.int32, (rows, lw), 1)
            ok_c = (idx_c >= tq_c) & (((idx_c - tq_c) & (d - 1)) == 0)
            tq_n = lax.broadcasted_iota(jnp.int32, (rows, n_tok), 0) & (n_tok - 1)
            idx_n = lax.broadcasted_iota(jnp.int32, (rows, n_tok), 1)
            ok_n = (idx_n <= tq_n) & (((tq_n - idx_n) & (d - 1)) == 0)
            s_c = jnp.where(ok_c, _qk(q, k_c), -jnp.inf)
            s_n = jnp.where(ok_n, _qk(q, k_n), -jnp.inf)
            m = jnp.maximum(jnp.max(s_c, axis=-1, keepdims=True), jnp.max(s_n, axis=-1, keepdims=True))
            p_c = jnp.exp(s_c - m)
            p_n = jnp.exp(s_n - m)
            l = jnp.sum(p_c, axis=-1, keepdims=True) + jnp.sum(p_n, axis=-1, keepdims=True)
            o = (jnp.dot(p_c.astype(BF16), v_c, preferred_element_type=F32)
                 + jnp.dot(p_n.astype(BF16), v_n, preferred_element_type=F32)) / l
            outs.append(o)
            lses.append(m + jnp.log(l))
        mx = jnp.maximum(jnp.maximum(lses[0], lses[1]), lses[2])
        w = [jnp.exp(x - mx) for x in lses]
        inv = 1.0 / (w[0] + w[1] + w[2])
        acc = (w[0] * inv) * outs[0] + (w[1] * inv) * outs[1] + (w[2] * inv) * outs[2]
        o_ref[h] = acc.astype(o_ref.dtype)


def _attn_sample_call(q_heads, kv_new, caches, n_tok):
    b = q_heads.shape[0]
    rows = REP * n_tok
    assert n_tok & (n_tok - 1) == 0
    in_specs = [pl.BlockSpec((None, N_DIL * KV_HEADS, rows, HEAD_DIM), lambda bi: (bi, 0, 0, 0)),
                pl.BlockSpec((N_DIL, None, kv_new.shape[2], HEAD_DIM), lambda bi: (0, bi, 0, 0))]
    in_specs += [pl.BlockSpec((None, c.shape[1], HEAD_DIM), lambda bi: (bi, 0, 0)) for c in caches]
    return pl.pallas_call(
        functools.partial(_attn_sample_kernel, n_tok=n_tok),
        grid=(b,),
        in_specs=in_specs,
        out_specs=pl.BlockSpec((None, KV_HEADS, rows, HEAD_DIM), lambda bi: (bi, 0, 0, 0)),
        out_shape=jax.ShapeDtypeStruct((b, KV_HEADS, rows, HEAD_DIM), BF16),
        compiler_params=_params("parallel"),
        name="dilated_attention_sample",
    )(q_heads, kv_new, *caches)


def _s5_weights(lam_re, lam_im, log_dt, b_re, b_im, c_re, c_im):
    g, p, c = S5_GROUPS, S5_STATE, S5_GROUP
    rep = lambda a: jnp.repeat(a, c, axis=1)
    a_re, a_im, bb_re, bb_im = _s5_disc_call(rep(lam_re), rep(lam_im), log_dt.reshape(g, 1),
                                             b_re.reshape(g, p * c), b_im.reshape(g, p * c))
    gl = S5_CH_BLOCK // c

    def block_diag(t, rows_per_group, cols_per_group):
        tiled = jnp.tile(t, (1, 1, gl))
        row_g = lax.broadcasted_iota(jnp.int32, tiled.shape, 1) // rows_per_group
        col_g = lax.broadcasted_iota(jnp.int32, tiled.shape, 2) // cols_per_group
        return jnp.where(row_g == col_g, tiled, 0.0).astype(BF16)

    def in_blocks(bb):
        t = bb.reshape(S5_N_BLOCKS, gl, p, c).transpose(0, 1, 3, 2)
        return block_diag(t.reshape(S5_N_BLOCKS, gl * c, p), c, p)

    def out_blocks(cm):
        t = cm.reshape(S5_N_BLOCKS, gl, c, p).transpose(0, 1, 3, 2)
        return block_diag(t.reshape(S5_N_BLOCKS, gl * p, c), p, c)

    lam_bar = lambda a: a.reshape(g, p, c)[:, :, 0].reshape(S5_N_BLOCKS, 1, S5_COL_BLOCK)
    return (in_blocks(bb_re), in_blocks(bb_im), lam_bar(a_re), lam_bar(a_im),
            out_blocks(c_re), out_blocks(c_im))


def _trunks(x_p, x_s, mod_p, mod_s, s5_state, kv_caches, wts, tiles):
    (norm_g, ffn_w_in, ffn_w_out, s5_mats, s5_d, s5_w_glu, kv_norm_g, w_kv,
     attn_w_q, attn_w_o, final_norm_g) = wts
    bp, lp, _ = x_p.shape
    bs, ls, _ = x_s.shape
    mp, ms = bp * lp, bs * ls
    tm = tiles["tm"]
    kv_rows = 2 * KV_HEADS

    def mods_p(*ks):
        return tuple(mod_p[:, k].reshape(bp, 1, D_MODEL) for k in ks)

    def mods_s(*ks):
        return tuple(jnp.repeat(mod_s[:, k], ls, axis=0) for k in ks)

    def as_block(a):
        return a.reshape(1, *a.shape)

    def norm_g_row(a):
        return a.reshape(1, D_MODEL)

    xp = x_p.reshape(mp, D_MODEL)
    xs = x_s.reshape(ms, D_MODEL)
    re_p, im_p, re_s, im_s = [], [], [], []
    for layer in range(DEPTH):
        base = 9 * layer
        if layer == N_A_LAYERS:
            k0 = 9 * DEPTH
            kvf_p, kv_p, _, kv_s = _norm_matmul_call(xp, *mods_p(k0, k0 + 1), xs, *mods_s(k0, k0 + 1),
                                                     norm_g_row(kv_norm_g), w_kv_bf, (), F32, tm,
                                                     KV_HEADS * HEAD_DIM, head_rows=kv_rows)
        later = () if layer < N_A_LAYERS else ((attn_w_q, (layer - N_A_LAYERS,)), (attn_w_o, (layer - N_A_LAYERS,)))
        xp, xs, rounded = _ffn_call(xp, mods_p(base, base + 1, base + 2), xs, mods_s(base, base + 1, base + 2),
                                    norm_g_row(norm_g[layer, 0]), ffn_w_in, ffn_w_out, layer, 0, tm, tiles["tf"],
                                    tiles["tn_down"], round_also=later)
        g_mix = norm_g_row(norm_g[layer, 1])
        if layer < N_A_LAYERS:
            d_skip = s5_d[layer].reshape(1, D_MODEL)
            zero_state = jnp.zeros((bp, S5_NSTATE), F32)
            zp, h_re, h_im = _s5_call(xp.reshape(bp, lp, D_MODEL), *mods_p(base + 3, base + 4), g_mix,
                                      *s5_mats[layer], d_skip, zero_state, zero_state,
                                      min(lp, tiles["s5_steps"]), BF16, True)
            re_p.append(h_re.reshape(bp, S5_GROUPS, S5_STATE))
            im_p.append(h_im.reshape(bp, S5_GROUPS, S5_STATE))
            zs, h_re, h_im = _s5_call(xs.reshape(bs, ls, D_MODEL), mod_s[:, base + 3].reshape(bs, 1, D_MODEL),
                                      mod_s[:, base + 4].reshape(bs, 1, D_MODEL), g_mix, *s5_mats[layer], d_skip,
                                      s5_state[0][layer].reshape(bs, S5_NSTATE),
                                      s5_state[1][layer].reshape(bs, S5_NSTATE), ls, F32, False)
            re_s.append(h_re.reshape(bs, S5_GROUPS, S5_STATE))
            im_s.append(h_im.reshape(bs, S5_GROUPS, S5_STATE))
            xp = _proj_residual_call(zp.reshape(mp, D_MODEL), xp, *mods_p(base + 5), s5_w_glu, (layer,),
                                     True, tm, tiles["tn_mix"], name="s5_glu")
            xs = _proj_residual_call(zs.reshape(ms, D_MODEL).astype(BF16), xs, as_block(*mods_s(base + 5)),
                                     s5_w_glu, (layer,), True, ms, tiles["tn_mix"], name="s5_glu")
        else:
            w_q_bf, w_o_bf = rounded
            q_p, q_s = _norm_matmul_call(xp, *mods_p(base + 3, base + 4), xs, *mods_s(base + 3, base + 4), g_mix,
                                         w_q_bf, (), BF16, tm, tiles["tn_q"])
            o_p = _attn_prompt_call(q_p.reshape(bp, lp, N_DIL * HEADS * HEAD_DIM),
                                    kvf_p.reshape(N_DIL, bp, lp, kv_rows * HEAD_DIM)).reshape(mp, D_MODEL)
            qh = q_s.reshape(bs, ls, N_DIL, KV_HEADS, REP, HEAD_DIM).transpose(0, 2, 3, 4, 1, 5)
            qh = qh.reshape(bs, N_DIL * KV_HEADS, REP * ls, HEAD_DIM)
            o_s = _attn_sample_call(qh, kv_s.reshape(N_DIL, bs, ls * kv_rows, HEAD_DIM), kv_caches, ls)
            o_s = o_s.reshape(bs, KV_HEADS, REP, ls, HEAD_DIM).transpose(0, 3, 1, 2, 4).reshape(ms, D_MODEL)
            xp = _proj_residual_call(o_p, xp, *mods_p(base + 5), w_o_bf, (), False, tiles["tm_out"],
                                     tiles["tn_mix"], name="attn_out")
            xs = _proj_residual_call(o_s, xs, as_block(*mods_s(base + 5)), w_o_bf, (), False, ms,
                                     tiles["tn_mix"], name="attn_out")
        later = ((w_kv, ()),) if layer == N_A_LAYERS - 1 else ()
        xp, xs, rounded = _ffn_call(xp, mods_p(base + 6, base + 7, base + 8), xs, mods_s(base + 6, base + 7, base + 8),
                                    norm_g_row(norm_g[layer, 2]), ffn_w_in, ffn_w_out, layer, 1, tm, tiles["tf"],
                                    tiles["tn_down"], round_also=later)
        if later:
            (w_kv_bf,) = rounded
    k1 = 9 * DEPTH + 2
    y_p = _final_norm_call(xp, *mods_p(k1, k1 + 1), norm_g_row(final_norm_g), tm)
    y_s = _final_norm_call(xs, *(as_block(a) for a in mods_s(k1, k1 + 1)), norm_g_row(final_norm_g), ms)
    kv_p = kv_p.reshape(N_DIL, bp, lp, 2, KV_HEADS, HEAD_DIM)
    kv_s = kv_s.reshape(N_DIL, bs, ls, 2, KV_HEADS, HEAD_DIM)
    return ((y_p.reshape(bp, lp, D_MODEL), jnp.stack(re_p, axis=0), jnp.stack(im_p, axis=0), kv_p),
            (y_s.reshape(bs, ls, D_MODEL), jnp.stack(re_s, axis=0), jnp.stack(im_s, axis=0), kv_s))


def kernel(x_prompt, x_sample, c_prompt, c_sample, state_s5_re, state_s5_im, cache_kv_g0, cache_kv_g1, cache_kv_g2, w_mod, b_mod, norm_g, ffn_w_in, ffn_w_out, s5_lambda_re, s5_lambda_im, s5_log_dt, s5_b_re, s5_b_im, s5_c_re, s5_c_im, s5_d, s5_w_glu, kv_norm_g, w_kv, attn_w_q, attn_w_o, final_norm_g):
    bp, seq_p, _ = x_prompt.shape
    bs, seq_s, _ = x_sample.shape
    caches = (cache_kv_g0, cache_kv_g1, cache_kv_g2)
    for (w, d), c in zip(DILATED_PATTERNS, caches):
        assert c.shape[1] == w and w == d * Q_BLOCK and seq_p % (d * Q_BLOCK) == 0
    assert bp <= SUBLANES and bs <= SUBLANES

    n_c = bp + bs
    c_all = jnp.pad(jnp.concatenate([c_prompt, c_sample], axis=0), ((0, (-n_c) % SUBLANES), (0, 0)))
    mod_all = _mod_call(c_all, w_mod, b_mod)
    mod_p = mod_all[:bp].reshape(bp, N_MOD, D_MODEL)
    mod_s = mod_all[bp:n_c].reshape(bs, N_MOD, D_MODEL)

    s5_mats = [_s5_weights(s5_lambda_re[l], s5_lambda_im[l], s5_log_dt[l], s5_b_re[l], s5_b_im[l],
                           s5_c_re[l], s5_c_im[l]) for l in range(N_A_LAYERS)]
    wts = (norm_g, ffn_w_in, ffn_w_out, s5_mats, s5_d, s5_w_glu, kv_norm_g, w_kv, attn_w_q, attn_w_o, final_norm_g)

    caches_flat = tuple(c.reshape(bs, c.shape[1] * 2 * KV_HEADS, HEAD_DIM) for c in caches)
    (y_p, re_p, im_p, kv_p), (y_s, re_s, im_s, kv_s) = _trunks(
        x_prompt, x_sample, mod_p, mod_s, (state_s5_re, state_s5_im), caches_flat, wts,
        _TILES)

    kvp = [kv_p[g][:, seq_p - min(w, seq_p):] for g, (w, _) in enumerate(DILATED_PATTERNS)]
    return (y_p, y_s, re_p, im_p, kvp[0], kvp[1], kvp[2], re_s, im_s, kv_s[0], kv_s[1], kv_s[2])
```

```python
import functools

import jax
import jax.numpy as jnp
from jax import lax
from jax.experimental import pallas as pl
from jax.experimental.pallas import tpu as pltpu

F32 = jnp.float32
BF16 = jnp.bfloat16

D_MODEL = 2048
DEPTH = 4
N_A_LAYERS = DEPTH // 2
D_FF = 5632
S5_GROUP = 16
S5_GROUPS = D_MODEL // S5_GROUP
S5_STATE = 64
S5_NSTATE = S5_GROUPS * S5_STATE
HEAD_DIM = 128
HEADS = 16
KV_HEADS = 4
REP = HEADS // KV_HEADS
DILATED_PATTERNS = ((128, 1), (512, 4), (2048, 16))
N_DIL = len(DILATED_PATTERNS)
Q_BLOCK = 128
N_MOD = 9 * DEPTH + 4
EPS = 1e-6
ATTN_SCALE = HEAD_DIM ** -0.5
ATTN_SCALE_LOG2 = ATTN_SCALE * 1.4426950408889634

VMEM_LIMIT_BYTES = 56 * 1024 * 1024
SUBLANES = 8
BF16_SUBLANES = 16
NORM_ROWS = 64
STAGE_ROWS = 256
S5_COL_BLOCK = 1024
S5_CH_BLOCK = S5_COL_BLOCK // S5_STATE * S5_GROUP
S5_N_BLOCKS = S5_NSTATE // S5_COL_BLOCK


_TILES = {"tm": 1024, "tf": 512, "tn_down": 512, "tm_out": 2048, "tn_mix": 512, "tn_q": 1024, "tn_mod": 1024,
          "s5_steps": 32}


def _params(*sem):
    return pltpu.CompilerParams(dimension_semantics=sem, vmem_limit_bytes=VMEM_LIMIT_BYTES)


def _rms_mod(x, g, shift, scale):
    xn = x * lax.rsqrt(jnp.mean(x * x, axis=-1, keepdims=True) + EPS)
    return xn * g * (1.0 + scale) + shift


def _norm_rows(x_ref, g_ref, sh_ref, sc_ref, h_ref, row_chunk):
    rows = x_ref.shape[0]
    g = g_ref[...]
    mod_rows = sh_ref.shape[0]
    if mod_rows == 1:
        gs = g * (1.0 + sc_ref[...])
        sh = sh_ref[...]
        row_chunk = min(row_chunk, BF16_SUBLANES)

        def body(c, carry):
            r0 = pl.multiple_of(c * row_chunk, row_chunk)
            x = x_ref[pl.ds(r0, row_chunk), :]
            xn = x * lax.rsqrt(jnp.mean(x * x, axis=-1, keepdims=True) + EPS)
            h_ref[pl.ds(r0, row_chunk), :] = (xn * gs + sh).astype(h_ref.dtype)
            return carry

        n_chunks = rows // row_chunk
        lax.fori_loop(0, n_chunks, body, 0, unroll=4 if n_chunks % 4 == 0 else 1)
        return

    def body(c, carry):
        r0 = pl.multiple_of(c * row_chunk, row_chunk)
        x = x_ref[pl.ds(r0, row_chunk), :]
        if mod_rows == rows and rows != row_chunk:
            sh = sh_ref[pl.ds(r0, row_chunk), :]
            sc = sc_ref[pl.ds(r0, row_chunk), :]
        else:
            sh = sh_ref[...]
            sc = sc_ref[...]
        h_ref[pl.ds(r0, row_chunk), :] = _rms_mod(x, g, sh, sc).astype(h_ref.dtype)
        return carry

    lax.fori_loop(0, rows // row_chunk, body, 0)


def _mod_kernel(c_ref, w_ref, b_ref, o_ref):
    c = c_ref[...]
    a = (c * jax.nn.sigmoid(c)).astype(BF16)
    o_ref[...] = jnp.dot(a, w_ref[...].astype(BF16), preferred_element_type=F32) + b_ref[...]


def _mod_call(c_all, w_mod, b_mod):
    rows = c_all.shape[0]
    n = w_mod.shape[1]
    tn = _TILES["tn_mod"]
    return pl.pallas_call(
        _mod_kernel,
        grid=(n // tn,),
        in_specs=[
            pl.BlockSpec((rows, D_MODEL), lambda j: (0, 0)),
            pl.BlockSpec((D_MODEL, tn), lambda j: (0, j)),
            pl.BlockSpec((1, tn), lambda j: (0, j)),
        ],
        out_specs=pl.BlockSpec((rows, tn), lambda j: (0, j)),
        out_shape=jax.ShapeDtypeStruct((rows, n), F32),
        compiler_params=_params("arbitrary"),
        name="adaln_mod",
    )(c_all, w_mod, b_mod.reshape(1, n))


def _norm_two_row_sets(x_ref, sh_ref, sc_ref, x2_ref, sh2_ref, sc2_ref, g_ref, h_ref, row_chunk):
    tm, r2 = x_ref.shape[0], x2_ref.shape[0]

    @pl.when(pl.program_id(1) == 0)
    def _():
        _norm_rows(x_ref, g_ref, sh_ref, sc_ref, h_ref.at[pl.ds(0, tm)], row_chunk)

    @pl.when((pl.program_id(0) == 0) & (pl.program_id(1) == 0))
    def _():
        _norm_rows(x2_ref, g_ref, sh2_ref, sc2_ref, h_ref.at[pl.ds(tm, r2)], r2)


def _ffn_up_kernel(x_ref, sh_ref, sc_ref, x2_ref, sh2_ref, sc2_ref, g_ref, wg_ref, wu_ref, *rest, row_chunk):
    n_side = (len(rest) - 3) // 2
    side_in, (a_ref, a2_ref), side_out, h_ref = (rest[:n_side], rest[n_side:n_side + 2],
                                                  rest[n_side + 2:2 * n_side + 2], rest[-1])
    tm = x_ref.shape[0]
    _norm_two_row_sets(x_ref, sh_ref, sc_ref, x2_ref, sh2_ref, sc2_ref, g_ref, h_ref, row_chunk)

    def act(h):
        for src, dst in zip(side_in, side_out):
            dst[...] = src[...].astype(dst.dtype)
        g = jnp.dot(h, wg_ref[...].astype(BF16), preferred_element_type=F32)
        u = jnp.dot(h, wu_ref[...].astype(BF16), preferred_element_type=F32)
        return (g * jax.nn.sigmoid(g) * u).astype(a_ref.dtype)

    @pl.when(pl.program_id(0) == 0)
    def _():
        a = act(h_ref[...])
        a_ref[...] = a[:tm]
        a2_ref[...] = a[tm:]

    @pl.when(pl.program_id(0) != 0)
    def _():
        a_ref[...] = act(h_ref[pl.ds(0, tm), :])


def _ffn_call(x, mods, x2, mods2, g_norm, w_in, w_out, layer, which, tm, tf, tn, round_also=()):
    m, r2 = x.shape[0], x2.shape[0]
    nb = mods[0].shape[0]
    tiles_per_seq = (m // nb) // tm
    nk = D_FF // tf
    n_steps = (m // tm) * nk
    mod_spec = pl.BlockSpec((None, 1, D_MODEL), lambda i, k: (i // tiles_per_seq, 0, 0))
    full2 = pl.BlockSpec((r2, D_MODEL), lambda i, k: (0, 0))

    side_in_specs, side_out_specs, side_out_shapes, side_args = [], [], [], []
    for w_side, lead in (((w_out, (layer, which)),) + tuple(round_also)):
        rows, cols = w_side.shape[-2:]
        rps = next(r for r in range(BF16_SUBLANES, rows + 1, BF16_SUBLANES)
                   if rows % r == 0 and r * n_steps >= rows)
        n_blk = rows // rps

        def blk(i, k, n_blk=n_blk):
            return jnp.minimum(i * nk + k, n_blk - 1)

        side_in_specs.append(pl.BlockSpec((None,) * len(lead) + (rps, cols),
                                          functools.partial(lambda i, k, lead, blk: (*lead, blk(i, k), 0),
                                                            lead=lead, blk=blk)))
        side_out_specs.append(pl.BlockSpec((rps, cols), functools.partial(lambda i, k, blk: (blk(i, k), 0), blk=blk)))
        side_out_shapes.append(jax.ShapeDtypeStruct((rows, cols), BF16))
        side_args.append(w_side)

    a, a2, w_out_bf, *rounded = pl.pallas_call(
        functools.partial(_ffn_up_kernel, row_chunk=min(tm, NORM_ROWS)),
        grid=(m // tm, nk),
        in_specs=[
            pl.BlockSpec((tm, D_MODEL), lambda i, k: (i, 0)),
            mod_spec, mod_spec,
            full2, full2, full2,
            pl.BlockSpec((1, D_MODEL), lambda i, k: (0, 0)),
            pl.BlockSpec((None, None, D_MODEL, tf), lambda i, k: (layer, which, 0, k)),
            pl.BlockSpec((None, None, D_MODEL, tf), lambda i, k: (layer, which, 0, k + nk)),
        ] + side_in_specs,
        out_specs=[pl.BlockSpec((tm, tf), lambda i, k: (i, k)),
                   pl.BlockSpec((r2, tf), lambda i, k: (0, jnp.where(i == 0, k, nk - 1)))] + side_out_specs,
        out_shape=[jax.ShapeDtypeStruct((m, D_FF), BF16), jax.ShapeDtypeStruct((r2, D_FF), BF16)] + side_out_shapes,
        scratch_shapes=[pltpu.VMEM((tm + r2, D_MODEL), BF16)],
        compiler_params=_params("arbitrary", "arbitrary"),
        name="ffn_up",
    )(x, mods[0], mods[1], x2, mods2[0], mods2[1], g_norm, w_in, w_in, *side_args)
    out = _proj_residual_call(a, x, mods[2], w_out_bf, (), False, tm, tn, gate_scale=0.5, name="ffn_down")
    out2 = _proj_residual_call(a2, x2, mods2[2].reshape(1, r2, D_MODEL), w_out_bf, (), False, r2, tn,
                               gate_scale=0.5, name="ffn_down")
    return out, out2, rounded


def _norm_matmul_kernel(x_ref, sh_ref, sc_ref, x2_ref, sh2_ref, sc2_ref, g_ref, w_ref, *rest, row_chunk, head_rows):
    h_ref = rest[-1]
    outs = rest[:-1]
    per_set = len(outs) // 2
    tm = x_ref.shape[0]
    _norm_two_row_sets(x_ref, sh_ref, sc_ref, x2_ref, sh2_ref, sc2_ref, g_ref, h_ref, row_chunk)
    def emit(res, refs):
        refs[0][...] = res.astype(refs[0].dtype)
        if head_rows:
            rows, cols = res.shape
            heads_per_tile = cols // HEAD_DIM
            first = (pl.program_id(1) % (head_rows // heads_per_tile)) * heads_per_tile
            for c in range(heads_per_tile):
                refs[1][pl.ds(first + c, rows, stride=head_rows), :] = res[:, c * HEAD_DIM:(c + 1) * HEAD_DIM]

    @pl.when(pl.program_id(0) == 0)
    def _():
        res = jnp.dot(h_ref[...], w_ref[...].astype(BF16), preferred_element_type=F32)
        emit(res[:tm], outs[:per_set])
        emit(res[tm:], outs[per_set:])

    @pl.when(pl.program_id(0) != 0)
    def _():
        emit(jnp.dot(h_ref[pl.ds(0, tm), :], w_ref[...].astype(BF16), preferred_element_type=F32),
             outs[:per_set])


def _norm_matmul_call(x, shift, scale, x2, shift2, scale2, g_norm, w, w_lead, out_dtype, tm, tn, head_rows=0):
    m, r2 = x.shape[0], x2.shape[0]
    nb = shift.shape[0]
    n = w.shape[-1]
    nj = n // tn
    tiles_per_seq = (m // nb) // tm
    mod_spec = pl.BlockSpec((None, 1, D_MODEL), lambda i, j: (i // tiles_per_seq, 0, 0))
    full2 = pl.BlockSpec((r2, D_MODEL), lambda i, j: (0, 0))

    def col2(i, j):
        return jnp.where(i == 0, j, nj - 1)

    if head_rows:
        slab = head_rows * HEAD_DIM
        tps = slab // tn
        out_spec = (pl.BlockSpec((None, tm, tn), lambda i, j: (j // tps, i, j % tps)),
                    pl.BlockSpec((None, tm * head_rows, HEAD_DIM), lambda i, j: (j // tps, i, 0)),
                    pl.BlockSpec((None, r2, tn), lambda i, j: (col2(i, j) // tps, 0, col2(i, j) % tps)),
                    pl.BlockSpec((None, r2 * head_rows, HEAD_DIM), lambda i, j: (col2(i, j) // tps, 0, 0)))
        out_shape = (jax.ShapeDtypeStruct((n // slab, m, slab), out_dtype),
                     jax.ShapeDtypeStruct((n // slab, m * head_rows, HEAD_DIM), out_dtype),
                     jax.ShapeDtypeStruct((n // slab, r2, slab), out_dtype),
                     jax.ShapeDtypeStruct((n // slab, r2 * head_rows, HEAD_DIM), out_dtype))
    else:
        out_spec = (pl.BlockSpec((tm, tn), lambda i, j: (i, j)),
                    pl.BlockSpec((r2, tn), lambda i, j: (0, col2(i, j))))
        out_shape = (jax.ShapeDtypeStruct((m, n), out_dtype), jax.ShapeDtypeStruct((r2, n), out_dtype))
    return pl.pallas_call(
        functools.partial(_norm_matmul_kernel, row_chunk=min(tm, NORM_ROWS), head_rows=head_rows),
        grid=(m // tm, nj),
        in_specs=[
            pl.BlockSpec((tm, D_MODEL), lambda i, j: (i, 0)),
            mod_spec, mod_spec,
            full2, full2, full2,
            pl.BlockSpec((1, D_MODEL), lambda i, j: (0, 0)),
            pl.BlockSpec((None,) * len(w_lead) + (D_MODEL, tn), lambda i, j: (*w_lead, 0, j)),
        ],
        out_specs=out_spec,
        out_shape=out_shape,
        scratch_shapes=[pltpu.VMEM((tm + r2, D_MODEL), BF16)],
        compiler_params=_params("arbitrary", "arbitrary"),
        name="norm_matmul",
    )(x, shift, scale, x2, shift2, scale2, g_norm, w)


def _proj_residual_kernel(z_ref, x_ref, gt_ref, *refs, glu, gate_scale):
    z = z_ref[...]
    if glu:
        wa_ref, wg_ref, o_ref = refs
        a = jnp.dot(z, wa_ref[...].astype(BF16), preferred_element_type=F32)
        g = jnp.dot(z, wg_ref[...].astype(BF16), preferred_element_type=F32)
        y = a * jax.nn.sigmoid(g)
    else:
        w_ref, o_ref = refs
        y = jnp.dot(z, w_ref[...].astype(BF16), preferred_element_type=F32)
    gate = gt_ref[...] if gate_scale == 1.0 else gate_scale * gt_ref[...]
    o_ref[...] = x_ref[...] + gate * y


def _proj_residual_call(z, x, gate, w, w_lead, glu, tm, tn, gate_scale=1.0, name="proj_residual", z_buffers=2):
    m = x.shape[0]
    kdim = z.shape[1]
    nb, r, _ = gate.shape
    tiles_per_seq = (m // nb) // tm if r == 1 else 1
    n_blocks = D_MODEL // tn
    w_block = (None,) * len(w_lead) + (kdim, tn)
    w_specs = [pl.BlockSpec(w_block, lambda i, j: (*w_lead, 0, j))]
    w_args = [w]
    if glu:
        w_specs.append(pl.BlockSpec(w_block, lambda i, j: (*w_lead, 0, j + n_blocks)))
        w_args.append(w)
    return pl.pallas_call(
        functools.partial(_proj_residual_kernel, glu=glu, gate_scale=gate_scale),
        grid=(m // tm, n_blocks),
        in_specs=[
            (pl.BlockSpec((tm, kdim), lambda i, j: (i, 0)) if z_buffers == 2 else
             pl.BlockSpec((tm, kdim), lambda i, j: (i, 0), pipeline_mode=pl.Buffered(z_buffers))),
            pl.BlockSpec((tm, tn), lambda i, j: (i, j)),
            pl.BlockSpec((None, r, tn), lambda i, j: (i // tiles_per_seq, 0, j)),
        ] + w_specs,
        out_specs=pl.BlockSpec((tm, tn), lambda i, j: (i, j)),
        out_shape=jax.ShapeDtypeStruct((m, D_MODEL), F32),
        compiler_params=_params("parallel", "arbitrary"),
        name=name,
    )(z, x, gate, *w_args)


def _final_norm_kernel(x_ref, sh_ref, sc_ref, g_ref, o_ref, *, row_chunk):
    _norm_rows(x_ref, g_ref, sh_ref, sc_ref, o_ref, row_chunk)


def _final_norm_call(x, shift, scale, g_norm, tm):
    m = x.shape[0]
    nb, r, _ = shift.shape
    tiles_per_seq = (m // nb) // tm if r == 1 else 1
    mod_spec = pl.BlockSpec((None, r, D_MODEL), lambda i: (i // tiles_per_seq, 0, 0))
    return pl.pallas_call(
        functools.partial(_final_norm_kernel, row_chunk=min(tm, NORM_ROWS)),
        grid=(m // tm,),
        in_specs=[
            pl.BlockSpec((tm, D_MODEL), lambda i: (i, 0)),
            mod_spec, mod_spec,
            pl.BlockSpec((1, D_MODEL), lambda i: (0, 0)),
        ],
        out_specs=pl.BlockSpec((tm, D_MODEL), lambda i: (i, 0)),
        out_shape=jax.ShapeDtypeStruct((m, D_MODEL), F32),
        compiler_params=_params("parallel"),
        name="final_norm",
    )(x, shift, scale, g_norm)


def _s5_discretise(lam_re, lam_im, log_dt):
    dt = jnp.exp(log_dt)
    decay = jnp.exp(lam_re * dt)
    ab_re = decay * jnp.cos(lam_im * dt)
    ab_im = decay * jnp.sin(lam_im * dt)
    den = lam_re * lam_re + lam_im * lam_im
    f_re = ((ab_re - 1.0) * lam_re + ab_im * lam_im) / den
    f_im = (ab_im * lam_re - (ab_re - 1.0) * lam_im) / den
    return ab_re, ab_im, f_re, f_im


def _s5_prep_kernel(lre_ref, lim_ref, ldt_ref, bre_ref, bim_ref, ctre_ref, ctim_ref, lre_g, lim_g, ldt_g,
                    are_ref, aim_ref, inre_ref, inim_ref, outre_ref, outim_ref):
    c, p = S5_GROUP, S5_STATE
    ab_re, ab_im, _, _ = _s5_discretise(lre_g[...], lim_g[...], ldt_g[...])
    are_ref[...] = ab_re
    aim_ref[...] = ab_im
    _, _, f_re, f_im = _s5_discretise(lre_ref[...], lim_ref[...], ldt_ref[...])
    b_re, b_im = bre_ref[...], bim_ref[...]
    bb_re = f_re * b_re - f_im * b_im
    bb_im = f_re * b_im + f_im * b_re
    for ref in (inre_ref, inim_ref, outre_ref, outim_ref):
        ref[...] = jnp.zeros(ref.shape, ref.dtype)
    for gl in range(S5_CH_BLOCK // c):
        rows_in, cols_in = slice(gl * c, (gl + 1) * c), slice(gl * p, (gl + 1) * p)
        inre_ref[rows_in, cols_in] = bb_re[rows_in].astype(inre_ref.dtype)
        inim_ref[rows_in, cols_in] = bb_im[rows_in].astype(inim_ref.dtype)
        outre_ref[cols_in, rows_in] = ctre_ref[cols_in, :].astype(outre_ref.dtype)
        outim_ref[cols_in, rows_in] = ctim_ref[cols_in, :].astype(outim_ref.dtype)


def _s5_prep_call(lam_re, lam_im, log_dt, b_re, b_im, c_re, c_im):
    g, p, c = S5_GROUPS, S5_STATE, S5_GROUP
    rep = lambda a: jnp.repeat(a, c, axis=0)
    b_t = lambda b: b.transpose(0, 2, 1).reshape(g * c, p)
    c_t = lambda m: m.transpose(0, 2, 1).reshape(g * p, c)
    gpb = S5_CH_BLOCK // c
    row_blk = lambda rows, cols: pl.BlockSpec((rows, cols), lambda i: (i, 0))
    blk3 = lambda rows, cols: pl.BlockSpec((None, rows, cols), lambda i: (i, 0, 0))
    return pl.pallas_call(
        _s5_prep_kernel,
        grid=(S5_N_BLOCKS,),
        in_specs=[row_blk(S5_CH_BLOCK, p), row_blk(S5_CH_BLOCK, p), row_blk(S5_CH_BLOCK, 1),
                  row_blk(S5_CH_BLOCK, p), row_blk(S5_CH_BLOCK, p),
                  row_blk(S5_COL_BLOCK, c), row_blk(S5_COL_BLOCK, c),
                  row_blk(gpb, p), row_blk(gpb, p), row_blk(gpb, 1)],
        out_specs=(row_blk(gpb, p), row_blk(gpb, p),
                   blk3(S5_CH_BLOCK, S5_COL_BLOCK), blk3(S5_CH_BLOCK, S5_COL_BLOCK),
                   blk3(S5_COL_BLOCK, S5_CH_BLOCK), blk3(S5_COL_BLOCK, S5_CH_BLOCK)),
        out_shape=(jax.ShapeDtypeStruct((g, p), F32), jax.ShapeDtypeStruct((g, p), F32),
                   jax.ShapeDtypeStruct((S5_N_BLOCKS, S5_CH_BLOCK, S5_COL_BLOCK), BF16),
                   jax.ShapeDtypeStruct((S5_N_BLOCKS, S5_CH_BLOCK, S5_COL_BLOCK), BF16),
                   jax.ShapeDtypeStruct((S5_N_BLOCKS, S5_COL_BLOCK, S5_CH_BLOCK), BF16),
                   jax.ShapeDtypeStruct((S5_N_BLOCKS, S5_COL_BLOCK, S5_CH_BLOCK), BF16)),
        compiler_params=_params("parallel"),
        name="s5_prepare",
    )(rep(lam_re), rep(lam_im), rep(log_dt.reshape(g, 1)), b_t(b_re), b_t(b_im), c_t(c_re), c_t(c_im),
      lam_re, lam_im, log_dt.reshape(g, 1))


def _s5_kernel(x_ref, sh_ref, sc_ref, g_ref, bbre_ref, bbim_ref, are_ref, aim_ref, cre_ref, cim_ref,
               d_ref, h0re_ref, h0im_ref, z_ref, hre_out, him_out,
               u_scr, sre_scr, sim_scr, y_scr, pwre_scr, pwim_scr, *, row_chunk):
    t = pl.program_id(0)
    nb, steps, _ = x_ref.shape
    rows = steps * nb
    lane = HEAD_DIM
    seg = SUBLANES // nb
    seg_len = steps // seg
    assert seg in (1, 2)

    @pl.when(t == 0)
    def _():
        hre_out[...] = h0re_ref[...]
        him_out[...] = h0im_ref[...]
        if seg > 1:
            zero = jnp.zeros((nb, S5_COL_BLOCK), F32)
            for cb in range(S5_N_BLOCKS):
                a_re = jnp.broadcast_to(are_ref[cb], (nb, S5_COL_BLOCK))
                a_im = jnp.broadcast_to(aim_ref[cb], (nb, S5_COL_BLOCK))
                p_re, p_im = a_re, a_im
                for i in range(seg_len):
                    pwre_scr[cb, i * SUBLANES:(i + 1) * SUBLANES, :] = jnp.concatenate([zero, p_re], axis=0)
                    pwim_scr[cb, i * SUBLANES:(i + 1) * SUBLANES, :] = jnp.concatenate([zero, p_im], axis=0)
                    p_re, p_im = p_re * a_re - p_im * a_im, p_re * a_im + p_im * a_re

    g = g_ref[...]
    for b in range(nb):
        for s in range(seg):
            def norm_chunk(c, carry, b=b, s=s):
                r0 = pl.multiple_of(c * row_chunk, row_chunk)
                u = _rms_mod(x_ref[b, pl.ds(s * seg_len + r0, row_chunk), :], g, sh_ref[b], sc_ref[b])
                for j in range(D_MODEL // lane):
                    u_scr[j, pl.ds(r0 * SUBLANES + s * nb + b, row_chunk, stride=SUBLANES), :] = \
                        u[:, j * lane:(j + 1) * lane]
                return carry

            lax.fori_loop(0, seg_len // row_chunk, norm_chunk, 0)

    tiles_per_block = S5_CH_BLOCK // lane
    per_block_scratch = sre_scr.shape[0] == S5_N_BLOCKS
    for cb in range(S5_N_BLOCKS):
        ch = slice(cb * S5_CH_BLOCK, (cb + 1) * S5_CH_BLOCK)
        st = slice(cb * S5_COL_BLOCK, (cb + 1) * S5_COL_BLOCK)
        slab = cb if per_block_scratch else 0
        sre, sim, ys = sre_scr.at[slab], sim_scr.at[slab], y_scr.at[slab]
        u_blk = jnp.concatenate([u_scr[cb * tiles_per_block + j] for j in range(tiles_per_block)], axis=1)
        u_bf = u_blk.astype(BF16)
        sre[...] = jnp.dot(u_bf, bbre_ref[cb], preferred_element_type=F32)
        sim[...] = jnp.dot(u_bf, bbim_ref[cb], preferred_element_type=F32)

        a_re = jnp.broadcast_to(are_ref[cb], (SUBLANES, S5_COL_BLOCK))
        a_im = jnp.broadcast_to(aim_ref[cb], (SUBLANES, S5_COL_BLOCK))

        def tile_step(i, carry, a_re=a_re, a_im=a_im, sre=sre, sim=sim):
            h_re, h_im = carry
            r0 = i * SUBLANES if isinstance(i, int) else pl.multiple_of(i * SUBLANES, SUBLANES)
            n_re = a_re * h_re - a_im * h_im + sre[pl.ds(r0, SUBLANES), :]
            n_im = a_re * h_im + a_im * h_re + sim[pl.ds(r0, SUBLANES), :]
            sre[pl.ds(r0, SUBLANES), :] = n_re
            sim[pl.ds(r0, SUBLANES), :] = n_im
            return n_re, n_im

        def first_rows(v):
            return jnp.concatenate([v[:nb]] * seg, axis=0)

        h_re, h_im = hre_out[:, st], him_out[:, st]
        if seg > 1:
            zero = jnp.zeros((SUBLANES - nb, S5_COL_BLOCK), F32)
            h_re, h_im = jnp.concatenate([h_re, zero], axis=0), jnp.concatenate([h_im, zero], axis=0)
        carry = (h_re, h_im)
        if per_block_scratch:
            for i in range(seg_len):
                carry = tile_step(i, carry)
        else:
            carry = lax.fori_loop(0, seg_len, tile_step, carry, unroll=min(seg_len, 4))
        h_re, h_im = carry
        if seg > 1:
            c_re, c_im = first_rows(h_re), first_rows(h_im)
            for i in range(seg_len):
                rows_i = slice(i * SUBLANES, (i + 1) * SUBLANES)
                p_re, p_im = pwre_scr[cb, rows_i, :], pwim_scr[cb, rows_i, :]
                f_re = sre[rows_i, :] + (p_re * c_re - p_im * c_im)
                f_im = sim[rows_i, :] + (p_re * c_im + p_im * c_re)
                sre[rows_i, :] = f_re
                sim[rows_i, :] = f_im
            h_re, h_im = f_re[SUBLANES - nb:], f_im[SUBLANES - nb:]
        hre_out[:, st] = h_re
        him_out[:, st] = h_im

        y = (jnp.dot(sre[...].astype(BF16), cre_ref[cb], preferred_element_type=F32)
             - jnp.dot(sim[...].astype(BF16), cim_ref[cb], preferred_element_type=F32))
        zf = jax.nn.gelu(y + d_ref[:, ch] * u_blk)
        for j in range(tiles_per_block):
            ys[j] = zf[:, j * lane:(j + 1) * lane]
        for b in range(nb):
            for s in range(seg):
                for j in range(tiles_per_block):
                    c0 = cb * S5_CH_BLOCK + j * lane
                    z_ref[b, s * seg_len:(s + 1) * seg_len, c0:c0 + lane] = \
                        ys[j, pl.ds(s * nb + b, seg_len, stride=SUBLANES), :].astype(z_ref.dtype)


def _s5_call(x, shift, scale, g_norm, bb_re, bb_im, a_re, a_im, c_re, c_im, d_skip,
             h0_re, h0_im, steps_per_chunk, z_dtype, per_block_scratch):
    nb, seq, _ = x.shape
    assert SUBLANES % nb == 0 and seq % steps_per_chunk == 0
    tr = steps_per_chunk * nb
    slabs = S5_N_BLOCKS if per_block_scratch else 1
    seg = SUBLANES // nb
    pw_rows = tr if seg > 1 else SUBLANES
    row_chunk = min(steps_per_chunk // (SUBLANES // nb), NORM_ROWS)
    const2 = lambda t: (0, 0)
    const3 = lambda t: (0, 0, 0)
    state_spec = pl.BlockSpec((nb, S5_NSTATE), const2)
    io_spec = pl.BlockSpec((nb, steps_per_chunk, D_MODEL), lambda t: (0, t, 0))
    return pl.pallas_call(
        functools.partial(_s5_kernel, row_chunk=row_chunk),
        grid=(seq // steps_per_chunk,),
        in_specs=[
            io_spec,
            pl.BlockSpec((nb, 1, D_MODEL), const3),
            pl.BlockSpec((nb, 1, D_MODEL), const3),
            pl.BlockSpec((1, D_MODEL), const2),
            pl.BlockSpec((S5_N_BLOCKS, S5_CH_BLOCK, S5_COL_BLOCK), const3, pipeline_mode=pl.Buffered(1)),
            pl.BlockSpec((S5_N_BLOCKS, S5_CH_BLOCK, S5_COL_BLOCK), const3, pipeline_mode=pl.Buffered(1)),
            pl.BlockSpec((S5_N_BLOCKS, 1, S5_COL_BLOCK), const3),
            pl.BlockSpec((S5_N_BLOCKS, 1, S5_COL_BLOCK), const3),
            pl.BlockSpec((S5_N_BLOCKS, S5_COL_BLOCK, S5_CH_BLOCK), const3, pipeline_mode=pl.Buffered(1)),
            pl.BlockSpec((S5_N_BLOCKS, S5_COL_BLOCK, S5_CH_BLOCK), const3, pipeline_mode=pl.Buffered(1)),
            pl.BlockSpec((1, D_MODEL), const2),
            state_spec, state_spec,
        ],
        out_specs=(io_spec, state_spec, state_spec),
        out_shape=(jax.ShapeDtypeStruct((nb, seq, D_MODEL), z_dtype),
                   jax.ShapeDtypeStruct((nb, S5_NSTATE), F32),
                   jax.ShapeDtypeStruct((nb, S5_NSTATE), F32)),
        scratch_shapes=[pltpu.VMEM((D_MODEL // HEAD_DIM, tr, HEAD_DIM), F32),
                        pltpu.VMEM((slabs, tr, S5_COL_BLOCK), F32),
                        pltpu.VMEM((slabs, tr, S5_COL_BLOCK), F32),
                        pltpu.VMEM((slabs, S5_CH_BLOCK // HEAD_DIM, tr, HEAD_DIM), F32),
                        pltpu.VMEM((S5_N_BLOCKS, pw_rows, S5_COL_BLOCK), F32),
                        pltpu.VMEM((S5_N_BLOCKS, pw_rows, S5_COL_BLOCK), F32)],
        compiler_params=_params("arbitrary"),
        name="s5_mixer",
    )(x, shift, scale, g_norm, bb_re, bb_im, a_re, a_im, c_re, c_im, d_skip, h0_re, h0_im)


def _softmax_pv(s2, v_bf):
    m = jnp.max(s2, axis=-1, keepdims=True)
    p = jnp.exp2(s2 - m)
    l = jnp.sum(p, axis=-1, keepdims=True)
    o = jnp.dot(p.astype(BF16), v_bf, preferred_element_type=F32) / l
    return o, jnp.broadcast_to(m + jnp.log2(l), o.shape)


def _qk(q_bf, k_bf, scale=ATTN_SCALE):
    return lax.dot_general(q_bf, k_bf, (((1,), (1,)), ((), ())), preferred_element_type=F32) * scale


def _attn_prompt_kernel(q0_ref, q1_ref, q2_ref, k0_ref, v0_ref, k1_ref, v1_ref, k2_ref, v2_ref, o_ref,
                        qf_scr, o1_scr, l1_scr, o2_scr, l2_scr, band_scr, first_scr):
    seq = q0_ref.shape[0]
    qb = Q_BLOCK
    row = lax.broadcasted_iota(jnp.int32, (REP * qb, 2 * qb), 0) & (qb - 1)
    col = lax.broadcasted_iota(jnp.int32, (REP * qb, 2 * qb), 1)
    band_scr[...] = jnp.where((col >= row) & (col <= row + qb), 0.0, -jnp.inf)
    row1 = lax.broadcasted_iota(jnp.int32, (REP * qb, qb), 0) & (qb - 1)
    col1 = lax.broadcasted_iota(jnp.int32, (REP * qb, qb), 1)
    first_scr[...] = jnp.where(col1 <= row1, 0.0, -jnp.inf)

    def rows_of(start, d):
        return pl.ds(start, qb) if d == 1 else pl.ds(start, qb, stride=d)

    def load_q(src_ref, start, d):
        if d == 1:
            parts = [src_ref[pl.ds(start, qb), e * HEAD_DIM:(e + 1) * HEAD_DIM] for e in range(REP)]
        else:
            parts = [src_ref[e, rows_of(start, d), :] for e in range(REP)]
        return jnp.concatenate(parts, axis=0).astype(BF16)

    def unit(src_q, k_ref, v_ref, start, d, first):
        q4 = load_q(src_q, start, d)
        k_cur = k_ref[rows_of(start, d), :]
        v_cur = v_ref[rows_of(start, d), :]
        if first:
            s = _qk(q4, k_cur.astype(BF16), ATTN_SCALE_LOG2) + first_scr[...]
            return _softmax_pv(s, v_cur.astype(BF16))
        prev = start - qb * d
        k_band = jnp.concatenate([k_ref[rows_of(prev, d), :], k_cur], axis=0)
        v_band = jnp.concatenate([v_ref[rows_of(prev, d), :], v_cur], axis=0)
        s = _qk(q4, k_band.astype(BF16), ATTN_SCALE_LOG2) + band_scr[...]
        return _softmax_pv(s, v_band.astype(BF16))

    def stage_group(q_ref, k_ref, v_ref, d, o_scr, l_scr):
        n_blocks = seq // (qb * d)

        def copy(c, carry):
            r0 = pl.multiple_of(c * STAGE_ROWS, STAGE_ROWS)
            for e in range(REP):
                qf_scr[e, pl.ds(r0, STAGE_ROWS), :] = \
                    q_ref[pl.ds(r0, STAGE_ROWS), e * HEAD_DIM:(e + 1) * HEAD_DIM].astype(F32)
            return carry

        lax.fori_loop(0, seq // STAGE_ROWS, copy, 0)

        def store(start, o, lse):
            for e in range(REP):
                sl = slice(e * qb, (e + 1) * qb)
                o_scr[e, rows_of(start, d), :] = o[sl]
                l_scr[e, rows_of(start, d), :] = lse[sl]

        def per_class(r, carry):
            store(r, *unit(qf_scr, k_ref, v_ref, r, d, True))

            def per_block(jb, c2):
                start = jb * (qb * d) + r
                store(start, *unit(qf_scr, k_ref, v_ref, start, d, False))
                return c2

            if n_blocks > 1:
                lax.fori_loop(1, n_blocks, per_block, 0, unroll=True)
            return carry

        lax.fori_loop(0, d, per_class, 0, unroll=4 if n_blocks == 1 else 2)

    stage_group(q2_ref, k2_ref, v2_ref, DILATED_PATTERNS[2][1], o2_scr, l2_scr)
    stage_group(q1_ref, k1_ref, v1_ref, DILATED_PATTERNS[1][1], o1_scr, l1_scr)

    def merge(start, o0, lse0):
        for e in range(REP):
            sl = slice(e * qb, (e + 1) * qb)
            cols = slice(e * HEAD_DIM, (e + 1) * HEAD_DIM)
            la = lse0[sl]
            lb = l1_scr[e, pl.ds(start, qb), :]
            lc = l2_scr[e, pl.ds(start, qb), :]
            mx = jnp.maximum(jnp.maximum(la, lb), lc)
            wa = jnp.exp2(la - mx)
            wb = jnp.exp2(lb - mx)
            wc = jnp.exp2(lc - mx)
            inv = 1.0 / (wa + wb + wc)
            acc = (wa * o0[sl] + wb * o1_scr[e, pl.ds(start, qb), :] + wc * o2_scr[e, pl.ds(start, qb), :]) * inv
            o_ref[pl.ds(start, qb), cols] = acc.astype(o_ref.dtype)

    merge(0, *unit(q0_ref, k0_ref, v0_ref, 0, 1, True))

    def per_block0(jb, carry):
        start = pl.multiple_of(jb * qb, qb)
        merge(start, *unit(q0_ref, k0_ref, v0_ref, start, 1, False))
        return carry

    n_blocks0 = seq // qb
    lax.fori_loop(1, n_blocks0, per_block0, 0, unroll=5 if (n_blocks0 - 1) % 5 == 0 else 1)


def _attn_prompt_call(q, kv):
    b, seq, _ = q.shape
    qw = REP * HEAD_DIM
    q_specs = [pl.BlockSpec((None, seq, qw), functools.partial(lambda bi, h, g: (bi, 0, g * KV_HEADS + h), g=g))
               for g in range(N_DIL)]
    kv_specs = []
    for g in range(N_DIL):
        kv_specs.append(pl.BlockSpec((None, None, seq, HEAD_DIM),
                                     functools.partial(lambda bi, h, g: (g, bi, 0, h), g=g)))
        kv_specs.append(pl.BlockSpec((None, None, seq, HEAD_DIM),
                                     functools.partial(lambda bi, h, g: (g, bi, 0, KV_HEADS + h), g=g)))
    kv_args = [kv] * (2 * N_DIL)
    return pl.pallas_call(
        _attn_prompt_kernel,
        grid=(b, KV_HEADS),
        in_specs=q_specs + kv_specs,
        out_specs=pl.BlockSpec((None, seq, qw), lambda bi, h: (bi, 0, h)),
        out_shape=jax.ShapeDtypeStruct((b, seq, HEADS * HEAD_DIM), BF16),
        scratch_shapes=[pltpu.VMEM((REP, seq, HEAD_DIM), F32),
                        pltpu.VMEM((REP, seq, HEAD_DIM), F32), pltpu.VMEM((REP, seq, HEAD_DIM), F32),
                        pltpu.VMEM((REP, seq, HEAD_DIM), F32), pltpu.VMEM((REP, seq, HEAD_DIM), F32),
                        pltpu.VMEM((REP * Q_BLOCK, 2 * Q_BLOCK), F32), pltpu.VMEM((REP * Q_BLOCK, Q_BLOCK), F32)],
        compiler_params=_params("parallel", "arbitrary"),
        name="dilated_attention_prompt",
    )(q, q, q, *kv_args)


def _attn_sample_kernel(q_ref, kvn_ref, c0_ref, c1_ref, c2_ref, o_ref, *, n_tok):
    rows = REP * n_tok
    cache_refs = (c0_ref, c1_ref, c2_ref)
    kv_rows = 2 * KV_HEADS
    for h in range(KV_HEADS):
        outs, lses = [], []
        for g, (_, d) in enumerate(DILATED_PATTERNS):
            q = q_ref[g * KV_HEADS + h]
            lw = cache_refs[g].shape[0] // kv_rows
            k_c = cache_refs[g][pl.ds(h, lw, stride=kv_rows), :].astype(BF16)
            v_c = cache_refs[g][pl.ds(KV_HEADS + h, lw, stride=kv_rows), :].astype(BF16)
            k_n = kvn_ref[g, pl.ds(h, n_tok, stride=kv_rows), :].astype(BF16)
            v_n = kvn_ref[g, pl.ds(KV_HEADS + h, n_tok, stride=kv_rows), :].astype(BF16)
            tq_c = lax.broadcasted_iota(jnp.int32, (rows, lw), 0) & (n_tok - 1)
            idx_c = lax.broadcasted_iota(jnp.int32, (rows, lw), 1)
            ok_c = (idx_c >= tq_c) & (((idx_c - tq_c) & (d - 1)) == 0)
            tq_n = lax.broadcasted_iota(jnp.int32, (rows, n_tok), 0) & (n_tok - 1)
            idx_n = lax.broadcasted_iota(jnp.int32, (rows, n_tok), 1)
            ok_n = (idx_n <= tq_n) & (((tq_n - idx_n) & (d - 1)) == 0)
            s_c = jnp.where(ok_c, _qk(q, k_c), -jnp.inf)
            s_n = jnp.where(ok_n, _qk(q, k_n), -jnp.inf)
            m = jnp.maximum(jnp.max(s_c, axis=-1, keepdims=True), jnp.max(s_n, axis=-1, keepdims=True))
            p_c = jnp.exp(s_c - m)
            p_n = jnp.exp(s_n - m)
            l = jnp.sum(p_c, axis=-1, keepdims=True) + jnp.sum(p_n, axis=-1, keepdims=True)
            o = (jnp.dot(p_c.astype(BF16), v_c, preferred_element_type=F32)
                 + jnp.dot(p_n.astype(BF16), v_n, preferred_element_type=F32)) / l
            outs.append(o)
            lses.append(m + jnp.log(l))
        mx = jnp.maximum(jnp.maximum(lses[0], lses[1]), lses[2])
        w = [jnp.exp(x - mx) for x in lses]
        inv = 1.0 / (w[0] + w[1] + w[2])
        acc = (w[0] * inv) * outs[0] + (w[1] * inv) * outs[1] + (w[2] * inv) * outs[2]
        o_ref[h] = acc.astype(o_ref.dtype)


def _attn_sample_call(q_heads, kv_new, caches, n_tok):
    b = q_heads.shape[0]
    rows = REP * n_tok
    assert n_tok & (n_tok - 1) == 0
    in_specs = [pl.BlockSpec((None, N_DIL * KV_HEADS, rows, HEAD_DIM), lambda bi: (bi, 0, 0, 0)),
                pl.BlockSpec((N_DIL, None, kv_new.shape[2], HEAD_DIM), lambda bi: (0, bi, 0, 0))]
    in_specs += [pl.BlockSpec((None, c.shape[1], HEAD_DIM), lambda bi: (bi, 0, 0)) for c in caches]
    return pl.pallas_call(
        functools.partial(_attn_sample_kernel, n_tok=n_tok),
        grid=(b,),
        in_specs=in_specs,
        out_specs=pl.BlockSpec((None, KV_HEADS, rows, HEAD_DIM), lambda bi: (bi, 0, 0, 0)),
        out_shape=jax.ShapeDtypeStruct((b, KV_HEADS, rows, HEAD_DIM), BF16),
        compiler_params=_params("parallel"),
        name="dilated_attention_sample",
    )(q_heads, kv_new, *caches)


def _s5_weights(lam_re, lam_im, log_dt, b_re, b_im, c_re, c_im):
    a_re, a_im, in_re, in_im, out_re, out_im = _s5_prep_call(lam_re, lam_im, log_dt, b_re, b_im, c_re, c_im)
    lam_bar = lambda a: a.reshape(S5_N_BLOCKS, 1, S5_COL_BLOCK)
    return in_re, in_im, lam_bar(a_re), lam_bar(a_im), out_re, out_im


def _trunks(x_p, x_s, mod_p, mod_s, s5_state, kv_caches, wts, tiles):
    (norm_g, ffn_w_in, ffn_w_out, s5_mats, s5_d, s5_w_glu, kv_norm_g, w_kv,
     attn_w_q, attn_w_o, final_norm_g) = wts
    bp, lp, _ = x_p.shape
    bs, ls, _ = x_s.shape
    mp, ms = bp * lp, bs * ls
    tm = tiles["tm"]
    kv_rows = 2 * KV_HEADS

    def mods_p(*ks):
        return tuple(mod_p[:, k].reshape(bp, 1, D_MODEL) for k in ks)

    def mods_s(*ks):
        return tuple(jnp.repeat(mod_s[:, k], ls, axis=0) for k in ks)

    def as_block(a):
        return a.reshape(1, *a.shape)

    def norm_g_row(a):
        return a.reshape(1, D_MODEL)

    xp = x_p.reshape(mp, D_MODEL)
    xs = x_s.reshape(ms, D_MODEL)
    re_p, im_p, re_s, im_s = [], [], [], []
    for layer in range(DEPTH):
        base = 9 * layer
        if layer == N_A_LAYERS:
            k0 = 9 * DEPTH
            kvf_p, kv_p, _, kv_s = _norm_matmul_call(xp, *mods_p(k0, k0 + 1), xs, *mods_s(k0, k0 + 1),
                                                     norm_g_row(kv_norm_g), w_kv_bf, (), F32, tm,
                                                     KV_HEADS * HEAD_DIM, head_rows=kv_rows)
        later = () if layer < N_A_LAYERS else ((attn_w_q, (layer - N_A_LAYERS,)), (attn_w_o, (layer - N_A_LAYERS,)))
        xp, xs, rounded = _ffn_call(xp, mods_p(base, base + 1, base + 2), xs, mods_s(base, base + 1, base + 2),
                                    norm_g_row(norm_g[layer, 0]), ffn_w_in, ffn_w_out, layer, 0, tm, tiles["tf"],
                                    tiles["tn_down"], round_also=later)
        g_mix = norm_g_row(norm_g[layer, 1])
        if layer < N_A_LAYERS:
            d_skip = s5_d[layer].reshape(1, D_MODEL)
            zero_state = jnp.zeros((bp, S5_NSTATE), F32)
            zp, h_re, h_im = _s5_call(xp.reshape(bp, lp, D_MODEL), *mods_p(base + 3, base + 4), g_mix,
                                      *s5_mats[layer], d_skip, zero_state, zero_state,
                                      min(lp, tiles["s5_steps"]), BF16, True)
            re_p.append(h_re.reshape(bp, S5_GROUPS, S5_STATE))
            im_p.append(h_im.reshape(bp, S5_GROUPS, S5_STATE))
            zs, h_re, h_im = _s5_call(xs.reshape(bs, ls, D_MODEL), mod_s[:, base + 3].reshape(bs, 1, D_MODEL),
                                      mod_s[:, base + 4].reshape(bs, 1, D_MODEL), g_mix, *s5_mats[layer], d_skip,
                                      s5_state[0][layer].reshape(bs, S5_NSTATE),
                                      s5_state[1][layer].reshape(bs, S5_NSTATE), ls, F32, False)
            re_s.append(h_re.reshape(bs, S5_GROUPS, S5_STATE))
            im_s.append(h_im.reshape(bs, S5_GROUPS, S5_STATE))
            xp = _proj_residual_call(zp.reshape(mp, D_MODEL), xp, *mods_p(base + 5), s5_w_glu, (layer,),
                                     True, tm, tiles["tn_mix"], name="s5_glu")
            xs = _proj_residual_call(zs.reshape(ms, D_MODEL).astype(BF16), xs, as_block(*mods_s(base + 5)),
                                     s5_w_glu, (layer,), True, ms, tiles["tn_mix"], name="s5_glu")
        else:
            w_q_bf, w_o_bf = rounded
            q_p, q_s = _norm_matmul_call(xp, *mods_p(base + 3, base + 4), xs, *mods_s(base + 3, base + 4), g_mix,
                                         w_q_bf, (), BF16, tm, tiles["tn_q"])
            o_p = _attn_prompt_call(q_p.reshape(bp, lp, N_DIL * HEADS * HEAD_DIM),
                                    kvf_p.reshape(N_DIL, bp, lp, kv_rows * HEAD_DIM)).reshape(mp, D_MODEL)
            qh = q_s.reshape(bs, ls, N_DIL, KV_HEADS, REP, HEAD_DIM).transpose(0, 2, 3, 4, 1, 5)
            qh = qh.reshape(bs, N_DIL * KV_HEADS, REP * ls, HEAD_DIM)
            o_s = _attn_sample_call(qh, kv_s.reshape(N_DIL, bs, ls * kv_rows, HEAD_DIM), kv_caches, ls)
            o_s = o_s.reshape(bs, KV_HEADS, REP, ls, HEAD_DIM).transpose(0, 3, 1, 2, 4).reshape(ms, D_MODEL)
            xp = _proj_residual_call(o_p, xp, *mods_p(base + 5), w_o_bf, (), False, tiles["tm_out"],
                                     tiles["tn_mix"], name="attn_out")
            xs = _proj_residual_call(o_s, xs, as_block(*mods_s(base + 5)), w_o_bf, (), False, ms,
                                     tiles["tn_mix"], name="attn_out")
        later = ((w_kv, ()),) if layer == N_A_LAYERS - 1 else ()
        xp, xs, rounded = _ffn_call(xp, mods_p(base + 6, base + 7, base + 8), xs, mods_s(base + 6, base + 7, base + 8),
                                    norm_g_row(norm_g[layer, 2]), ffn_w_in, ffn_w_out, layer, 1, tm, tiles["tf"],
                                    tiles["tn_down"], round_also=later)
        if later:
            (w_kv_bf,) = rounded
    k1 = 9 * DEPTH + 2
    y_p = _final_norm_call(xp, *mods_p(k1, k1 + 1), norm_g_row(final_norm_g), tm)
    y_s = _final_norm_call(xs, *(as_block(a) for a in mods_s(k1, k1 + 1)), norm_g_row(final_norm_g), ms)
    kv_p = kv_p.reshape(N_DIL, bp, lp, 2, KV_HEADS, HEAD_DIM)
    kv_s = kv_s.reshape(N_DIL, bs, ls, 2, KV_HEADS, HEAD_DIM)
    return ((y_p.reshape(bp, lp, D_MODEL), jnp.stack(re_p, axis=0), jnp.stack(im_p, axis=0), kv_p),
            (y_s.reshape(bs, ls, D_MODEL), jnp.stack(re_s, axis=0), jnp.stack(im_s, axis=0), kv_s))


def kernel(x_prompt, x_sample, c_prompt, c_sample, state_s5_re, state_s5_im, cache_kv_g0, cache_kv_g1, cache_kv_g2, w_mod, b_mod, norm_g, ffn_w_in, ffn_w_out, s5_lambda_re, s5_lambda_im, s5_log_dt, s5_b_re, s5_b_im, s5_c_re, s5_c_im, s5_d, s5_w_glu, kv_norm_g, w_kv, attn_w_q, attn_w_o, final_norm_g):
    bp, seq_p, _ = x_prompt.shape
    bs, seq_s, _ = x_sample.shape
    caches = (cache_kv_g0, cache_kv_g1, cache_kv_g2)
    for (w, d), c in zip(DILATED_PATTERNS, caches):
        assert c.shape[1] == w and w == d * Q_BLOCK and seq_p % (d * Q_BLOCK) == 0
    assert bp <= SUBLANES and bs <= SUBLANES

    n_c = bp + bs
    c_all = jnp.pad(jnp.concatenate([c_prompt, c_sample], axis=0), ((0, (-n_c) % SUBLANES), (0, 0)))
    mod_all = _mod_call(c_all, w_mod, b_mod)
    mod_p = mod_all[:bp].reshape(bp, N_MOD, D_MODEL)
    mod_s = mod_all[bp:n_c].reshape(bs, N_MOD, D_MODEL)

    s5_mats = [_s5_weights(s5_lambda_re[l], s5_lambda_im[l], s5_log_dt[l], s5_b_re[l], s5_b_im[l],
                           s5_c_re[l], s5_c_im[l]) for l in range(N_A_LAYERS)]
    wts = (norm_g, ffn_w_in, ffn_w_out, s5_mats, s5_d, s5_w_glu, kv_norm_g, w_kv, attn_w_q, attn_w_o, final_norm_g)

    caches_flat = tuple(c.reshape(bs, c.shape[1] * 2 * KV_HEADS, HEAD_DIM) for c in caches)
    (y_p, re_p, im_p, kv_p), (y_s, re_s, im_s, kv_s) = _trunks(
        x_prompt, x_sample, mod_p, mod_s, (state_s5_re, state_s5_im), caches_flat, wts,
        _TILES)

    kvp = [kv_p[g][:, seq_p - min(w, seq_p):] for g, (w, _) in enumerate(DILATED_PATTERNS)]
    return (y_p, y_s, re_p, im_p, kvp[0], kvp[1], kvp[2], re_s, im_s, kv_s[0], kv_s[1], kv_s[2])
```

```python
import functools

import jax
import jax.numpy as jnp
from jax import lax
from jax.experimental import pallas as pl
from jax.experimental.pallas import tpu as pltpu

F32 = jnp.float32
BF16 = jnp.bfloat16

D_MODEL = 2048
DEPTH = 4
N_A_LAYERS = DEPTH // 2
D_FF = 5632
S5_GROUP = 16
S5_GROUPS = D_MODEL // S5_GROUP
S5_STATE = 64
S5_NSTATE = S5_GROUPS * S5_STATE
HEAD_DIM = 128
HEADS = 16
KV_HEADS = 4
REP = HEADS // KV_HEADS
DILATED_PATTERNS = ((128, 1), (512, 4), (2048, 16))
N_DIL = len(DILATED_PATTERNS)
Q_BLOCK = 128
N_MOD = 9 * DEPTH + 4
EPS = 1e-6
ATTN_SCALE = HEAD_DIM ** -0.5
ATTN_SCALE_LOG2 = ATTN_SCALE * 1.4426950408889634

VMEM_LIMIT_BYTES = 56 * 1024 * 1024
SUBLANES = 8
BF16_SUBLANES = 16
NORM_ROWS = 64
STAGE_ROWS = 256
S5_COL_BLOCK = 1024
S5_CH_BLOCK = S5_COL_BLOCK // S5_STATE * S5_GROUP
S5_N_BLOCKS = S5_NSTATE // S5_COL_BLOCK


_TILES = {"tm": 1024, "tf": 512, "tn_down": 512, "tm_out": 2048, "tn_mix": 512, "tn_q": 1024, "tn_mod": 1024,
          "s5_steps": 32}


def _params(*sem):
    return pltpu.CompilerParams(dimension_semantics=sem, vmem_limit_bytes=VMEM_LIMIT_BYTES)


def _rms_mod(x, g, shift, scale):
    xn = x * lax.rsqrt(jnp.mean(x * x, axis=-1, keepdims=True) + EPS)
    return xn * g * (1.0 + scale) + shift


def _norm_rows(x_ref, g_ref, sh_ref, sc_ref, h_ref, row_chunk):
    rows = x_ref.shape[0]
    g = g_ref[...]
    mod_rows = sh_ref.shape[0]
    if mod_rows == 1:
        gs = g * (1.0 + sc_ref[...])
        sh = sh_ref[...]
        row_chunk = min(row_chunk, BF16_SUBLANES)

        def body(c, carry):
            r0 = pl.multiple_of(c * row_chunk, row_chunk)
            x = x_ref[pl.ds(r0, row_chunk), :]
            xn = x * lax.rsqrt(jnp.mean(x * x, axis=-1, keepdims=True) + EPS)
            h_ref[pl.ds(r0, row_chunk), :] = (xn * gs + sh).astype(h_ref.dtype)
            return carry

        n_chunks = rows // row_chunk
        lax.fori_loop(0, n_chunks, body, 0, unroll=4 if n_chunks % 4 == 0 else 1)
        return

    def body(c, carry):
        r0 = pl.multiple_of(c * row_chunk, row_chunk)
        x = x_ref[pl.ds(r0, row_chunk), :]
        if mod_rows == rows and rows != row_chunk:
            sh = sh_ref[pl.ds(r0, row_chunk), :]
            sc = sc_ref[pl.ds(r0, row_chunk), :]
        else:
            sh = sh_ref[...]
            sc = sc_ref[...]
        h_ref[pl.ds(r0, row_chunk), :] = _rms_mod(x, g, sh, sc).astype(h_ref.dtype)
        return carry

    lax.fori_loop(0, rows // row_chunk, body, 0)


def _mod_kernel(c_ref, w_ref, b_ref, o_ref):
    c = c_ref[...]
    a = (c * jax.nn.sigmoid(c)).astype(BF16)
    o_ref[...] = jnp.dot(a, w_ref[...].astype(BF16), preferred_element_type=F32) + b_ref[...]


def _mod_call(c_all, w_mod, b_mod):
    rows = c_all.shape[0]
    n = w_mod.shape[1]
    tn = _TILES["tn_mod"]
    return pl.pallas_call(
        _mod_kernel,
        grid=(n // tn,),
        in_specs=[
            pl.BlockSpec((rows, D_MODEL), lambda j: (0, 0)),
            pl.BlockSpec((D_MODEL, tn), lambda j: (0, j)),
            pl.BlockSpec((1, tn), lambda j: (0, j)),
        ],
        out_specs=pl.BlockSpec((rows, tn), lambda j: (0, j)),
        out_shape=jax.ShapeDtypeStruct((rows, n), F32),
        compiler_params=_params("arbitrary"),
        name="adaln_mod",
    )(c_all, w_mod, b_mod.reshape(1, n))


def _norm_two_row_sets(x_ref, sh_ref, sc_ref, x2_ref, sh2_ref, sc2_ref, g_ref, h_ref, row_chunk):
    tm, r2 = x_ref.shape[0], x2_ref.shape[0]

    @pl.when(pl.program_id(1) == 0)
    def _():
        _norm_rows(x_ref, g_ref, sh_ref, sc_ref, h_ref.at[pl.ds(0, tm)], row_chunk)

    @pl.when((pl.program_id(0) == 0) & (pl.program_id(1) == 0))
    def _():
        _norm_rows(x2_ref, g_ref, sh2_ref, sc2_ref, h_ref.at[pl.ds(tm, r2)], r2)


def _ffn_up_kernel(x_ref, sh_ref, sc_ref, x2_ref, sh2_ref, sc2_ref, g_ref, wg_ref, wu_ref, *rest, row_chunk):
    n_side = (len(rest) - 3) // 2
    side_in, (a_ref, a2_ref), side_out, h_ref = (rest[:n_side], rest[n_side:n_side + 2],
                                                  rest[n_side + 2:2 * n_side + 2], rest[-1])
    tm = x_ref.shape[0]
    _norm_two_row_sets(x_ref, sh_ref, sc_ref, x2_ref, sh2_ref, sc2_ref, g_ref, h_ref, row_chunk)

    def act(h):
        for src, dst in zip(side_in, side_out):
            dst[...] = src[...].astype(dst.dtype)
        g = jnp.dot(h, wg_ref[...].astype(BF16), preferred_element_type=F32)
        u = jnp.dot(h, wu_ref[...].astype(BF16), preferred_element_type=F32)
        return (g * jax.nn.sigmoid(g) * u).astype(a_ref.dtype)

    @pl.when(pl.program_id(0) == 0)
    def _():
        a = act(h_ref[...])
        a_ref[...] = a[:tm]
        a2_ref[...] = a[tm:]

    @pl.when(pl.program_id(0) != 0)
    def _():
        a_ref[...] = act(h_ref[pl.ds(0, tm), :])


def _ffn_call(x, mods, x2, mods2, g_norm, w_in, w_out, layer, which, tm, tf, tn, round_also=()):
    m, r2 = x.shape[0], x2.shape[0]
    nb = mods[0].shape[0]
    tiles_per_seq = (m // nb) // tm
    nk = D_FF // tf
    n_steps = (m // tm) * nk
    mod_spec = pl.BlockSpec((None, 1, D_MODEL), lambda i, k: (i // tiles_per_seq, 0, 0))
    full2 = pl.BlockSpec((r2, D_MODEL), lambda i, k: (0, 0))

    side_in_specs, side_out_specs, side_out_shapes, side_args = [], [], [], []
    for w_side, lead in (((w_out, (layer, which)),) + tuple(round_also)):
        rows, cols = w_side.shape[-2:]
        rps = next(r for r in range(BF16_SUBLANES, rows + 1, BF16_SUBLANES)
                   if rows % r == 0 and r * n_steps >= rows)
        n_blk = rows // rps

        def blk(i, k, n_blk=n_blk):
            return jnp.minimum(i * nk + k, n_blk - 1)

        side_in_specs.append(pl.BlockSpec((None,) * len(lead) + (rps, cols),
                                          functools.partial(lambda i, k, lead, blk: (*lead, blk(i, k), 0),
                                                            lead=lead, blk=blk)))
        side_out_specs.append(pl.BlockSpec((rps, cols), functools.partial(lambda i, k, blk: (blk(i, k), 0), blk=blk)))
        side_out_shapes.append(jax.ShapeDtypeStruct((rows, cols), BF16))
        side_args.append(w_side)

    a, a2, w_out_bf, *rounded = pl.pallas_call(
        functools.partial(_ffn_up_kernel, row_chunk=min(tm, NORM_ROWS)),
        grid=(m // tm, nk),
        in_specs=[
            pl.BlockSpec((tm, D_MODEL), lambda i, k: (i, 0)),
            mod_spec, mod_spec,
            full2, full2, full2,
            pl.BlockSpec((1, D_MODEL), lambda i, k: (0, 0)),
            pl.BlockSpec((None, None, D_MODEL, tf), lambda i, k: (layer, which, 0, k)),
            pl.BlockSpec((None, None, D_MODEL, tf), lambda i, k: (layer, which, 0, k + nk)),
        ] + side_in_specs,
        out_specs=[pl.BlockSpec((tm, tf), lambda i, k: (i, k)),
                   pl.BlockSpec((r2, tf), lambda i, k: (0, jnp.where(i == 0, k, nk - 1)))] + side_out_specs,
        out_shape=[jax.ShapeDtypeStruct((m, D_FF), BF16), jax.ShapeDtypeStruct((r2, D_FF), BF16)] + side_out_shapes,
        scratch_shapes=[pltpu.VMEM((tm + r2, D_MODEL), BF16)],
        compiler_params=_params("arbitrary", "arbitrary"),
        name="ffn_up",
    )(x, mods[0], mods[1], x2, mods2[0], mods2[1], g_norm, w_in, w_in, *side_args)
    out = _proj_residual_call(a, x, mods[2], w_out_bf, (), False, tm, tn, gate_scale=0.5, name="ffn_down")
    out2 = _proj_residual_call(a2, x2, mods2[2].reshape(1, r2, D_MODEL), w_out_bf, (), False, r2, tn,
                               gate_scale=0.5, name="ffn_down")
    return out, out2, rounded


def _norm_matmul_kernel(x_ref, sh_ref, sc_ref, x2_ref, sh2_ref, sc2_ref, g_ref, w_ref, *rest, row_chunk, head_rows):
    h_ref = rest[-1]
    outs = rest[:-1]
    n_large = len(outs) - (2 if head_rows else 1)
    tm = x_ref.shape[0]
    _norm_two_row_sets(x_ref, sh_ref, sc_ref, x2_ref, sh2_ref, sc2_ref, g_ref, h_ref, row_chunk)

    def to_heads(res, ref):
        rows, cols = res.shape
        heads_per_tile = cols // HEAD_DIM
        first = (pl.program_id(1) % (head_rows // heads_per_tile)) * heads_per_tile
        for c in range(heads_per_tile):
            ref[pl.ds(first + c, rows, stride=head_rows), :] = res[:, c * HEAD_DIM:(c + 1) * HEAD_DIM]

    def emit_large(res):
        outs[0][...] = res.astype(outs[0].dtype)
        if head_rows:
            tiles_per_slab = head_rows * HEAD_DIM // res.shape[1]
            for s, ref in enumerate(outs[1:n_large]):
                @pl.when(pl.program_id(1) // tiles_per_slab == s)
                def _(ref=ref):
                    to_heads(res, ref)

    def emit_small(res):
        outs[n_large][...] = res.astype(outs[n_large].dtype)
        if head_rows:
            to_heads(res, outs[n_large + 1])

    @pl.when(pl.program_id(0) == 0)
    def _():
        res = jnp.dot(h_ref[...], w_ref[...].astype(BF16), preferred_element_type=F32)
        emit_large(res[:tm])
        emit_small(res[tm:])

    @pl.when(pl.program_id(0) != 0)
    def _():
        emit_large(jnp.dot(h_ref[pl.ds(0, tm), :], w_ref[...].astype(BF16), preferred_element_type=F32))


def _norm_matmul_call(x, shift, scale, x2, shift2, scale2, g_norm, w, w_lead, out_dtype, tm, tn, head_rows=0):
    m, r2 = x.shape[0], x2.shape[0]
    nb = shift.shape[0]
    n = w.shape[-1]
    nj = n // tn
    tiles_per_seq = (m // nb) // tm
    mod_spec = pl.BlockSpec((None, 1, D_MODEL), lambda i, j: (i // tiles_per_seq, 0, 0))
    full2 = pl.BlockSpec((r2, D_MODEL), lambda i, j: (0, 0))

    def col2(i, j):
        return jnp.where(i == 0, j, nj - 1)

    if head_rows:
        slab = head_rows * HEAD_DIM
        tps = slab // tn
        n_slabs = n // slab

        def heads_spec(s):
            return pl.BlockSpec((tm * head_rows, HEAD_DIM),
                                lambda i, j: (jnp.where(j // tps >= s, i, jnp.maximum(i - 1, 0)), 0))

        out_spec = ((pl.BlockSpec((None, tm, tn), lambda i, j: (j // tps, i, j % tps)),)
                    + tuple(heads_spec(s) for s in range(n_slabs))
                    + (pl.BlockSpec((None, r2, tn), lambda i, j: (col2(i, j) // tps, 0, col2(i, j) % tps)),
                       pl.BlockSpec((None, r2 * head_rows, HEAD_DIM), lambda i, j: (col2(i, j) // tps, 0, 0))))
        out_shape = ((jax.ShapeDtypeStruct((n_slabs, m, slab), out_dtype),)
                     + tuple(jax.ShapeDtypeStruct((m * head_rows, HEAD_DIM), out_dtype) for _ in range(n_slabs))
                     + (jax.ShapeDtypeStruct((n_slabs, r2, slab), out_dtype),
                        jax.ShapeDtypeStruct((n_slabs, r2 * head_rows, HEAD_DIM), out_dtype)))
    else:
        out_spec = (pl.BlockSpec((tm, tn), lambda i, j: (i, j)),
                    pl.BlockSpec((r2, tn), lambda i, j: (0, col2(i, j))))
        out_shape = (jax.ShapeDtypeStruct((m, n), out_dtype), jax.ShapeDtypeStruct((r2, n), out_dtype))
    return pl.pallas_call(
        functools.partial(_norm_matmul_kernel, row_chunk=min(tm, NORM_ROWS), head_rows=head_rows),
        grid=(m // tm, nj),
        in_specs=[
            pl.BlockSpec((tm, D_MODEL), lambda i, j: (i, 0)),
            mod_spec, mod_spec,
            full2, full2, full2,
            pl.BlockSpec((1, D_MODEL), lambda i, j: (0, 0)),
            pl.BlockSpec((None,) * len(w_lead) + (D_MODEL, tn), lambda i, j: (*w_lead, 0, j)),
        ],
        out_specs=out_spec,
        out_shape=out_shape,
        scratch_shapes=[pltpu.VMEM((tm + r2, D_MODEL), BF16)],
        compiler_params=_params("arbitrary", "arbitrary"),
        name="norm_matmul",
    )(x, shift, scale, x2, shift2, scale2, g_norm, w)


def _proj_residual_kernel(z_ref, x_ref, gt_ref, *refs, glu, gate_scale):
    z = z_ref[...]
    if glu:
        wa_ref, wg_ref, o_ref = refs
        a = jnp.dot(z, wa_ref[...].astype(BF16), preferred_element_type=F32)
        g = jnp.dot(z, wg_ref[...].astype(BF16), preferred_element_type=F32)
        y = a * jax.nn.sigmoid(g)
    else:
        w_ref, o_ref = refs
        y = jnp.dot(z, w_ref[...].astype(BF16), preferred_element_type=F32)
    gate = gt_ref[...] if gate_scale == 1.0 else gate_scale * gt_ref[...]
    o_ref[...] = x_ref[...] + gate * y


def _proj_residual_call(z, x, gate, w, w_lead, glu, tm, tn, gate_scale=1.0, name="proj_residual", z_buffers=2):
    m = x.shape[0]
    kdim = z.shape[1]
    nb, r, _ = gate.shape
    tiles_per_seq = (m // nb) // tm if r == 1 else 1
    n_blocks = D_MODEL // tn
    w_block = (None,) * len(w_lead) + (kdim, tn)
    w_specs = [pl.BlockSpec(w_block, lambda i, j: (*w_lead, 0, j))]
    w_args = [w]
    if glu:
        w_specs.append(pl.BlockSpec(w_block, lambda i, j: (*w_lead, 0, j + n_blocks)))
        w_args.append(w)
    return pl.pallas_call(
        functools.partial(_proj_residual_kernel, glu=glu, gate_scale=gate_scale),
        grid=(m // tm, n_blocks),
        in_specs=[
            (pl.BlockSpec((tm, kdim), lambda i, j: (i, 0)) if z_buffers == 2 else
             pl.BlockSpec((tm, kdim), lambda i, j: (i, 0), pipeline_mode=pl.Buffered(z_buffers))),
            pl.BlockSpec((tm, tn), lambda i, j: (i, j)),
            pl.BlockSpec((None, r, tn), lambda i, j: (i // tiles_per_seq, 0, j)),
        ] + w_specs,
        out_specs=pl.BlockSpec((tm, tn), lambda i, j: (i, j)),
        out_shape=jax.ShapeDtypeStruct((m, D_MODEL), F32),
        compiler_params=_params("parallel", "arbitrary"),
        name=name,
    )(z, x, gate, *w_args)


def _final_norm_kernel(x_ref, sh_ref, sc_ref, g_ref, o_ref, *, row_chunk):
    _norm_rows(x_ref, g_ref, sh_ref, sc_ref, o_ref, row_chunk)


def _final_norm_call(x, shift, scale, g_norm, tm):
    m = x.shape[0]
    nb, r, _ = shift.shape
    tiles_per_seq = (m // nb) // tm if r == 1 else 1
    mod_spec = pl.BlockSpec((None, r, D_MODEL), lambda i: (i // tiles_per_seq, 0, 0))
    return pl.pallas_call(
        functools.partial(_final_norm_kernel, row_chunk=min(tm, NORM_ROWS)),
        grid=(m // tm,),
        in_specs=[
            pl.BlockSpec((tm, D_MODEL), lambda i: (i, 0)),
            mod_spec, mod_spec,
            pl.BlockSpec((1, D_MODEL), lambda i: (0, 0)),
        ],
        out_specs=pl.BlockSpec((tm, D_MODEL), lambda i: (i, 0)),
        out_shape=jax.ShapeDtypeStruct((m, D_MODEL), F32),
        compiler_params=_params("parallel"),
        name="final_norm",
    )(x, shift, scale, g_norm)


def _s5_discretise(lam_re, lam_im, log_dt):
    dt = jnp.exp(log_dt)
    decay = jnp.exp(lam_re * dt)
    ab_re = decay * jnp.cos(lam_im * dt)
    ab_im = decay * jnp.sin(lam_im * dt)
    den = lam_re * lam_re + lam_im * lam_im
    f_re = ((ab_re - 1.0) * lam_re + ab_im * lam_im) / den
    f_im = (ab_im * lam_re - (ab_re - 1.0) * lam_im) / den
    return ab_re, ab_im, f_re, f_im


def _s5_prep_kernel(lre_ref, lim_ref, ldt_ref, bre_ref, bim_ref, ctre_ref, ctim_ref, lre_g, lim_g, ldt_g,
                    are_ref, aim_ref, inre_ref, inim_ref, outre_ref, outim_ref):
    c, p = S5_GROUP, S5_STATE
    ab_re, ab_im, _, _ = _s5_discretise(lre_g[...], lim_g[...], ldt_g[...])
    are_ref[...] = ab_re
    aim_ref[...] = ab_im
    _, _, f_re, f_im = _s5_discretise(lre_ref[...], lim_ref[...], ldt_ref[...])
    b_re, b_im = bre_ref[...], bim_ref[...]
    bb_re = f_re * b_re - f_im * b_im
    bb_im = f_re * b_im + f_im * b_re
    for ref in (inre_ref, inim_ref, outre_ref, outim_ref):
        ref[...] = jnp.zeros(ref.shape, ref.dtype)
    for gl in range(S5_CH_BLOCK // c):
        rows_in, cols_in = slice(gl * c, (gl + 1) * c), slice(gl * p, (gl + 1) * p)
        inre_ref[rows_in, cols_in] = bb_re[rows_in].astype(inre_ref.dtype)
        inim_ref[rows_in, cols_in] = bb_im[rows_in].astype(inim_ref.dtype)
        outre_ref[cols_in, rows_in] = ctre_ref[cols_in, :].astype(outre_ref.dtype)
        outim_ref[cols_in, rows_in] = ctim_ref[cols_in, :].astype(outim_ref.dtype)


def _s5_prep_call(lam_re, lam_im, log_dt, b_re, b_im, c_re, c_im):
    g, p, c = S5_GROUPS, S5_STATE, S5_GROUP
    rep = lambda a: jnp.repeat(a, c, axis=0)
    b_t = lambda b: b.transpose(0, 2, 1).reshape(g * c, p)
    c_t = lambda m: m.transpose(0, 2, 1).reshape(g * p, c)
    gpb = S5_CH_BLOCK // c
    row_blk = lambda rows, cols: pl.BlockSpec((rows, cols), lambda i: (i, 0))
    blk3 = lambda rows, cols: pl.BlockSpec((None, rows, cols), lambda i: (i, 0, 0))
    return pl.pallas_call(
        _s5_prep_kernel,
        grid=(S5_N_BLOCKS,),
        in_specs=[row_blk(S5_CH_BLOCK, p), row_blk(S5_CH_BLOCK, p), row_blk(S5_CH_BLOCK, 1),
                  row_blk(S5_CH_BLOCK, p), row_blk(S5_CH_BLOCK, p),
                  row_blk(S5_COL_BLOCK, c), row_blk(S5_COL_BLOCK, c),
                  row_blk(gpb, p), row_blk(gpb, p), row_blk(gpb, 1)],
        out_specs=(row_blk(gpb, p), row_blk(gpb, p),
                   blk3(S5_CH_BLOCK, S5_COL_BLOCK), blk3(S5_CH_BLOCK, S5_COL_BLOCK),
                   blk3(S5_COL_BLOCK, S5_CH_BLOCK), blk3(S5_COL_BLOCK, S5_CH_BLOCK)),
        out_shape=(jax.ShapeDtypeStruct((g, p), F32), jax.ShapeDtypeStruct((g, p), F32),
                   jax.ShapeDtypeStruct((S5_N_BLOCKS, S5_CH_BLOCK, S5_COL_BLOCK), BF16),
                   jax.ShapeDtypeStruct((S5_N_BLOCKS, S5_CH_BLOCK, S5_COL_BLOCK), BF16),
                   jax.ShapeDtypeStruct((S5_N_BLOCKS, S5_COL_BLOCK, S5_CH_BLOCK), BF16),
                   jax.ShapeDtypeStruct((S5_N_BLOCKS, S5_COL_BLOCK, S5_CH_BLOCK), BF16)),
        compiler_params=_params("parallel"),
        name="s5_prepare",
    )(rep(lam_re), rep(lam_im), rep(log_dt.reshape(g, 1)), b_t(b_re), b_t(b_im), c_t(c_re), c_t(c_im),
      lam_re, lam_im, log_dt.reshape(g, 1))


def _s5_kernel(x_ref, sh_ref, sc_ref, g_ref, bbre_ref, bbim_ref, are_ref, aim_ref, cre_ref, cim_ref,
               d_ref, h0re_ref, h0im_ref, z_ref, hre_out, him_out,
               u_scr, sre_scr, sim_scr, y_scr, pwre_scr, pwim_scr, *, row_chunk):
    t = pl.program_id(0)
    nb, steps, _ = x_ref.shape
    rows = steps * nb
    lane = HEAD_DIM
    seg = SUBLANES // nb
    seg_len = steps // seg
    assert seg in (1, 2)

    @pl.when(t == 0)
    def _():
        hre_out[...] = h0re_ref[...]
        him_out[...] = h0im_ref[...]
        if seg > 1:
            zero = jnp.zeros((nb, S5_COL_BLOCK), F32)
            for cb in range(S5_N_BLOCKS):
                a_re = jnp.broadcast_to(are_ref[cb], (nb, S5_COL_BLOCK))
                a_im = jnp.broadcast_to(aim_ref[cb], (nb, S5_COL_BLOCK))
                p_re, p_im = a_re, a_im
                for i in range(seg_len):
                    pwre_scr[cb, i * SUBLANES:(i + 1) * SUBLANES, :] = jnp.concatenate([zero, p_re], axis=0)
                    pwim_scr[cb, i * SUBLANES:(i + 1) * SUBLANES, :] = jnp.concatenate([zero, p_im], axis=0)
                    p_re, p_im = p_re * a_re - p_im * a_im, p_re * a_im + p_im * a_re

    g = g_ref[...]
    for b in range(nb):
        for s in range(seg):
            def norm_chunk(c, carry, b=b, s=s):
                r0 = pl.multiple_of(c * row_chunk, row_chunk)
                u = _rms_mod(x_ref[b, pl.ds(s * seg_len + r0, row_chunk), :], g, sh_ref[b], sc_ref[b])
                for j in range(D_MODEL // lane):
                    u_scr[j, pl.ds(r0 * SUBLANES + s * nb + b, row_chunk, stride=SUBLANES), :] = \
                        u[:, j * lane:(j + 1) * lane]
                return carry

            lax.fori_loop(0, seg_len // row_chunk, norm_chunk, 0)

    tiles_per_block = S5_CH_BLOCK // lane
    per_block_scratch = sre_scr.shape[0] == S5_N_BLOCKS
    for cb in range(S5_N_BLOCKS):
        ch = slice(cb * S5_CH_BLOCK, (cb + 1) * S5_CH_BLOCK)
        st = slice(cb * S5_COL_BLOCK, (cb + 1) * S5_COL_BLOCK)
        slab = cb if per_block_scratch else 0
        sre, sim, ys = sre_scr.at[slab], sim_scr.at[slab], y_scr.at[slab]
        u_blk = jnp.concatenate([u_scr[cb * tiles_per_block + j] for j in range(tiles_per_block)], axis=1)
        u_bf = u_blk.astype(BF16)
        sre[...] = jnp.dot(u_bf, bbre_ref[cb], preferred_element_type=F32)
        sim[...] = jnp.dot(u_bf, bbim_ref[cb], preferred_element_type=F32)

        a_re = jnp.broadcast_to(are_ref[cb], (SUBLANES, S5_COL_BLOCK))
        a_im = jnp.broadcast_to(aim_ref[cb], (SUBLANES, S5_COL_BLOCK))

        def tile_step(i, carry, a_re=a_re, a_im=a_im, sre=sre, sim=sim):
            h_re, h_im = carry
            r0 = i * SUBLANES if isinstance(i, int) else pl.multiple_of(i * SUBLANES, SUBLANES)
            n_re = a_re * h_re - a_im * h_im + sre[pl.ds(r0, SUBLANES), :]
            n_im = a_re * h_im + a_im * h_re + sim[pl.ds(r0, SUBLANES), :]
            sre[pl.ds(r0, SUBLANES), :] = n_re
            sim[pl.ds(r0, SUBLANES), :] = n_im
            return n_re, n_im

        def first_rows(v):
            return jnp.concatenate([v[:nb]] * seg, axis=0)

        h_re, h_im = hre_out[:, st], him_out[:, st]
        if seg > 1:
            zero = jnp.zeros((SUBLANES - nb, S5_COL_BLOCK), F32)
            h_re, h_im = jnp.concatenate([h_re, zero], axis=0), jnp.concatenate([h_im, zero], axis=0)
        carry = (h_re, h_im)
        if per_block_scratch:
            for i in range(seg_len):
                carry = tile_step(i, carry)
        else:
            carry = lax.fori_loop(0, seg_len, tile_step, carry, unroll=min(seg_len, 4))
        h_re, h_im = carry
        if seg > 1:
            c_re, c_im = first_rows(h_re), first_rows(h_im)
            for i in range(seg_len):
                rows_i = slice(i * SUBLANES, (i + 1) * SUBLANES)
                p_re, p_im = pwre_scr[cb, rows_i, :], pwim_scr[cb, rows_i, :]
                f_re = sre[rows_i, :] + (p_re * c_re - p_im * c_im)
                f_im = sim[rows_i, :] + (p_re * c_im + p_im * c_re)
                sre[rows_i, :] = f_re
                sim[rows_i, :] = f_im
            h_re, h_im = f_re[SUBLANES - nb:], f_im[SUBLANES - nb:]
        hre_out[:, st] = h_re
        him_out[:, st] = h_im

        y = (jnp.dot(sre[...].astype(BF16), cre_ref[cb], preferred_element_type=F32)
             - jnp.dot(sim[...].astype(BF16), cim_ref[cb], preferred_element_type=F32))
        zf = jax.nn.gelu(y + d_ref[:, ch] * u_blk)
        for j in range(tiles_per_block):
            ys[j] = zf[:, j * lane:(j + 1) * lane]
        for b in range(nb):
            for s in range(seg):
                for j in range(tiles_per_block):
                    c0 = cb * S5_CH_BLOCK + j * lane
                    z_ref[b, s * seg_len:(s + 1) * seg_len, c0:c0 + lane] = \
                        ys[j, pl.ds(s * nb + b, seg_len, stride=SUBLANES), :].astype(z_ref.dtype)


def _s5_call(x, shift, scale, g_norm, bb_re, bb_im, a_re, a_im, c_re, c_im, d_skip,
             h0_re, h0_im, steps_per_chunk, z_dtype, per_block_scratch):
    nb, seq, _ = x.shape
    assert SUBLANES % nb == 0 and seq % steps_per_chunk == 0
    tr = steps_per_chunk * nb
    slabs = S5_N_BLOCKS if per_block_scratch else 1
    seg = SUBLANES // nb
    pw_rows = tr if seg > 1 else SUBLANES
    row_chunk = min(steps_per_chunk // (SUBLANES // nb), NORM_ROWS)
    const2 = lambda t: (0, 0)
    const3 = lambda t: (0, 0, 0)
    state_spec = pl.BlockSpec((nb, S5_NSTATE), const2)
    io_spec = pl.BlockSpec((nb, steps_per_chunk, D_MODEL), lambda t: (0, t, 0))
    return pl.pallas_call(
        functools.partial(_s5_kernel, row_chunk=row_chunk),
        grid=(seq // steps_per_chunk,),
        in_specs=[
            io_spec,
            pl.BlockSpec((nb, 1, D_MODEL), const3),
            pl.BlockSpec((nb, 1, D_MODEL), const3),
            pl.BlockSpec((1, D_MODEL), const2),
            pl.BlockSpec((S5_N_BLOCKS, S5_CH_BLOCK, S5_COL_BLOCK), const3, pipeline_mode=pl.Buffered(1)),
            pl.BlockSpec((S5_N_BLOCKS, S5_CH_BLOCK, S5_COL_BLOCK), const3, pipeline_mode=pl.Buffered(1)),
            pl.BlockSpec((S5_N_BLOCKS, 1, S5_COL_BLOCK), const3),
            pl.BlockSpec((S5_N_BLOCKS, 1, S5_COL_BLOCK), const3),
            pl.BlockSpec((S5_N_BLOCKS, S5_COL_BLOCK, S5_CH_BLOCK), const3, pipeline_mode=pl.Buffered(1)),
            pl.BlockSpec((S5_N_BLOCKS, S5_COL_BLOCK, S5_CH_BLOCK), const3, pipeline_mode=pl.Buffered(1)),
            pl.BlockSpec((1, D_MODEL), const2),
            state_spec, state_spec,
        ],
        out_specs=(io_spec, state_spec, state_spec),
        out_shape=(jax.ShapeDtypeStruct((nb, seq, D_MODEL), z_dtype),
                   jax.ShapeDtypeStruct((nb, S5_NSTATE), F32),
                   jax.ShapeDtypeStruct((nb, S5_NSTATE), F32)),
        scratch_shapes=[pltpu.VMEM((D_MODEL // HEAD_DIM, tr, HEAD_DIM), F32),
                        pltpu.VMEM((slabs, tr, S5_COL_BLOCK), F32),
                        pltpu.VMEM((slabs, tr, S5_COL_BLOCK), F32),
                        pltpu.VMEM((slabs, S5_CH_BLOCK // HEAD_DIM, tr, HEAD_DIM), F32),
                        pltpu.VMEM((S5_N_BLOCKS, pw_rows, S5_COL_BLOCK), F32),
                        pltpu.VMEM((S5_N_BLOCKS, pw_rows, S5_COL_BLOCK), F32)],
        compiler_params=_params("arbitrary"),
        name="s5_mixer",
    )(x, shift, scale, g_norm, bb_re, bb_im, a_re, a_im, c_re, c_im, d_skip, h0_re, h0_im)


def _softmax_pv(s2, v_bf):
    m = jnp.max(s2, axis=-1, keepdims=True)
    p = jnp.exp2(s2 - m)
    l = jnp.sum(p, axis=-1, keepdims=True)
    o = jnp.dot(p.astype(BF16), v_bf, preferred_element_type=F32) / l
    return o, jnp.broadcast_to(m + jnp.log2(l), o.shape)


def _qk(q_bf, k_bf, scale=ATTN_SCALE):
    return lax.dot_general(q_bf, k_bf, (((1,), (1,)), ((), ())), preferred_element_type=F32) * scale


def _attn_prompt_kernel(q0_ref, q1_ref, q2_ref, k0_ref, v0_ref, k1_ref, v1_ref, k2_ref, v2_ref, o_ref,
                        qf_scr, o1_scr, l1_scr, o2_scr, l2_scr, band_scr, first_scr):
    seq = q0_ref.shape[0]
    qb = Q_BLOCK
    row = lax.broadcasted_iota(jnp.int32, (REP * qb, 2 * qb), 0) & (qb - 1)
    col = lax.broadcasted_iota(jnp.int32, (REP * qb, 2 * qb), 1)
    band_scr[...] = jnp.where((col >= row) & (col <= row + qb), 0.0, -jnp.inf)
    row1 = lax.broadcasted_iota(jnp.int32, (REP * qb, qb), 0) & (qb - 1)
    col1 = lax.broadcasted_iota(jnp.int32, (REP * qb, qb), 1)
    first_scr[...] = jnp.where(col1 <= row1, 0.0, -jnp.inf)

    def rows_of(start, d):
        return pl.ds(start, qb) if d == 1 else pl.ds(start, qb, stride=d)

    def load_q(src_ref, start, d):
        if d == 1:
            parts = [src_ref[pl.ds(start, qb), e * HEAD_DIM:(e + 1) * HEAD_DIM] for e in range(REP)]
        else:
            parts = [src_ref[e, rows_of(start, d), :] for e in range(REP)]
        return jnp.concatenate(parts, axis=0).astype(BF16)

    def unit(src_q, k_ref, v_ref, start, d, first):
        q4 = load_q(src_q, start, d)
        k_cur = k_ref[rows_of(start, d), :]
        v_cur = v_ref[rows_of(start, d), :]
        if first:
            s = _qk(q4, k_cur.astype(BF16), ATTN_SCALE_LOG2) + first_scr[...]
            return _softmax_pv(s, v_cur.astype(BF16))
        prev = start - qb * d
        k_band = jnp.concatenate([k_ref[rows_of(prev, d), :], k_cur], axis=0)
        v_band = jnp.concatenate([v_ref[rows_of(prev, d), :], v_cur], axis=0)
        s = _qk(q4, k_band.astype(BF16), ATTN_SCALE_LOG2) + band_scr[...]
        return _softmax_pv(s, v_band.astype(BF16))

    def stage_group(q_ref, k_ref, v_ref, d, o_scr, l_scr):
        n_blocks = seq // (qb * d)

        def copy(c, carry):
            r0 = pl.multiple_of(c * STAGE_ROWS, STAGE_ROWS)
            for e in range(REP):
                qf_scr[e, pl.ds(r0, STAGE_ROWS), :] = \
                    q_ref[pl.ds(r0, STAGE_ROWS), e * HEAD_DIM:(e + 1) * HEAD_DIM].astype(F32)
            return carry

        lax.fori_loop(0, seq // STAGE_ROWS, copy, 0)

        def store(start, o, lse):
            for e in range(REP):
                sl = slice(e * qb, (e + 1) * qb)
                o_scr[e, rows_of(start, d), :] = o[sl]
                l_scr[e, rows_of(start, d), :] = lse[sl]

        def per_class(r, carry):
            store(r, *unit(qf_scr, k_ref, v_ref, r, d, True))

            def per_block(jb, c2):
                start = jb * (qb * d) + r
                store(start, *unit(qf_scr, k_ref, v_ref, start, d, False))
                return c2

            if n_blocks > 1:
                lax.fori_loop(1, n_blocks, per_block, 0, unroll=True)
            return carry

        lax.fori_loop(0, d, per_class, 0, unroll=4 if n_blocks == 1 else 2)

    stage_group(q2_ref, k2_ref, v2_ref, DILATED_PATTERNS[2][1], o2_scr, l2_scr)
    stage_group(q1_ref, k1_ref, v1_ref, DILATED_PATTERNS[1][1], o1_scr, l1_scr)

    def merge(start, o0, lse0):
        for e in range(REP):
            sl = slice(e * qb, (e + 1) * qb)
            cols = slice(e * HEAD_DIM, (e + 1) * HEAD_DIM)
            la = lse0[sl]
            lb = l1_scr[e, pl.ds(start, qb), :]
            lc = l2_scr[e, pl.ds(start, qb), :]
            mx = jnp.maximum(jnp.maximum(la, lb), lc)
            wa = jnp.exp2(la - mx)
            wb = jnp.exp2(lb - mx)
            wc = jnp.exp2(lc - mx)
            inv = 1.0 / (wa + wb + wc)
            acc = (wa * o0[sl] + wb * o1_scr[e, pl.ds(start, qb), :] + wc * o2_scr[e, pl.ds(start, qb), :]) * inv
            o_ref[pl.ds(start, qb), cols] = acc.astype(o_ref.dtype)

    merge(0, *unit(q0_ref, k0_ref, v0_ref, 0, 1, True))

    def per_block0(jb, carry):
        start = pl.multiple_of(jb * qb, qb)
        merge(start, *unit(q0_ref, k0_ref, v0_ref, start, 1, False))
        return carry

    n_blocks0 = seq // qb
    lax.fori_loop(1, n_blocks0, per_block0, 0, unroll=5 if (n_blocks0 - 1) % 5 == 0 else 1)


def _attn_prompt_call(q, kv):
    b, seq, _ = q.shape
    qw = REP * HEAD_DIM
    q_specs = [pl.BlockSpec((None, seq, qw), functools.partial(lambda bi, h, g: (bi, 0, g * KV_HEADS + h), g=g))
               for g in range(N_DIL)]
    kv_specs = []
    for g in range(N_DIL):
        kv_specs.append(pl.BlockSpec((None, None, seq, HEAD_DIM),
                                     functools.partial(lambda bi, h, g: (g, bi, 0, h), g=g)))
        kv_specs.append(pl.BlockSpec((None, None, seq, HEAD_DIM),
                                     functools.partial(lambda bi, h, g: (g, bi, 0, KV_HEADS + h), g=g)))
    kv_args = [kv] * (2 * N_DIL)
    return pl.pallas_call(
        _attn_prompt_kernel,
        grid=(b, KV_HEADS),
        in_specs=q_specs + kv_specs,
        out_specs=pl.BlockSpec((None, seq, qw), lambda bi, h: (bi, 0, h)),
        out_shape=jax.ShapeDtypeStruct((b, seq, HEADS * HEAD_DIM), BF16),
        scratch_shapes=[pltpu.VMEM((REP, seq, HEAD_DIM), F32),
                        pltpu.VMEM((REP, seq, HEAD_DIM), F32), pltpu.VMEM((REP, seq, HEAD_DIM), F32),
                        pltpu.VMEM((REP, seq, HEAD_DIM), F32), pltpu.VMEM((REP, seq, HEAD_DIM), F32),
                        pltpu.VMEM((REP * Q_BLOCK, 2 * Q_BLOCK), F32), pltpu.VMEM((REP * Q_BLOCK, Q_BLOCK), F32)],
        compiler_params=_params("parallel", "arbitrary"),
        name="dilated_attention_prompt",
    )(q, q, q, *kv_args)


def _attn_sample_kernel(q_ref, kvn_ref, c0_ref, c1_ref, c2_ref, o_ref, *, n_tok):
    rows = REP * n_tok
    cache_refs = (c0_ref, c1_ref, c2_ref)
    kv_rows = 2 * KV_HEADS
    for h in range(KV_HEADS):
        outs, lses = [], []
        for g, (_, d) in enumerate(DILATED_PATTERNS):
            q = q_ref[g * KV_HEADS + h]
            lw = cache_refs[g].shape[0] // kv_rows
            k_c = cache_refs[g][pl.ds(h, lw, stride=kv_rows), :].astype(BF16)
            v_c = cache_refs[g][pl.ds(KV_HEADS + h, lw, stride=kv_rows), :].astype(BF16)
            k_n = kvn_ref[g, pl.ds(h, n_tok, stride=kv_rows), :].astype(BF16)
            v_n = kvn_ref[g, pl.ds(KV_HEADS + h, n_tok, stride=kv_rows), :].astype(BF16)
            tq_c = lax.broadcasted_iota(jnp.int32, (rows, lw), 0) & (n_tok - 1)
            idx_c = lax.broadcasted_iota(jnp.int32, (rows, lw), 1)
            ok_c = (idx_c >= tq_c) & (((idx_c - tq_c) & (d - 1)) == 0)
            tq_n = lax.broadcasted_iota(jnp.int32, (rows, n_tok), 0) & (n_tok - 1)
            idx_n = lax.broadcasted_iota(jnp.int32, (rows, n_tok), 1)
            ok_n = (idx_n <= tq_n) & (((tq_n - idx_n) & (d - 1)) == 0)
            s_c = jnp.where(ok_c, _qk(q, k_c), -jnp.inf)
            s_n = jnp.where(ok_n, _qk(q, k_n), -jnp.inf)
            m = jnp.maximum(jnp.max(s_c, axis=-1, keepdims=True), jnp.max(s_n, axis=-1, keepdims=True))
            p_c = jnp.exp(s_c - m)
            p_n = jnp.exp(s_n - m)
            l = jnp.sum(p_c, axis=-1, keepdims=True) + jnp.sum(p_n, axis=-1, keepdims=True)
            o = (jnp.dot(p_c.astype(BF16), v_c, preferred_element_type=F32)
                 + jnp.dot(p_n.astype(BF16), v_n, preferred_element_type=F32)) / l
            outs.append(o)
            lses.append(m + jnp.log(l))
        mx = jnp.maximum(jnp.maximum(lses[0], lses[1]), lses[2])
        w = [jnp.exp(x - mx) for x in lses]
        inv = 1.0 / (w[0] + w[1] + w[2])
        acc = (w[0] * inv) * outs[0] + (w[1] * inv) * outs[1] + (w[2] * inv) * outs[2]
        o_ref[h] = acc.astype(o_ref.dtype)


def _attn_sample_call(q_heads, kv_new, caches, n_tok):
    b = q_heads.shape[0]
    rows = REP * n_tok
    assert n_tok & (n_tok - 1) == 0
    in_specs = [pl.BlockSpec((None, N_DIL * KV_HEADS, rows, HEAD_DIM), lambda bi: (bi, 0, 0, 0)),
                pl.BlockSpec((N_DIL, None, kv_new.shape[2], HEAD_DIM), lambda bi: (0, bi, 0, 0))]
    in_specs += [pl.BlockSpec((None, c.shape[1], HEAD_DIM), lambda bi: (bi, 0, 0)) for c in caches]
    return pl.pallas_call(
        functools.partial(_attn_sample_kernel, n_tok=n_tok),
        grid=(b,),
        in_specs=in_specs,
        out_specs=pl.BlockSpec((None, KV_HEADS, rows, HEAD_DIM), lambda bi: (bi, 0, 0, 0)),
        out_shape=jax.ShapeDtypeStruct((b, KV_HEADS, rows, HEAD_DIM), BF16),
        compiler_params=_params("parallel"),
        name="dilated_attention_sample",
    )(q_heads, kv_new, *caches)


def _s5_weights(lam_re, lam_im, log_dt, b_re, b_im, c_re, c_im):
    a_re, a_im, in_re, in_im, out_re, out_im = _s5_prep_call(lam_re, lam_im, log_dt, b_re, b_im, c_re, c_im)
    lam_bar = lambda a: a.reshape(S5_N_BLOCKS, 1, S5_COL_BLOCK)
    return in_re, in_im, lam_bar(a_re), lam_bar(a_im), out_re, out_im


def _trunks(x_p, x_s, mod_p, mod_s, s5_state, kv_caches, wts, tiles):
    (norm_g, ffn_w_in, ffn_w_out, s5_mats, s5_d, s5_w_glu, kv_norm_g, w_kv,
     attn_w_q, attn_w_o, final_norm_g) = wts
    bp, lp, _ = x_p.shape
    bs, ls, _ = x_s.shape
    mp, ms = bp * lp, bs * ls
    tm = tiles["tm"]
    kv_rows = 2 * KV_HEADS

    def mods_p(*ks):
        return tuple(mod_p[:, k].reshape(bp, 1, D_MODEL) for k in ks)

    def mods_s(*ks):
        return tuple(jnp.repeat(mod_s[:, k], ls, axis=0) for k in ks)

    def as_block(a):
        return a.reshape(1, *a.shape)

    def norm_g_row(a):
        return a.reshape(1, D_MODEL)

    xp = x_p.reshape(mp, D_MODEL)
    xs = x_s.reshape(ms, D_MODEL)
    re_p, im_p, re_s, im_s = [], [], [], []
    for layer in range(DEPTH):
        base = 9 * layer
        if layer == N_A_LAYERS:
            k0 = 9 * DEPTH
            kvf_p, *kv_p, _, kv_s = _norm_matmul_call(xp, *mods_p(k0, k0 + 1), xs, *mods_s(k0, k0 + 1),
                                                     norm_g_row(kv_norm_g), w_kv_bf, (), F32, tm,
                                                     KV_HEADS * HEAD_DIM, head_rows=kv_rows)
        later = () if layer < N_A_LAYERS else ((attn_w_q, (layer - N_A_LAYERS,)), (attn_w_o, (layer - N_A_LAYERS,)))
        xp, xs, rounded = _ffn_call(xp, mods_p(base, base + 1, base + 2), xs, mods_s(base, base + 1, base + 2),
                                    norm_g_row(norm_g[layer, 0]), ffn_w_in, ffn_w_out, layer, 0, tm, tiles["tf"],
                                    tiles["tn_down"], round_also=later)
        g_mix = norm_g_row(norm_g[layer, 1])
        if layer < N_A_LAYERS:
            d_skip = s5_d[layer].reshape(1, D_MODEL)
            zero_state = jnp.zeros((bp, S5_NSTATE), F32)
            zp, h_re, h_im = _s5_call(xp.reshape(bp, lp, D_MODEL), *mods_p(base + 3, base + 4), g_mix,
                                      *s5_mats[layer], d_skip, zero_state, zero_state,
                                      min(lp, tiles["s5_steps"]), BF16, True)
            re_p.append(h_re.reshape(bp, S5_GROUPS, S5_STATE))
            im_p.append(h_im.reshape(bp, S5_GROUPS, S5_STATE))
            zs, h_re, h_im = _s5_call(xs.reshape(bs, ls, D_MODEL), mod_s[:, base + 3].reshape(bs, 1, D_MODEL),
                                      mod_s[:, base + 4].reshape(bs, 1, D_MODEL), g_mix, *s5_mats[layer], d_skip,
                                      s5_state[0][layer].reshape(bs, S5_NSTATE),
                                      s5_state[1][layer].reshape(bs, S5_NSTATE), ls, F32, False)
            re_s.append(h_re.reshape(bs, S5_GROUPS, S5_STATE))
            im_s.append(h_im.reshape(bs, S5_GROUPS, S5_STATE))
            xp = _proj_residual_call(zp.reshape(mp, D_MODEL), xp, *mods_p(base + 5), s5_w_glu, (layer,),
                                     True, tm, tiles["tn_mix"], name="s5_glu")
            xs = _proj_residual_call(zs.reshape(ms, D_MODEL).astype(BF16), xs, as_block(*mods_s(base + 5)),
                                     s5_w_glu, (layer,), True, ms, tiles["tn_mix"], name="s5_glu")
        else:
            w_q_bf, w_o_bf = rounded
            q_p, q_s = _norm_matmul_call(xp, *mods_p(base + 3, base + 4), xs, *mods_s(base + 3, base + 4), g_mix,
                                         w_q_bf, (), BF16, tm, tiles["tn_q"])
            o_p = _attn_prompt_call(q_p.reshape(bp, lp, N_DIL * HEADS * HEAD_DIM),
                                    kvf_p.reshape(N_DIL, bp, lp, kv_rows * HEAD_DIM)).reshape(mp, D_MODEL)
            qh = q_s.reshape(bs, ls, N_DIL, KV_HEADS, REP, HEAD_DIM).transpose(0, 2, 3, 4, 1, 5)
            qh = qh.reshape(bs, N_DIL * KV_HEADS, REP * ls, HEAD_DIM)
            o_s = _attn_sample_call(qh, kv_s.reshape(N_DIL, bs, ls * kv_rows, HEAD_DIM), kv_caches, ls)
            o_s = o_s.reshape(bs, KV_HEADS, REP, ls, HEAD_DIM).transpose(0, 3, 1, 2, 4).reshape(ms, D_MODEL)
            xp = _proj_residual_call(o_p, xp, *mods_p(base + 5), w_o_bf, (), False, tiles["tm_out"],
                                     tiles["tn_mix"], name="attn_out")
            xs = _proj_residual_call(o_s, xs, as_block(*mods_s(base + 5)), w_o_bf, (), False, ms,
                                     tiles["tn_mix"], name="attn_out")
        later = ((w_kv, ()),) if layer == N_A_LAYERS - 1 else ()
        xp, xs, rounded = _ffn_call(xp, mods_p(base + 6, base + 7, base + 8), xs, mods_s(base + 6, base + 7, base + 8),
                                    norm_g_row(norm_g[layer, 2]), ffn_w_in, ffn_w_out, layer, 1, tm, tiles["tf"],
                                    tiles["tn_down"], round_also=later)
        if later:
            (w_kv_bf,) = rounded
    k1 = 9 * DEPTH + 2
    y_p = _final_norm_call(xp, *mods_p(k1, k1 + 1), norm_g_row(final_norm_g), tm)
    y_s = _final_norm_call(xs, *(as_block(a) for a in mods_s(k1, k1 + 1)), norm_g_row(final_norm_g), ms)
    kv_p = [a.reshape(bp, lp, 2, KV_HEADS, HEAD_DIM) for a in kv_p]
    kv_s = kv_s.reshape(N_DIL, bs, ls, 2, KV_HEADS, HEAD_DIM)
    return ((y_p.reshape(bp, lp, D_MODEL), jnp.stack(re_p, axis=0), jnp.stack(im_p, axis=0), kv_p),
            (y_s.reshape(bs, ls, D_MODEL), jnp.stack(re_s, axis=0), jnp.stack(im_s, axis=0), kv_s))


def kernel(x_prompt, x_sample, c_prompt, c_sample, state_s5_re, state_s5_im, cache_kv_g0, cache_kv_g1, cache_kv_g2, w_mod, b_mod, norm_g, ffn_w_in, ffn_w_out, s5_lambda_re, s5_lambda_im, s5_log_dt, s5_b_re, s5_b_im, s5_c_re, s5_c_im, s5_d, s5_w_glu, kv_norm_g, w_kv, attn_w_q, attn_w_o, final_norm_g):
    bp, seq_p, _ = x_prompt.shape
    bs, seq_s, _ = x_sample.shape
    caches = (cache_kv_g0, cache_kv_g1, cache_kv_g2)
    for (w, d), c in zip(DILATED_PATTERNS, caches):
        assert c.shape[1] == w and w == d * Q_BLOCK and seq_p % (d * Q_BLOCK) == 0
    assert bp <= SUBLANES and bs <= SUBLANES

    n_c = bp + bs
    c_all = jnp.pad(jnp.concatenate([c_prompt, c_sample], axis=0), ((0, (-n_c) % SUBLANES), (0, 0)))
    mod_all = _mod_call(c_all, w_mod, b_mod)
    mod_p = mod_all[:bp].reshape(bp, N_MOD, D_MODEL)
    mod_s = mod_all[bp:n_c].reshape(bs, N_MOD, D_MODEL)

    s5_mats = [_s5_weights(s5_lambda_re[l], s5_lambda_im[l], s5_log_dt[l], s5_b_re[l], s5_b_im[l],
                           s5_c_re[l], s5_c_im[l]) for l in range(N_A_LAYERS)]
    wts = (norm_g, ffn_w_in, ffn_w_out, s5_mats, s5_d, s5_w_glu, kv_norm_g, w_kv, attn_w_q, attn_w_o, final_norm_g)

    caches_flat = tuple(c.reshape(bs, c.shape[1] * 2 * KV_HEADS, HEAD_DIM) for c in caches)
    (y_p, re_p, im_p, kv_p), (y_s, re_s, im_s, kv_s) = _trunks(
        x_prompt, x_sample, mod_p, mod_s, (state_s5_re, state_s5_im), caches_flat, wts,
        _TILES)

    kvp = [kv_p[g][:, seq_p - min(w, seq_p):] for g, (w, _) in enumerate(DILATED_PATTERNS)]
    return (y_p, y_s, re_p, im_p, kvp[0], kvp[1], kvp[2], re_s, im_s, kv_s[0], kv_s[1], kv_s[2])
```

```python
import functools

import jax
import jax.numpy as jnp
from jax import lax
from jax.experimental import pallas as pl
from jax.experimental.pallas import tpu as pltpu

F32 = jnp.float32
BF16 = jnp.bfloat16

D_MODEL = 2048
DEPTH = 4
N_A_LAYERS = DEPTH // 2
D_FF = 5632
S5_GROUP = 16
S5_GROUPS = D_MODEL // S5_GROUP
S5_STATE = 64
S5_NSTATE = S5_GROUPS * S5_STATE
HEAD_DIM = 128
HEADS = 16
KV_HEADS = 4
REP = HEADS // KV_HEADS
DILATED_PATTERNS = ((128, 1), (512, 4), (2048, 16))
N_DIL = len(DILATED_PATTERNS)
Q_BLOCK = 128
N_MOD = 9 * DEPTH + 4
EPS = 1e-6
ATTN_SCALE = HEAD_DIM ** -0.5
ATTN_SCALE_LOG2 = ATTN_SCALE * 1.4426950408889634

VMEM_LIMIT_BYTES = 56 * 1024 * 1024
SUBLANES = 8
BF16_SUBLANES = 16
NORM_ROWS = 64
STAGE_ROWS = 256
S5_COL_BLOCK = 1024
S5_CH_BLOCK = S5_COL_BLOCK // S5_STATE * S5_GROUP
S5_N_BLOCKS = S5_NSTATE // S5_COL_BLOCK


_TILES = {"tm": 1024, "tf": 512, "tn_down": 512, "tm_out": 2048, "tn_mix": 512, "tn_q": 1024, "tn_mod": 1024,
          "s5_steps": 32}


def _params(*sem):
    return pltpu.CompilerParams(dimension_semantics=sem, vmem_limit_bytes=VMEM_LIMIT_BYTES)


def _rms_mod(x, g, shift, scale):
    xn = x * lax.rsqrt(jnp.mean(x * x, axis=-1, keepdims=True) + EPS)
    return xn * g * (1.0 + scale) + shift


def _norm_rows(x_ref, g_ref, sh_ref, sc_ref, h_ref, row_chunk):
    rows = x_ref.shape[0]
    g = g_ref[...]
    mod_rows = sh_ref.shape[0]
    if mod_rows == 1:
        gs = g * (1.0 + sc_ref[...])
        sh = sh_ref[...]
        row_chunk = min(row_chunk, BF16_SUBLANES)

        def body(c, carry):
            r0 = pl.multiple_of(c * row_chunk, row_chunk)
            x = x_ref[pl.ds(r0, row_chunk), :]
            xn = x * lax.rsqrt(jnp.mean(x * x, axis=-1, keepdims=True) + EPS)
            h_ref[pl.ds(r0, row_chunk), :] = (xn * gs + sh).astype(h_ref.dtype)
            return carry

        n_chunks = rows // row_chunk
        lax.fori_loop(0, n_chunks, body, 0, unroll=4 if n_chunks % 4 == 0 else 1)
        return

    def body(c, carry):
        r0 = pl.multiple_of(c * row_chunk, row_chunk)
        x = x_ref[pl.ds(r0, row_chunk), :]
        if mod_rows == rows and rows != row_chunk:
            sh = sh_ref[pl.ds(r0, row_chunk), :]
            sc = sc_ref[pl.ds(r0, row_chunk), :]
        else:
            sh = sh_ref[...]
            sc = sc_ref[...]
        h_ref[pl.ds(r0, row_chunk), :] = _rms_mod(x, g, sh, sc).astype(h_ref.dtype)
        return carry

    lax.fori_loop(0, rows // row_chunk, body, 0)


def _mod_kernel(c_ref, w_ref, b_ref, o_ref):
    c = c_ref[...]
    a = (c * jax.nn.sigmoid(c)).astype(BF16)
    o_ref[...] = jnp.dot(a, w_ref[...].astype(BF16), preferred_element_type=F32) + b_ref[...]


def _mod_call(c_all, w_mod, b_mod):
    rows = c_all.shape[0]
    n = w_mod.shape[1]
    tn = _TILES["tn_mod"]
    return pl.pallas_call(
        _mod_kernel,
        grid=(n // tn,),
        in_specs=[
            pl.BlockSpec((rows, D_MODEL), lambda j: (0, 0)),
            pl.BlockSpec((D_MODEL, tn), lambda j: (0, j)),
            pl.BlockSpec((1, tn), lambda j: (0, j)),
        ],
        out_specs=pl.BlockSpec((rows, tn), lambda j: (0, j)),
        out_shape=jax.ShapeDtypeStruct((rows, n), F32),
        compiler_params=_params("arbitrary"),
        name="adaln_mod",
    )(c_all, w_mod, b_mod.reshape(1, n))


def _norm_two_row_sets(x_ref, sh_ref, sc_ref, x2_ref, sh2_ref, sc2_ref, g_ref, h_ref, row_chunk):
    tm, r2 = x_ref.shape[0], x2_ref.shape[0]

    @pl.when(pl.program_id(1) == 0)
    def _():
        _norm_rows(x_ref, g_ref, sh_ref, sc_ref, h_ref.at[pl.ds(0, tm)], row_chunk)

    @pl.when((pl.program_id(0) == 0) & (pl.program_id(1) == 0))
    def _():
        _norm_rows(x2_ref, g_ref, sh2_ref, sc2_ref, h_ref.at[pl.ds(tm, r2)], r2)


def _ffn_up_kernel(x_ref, sh_ref, sc_ref, x2_ref, sh2_ref, sc2_ref, g_ref, wg_ref, wu_ref, *rest, row_chunk):
    n_side = (len(rest) - 3) // 2
    side_in, (a_ref, a2_ref), side_out, h_ref = (rest[:n_side], rest[n_side:n_side + 2],
                                                  rest[n_side + 2:2 * n_side + 2], rest[-1])
    tm = x_ref.shape[0]
    _norm_two_row_sets(x_ref, sh_ref, sc_ref, x2_ref, sh2_ref, sc2_ref, g_ref, h_ref, row_chunk)

    def act(h):
        for src, dst in zip(side_in, side_out):
            dst[...] = src[...].astype(dst.dtype)
        g = jnp.dot(h, wg_ref[...].astype(BF16), preferred_element_type=F32)
        u = jnp.dot(h, wu_ref[...].astype(BF16), preferred_element_type=F32)
        return (g * jax.nn.sigmoid(g) * u).astype(a_ref.dtype)

    @pl.when(pl.program_id(0) == 0)
    def _():
        a = act(h_ref[...])
        a_ref[...] = a[:tm]
        a2_ref[...] = a[tm:]

    @pl.when(pl.program_id(0) != 0)
    def _():
        a_ref[...] = act(h_ref[pl.ds(0, tm), :])


def _ffn_call(x, mods, x2, mods2, g_norm, w_in, w_out, layer, which, tm, tf, tn, round_also=()):
    m, r2 = x.shape[0], x2.shape[0]
    nb = mods[0].shape[0]
    tiles_per_seq = (m // nb) // tm
    nk = D_FF // tf
    n_steps = (m // tm) * nk
    mod_spec = pl.BlockSpec((None, 1, D_MODEL), lambda i, k: (i // tiles_per_seq, 0, 0))
    full2 = pl.BlockSpec((r2, D_MODEL), lambda i, k: (0, 0))

    side_in_specs, side_out_specs, side_out_shapes, side_args = [], [], [], []
    for w_side, lead in (((w_out, (layer, which)),) + tuple(round_also)):
        rows, cols = w_side.shape[-2:]
        rps = next(r for r in range(BF16_SUBLANES, rows + 1, BF16_SUBLANES)
                   if rows % r == 0 and r * n_steps >= rows)
        n_blk = rows // rps

        def blk(i, k, n_blk=n_blk):
            return jnp.minimum(i * nk + k, n_blk - 1)

        side_in_specs.append(pl.BlockSpec((None,) * len(lead) + (rps, cols),
                                          functools.partial(lambda i, k, lead, blk: (*lead, blk(i, k), 0),
                                                            lead=lead, blk=blk)))
        side_out_specs.append(pl.BlockSpec((rps, cols), functools.partial(lambda i, k, blk: (blk(i, k), 0), blk=blk)))
        side_out_shapes.append(jax.ShapeDtypeStruct((rows, cols), BF16))
        side_args.append(w_side)

    a, a2, w_out_bf, *rounded = pl.pallas_call(
        functools.partial(_ffn_up_kernel, row_chunk=min(tm, NORM_ROWS)),
        grid=(m // tm, nk),
        in_specs=[
            pl.BlockSpec((tm, D_MODEL), lambda i, k: (i, 0)),
            mod_spec, mod_spec,
            full2, full2, full2,
            pl.BlockSpec((1, D_MODEL), lambda i, k: (0, 0)),
            pl.BlockSpec((None, None, D_MODEL, tf), lambda i, k: (layer, which, 0, k)),
            pl.BlockSpec((None, None, D_MODEL, tf), lambda i, k: (layer, which, 0, k + nk)),
        ] + side_in_specs,
        out_specs=[pl.BlockSpec((tm, tf), lambda i, k: (i, k)),
                   pl.BlockSpec((r2, tf), lambda i, k: (0, jnp.where(i == 0, k, nk - 1)))] + side_out_specs,
        out_shape=[jax.ShapeDtypeStruct((m, D_FF), BF16), jax.ShapeDtypeStruct((r2, D_FF), BF16)] + side_out_shapes,
        scratch_shapes=[pltpu.VMEM((tm + r2, D_MODEL), BF16)],
        compiler_params=_params("arbitrary", "arbitrary"),
        name="ffn_up",
    )(x, mods[0], mods[1], x2, mods2[0], mods2[1], g_norm, w_in, w_in, *side_args)
    out = _proj_residual_call(a, x, mods[2], w_out_bf, (), False, tm, tn, gate_scale=0.5, name="ffn_down")
    out2 = _proj_residual_call(a2, x2, mods2[2].reshape(1, r2, D_MODEL), w_out_bf, (), False, r2, tn,
                               gate_scale=0.5, name="ffn_down")
    return out, out2, rounded


def _norm_matmul_kernel(x_ref, sh_ref, sc_ref, x2_ref, sh2_ref, sc2_ref, g_ref, w_ref, *rest, row_chunk, head_rows):
    h_ref = rest[-1]
    outs = rest[:-1]
    n_large = len(outs) - (2 if head_rows else 1)
    tm = x_ref.shape[0]
    _norm_two_row_sets(x_ref, sh_ref, sc_ref, x2_ref, sh2_ref, sc2_ref, g_ref, h_ref, row_chunk)

    def to_heads(res, ref):
        rows, cols = res.shape
        heads_per_tile = cols // HEAD_DIM
        first = (pl.program_id(1) % (head_rows // heads_per_tile)) * heads_per_tile
        for c in range(heads_per_tile):
            ref[pl.ds(first + c, rows, stride=head_rows), :] = res[:, c * HEAD_DIM:(c + 1) * HEAD_DIM]

    def emit_large(res):
        outs[0][...] = res.astype(outs[0].dtype)
        if head_rows:
            tiles_per_slab = head_rows * HEAD_DIM // res.shape[1]
            for s, ref in enumerate(outs[1:n_large]):
                @pl.when(pl.program_id(1) // tiles_per_slab == s)
                def _(ref=ref):
                    to_heads(res, ref)

    def emit_small(res):
        outs[n_large][...] = res.astype(outs[n_large].dtype)
        if head_rows:
            to_heads(res, outs[n_large + 1])

    @pl.when(pl.program_id(0) == 0)
    def _():
        res = jnp.dot(h_ref[...], w_ref[...].astype(BF16), preferred_element_type=F32)
        emit_large(res[:tm])
        emit_small(res[tm:])

    @pl.when(pl.program_id(0) != 0)
    def _():
        emit_large(jnp.dot(h_ref[pl.ds(0, tm), :], w_ref[...].astype(BF16), preferred_element_type=F32))


def _norm_matmul_call(x, shift, scale, x2, shift2, scale2, g_norm, w, w_lead, out_dtype, tm, tn, head_rows=0):
    m, r2 = x.shape[0], x2.shape[0]
    nb = shift.shape[0]
    n = w.shape[-1]
    nj = n // tn
    tiles_per_seq = (m // nb) // tm
    mod_spec = pl.BlockSpec((None, 1, D_MODEL), lambda i, j: (i // tiles_per_seq, 0, 0))
    full2 = pl.BlockSpec((r2, D_MODEL), lambda i, j: (0, 0))

    def col2(i, j):
        return jnp.where(i == 0, j, nj - 1)

    if head_rows:
        slab = head_rows * HEAD_DIM
        tps = slab // tn
        n_slabs = n // slab

        def heads_spec(s):
            return pl.BlockSpec((tm * head_rows, HEAD_DIM),
                                lambda i, j: (jnp.where(j // tps >= s, i, jnp.maximum(i - 1, 0)), 0))

        out_spec = ((pl.BlockSpec((None, tm, tn), lambda i, j: (j // tps, i, j % tps)),)
                    + tuple(heads_spec(s) for s in range(n_slabs))
                    + (pl.BlockSpec((None, r2, tn), lambda i, j: (col2(i, j) // tps, 0, col2(i, j) % tps)),
                       pl.BlockSpec((None, r2 * head_rows, HEAD_DIM), lambda i, j: (col2(i, j) // tps, 0, 0))))
        out_shape = ((jax.ShapeDtypeStruct((n_slabs, m, slab), out_dtype),)
                     + tuple(jax.ShapeDtypeStruct((m * head_rows, HEAD_DIM), out_dtype) for _ in range(n_slabs))
                     + (jax.ShapeDtypeStruct((n_slabs, r2, slab), out_dtype),
                        jax.ShapeDtypeStruct((n_slabs, r2 * head_rows, HEAD_DIM), out_dtype)))
    else:
        out_spec = (pl.BlockSpec((tm, tn), lambda i, j: (i, j)),
                    pl.BlockSpec((r2, tn), lambda i, j: (0, col2(i, j))))
        out_shape = (jax.ShapeDtypeStruct((m, n), out_dtype), jax.ShapeDtypeStruct((r2, n), out_dtype))
    return pl.pallas_call(
        functools.partial(_norm_matmul_kernel, row_chunk=min(tm, NORM_ROWS), head_rows=head_rows),
        grid=(m // tm, nj),
        in_specs=[
            pl.BlockSpec((tm, D_MODEL), lambda i, j: (i, 0)),
            mod_spec, mod_spec,
            full2, full2, full2,
            pl.BlockSpec((1, D_MODEL), lambda i, j: (0, 0)),
            pl.BlockSpec((None,) * len(w_lead) + (D_MODEL, tn), lambda i, j: (*w_lead, 0, j)),
        ],
        out_specs=out_spec,
        out_shape=out_shape,
        scratch_shapes=[pltpu.VMEM((tm + r2, D_MODEL), BF16)],
        compiler_params=_params("arbitrary", "arbitrary"),
        name="norm_matmul",
    )(x, shift, scale, x2, shift2, scale2, g_norm, w)


def _proj_residual_kernel(z_ref, x_ref, gt_ref, *refs, glu, gate_scale):
    z = z_ref[...]
    if glu:
        wa_ref, wg_ref, o_ref = refs
        a = jnp.dot(z, wa_ref[...].astype(BF16), preferred_element_type=F32)
        g = jnp.dot(z, wg_ref[...].astype(BF16), preferred_element_type=F32)
        y = a * jax.nn.sigmoid(g)
    else:
        w_ref, o_ref = refs
        y = jnp.dot(z, w_ref[...].astype(BF16), preferred_element_type=F32)
    gate = gt_ref[...] if gate_scale == 1.0 else gate_scale * gt_ref[...]
    o_ref[...] = x_ref[...] + gate * y


def _proj_residual_call(z, x, gate, w, w_lead, glu, tm, tn, gate_scale=1.0, name="proj_residual", z_buffers=2):
    m = x.shape[0]
    kdim = z.shape[1]
    nb, r, _ = gate.shape
    tiles_per_seq = (m // nb) // tm if r == 1 else 1
    n_blocks = D_MODEL // tn
    w_block = (None,) * len(w_lead) + (kdim, tn)
    w_specs = [pl.BlockSpec(w_block, lambda i, j: (*w_lead, 0, j))]
    w_args = [w]
    if glu:
        w_specs.append(pl.BlockSpec(w_block, lambda i, j: (*w_lead, 0, j + n_blocks)))
        w_args.append(w)
    return pl.pallas_call(
        functools.partial(_proj_residual_kernel, glu=glu, gate_scale=gate_scale),
        grid=(m // tm, n_blocks),
        in_specs=[
            (pl.BlockSpec((tm, kdim), lambda i, j: (i, 0)) if z_buffers == 2 else
             pl.BlockSpec((tm, kdim), lambda i, j: (i, 0), pipeline_mode=pl.Buffered(z_buffers))),
            pl.BlockSpec((tm, tn), lambda i, j: (i, j)),
            pl.BlockSpec((None, r, tn), lambda i, j: (i // tiles_per_seq, 0, j)),
        ] + w_specs,
        out_specs=pl.BlockSpec((tm, tn), lambda i, j: (i, j)),
        out_shape=jax.ShapeDtypeStruct((m, D_MODEL), F32),
        compiler_params=_params("parallel", "arbitrary"),
        name=name,
    )(z, x, gate, *w_args)


def _final_norm_kernel(x_ref, sh_ref, sc_ref, g_ref, o_ref, *, row_chunk):
    _norm_rows(x_ref, g_ref, sh_ref, sc_ref, o_ref, row_chunk)


def _final_norm_call(x, shift, scale, g_norm, tm):
    m = x.shape[0]
    nb, r, _ = shift.shape
    tiles_per_seq = (m // nb) // tm if r == 1 else 1
    mod_spec = pl.BlockSpec((None, r, D_MODEL), lambda i: (i // tiles_per_seq, 0, 0))
    return pl.pallas_call(
        functools.partial(_final_norm_kernel, row_chunk=min(tm, NORM_ROWS)),
        grid=(m // tm,),
        in_specs=[
            pl.BlockSpec((tm, D_MODEL), lambda i: (i, 0)),
            mod_spec, mod_spec,
            pl.BlockSpec((1, D_MODEL), lambda i: (0, 0)),
        ],
        out_specs=pl.BlockSpec((tm, D_MODEL), lambda i: (i, 0)),
        out_shape=jax.ShapeDtypeStruct((m, D_MODEL), F32),
        compiler_params=_params("parallel"),
        name="final_norm",
    )(x, shift, scale, g_norm)


def _s5_discretise(lam_re, lam_im, log_dt):
    dt = jnp.exp(log_dt)
    decay = jnp.exp(lam_re * dt)
    ab_re = decay * jnp.cos(lam_im * dt)
    ab_im = decay * jnp.sin(lam_im * dt)
    den = lam_re * lam_re + lam_im * lam_im
    f_re = ((ab_re - 1.0) * lam_re + ab_im * lam_im) / den
    f_im = (ab_im * lam_re - (ab_re - 1.0) * lam_im) / den
    return ab_re, ab_im, f_re, f_im


def _s5_prep_kernel(lre_ref, lim_ref, ldt_ref, bre_ref, bim_ref, ctre_ref, ctim_ref, lre_g, lim_g, ldt_g,
                    are_ref, aim_ref, inre_ref, inim_ref, outre_ref, outim_ref):
    c, p = S5_GROUP, S5_STATE
    ab_re, ab_im, _, _ = _s5_discretise(lre_g[...], lim_g[...], ldt_g[...])
    are_ref[...] = ab_re
    aim_ref[...] = ab_im
    _, _, f_re, f_im = _s5_discretise(lre_ref[...], lim_ref[...], ldt_ref[...])
    b_re, b_im = bre_ref[...], bim_ref[...]
    bb_re = f_re * b_re - f_im * b_im
    bb_im = f_re * b_im + f_im * b_re
    for ref in (inre_ref, inim_ref, outre_ref, outim_ref):
        ref[...] = jnp.zeros(ref.shape, ref.dtype)
    for gl in range(S5_CH_BLOCK // c):
        rows_in, cols_in = slice(gl * c, (gl + 1) * c), slice(gl * p, (gl + 1) * p)
        inre_ref[rows_in, cols_in] = bb_re[rows_in].astype(inre_ref.dtype)
        inim_ref[rows_in, cols_in] = bb_im[rows_in].astype(inim_ref.dtype)
        outre_ref[cols_in, rows_in] = ctre_ref[cols_in, :].astype(outre_ref.dtype)
        outim_ref[cols_in, rows_in] = ctim_ref[cols_in, :].astype(outim_ref.dtype)


def _s5_prep_call(lam_re, lam_im, log_dt, b_re, b_im, c_re, c_im):
    g, p, c = S5_GROUPS, S5_STATE, S5_GROUP
    rep = lambda a: jnp.repeat(a, c, axis=0)
    b_t = lambda b: b.transpose(0, 2, 1).reshape(g * c, p)
    c_t = lambda m: m.transpose(0, 2, 1).reshape(g * p, c)
    gpb = S5_CH_BLOCK // c
    row_blk = lambda rows, cols: pl.BlockSpec((rows, cols), lambda i: (i, 0))
    blk3 = lambda rows, cols: pl.BlockSpec((None, rows, cols), lambda i: (i, 0, 0))
    return pl.pallas_call(
        _s5_prep_kernel,
        grid=(S5_N_BLOCKS,),
        in_specs=[row_blk(S5_CH_BLOCK, p), row_blk(S5_CH_BLOCK, p), row_blk(S5_CH_BLOCK, 1),
                  row_blk(S5_CH_BLOCK, p), row_blk(S5_CH_BLOCK, p),
                  row_blk(S5_COL_BLOCK, c), row_blk(S5_COL_BLOCK, c),
                  row_blk(gpb, p), row_blk(gpb, p), row_blk(gpb, 1)],
        out_specs=(row_blk(gpb, p), row_blk(gpb, p),
                   blk3(S5_CH_BLOCK, S5_COL_BLOCK), blk3(S5_CH_BLOCK, S5_COL_BLOCK),
                   blk3(S5_COL_BLOCK, S5_CH_BLOCK), blk3(S5_COL_BLOCK, S5_CH_BLOCK)),
        out_shape=(jax.ShapeDtypeStruct((g, p), F32), jax.ShapeDtypeStruct((g, p), F32),
                   jax.ShapeDtypeStruct((S5_N_BLOCKS, S5_CH_BLOCK, S5_COL_BLOCK), BF16),
                   jax.ShapeDtypeStruct((S5_N_BLOCKS, S5_CH_BLOCK, S5_COL_BLOCK), BF16),
                   jax.ShapeDtypeStruct((S5_N_BLOCKS, S5_COL_BLOCK, S5_CH_BLOCK), BF16),
                   jax.ShapeDtypeStruct((S5_N_BLOCKS, S5_COL_BLOCK, S5_CH_BLOCK), BF16)),
        compiler_params=_params("parallel"),
        name="s5_prepare",
    )(rep(lam_re), rep(lam_im), rep(log_dt.reshape(g, 1)), b_t(b_re), b_t(b_im), c_t(c_re), c_t(c_im),
      lam_re, lam_im, log_dt.reshape(g, 1))


def _s5_kernel(x_ref, sh_ref, sc_ref, g_ref, bbre_ref, bbim_ref, are_ref, aim_ref, cre_ref, cim_ref,
               d_ref, h0re_ref, h0im_ref, z_ref, hre_out, him_out,
               u_scr, sre_scr, sim_scr, y_scr, pwre_scr, pwim_scr, *, row_chunk):
    t = pl.program_id(0)
    nb, steps, _ = x_ref.shape
    rows = steps * nb
    lane = HEAD_DIM
    seg = SUBLANES // nb
    seg_len = steps // seg
    assert seg in (1, 2)

    @pl.when(t == 0)
    def _():
        hre_out[...] = h0re_ref[...]
        him_out[...] = h0im_ref[...]
        if seg > 1:
            zero = jnp.zeros((nb, S5_COL_BLOCK), F32)
            for cb in range(S5_N_BLOCKS):
                a_re = jnp.broadcast_to(are_ref[cb], (nb, S5_COL_BLOCK))
                a_im = jnp.broadcast_to(aim_ref[cb], (nb, S5_COL_BLOCK))
                p_re, p_im = a_re, a_im
                for i in range(seg_len):
                    pwre_scr[cb, i * SUBLANES:(i + 1) * SUBLANES, :] = jnp.concatenate([zero, p_re], axis=0)
                    pwim_scr[cb, i * SUBLANES:(i + 1) * SUBLANES, :] = jnp.concatenate([zero, p_im], axis=0)
                    p_re, p_im = p_re * a_re - p_im * a_im, p_re * a_im + p_im * a_re

    g = g_ref[...]
    for b in range(nb):
        for s in range(seg):
            def norm_chunk(c, carry, b=b, s=s):
                r0 = pl.multiple_of(c * row_chunk, row_chunk)
                u = _rms_mod(x_ref[b, pl.ds(s * seg_len + r0, row_chunk), :], g, sh_ref[b], sc_ref[b])
                for j in range(D_MODEL // lane):
                    u_scr[j, pl.ds(r0 * SUBLANES + s * nb + b, row_chunk, stride=SUBLANES), :] = \
                        u[:, j * lane:(j + 1) * lane]
                return carry

            lax.fori_loop(0, seg_len // row_chunk, norm_chunk, 0)

    tiles_per_block = S5_CH_BLOCK // lane
    per_block_scratch = sre_scr.shape[0] == S5_N_BLOCKS
    for cb in range(S5_N_BLOCKS):
        ch = slice(cb * S5_CH_BLOCK, (cb + 1) * S5_CH_BLOCK)
        st = slice(cb * S5_COL_BLOCK, (cb + 1) * S5_COL_BLOCK)
        slab = cb if per_block_scratch else 0
        sre, sim, ys = sre_scr.at[slab], sim_scr.at[slab], y_scr.at[slab]
        u_blk = jnp.concatenate([u_scr[cb * tiles_per_block + j] for j in range(tiles_per_block)], axis=1)
        u_bf = u_blk.astype(BF16)
        sre[...] = jnp.dot(u_bf, bbre_ref[cb], preferred_element_type=F32)
        sim[...] = jnp.dot(u_bf, bbim_ref[cb], preferred_element_type=F32)

        a_re = jnp.broadcast_to(are_ref[cb], (SUBLANES, S5_COL_BLOCK))
        a_im = jnp.broadcast_to(aim_ref[cb], (SUBLANES, S5_COL_BLOCK))

        def tile_step(i, carry, a_re=a_re, a_im=a_im, sre=sre, sim=sim):
            h_re, h_im = carry
            r0 = i * SUBLANES if isinstance(i, int) else pl.multiple_of(i * SUBLANES, SUBLANES)
            n_re = a_re * h_re - a_im * h_im + sre[pl.ds(r0, SUBLANES), :]
            n_im = a_re * h_im + a_im * h_re + sim[pl.ds(r0, SUBLANES), :]
            sre[pl.ds(r0, SUBLANES), :] = n_re
            sim[pl.ds(r0, SUBLANES), :] = n_im
            return n_re, n_im

        def first_rows(v):
            return jnp.concatenate([v[:nb]] * seg, axis=0)

        h_re, h_im = hre_out[:, st], him_out[:, st]
        if seg > 1:
            zero = jnp.zeros((SUBLANES - nb, S5_COL_BLOCK), F32)
            h_re, h_im = jnp.concatenate([h_re, zero], axis=0), jnp.concatenate([h_im, zero], axis=0)
        carry = (h_re, h_im)
        if per_block_scratch:
            for i in range(seg_len):
                carry = tile_step(i, carry)
        else:
            carry = lax.fori_loop(0, seg_len, tile_step, carry, unroll=min(seg_len, 4))
        h_re, h_im = carry
        if seg > 1:
            c_re, c_im = first_rows(h_re), first_rows(h_im)
            for i in range(seg_len):
                rows_i = slice(i * SUBLANES, (i + 1) * SUBLANES)
                p_re, p_im = pwre_scr[cb, rows_i, :], pwim_scr[cb, rows_i, :]
                f_re = sre[rows_i, :] + (p_re * c_re - p_im * c_im)
                f_im = sim[rows_i, :] + (p_re * c_im + p_im * c_re)
                sre[rows_i, :] = f_re
                sim[rows_i, :] = f_im
            h_re, h_im = f_re[SUBLANES - nb:], f_im[SUBLANES - nb:]
        hre_out[:, st] = h_re
        him_out[:, st] = h_im

        y = (jnp.dot(sre[...].astype(BF16), cre_ref[cb], preferred_element_type=F32)
             - jnp.dot(sim[...].astype(BF16), cim_ref[cb], preferred_element_type=F32))
        zf = jax.nn.gelu(y + d_ref[:, ch] * u_blk)
        for j in range(tiles_per_block):
            ys[j] = zf[:, j * lane:(j + 1) * lane]
        for b in range(nb):
            for s in range(seg):
                for j in range(tiles_per_block):
                    c0 = cb * S5_CH_BLOCK + j * lane
                    z_ref[b, s * seg_len:(s + 1) * seg_len, c0:c0 + lane] = \
                        ys[j, pl.ds(s * nb + b, seg_len, stride=SUBLANES), :].astype(z_ref.dtype)


def _s5_call(x, shift, scale, g_norm, bb_re, bb_im, a_re, a_im, c_re, c_im, d_skip,
             h0_re, h0_im, steps_per_chunk, z_dtype, per_block_scratch):
    nb, seq, _ = x.shape
    assert SUBLANES % nb == 0 and seq % steps_per_chunk == 0
    tr = steps_per_chunk * nb
    slabs = S5_N_BLOCKS if per_block_scratch else 1
    seg = SUBLANES // nb
    pw_rows = tr if seg > 1 else SUBLANES
    row_chunk = min(steps_per_chunk // (SUBLANES // nb), NORM_ROWS)
    const2 = lambda t: (0, 0)
    const3 = lambda t: (0, 0, 0)
    state_spec = pl.BlockSpec((nb, S5_NSTATE), const2)
    io_spec = pl.BlockSpec((nb, steps_per_chunk, D_MODEL), lambda t: (0, t, 0))
    return pl.pallas_call(
        functools.partial(_s5_kernel, row_chunk=row_chunk),
        grid=(seq // steps_per_chunk,),
        in_specs=[
            io_spec,
            pl.BlockSpec((nb, 1, D_MODEL), const3),
            pl.BlockSpec((nb, 1, D_MODEL), const3),
            pl.BlockSpec((1, D_MODEL), const2),
            pl.BlockSpec((S5_N_BLOCKS, S5_CH_BLOCK, S5_COL_BLOCK), const3, pipeline_mode=pl.Buffered(1)),
            pl.BlockSpec((S5_N_BLOCKS, S5_CH_BLOCK, S5_COL_BLOCK), const3, pipeline_mode=pl.Buffered(1)),
            pl.BlockSpec((S5_N_BLOCKS, 1, S5_COL_BLOCK), const3),
            pl.BlockSpec((S5_N_BLOCKS, 1, S5_COL_BLOCK), const3),
            pl.BlockSpec((S5_N_BLOCKS, S5_COL_BLOCK, S5_CH_BLOCK), const3, pipeline_mode=pl.Buffered(1)),
            pl.BlockSpec((S5_N_BLOCKS, S5_COL_BLOCK, S5_CH_BLOCK), const3, pipeline_mode=pl.Buffered(1)),
            pl.BlockSpec((1, D_MODEL), const2),
            state_spec, state_spec,
        ],
        out_specs=(io_spec, state_spec, state_spec),
        out_shape=(jax.ShapeDtypeStruct((nb, seq, D_MODEL), z_dtype),
                   jax.ShapeDtypeStruct((nb, S5_NSTATE), F32),
                   jax.ShapeDtypeStruct((nb, S5_NSTATE), F32)),
        scratch_shapes=[pltpu.VMEM((D_MODEL // HEAD_DIM, tr, HEAD_DIM), F32),
                        pltpu.VMEM((slabs, tr, S5_COL_BLOCK), F32),
                        pltpu.VMEM((slabs, tr, S5_COL_BLOCK), F32),
                        pltpu.VMEM((slabs, S5_CH_BLOCK // HEAD_DIM, tr, HEAD_DIM), F32),
                        pltpu.VMEM((S5_N_BLOCKS, pw_rows, S5_COL_BLOCK), F32),
                        pltpu.VMEM((S5_N_BLOCKS, pw_rows, S5_COL_BLOCK), F32)],
        compiler_params=_params("arbitrary"),
        name="s5_mixer",
    )(x, shift, scale, g_norm, bb_re, bb_im, a_re, a_im, c_re, c_im, d_skip, h0_re, h0_im)


def _softmax_pv(s2, v_bf):
    m = jnp.max(s2, axis=-1, keepdims=True)
    p = jnp.exp2(s2 - m)
    l = jnp.sum(p, axis=-1, keepdims=True)
    o = jnp.dot(p.astype(BF16), v_bf, preferred_element_type=F32) / l
    return o, jnp.broadcast_to(m + jnp.log2(l), o.shape)


def _qk(q_bf, k_bf, scale=ATTN_SCALE):
    return lax.dot_general(q_bf, k_bf, (((1,), (1,)), ((), ())), preferred_element_type=F32) * scale


def _attn_prompt_kernel(q0_ref, q1_ref, q2_ref, k0_ref, v0_ref, k1_ref, v1_ref, k2_ref, v2_ref, o_ref,
                        qf_scr, o1_scr, l1_scr, o2_scr, l2_scr, band_scr, first_scr):
    seq = q0_ref.shape[0]
    qb = Q_BLOCK
    row = lax.broadcasted_iota(jnp.int32, (REP * qb, 2 * qb), 0) & (qb - 1)
    col = lax.broadcasted_iota(jnp.int32, (REP * qb, 2 * qb), 1)
    band_scr[...] = jnp.where((col >= row) & (col <= row + qb), 0.0, -jnp.inf)
    row1 = lax.broadcasted_iota(jnp.int32, (REP * qb, qb), 0) & (qb - 1)
    col1 = lax.broadcasted_iota(jnp.int32, (REP * qb, qb), 1)
    first_scr[...] = jnp.where(col1 <= row1, 0.0, -jnp.inf)

    def rows_of(start, d):
        return pl.ds(start, qb) if d == 1 else pl.ds(start, qb, stride=d)

    def load_q(src_ref, start, d):
        if d == 1:
            parts = [src_ref[pl.ds(start, qb), e * HEAD_DIM:(e + 1) * HEAD_DIM] for e in range(REP)]
        else:
            parts = [src_ref[e, rows_of(start, d), :] for e in range(REP)]
        return jnp.concatenate(parts, axis=0).astype(BF16)

    def unit(src_q, k_ref, v_ref, start, d, first):
        q4 = load_q(src_q, start, d)
        k_cur = k_ref[rows_of(start, d), :]
        v_cur = v_ref[rows_of(start, d), :]
        if first:
            s = _qk(q4, k_cur.astype(BF16), ATTN_SCALE_LOG2) + first_scr[...]
            return _softmax_pv(s, v_cur.astype(BF16))
        prev = start - qb * d
        k_band = jnp.concatenate([k_ref[rows_of(prev, d), :], k_cur], axis=0)
        v_band = jnp.concatenate([v_ref[rows_of(prev, d), :], v_cur], axis=0)
        s = _qk(q4, k_band.astype(BF16), ATTN_SCALE_LOG2) + band_scr[...]
        return _softmax_pv(s, v_band.astype(BF16))

    def stage_group(q_ref, k_ref, v_ref, d, o_scr, l_scr):
        n_blocks = seq // (qb * d)

        def copy(c, carry):
            r0 = pl.multiple_of(c * STAGE_ROWS, STAGE_ROWS)
            for e in range(REP):
                qf_scr[e, pl.ds(r0, STAGE_ROWS), :] = \
                    q_ref[pl.ds(r0, STAGE_ROWS), e * HEAD_DIM:(e + 1) * HEAD_DIM].astype(F32)
            return carry

        lax.fori_loop(0, seq // STAGE_ROWS, copy, 0)

        def store(start, o, lse):
            for e in range(REP):
                sl = slice(e * qb, (e + 1) * qb)
                o_scr[e, rows_of(start, d), :] = o[sl]
                l_scr[e, rows_of(start, d), :] = lse[sl]

        def per_class(r, carry):
            store(r, *unit(qf_scr, k_ref, v_ref, r, d, True))

            def per_block(jb, c2):
                start = jb * (qb * d) + r
                store(start, *unit(qf_scr, k_ref, v_ref, start, d, False))
                return c2

            if n_blocks > 1:
                lax.fori_loop(1, n_blocks, per_block, 0, unroll=True)
            return carry

        lax.fori_loop(0, d, per_class, 0, unroll=4 if n_blocks == 1 else 2)

    stage_group(q2_ref, k2_ref, v2_ref, DILATED_PATTERNS[2][1], o2_scr, l2_scr)
    stage_group(q1_ref, k1_ref, v1_ref, DILATED_PATTERNS[1][1], o1_scr, l1_scr)

    def merge(start, o0, lse0):
        for e in range(REP):
            sl = slice(e * qb, (e + 1) * qb)
            cols = slice(e * HEAD_DIM, (e + 1) * HEAD_DIM)
            la = lse0[sl]
            lb = l1_scr[e, pl.ds(start, qb), :]
            lc = l2_scr[e, pl.ds(start, qb), :]
            mx = jnp.maximum(jnp.maximum(la, lb), lc)
            wa = jnp.exp2(la - mx)
            wb = jnp.exp2(lb - mx)
            wc = jnp.exp2(lc - mx)
            inv = 1.0 / (wa + wb + wc)
            acc = (wa * o0[sl] + wb * o1_scr[e, pl.ds(start, qb), :] + wc * o2_scr[e, pl.ds(start, qb), :]) * inv
            o_ref[pl.ds(start, qb), cols] = acc.astype(o_ref.dtype)

    merge(0, *unit(q0_ref, k0_ref, v0_ref, 0, 1, True))

    def per_block0(jb, carry):
        start = pl.multiple_of(jb * qb, qb)
        merge(start, *unit(q0_ref, k0_ref, v0_ref, start, 1, False))
        return carry

    n_blocks0 = seq // qb
    lax.fori_loop(1, n_blocks0, per_block0, 0, unroll=5 if (n_blocks0 - 1) % 5 == 0 else 1)


def _attn_prompt_call(q, kv):
    b, seq, _ = q.shape
    qw = REP * HEAD_DIM
    q_specs = [pl.BlockSpec((None, seq, qw), functools.partial(lambda bi, h, g: (bi, 0, g * KV_HEADS + h), g=g))
               for g in range(N_DIL)]
    kv_specs = []
    for g in range(N_DIL):
        kv_specs.append(pl.BlockSpec((None, None, seq, HEAD_DIM),
                                     functools.partial(lambda bi, h, g: (g, bi, 0, h), g=g)))
        kv_specs.append(pl.BlockSpec((None, None, seq, HEAD_DIM),
                                     functools.partial(lambda bi, h, g: (g, bi, 0, KV_HEADS + h), g=g)))
    kv_args = [kv] * (2 * N_DIL)
    return pl.pallas_call(
        _attn_prompt_kernel,
        grid=(b, KV_HEADS),
        in_specs=q_specs + kv_specs,
        out_specs=pl.BlockSpec((None, seq, qw), lambda bi, h: (bi, 0, h)),
        out_shape=jax.ShapeDtypeStruct((b, seq, HEADS * HEAD_DIM), BF16),
        scratch_shapes=[pltpu.VMEM((REP, seq, HEAD_DIM), F32),
                        pltpu.VMEM((REP, seq, HEAD_DIM), F32), pltpu.VMEM((REP, seq, HEAD_DIM), F32),
                        pltpu.VMEM((REP, seq, HEAD_DIM), F32), pltpu.VMEM((REP, seq, HEAD_DIM), F32),
                        pltpu.VMEM((REP * Q_BLOCK, 2 * Q_BLOCK), F32), pltpu.VMEM((REP * Q_BLOCK, Q_BLOCK), F32)],
        compiler_params=_params("parallel", "arbitrary"),
        name="dilated_attention_prompt",
    )(q, q, q, *kv_args)


def _attn_sample_kernel(q_ref, kvn_ref, c0_ref, c1_ref, c2_ref, o_ref, *, n_tok):
    rows = REP * n_tok
    cache_refs = (c0_ref, c1_ref, c2_ref)
    kv_rows = 2 * KV_HEADS
    for h in range(KV_HEADS):
        outs, lses = [], []
        for g, (_, d) in enumerate(DILATED_PATTERNS):
            q = q_ref[g * KV_HEADS + h]
            lw = cache_refs[g].shape[0] // kv_rows
            k_c = cache_refs[g][pl.ds(h, lw, stride=kv_rows), :].astype(BF16)
            v_c = cache_refs[g][pl.ds(KV_HEADS + h, lw, stride=kv_rows), :].astype(BF16)
            k_n = kvn_ref[g, pl.ds(h, n_tok, stride=kv_rows), :].astype(BF16)
            v_n = kvn_ref[g, pl.ds(KV_HEADS + h, n_tok, stride=kv_rows), :].astype(BF16)
            tq_c = lax.broadcasted_iota(jnp.int32, (rows, lw), 0) & (n_tok - 1)
            idx_c = lax.broadcasted_iota(jnp.int32, (rows, lw), 1)
            ok_c = (idx_c >= tq_c) & (((idx_c - tq_c) & (d - 1)) == 0)
            tq_n = lax.broadcasted_iota(jnp.int32, (rows, n_tok), 0) & (n_tok - 1)
            idx_n = lax.broadcasted_iota(jnp.int32, (rows, n_tok), 1)
            ok_n = (idx_n <= tq_n) & (((tq_n - idx_n) & (d - 1)) == 0)
            s_c = jnp.where(ok_c, _qk(q, k_c), -jnp.inf)
            s_n = jnp.where(ok_n, _qk(q, k_n), -jnp.inf)
            m = jnp.maximum(jnp.max(s_c, axis=-1, keepdims=True), jnp.max(s_n, axis=-1, keepdims=True))
            p_c = jnp.exp(s_c - m)
            p_n = jnp.exp(s_n - m)
            l = jnp.sum(p_c, axis=-1, keepdims=True) + jnp.sum(p_n, axis=-1, keepdims=True)
            o = (jnp.dot(p_c.astype(BF16), v_c, preferred_element_type=F32)
                 + jnp.dot(p_n.astype(BF16), v_n, preferred_element_type=F32)) / l
            outs.append(o)
            lses.append(m + jnp.log(l))
        mx = jnp.maximum(jnp.maximum(lses[0], lses[1]), lses[2])
        w = [jnp.exp(x - mx) for x in lses]
        inv = 1.0 / (w[0] + w[1] + w[2])
        acc = (w[0] * inv) * outs[0] + (w[1] * inv) * outs[1] + (w[2] * inv) * outs[2]
        o_ref[h] = acc.astype(o_ref.dtype)


def _attn_sample_call(q_heads, kv_new, caches, n_tok):
    b = q_heads.shape[0]
    rows = REP * n_tok
    assert n_tok & (n_tok - 1) == 0
    in_specs = [pl.BlockSpec((None, N_DIL * KV_HEADS, rows, HEAD_DIM), lambda bi: (bi, 0, 0, 0)),
                pl.BlockSpec((N_DIL, None, kv_new.shape[2], HEAD_DIM), lambda bi: (0, bi, 0, 0))]
    in_specs += [pl.BlockSpec((None, c.shape[1], HEAD_DIM), lambda bi: (bi, 0, 0)) for c in caches]
    return pl.pallas_call(
        functools.partial(_attn_sample_kernel, n_tok=n_tok),
        grid=(b,),
        in_specs=in_specs,
        out_specs=pl.BlockSpec((None, KV_HEADS, rows, HEAD_DIM), lambda bi: (bi, 0, 0, 0)),
        out_shape=jax.ShapeDtypeStruct((b, KV_HEADS, rows, HEAD_DIM), BF16),
        compiler_params=_params("parallel"),
        name="dilated_attention_sample",
    )(q_heads, kv_new, *caches)


def _s5_weights(lam_re, lam_im, log_dt, b_re, b_im, c_re, c_im):
    a_re, a_im, in_re, in_im, out_re, out_im = _s5_prep_call(lam_re, lam_im, log_dt, b_re, b_im, c_re, c_im)
    lam_bar = lambda a: a.reshape(S5_N_BLOCKS, 1, S5_COL_BLOCK)
    return in_re, in_im, lam_bar(a_re), lam_bar(a_im), out_re, out_im


def _trunks(x_p, x_s, mod_p, mod_s, s5_state, kv_caches, wts, tiles):
    (norm_g, ffn_w_in, ffn_w_out, s5_mats, s5_d, s5_w_glu, kv_norm_g, w_kv,
     attn_w_q, attn_w_o, final_norm_g) = wts
    bp, lp, _ = x_p.shape
    bs, ls, _ = x_s.shape
    mp, ms = bp * lp, bs * ls
    tm = tiles["tm"]
    kv_rows = 2 * KV_HEADS

    def mods_p(*ks):
        return tuple(mod_p[:, k].reshape(bp, 1, D_MODEL) for k in ks)

    def mods_s(*ks):
        return tuple(jnp.repeat(mod_s[:, k], ls, axis=0) for k in ks)

    def as_block(a):
        return a.reshape(1, *a.shape)

    def norm_g_row(a):
        return a.reshape(1, D_MODEL)

    xp = x_p.reshape(mp, D_MODEL)
    xs = x_s.reshape(ms, D_MODEL)
    re_p, im_p, re_s, im_s = [], [], [], []
    for layer in range(DEPTH):
        base = 9 * layer
        if layer == N_A_LAYERS:
            k0 = 9 * DEPTH
            kvf_p, *kv_p, _, kv_s = _norm_matmul_call(xp, *mods_p(k0, k0 + 1), xs, *mods_s(k0, k0 + 1),
                                                     norm_g_row(kv_norm_g), w_kv_bf, (), F32, tm,
                                                     KV_HEADS * HEAD_DIM, head_rows=kv_rows)
        later = (((s5_w_glu, (layer,)),) if layer < N_A_LAYERS else
                 ((attn_w_q, (layer - N_A_LAYERS,)), (attn_w_o, (layer - N_A_LAYERS,))))
        xp, xs, rounded = _ffn_call(xp, mods_p(base, base + 1, base + 2), xs, mods_s(base, base + 1, base + 2),
                                    norm_g_row(norm_g[layer, 0]), ffn_w_in, ffn_w_out, layer, 0, tm, tiles["tf"],
                                    tiles["tn_down"], round_also=later)
        g_mix = norm_g_row(norm_g[layer, 1])
        if layer < N_A_LAYERS:
            d_skip = s5_d[layer].reshape(1, D_MODEL)
            zero_state = jnp.zeros((bp, S5_NSTATE), F32)
            zp, h_re, h_im = _s5_call(xp.reshape(bp, lp, D_MODEL), *mods_p(base + 3, base + 4), g_mix,
                                      *s5_mats[layer], d_skip, zero_state, zero_state,
                                      min(lp, tiles["s5_steps"]), BF16, True)
            re_p.append(h_re.reshape(bp, S5_GROUPS, S5_STATE))
            im_p.append(h_im.reshape(bp, S5_GROUPS, S5_STATE))
            zs, h_re, h_im = _s5_call(xs.reshape(bs, ls, D_MODEL), mod_s[:, base + 3].reshape(bs, 1, D_MODEL),
                                      mod_s[:, base + 4].reshape(bs, 1, D_MODEL), g_mix, *s5_mats[layer], d_skip,
                                      s5_state[0][layer].reshape(bs, S5_NSTATE),
                                      s5_state[1][layer].reshape(bs, S5_NSTATE), ls, F32, False)
            re_s.append(h_re.reshape(bs, S5_GROUPS, S5_STATE))
            im_s.append(h_im.reshape(bs, S5_GROUPS, S5_STATE))
            (w_glu_bf,) = rounded
            xp = _proj_residual_call(zp.reshape(mp, D_MODEL), xp, *mods_p(base + 5), w_glu_bf, (),
                                     True, tm, tiles["tn_mix"], name="s5_glu")
            xs = _proj_residual_call(zs.reshape(ms, D_MODEL).astype(BF16), xs, as_block(*mods_s(base + 5)),
                                     w_glu_bf, (), True, ms, tiles["tn_mix"], name="s5_glu")
        else:
            w_q_bf, w_o_bf = rounded
            q_p, q_s = _norm_matmul_call(xp, *mods_p(base + 3, base + 4), xs, *mods_s(base + 3, base + 4), g_mix,
                                         w_q_bf, (), BF16, tm, tiles["tn_q"])
            o_p = _attn_prompt_call(q_p.reshape(bp, lp, N_DIL * HEADS * HEAD_DIM),
                                    kvf_p.reshape(N_DIL, bp, lp, kv_rows * HEAD_DIM)).reshape(mp, D_MODEL)
            qh = q_s.reshape(bs, ls, N_DIL, KV_HEADS, REP, HEAD_DIM).transpose(0, 2, 3, 4, 1, 5)
            qh = qh.reshape(bs, N_DIL * KV_HEADS, REP * ls, HEAD_DIM)
            o_s = _attn_sample_call(qh, kv_s.reshape(N_DIL, bs, ls * kv_rows, HEAD_DIM), kv_caches, ls)
            o_s = o_s.reshape(bs, KV_HEADS, REP, ls, HEAD_DIM).transpose(0, 3, 1, 2, 4).reshape(ms, D_MODEL)
            xp = _proj_residual_call(o_p, xp, *mods_p(base + 5), w_o_bf, (), False, tiles["tm_out"],
                                     tiles["tn_mix"], name="attn_out")
            xs = _proj_residual_call(o_s, xs, as_block(*mods_s(base + 5)), w_o_bf, (), False, ms,
                                     tiles["tn_mix"], name="attn_out")
        later = ((w_kv, ()),) if layer == N_A_LAYERS - 1 else ()
        xp, xs, rounded = _ffn_call(xp, mods_p(base + 6, base + 7, base + 8), xs, mods_s(base + 6, base + 7, base + 8),
                                    norm_g_row(norm_g[layer, 2]), ffn_w_in, ffn_w_out, layer, 1, tm, tiles["tf"],
                                    tiles["tn_down"], round_also=later)
        if later:
            (w_kv_bf,) = rounded
    k1 = 9 * DEPTH + 2
    y_p = _final_norm_call(xp, *mods_p(k1, k1 + 1), norm_g_row(final_norm_g), tm)
    y_s = _final_norm_call(xs, *(as_block(a) for a in mods_s(k1, k1 + 1)), norm_g_row(final_norm_g), ms)
    kv_p = [a.reshape(bp, lp, 2, KV_HEADS, HEAD_DIM) for a in kv_p]
    kv_s = kv_s.reshape(N_DIL, bs, ls, 2, KV_HEADS, HEAD_DIM)
    return ((y_p.reshape(bp, lp, D_MODEL), jnp.stack(re_p, axis=0), jnp.stack(im_p, axis=0), kv_p),
            (y_s.reshape(bs, ls, D_MODEL), jnp.stack(re_s, axis=0), jnp.stack(im_s, axis=0), kv_s))


def kernel(x_prompt, x_sample, c_prompt, c_sample, state_s5_re, state_s5_im, cache_kv_g0, cache_kv_g1, cache_kv_g2, w_mod, b_mod, norm_g, ffn_w_in, ffn_w_out, s5_lambda_re, s5_lambda_im, s5_log_dt, s5_b_re, s5_b_im, s5_c_re, s5_c_im, s5_d, s5_w_glu, kv_norm_g, w_kv, attn_w_q, attn_w_o, final_norm_g):
    bp, seq_p, _ = x_prompt.shape
    bs, seq_s, _ = x_sample.shape
    caches = (cache_kv_g0, cache_kv_g1, cache_kv_g2)
    for (w, d), c in zip(DILATED_PATTERNS, caches):
        assert c.shape[1] == w and w == d * Q_BLOCK and seq_p % (d * Q_BLOCK) == 0
    assert bp <= SUBLANES and bs <= SUBLANES

    n_c = bp + bs
    c_all = jnp.pad(jnp.concatenate([c_prompt, c_sample], axis=0), ((0, (-n_c) % SUBLANES), (0, 0)))
    mod_all = _mod_call(c_all, w_mod, b_mod)
    mod_p = mod_all[:bp].reshape(bp, N_MOD, D_MODEL)
    mod_s = mod_all[bp:n_c].reshape(bs, N_MOD, D_MODEL)

    s5_mats = [_s5_weights(s5_lambda_re[l], s5_lambda_im[l], s5_log_dt[l], s5_b_re[l], s5_b_im[l],
                           s5_c_re[l], s5_c_im[l]) for l in range(N_A_LAYERS)]
    wts = (norm_g, ffn_w_in, ffn_w_out, s5_mats, s5_d, s5_w_glu, kv_norm_g, w_kv, attn_w_q, attn_w_o, final_norm_g)

    caches_flat = tuple(c.reshape(bs, c.shape[1] * 2 * KV_HEADS, HEAD_DIM) for c in caches)
    (y_p, re_p, im_p, kv_p), (y_s, re_s, im_s, kv_s) = _trunks(
        x_prompt, x_sample, mod_p, mod_s, (state_s5_re, state_s5_im), caches_flat, wts,
        _TILES)

    kvp = [kv_p[g][:, seq_p - min(w, seq_p):] for g, (w, _) in enumerate(DILATED_PATTERNS)]
    return (y_p, y_s, re_p, im_p, kvp[0], kvp[1], kvp[2], re_s, im_s, kv_s[0], kv_s[1], kv_s[2])
```

```python
import functools

import jax
import jax.numpy as jnp
from jax import lax
from jax.experimental import pallas as pl
from jax.experimental.pallas import tpu as pltpu

F32 = jnp.float32
BF16 = jnp.bfloat16

D_MODEL = 2048
DEPTH = 4
N_A_LAYERS = DEPTH // 2
D_FF = 5632
S5_GROUP = 16
S5_GROUPS = D_MODEL // S5_GROUP
S5_STATE = 64
S5_NSTATE = S5_GROUPS * S5_STATE
HEAD_DIM = 128
HEADS = 16
KV_HEADS = 4
REP = HEADS // KV_HEADS
DILATED_PATTERNS = ((128, 1), (512, 4), (2048, 16))
N_DIL = len(DILATED_PATTERNS)
Q_BLOCK = 128
N_MOD = 9 * DEPTH + 4
EPS = 1e-6
ATTN_SCALE = HEAD_DIM ** -0.5
ATTN_SCALE_LOG2 = ATTN_SCALE * 1.4426950408889634

VMEM_LIMIT_BYTES = 56 * 1024 * 1024
SUBLANES = 8
BF16_SUBLANES = 16
NORM_ROWS = 64
STAGE_ROWS = 256
S5_COL_BLOCK = 1024
S5_CH_BLOCK = S5_COL_BLOCK // S5_STATE * S5_GROUP
S5_N_BLOCKS = S5_NSTATE // S5_COL_BLOCK


_TILES = {"tm": 1024, "tf": 512, "tn_down": 512, "tm_out": 2048, "tn_mix": 512, "tn_q": 1024, "tn_mod": 1024,
          "s5_steps": 32}


def _params(*sem):
    return pltpu.CompilerParams(dimension_semantics=sem, vmem_limit_bytes=VMEM_LIMIT_BYTES)


def _rms_mod(x, g, shift, scale):
    xn = x * lax.rsqrt(jnp.mean(x * x, axis=-1, keepdims=True) + EPS)
    return xn * g * (1.0 + scale) + shift


def _norm_rows(x_ref, g_ref, sh_ref, sc_ref, h_ref, row_chunk):
    rows = x_ref.shape[0]
    g = g_ref[...]
    mod_rows = sh_ref.shape[0]
    if mod_rows == 1:
        gs = g * (1.0 + sc_ref[...])
        sh = sh_ref[...]
        row_chunk = min(row_chunk, BF16_SUBLANES)

        def body(c, carry):
            r0 = pl.multiple_of(c * row_chunk, row_chunk)
            x = x_ref[pl.ds(r0, row_chunk), :]
            xn = x * lax.rsqrt(jnp.mean(x * x, axis=-1, keepdims=True) + EPS)
            h_ref[pl.ds(r0, row_chunk), :] = (xn * gs + sh).astype(h_ref.dtype)
            return carry

        n_chunks = rows // row_chunk
        lax.fori_loop(0, n_chunks, body, 0, unroll=4 if n_chunks % 4 == 0 else 1)
        return

    def body(c, carry):
        r0 = pl.multiple_of(c * row_chunk, row_chunk)
        x = x_ref[pl.ds(r0, row_chunk), :]
        if mod_rows == rows and rows != row_chunk:
            sh = sh_ref[pl.ds(r0, row_chunk), :]
            sc = sc_ref[pl.ds(r0, row_chunk), :]
        else:
            sh = sh_ref[...]
            sc = sc_ref[...]
        h_ref[pl.ds(r0, row_chunk), :] = _rms_mod(x, g, sh, sc).astype(h_ref.dtype)
        return carry

    lax.fori_loop(0, rows // row_chunk, body, 0)


def _mod_kernel(c_ref, w_ref, b_ref, o_ref):
    c = c_ref[...]
    a = (c * jax.nn.sigmoid(c)).astype(BF16)
    o_ref[...] = jnp.dot(a, w_ref[...].astype(BF16), preferred_element_type=F32) + b_ref[...]


def _mod_call(c_all, w_mod, b_mod):
    rows = c_all.shape[0]
    n = w_mod.shape[1]
    tn = _TILES["tn_mod"]
    return pl.pallas_call(
        _mod_kernel,
        grid=(n // tn,),
        in_specs=[
            pl.BlockSpec((rows, D_MODEL), lambda j: (0, 0)),
            pl.BlockSpec((D_MODEL, tn), lambda j: (0, j)),
            pl.BlockSpec((1, tn), lambda j: (0, j)),
        ],
        out_specs=pl.BlockSpec((rows, tn), lambda j: (0, j)),
        out_shape=jax.ShapeDtypeStruct((rows, n), F32),
        compiler_params=_params("arbitrary"),
        name="adaln_mod",
    )(c_all, w_mod, b_mod.reshape(1, n))


def _norm_two_row_sets(x_ref, sh_ref, sc_ref, x2_ref, sh2_ref, sc2_ref, g_ref, h_ref, row_chunk):
    tm, r2 = x_ref.shape[0], x2_ref.shape[0]

    @pl.when(pl.program_id(1) == 0)
    def _():
        _norm_rows(x_ref, g_ref, sh_ref, sc_ref, h_ref.at[pl.ds(0, tm)], row_chunk)

    @pl.when((pl.program_id(0) == 0) & (pl.program_id(1) == 0))
    def _():
        _norm_rows(x2_ref, g_ref, sh2_ref, sc2_ref, h_ref.at[pl.ds(tm, r2)], r2)


def _ffn_up_kernel(x_ref, sh_ref, sc_ref, x2_ref, sh2_ref, sc2_ref, g_ref, wg_ref, wu_ref, *rest, row_chunk):
    n_side = (len(rest) - 3) // 2
    side_in, (a_ref, a2_ref), side_out, h_ref = (rest[:n_side], rest[n_side:n_side + 2],
                                                  rest[n_side + 2:2 * n_side + 2], rest[-1])
    tm = x_ref.shape[0]
    _norm_two_row_sets(x_ref, sh_ref, sc_ref, x2_ref, sh2_ref, sc2_ref, g_ref, h_ref, row_chunk)

    def act(h):
        for src, dst in zip(side_in, side_out):
            dst[...] = src[...].astype(dst.dtype)
        g = jnp.dot(h, wg_ref[...].astype(BF16), preferred_element_type=F32)
        u = jnp.dot(h, wu_ref[...].astype(BF16), preferred_element_type=F32)
        return (g * jax.nn.sigmoid(g) * u).astype(a_ref.dtype)

    @pl.when(pl.program_id(0) == 0)
    def _():
        a = act(h_ref[...])
        a_ref[...] = a[:tm]
        a2_ref[...] = a[tm:]

    @pl.when(pl.program_id(0) != 0)
    def _():
        a_ref[...] = act(h_ref[pl.ds(0, tm), :])


def _ffn_call(x, mods, x2, mods2, g_norm, w_in, w_out, layer, which, tm, tf, tn, round_also=()):
    m, r2 = x.shape[0], x2.shape[0]
    nb = mods[0].shape[0]
    tiles_per_seq = (m // nb) // tm
    nk = D_FF // tf
    n_steps = (m // tm) * nk
    mod_spec = pl.BlockSpec((None, 1, D_MODEL), lambda i, k: (i // tiles_per_seq, 0, 0))
    full2 = pl.BlockSpec((r2, D_MODEL), lambda i, k: (0, 0))

    side_in_specs, side_out_specs, side_out_shapes, side_args = [], [], [], []
    for w_side, lead in (((w_out, (layer, which)),) + tuple(round_also)):
        rows, cols = w_side.shape[-2:]
        rps = next(r for r in range(BF16_SUBLANES, rows + 1, BF16_SUBLANES)
                   if rows % r == 0 and r * n_steps >= rows)
        n_blk = rows // rps

        def blk(i, k, n_blk=n_blk):
            return jnp.minimum(i * nk + k, n_blk - 1)

        side_in_specs.append(pl.BlockSpec((None,) * len(lead) + (rps, cols),
                                          functools.partial(lambda i, k, lead, blk: (*lead, blk(i, k), 0),
                                                            lead=lead, blk=blk)))
        side_out_specs.append(pl.BlockSpec((rps, cols), functools.partial(lambda i, k, blk: (blk(i, k), 0), blk=blk)))
        side_out_shapes.append(jax.ShapeDtypeStruct((rows, cols), BF16))
        side_args.append(w_side)

    a, a2, w_out_bf, *rounded = pl.pallas_call(
        functools.partial(_ffn_up_kernel, row_chunk=min(tm, NORM_ROWS)),
        grid=(m // tm, nk),
        in_specs=[
            pl.BlockSpec((tm, D_MODEL), lambda i, k: (i, 0)),
            mod_spec, mod_spec,
            full2, full2, full2,
            pl.BlockSpec((1, D_MODEL), lambda i, k: (0, 0)),
            pl.BlockSpec((None, None, D_MODEL, tf), lambda i, k: (layer, which, 0, k)),
            pl.BlockSpec((None, None, D_MODEL, tf), lambda i, k: (layer, which, 0, k + nk)),
        ] + side_in_specs,
        out_specs=[pl.BlockSpec((tm, tf), lambda i, k: (i, k)),
                   pl.BlockSpec((r2, tf), lambda i, k: (0, jnp.where(i == 0, k, nk - 1)))] + side_out_specs,
        out_shape=[jax.ShapeDtypeStruct((m, D_FF), BF16), jax.ShapeDtypeStruct((r2, D_FF), BF16)] + side_out_shapes,
        scratch_shapes=[pltpu.VMEM((tm + r2, D_MODEL), BF16)],
        compiler_params=_params("arbitrary", "arbitrary"),
        name="ffn_up",
    )(x, mods[0], mods[1], x2, mods2[0], mods2[1], g_norm, w_in, w_in, *side_args)
    out = _proj_residual_call(a, x, mods[2], w_out_bf, (), False, tm, tn, gate_scale=0.5, name="ffn_down")
    out2 = _proj_residual_call(a2, x2, mods2[2].reshape(1, r2, D_MODEL), w_out_bf, (), False, r2, tn,
                               gate_scale=0.5, name="ffn_down")
    return out, out2, rounded


def _norm_matmul_kernel(x_ref, sh_ref, sc_ref, x2_ref, sh2_ref, sc2_ref, g_ref, w_ref, *rest, row_chunk, head_rows):
    h_ref = rest[-1]
    outs = rest[:-1]
    n_large = len(outs) - (2 if head_rows else 1)
    tm = x_ref.shape[0]
    _norm_two_row_sets(x_ref, sh_ref, sc_ref, x2_ref, sh2_ref, sc2_ref, g_ref, h_ref, row_chunk)

    def to_heads(res, ref):
        rows, cols = res.shape
        heads_per_tile = cols // HEAD_DIM
        first = (pl.program_id(1) % (head_rows // heads_per_tile)) * heads_per_tile
        for c in range(heads_per_tile):
            ref[pl.ds(first + c, rows, stride=head_rows), :] = res[:, c * HEAD_DIM:(c + 1) * HEAD_DIM]

    def emit_large(res):
        outs[0][...] = res.astype(outs[0].dtype)
        if head_rows:
            tiles_per_slab = head_rows * HEAD_DIM // res.shape[1]
            for s, ref in enumerate(outs[1:n_large]):
                @pl.when(pl.program_id(1) // tiles_per_slab == s)
                def _(ref=ref):
                    to_heads(res, ref)

    def emit_small(res):
        outs[n_large][...] = res.astype(outs[n_large].dtype)
        if head_rows:
            to_heads(res, outs[n_large + 1])

    @pl.when(pl.program_id(0) == 0)
    def _():
        res = jnp.dot(h_ref[...], w_ref[...].astype(BF16), preferred_element_type=F32)
        emit_large(res[:tm])
        emit_small(res[tm:])

    @pl.when(pl.program_id(0) != 0)
    def _():
        emit_large(jnp.dot(h_ref[pl.ds(0, tm), :], w_ref[...].astype(BF16), preferred_element_type=F32))


def _norm_matmul_call(x, shift, scale, x2, shift2, scale2, g_norm, w, w_lead, out_dtype, tm, tn, head_rows=0):
    m, r2 = x.shape[0], x2.shape[0]
    nb = shift.shape[0]
    n = w.shape[-1]
    nj = n // tn
    tiles_per_seq = (m // nb) // tm
    mod_spec = pl.BlockSpec((None, 1, D_MODEL), lambda i, j: (i // tiles_per_seq, 0, 0))
    full2 = pl.BlockSpec((r2, D_MODEL), lambda i, j: (0, 0))

    def col2(i, j):
        return jnp.where(i == 0, j, nj - 1)

    if head_rows:
        slab = head_rows * HEAD_DIM
        tps = slab // tn
        n_slabs = n // slab

        def heads_spec(s):
            return pl.BlockSpec((tm * head_rows, HEAD_DIM),
                                lambda i, j: (jnp.where(j // tps >= s, i, jnp.maximum(i - 1, 0)), 0))

        out_spec = ((pl.BlockSpec((None, tm, tn), lambda i, j: (j // tps, i, j % tps)),)
                    + tuple(heads_spec(s) for s in range(n_slabs))
                    + (pl.BlockSpec((None, r2, tn), lambda i, j: (col2(i, j) // tps, 0, col2(i, j) % tps)),
                       pl.BlockSpec((None, r2 * head_rows, HEAD_DIM), lambda i, j: (col2(i, j) // tps, 0, 0))))
        out_shape = ((jax.ShapeDtypeStruct((n_slabs, m, slab), out_dtype),)
                     + tuple(jax.ShapeDtypeStruct((m * head_rows, HEAD_DIM), out_dtype) for _ in range(n_slabs))
                     + (jax.ShapeDtypeStruct((n_slabs, r2, slab), out_dtype),
                        jax.ShapeDtypeStruct((n_slabs, r2 * head_rows, HEAD_DIM), out_dtype)))
    else:
        out_spec = (pl.BlockSpec((tm, tn), lambda i, j: (i, j)),
                    pl.BlockSpec((r2, tn), lambda i, j: (0, col2(i, j))))
        out_shape = (jax.ShapeDtypeStruct((m, n), out_dtype), jax.ShapeDtypeStruct((r2, n), out_dtype))
    return pl.pallas_call(
        functools.partial(_norm_matmul_kernel, row_chunk=min(tm, NORM_ROWS), head_rows=head_rows),
        grid=(m // tm, nj),
        in_specs=[
            pl.BlockSpec((tm, D_MODEL), lambda i, j: (i, 0)),
            mod_spec, mod_spec,
            full2, full2, full2,
            pl.BlockSpec((1, D_MODEL), lambda i, j: (0, 0)),
            pl.BlockSpec((None,) * len(w_lead) + (D_MODEL, tn), lambda i, j: (*w_lead, 0, j)),
        ],
        out_specs=out_spec,
        out_shape=out_shape,
        scratch_shapes=[pltpu.VMEM((tm + r2, D_MODEL), BF16)],
        compiler_params=_params("arbitrary", "arbitrary"),
        name="norm_matmul",
    )(x, shift, scale, x2, shift2, scale2, g_norm, w)


def _proj_residual_kernel(z_ref, x_ref, gt_ref, *refs, glu, gate_scale):
    z = z_ref[...]
    if glu:
        wa_ref, wg_ref, o_ref = refs
        a = jnp.dot(z, wa_ref[...].astype(BF16), preferred_element_type=F32)
        g = jnp.dot(z, wg_ref[...].astype(BF16), preferred_element_type=F32)
        y = a * jax.nn.sigmoid(g)
    else:
        w_ref, o_ref = refs
        y = jnp.dot(z, w_ref[...].astype(BF16), preferred_element_type=F32)
    gate = gt_ref[...] if gate_scale == 1.0 else gate_scale * gt_ref[...]
    o_ref[...] = x_ref[...] + gate * y


def _proj_residual_call(z, x, gate, w, w_lead, glu, tm, tn, gate_scale=1.0, name="proj_residual"):
    m = x.shape[0]
    kdim = z.shape[1]
    nb, r, _ = gate.shape
    tiles_per_seq = (m // nb) // tm if r == 1 else 1
    n_blocks = D_MODEL // tn
    w_block = (None,) * len(w_lead) + (kdim, tn)
    w_specs = [pl.BlockSpec(w_block, lambda i, j: (*w_lead, 0, j))]
    w_args = [w]
    if glu:
        w_specs.append(pl.BlockSpec(w_block, lambda i, j: (*w_lead, 0, j + n_blocks)))
        w_args.append(w)
    return pl.pallas_call(
        functools.partial(_proj_residual_kernel, glu=glu, gate_scale=gate_scale),
        grid=(m // tm, n_blocks),
        in_specs=[
            pl.BlockSpec((tm, kdim), lambda i, j: (i, 0)),
            pl.BlockSpec((tm, tn), lambda i, j: (i, j)),
            pl.BlockSpec((None, r, tn), lambda i, j: (i // tiles_per_seq, 0, j)),
        ] + w_specs,
        out_specs=pl.BlockSpec((tm, tn), lambda i, j: (i, j)),
        out_shape=jax.ShapeDtypeStruct((m, D_MODEL), F32),
        compiler_params=_params("parallel", "arbitrary"),
        name=name,
    )(z, x, gate, *w_args)


def _final_norm_kernel(x_ref, sh_ref, sc_ref, g_ref, o_ref, *, row_chunk):
    _norm_rows(x_ref, g_ref, sh_ref, sc_ref, o_ref, row_chunk)


def _final_norm_call(x, shift, scale, g_norm, tm):
    m = x.shape[0]
    nb, r, _ = shift.shape
    tiles_per_seq = (m // nb) // tm if r == 1 else 1
    mod_spec = pl.BlockSpec((None, r, D_MODEL), lambda i: (i // tiles_per_seq, 0, 0))
    return pl.pallas_call(
        functools.partial(_final_norm_kernel, row_chunk=min(tm, NORM_ROWS)),
        grid=(m // tm,),
        in_specs=[
            pl.BlockSpec((tm, D_MODEL), lambda i: (i, 0)),
            mod_spec, mod_spec,
            pl.BlockSpec((1, D_MODEL), lambda i: (0, 0)),
        ],
        out_specs=pl.BlockSpec((tm, D_MODEL), lambda i: (i, 0)),
        out_shape=jax.ShapeDtypeStruct((m, D_MODEL), F32),
        compiler_params=_params("parallel"),
        name="final_norm",
    )(x, shift, scale, g_norm)


def _s5_discretise(lam_re, lam_im, log_dt):
    dt = jnp.exp(log_dt)
    decay = jnp.exp(lam_re * dt)
    ab_re = decay * jnp.cos(lam_im * dt)
    ab_im = decay * jnp.sin(lam_im * dt)
    den = lam_re * lam_re + lam_im * lam_im
    f_re = ((ab_re - 1.0) * lam_re + ab_im * lam_im) / den
    f_im = (ab_im * lam_re - (ab_re - 1.0) * lam_im) / den
    return ab_re, ab_im, f_re, f_im


def _s5_prep_kernel(lre_ref, lim_ref, ldt_ref, bre_ref, bim_ref, ctre_ref, ctim_ref, lre_g, lim_g, ldt_g,
                    are_ref, aim_ref, inre_ref, inim_ref, outre_ref, outim_ref):
    c, p = S5_GROUP, S5_STATE
    ab_re, ab_im, _, _ = _s5_discretise(lre_g[...], lim_g[...], ldt_g[...])
    are_ref[...] = ab_re
    aim_ref[...] = ab_im
    _, _, f_re, f_im = _s5_discretise(lre_ref[...], lim_ref[...], ldt_ref[...])
    b_re, b_im = bre_ref[...], bim_ref[...]
    bb_re = f_re * b_re - f_im * b_im
    bb_im = f_re * b_im + f_im * b_re
    for ref in (inre_ref, inim_ref, outre_ref, outim_ref):
        ref[...] = jnp.zeros(ref.shape, ref.dtype)
    for gl in range(S5_CH_BLOCK // c):
        rows_in, cols_in = slice(gl * c, (gl + 1) * c), slice(gl * p, (gl + 1) * p)
        inre_ref[rows_in, cols_in] = bb_re[rows_in].astype(inre_ref.dtype)
        inim_ref[rows_in, cols_in] = bb_im[rows_in].astype(inim_ref.dtype)
        outre_ref[cols_in, rows_in] = ctre_ref[cols_in, :].astype(outre_ref.dtype)
        outim_ref[cols_in, rows_in] = ctim_ref[cols_in, :].astype(outim_ref.dtype)


def _s5_prep_call(lam_re, lam_im, log_dt, b_re, b_im, c_re, c_im):
    g, p, c = S5_GROUPS, S5_STATE, S5_GROUP
    rep = lambda a: jnp.repeat(a, c, axis=0)
    b_t = lambda b: b.transpose(0, 2, 1).reshape(g * c, p)
    c_t = lambda m: m.transpose(0, 2, 1).reshape(g * p, c)
    gpb = S5_CH_BLOCK // c
    row_blk = lambda rows, cols: pl.BlockSpec((rows, cols), lambda i: (i, 0))
    blk3 = lambda rows, cols: pl.BlockSpec((None, rows, cols), lambda i: (i, 0, 0))
    return pl.pallas_call(
        _s5_prep_kernel,
        grid=(S5_N_BLOCKS,),
        in_specs=[row_blk(S5_CH_BLOCK, p), row_blk(S5_CH_BLOCK, p), row_blk(S5_CH_BLOCK, 1),
                  row_blk(S5_CH_BLOCK, p), row_blk(S5_CH_BLOCK, p),
                  row_blk(S5_COL_BLOCK, c), row_blk(S5_COL_BLOCK, c),
                  row_blk(gpb, p), row_blk(gpb, p), row_blk(gpb, 1)],
        out_specs=(row_blk(gpb, p), row_blk(gpb, p),
                   blk3(S5_CH_BLOCK, S5_COL_BLOCK), blk3(S5_CH_BLOCK, S5_COL_BLOCK),
                   blk3(S5_COL_BLOCK, S5_CH_BLOCK), blk3(S5_COL_BLOCK, S5_CH_BLOCK)),
        out_shape=(jax.ShapeDtypeStruct((g, p), F32), jax.ShapeDtypeStruct((g, p), F32),
                   jax.ShapeDtypeStruct((S5_N_BLOCKS, S5_CH_BLOCK, S5_COL_BLOCK), BF16),
                   jax.ShapeDtypeStruct((S5_N_BLOCKS, S5_CH_BLOCK, S5_COL_BLOCK), BF16),
                   jax.ShapeDtypeStruct((S5_N_BLOCKS, S5_COL_BLOCK, S5_CH_BLOCK), BF16),
                   jax.ShapeDtypeStruct((S5_N_BLOCKS, S5_COL_BLOCK, S5_CH_BLOCK), BF16)),
        compiler_params=_params("parallel"),
        name="s5_prepare",
    )(rep(lam_re), rep(lam_im), rep(log_dt.reshape(g, 1)), b_t(b_re), b_t(b_im), c_t(c_re), c_t(c_im),
      lam_re, lam_im, log_dt.reshape(g, 1))


def _s5_kernel(x_ref, sh_ref, sc_ref, g_ref, bbre_ref, bbim_ref, are_ref, aim_ref, cre_ref, cim_ref,
               d_ref, h0re_ref, h0im_ref, z_ref, hre_out, him_out,
               u_scr, sre_scr, sim_scr, y_scr, pwre_scr, pwim_scr, *, row_chunk):
    t = pl.program_id(0)
    nb, steps, _ = x_ref.shape
    rows = steps * nb
    lane = HEAD_DIM
    seg = SUBLANES // nb
    seg_len = steps // seg
    assert seg in (1, 2)

    @pl.when(t == 0)
    def _():
        hre_out[...] = h0re_ref[...]
        him_out[...] = h0im_ref[...]
        if seg > 1:
            zero = jnp.zeros((nb, S5_COL_BLOCK), F32)
            for cb in range(S5_N_BLOCKS):
                a_re = jnp.broadcast_to(are_ref[cb], (nb, S5_COL_BLOCK))
                a_im = jnp.broadcast_to(aim_ref[cb], (nb, S5_COL_BLOCK))
                p_re, p_im = a_re, a_im
                for i in range(seg_len):
                    pwre_scr[cb, i * SUBLANES:(i + 1) * SUBLANES, :] = jnp.concatenate([zero, p_re], axis=0)
                    pwim_scr[cb, i * SUBLANES:(i + 1) * SUBLANES, :] = jnp.concatenate([zero, p_im], axis=0)
                    p_re, p_im = p_re * a_re - p_im * a_im, p_re * a_im + p_im * a_re

    g = g_ref[...]
    for b in range(nb):
        for s in range(seg):
            def norm_chunk(c, carry, b=b, s=s):
                r0 = pl.multiple_of(c * row_chunk, row_chunk)
                u = _rms_mod(x_ref[b, pl.ds(s * seg_len + r0, row_chunk), :], g, sh_ref[b], sc_ref[b])
                for j in range(D_MODEL // lane):
                    u_scr[j, pl.ds(r0 * SUBLANES + s * nb + b, row_chunk, stride=SUBLANES), :] = \
                        u[:, j * lane:(j + 1) * lane]
                return carry

            lax.fori_loop(0, seg_len // row_chunk, norm_chunk, 0)

    tiles_per_block = S5_CH_BLOCK // lane
    per_block_scratch = sre_scr.shape[0] == S5_N_BLOCKS
    for cb in range(S5_N_BLOCKS):
        ch = slice(cb * S5_CH_BLOCK, (cb + 1) * S5_CH_BLOCK)
        st = slice(cb * S5_COL_BLOCK, (cb + 1) * S5_COL_BLOCK)
        slab = cb if per_block_scratch else 0
        sre, sim, ys = sre_scr.at[slab], sim_scr.at[slab], y_scr.at[slab]
        u_blk = jnp.concatenate([u_scr[cb * tiles_per_block + j] for j in range(tiles_per_block)], axis=1)
        u_bf = u_blk.astype(BF16)
        sre[...] = jnp.dot(u_bf, bbre_ref[cb], preferred_element_type=F32)
        sim[...] = jnp.dot(u_bf, bbim_ref[cb], preferred_element_type=F32)

        a_re = jnp.broadcast_to(are_ref[cb], (SUBLANES, S5_COL_BLOCK))
        a_im = jnp.broadcast_to(aim_ref[cb], (SUBLANES, S5_COL_BLOCK))

        def tile_step(i, carry, a_re=a_re, a_im=a_im, sre=sre, sim=sim):
            h_re, h_im = carry
            r0 = i * SUBLANES if isinstance(i, int) else pl.multiple_of(i * SUBLANES, SUBLANES)
            n_re = a_re * h_re - a_im * h_im + sre[pl.ds(r0, SUBLANES), :]
            n_im = a_re * h_im + a_im * h_re + sim[pl.ds(r0, SUBLANES), :]
            sre[pl.ds(r0, SUBLANES), :] = n_re
            sim[pl.ds(r0, SUBLANES), :] = n_im
            return n_re, n_im

        def first_rows(v):
            return jnp.concatenate([v[:nb]] * seg, axis=0)

        h_re, h_im = hre_out[:, st], him_out[:, st]
        if seg > 1:
            zero = jnp.zeros((SUBLANES - nb, S5_COL_BLOCK), F32)
            h_re, h_im = jnp.concatenate([h_re, zero], axis=0), jnp.concatenate([h_im, zero], axis=0)
        carry = (h_re, h_im)
        if per_block_scratch:
            for i in range(seg_len):
                carry = tile_step(i, carry)
        else:
            carry = lax.fori_loop(0, seg_len, tile_step, carry, unroll=min(seg_len, 4))
        h_re, h_im = carry
        if seg > 1:
            c_re, c_im = first_rows(h_re), first_rows(h_im)
            for i in range(seg_len):
                rows_i = slice(i * SUBLANES, (i + 1) * SUBLANES)
                p_re, p_im = pwre_scr[cb, rows_i, :], pwim_scr[cb, rows_i, :]
                f_re = sre[rows_i, :] + (p_re * c_re - p_im * c_im)
                f_im = sim[rows_i, :] + (p_re * c_im + p_im * c_re)
                sre[rows_i, :] = f_re
                sim[rows_i, :] = f_im
            h_re, h_im = f_re[SUBLANES - nb:], f_im[SUBLANES - nb:]
        hre_out[:, st] = h_re
        him_out[:, st] = h_im

        y = (jnp.dot(sre[...].astype(BF16), cre_ref[cb], preferred_element_type=F32)
             - jnp.dot(sim[...].astype(BF16), cim_ref[cb], preferred_element_type=F32))
        zf = jax.nn.gelu(y + d_ref[:, ch] * u_blk)
        for j in range(tiles_per_block):
            ys[j] = zf[:, j * lane:(j + 1) * lane]
        for b in range(nb):
            for s in range(seg):
                for j in range(tiles_per_block):
                    c0 = cb * S5_CH_BLOCK + j * lane
                    z_ref[b, s * seg_len:(s + 1) * seg_len, c0:c0 + lane] = \
                        ys[j, pl.ds(s * nb + b, seg_len, stride=SUBLANES), :].astype(z_ref.dtype)


def _s5_call(x, shift, scale, g_norm, bb_re, bb_im, a_re, a_im, c_re, c_im, d_skip,
             h0_re, h0_im, steps_per_chunk, z_dtype, per_block_scratch):
    nb, seq, _ = x.shape
    assert SUBLANES % nb == 0 and seq % steps_per_chunk == 0
    tr = steps_per_chunk * nb
    slabs = S5_N_BLOCKS if per_block_scratch else 1
    seg = SUBLANES // nb
    pw_rows = tr if seg > 1 else SUBLANES
    row_chunk = min(steps_per_chunk // (SUBLANES // nb), NORM_ROWS)
    const2 = lambda t: (0, 0)
    const3 = lambda t: (0, 0, 0)
    state_spec = pl.BlockSpec((nb, S5_NSTATE), const2)
    io_spec = pl.BlockSpec((nb, steps_per_chunk, D_MODEL), lambda t: (0, t, 0))
    return pl.pallas_call(
        functools.partial(_s5_kernel, row_chunk=row_chunk),
        grid=(seq // steps_per_chunk,),
        in_specs=[
            io_spec,
            pl.BlockSpec((nb, 1, D_MODEL), const3),
            pl.BlockSpec((nb, 1, D_MODEL), const3),
            pl.BlockSpec((1, D_MODEL), const2),
            pl.BlockSpec((S5_N_BLOCKS, S5_CH_BLOCK, S5_COL_BLOCK), const3, pipeline_mode=pl.Buffered(1)),
            pl.BlockSpec((S5_N_BLOCKS, S5_CH_BLOCK, S5_COL_BLOCK), const3, pipeline_mode=pl.Buffered(1)),
            pl.BlockSpec((S5_N_BLOCKS, 1, S5_COL_BLOCK), const3),
            pl.BlockSpec((S5_N_BLOCKS, 1, S5_COL_BLOCK), const3),
            pl.BlockSpec((S5_N_BLOCKS, S5_COL_BLOCK, S5_CH_BLOCK), const3, pipeline_mode=pl.Buffered(1)),
            pl.BlockSpec((S5_N_BLOCKS, S5_COL_BLOCK, S5_CH_BLOCK), const3, pipeline_mode=pl.Buffered(1)),
            pl.BlockSpec((1, D_MODEL), const2),
            state_spec, state_spec,
        ],
        out_specs=(io_spec, state_spec, state_spec),
        out_shape=(jax.ShapeDtypeStruct((nb, seq, D_MODEL), z_dtype),
                   jax.ShapeDtypeStruct((nb, S5_NSTATE), F32),
                   jax.ShapeDtypeStruct((nb, S5_NSTATE), F32)),
        scratch_shapes=[pltpu.VMEM((D_MODEL // HEAD_DIM, tr, HEAD_DIM), F32),
                        pltpu.VMEM((slabs, tr, S5_COL_BLOCK), F32),
                        pltpu.VMEM((slabs, tr, S5_COL_BLOCK), F32),
                        pltpu.VMEM((slabs, S5_CH_BLOCK // HEAD_DIM, tr, HEAD_DIM), F32),
                        pltpu.VMEM((S5_N_BLOCKS, pw_rows, S5_COL_BLOCK), F32),
                        pltpu.VMEM((S5_N_BLOCKS, pw_rows, S5_COL_BLOCK), F32)],
        compiler_params=_params("arbitrary"),
        name="s5_mixer",
    )(x, shift, scale, g_norm, bb_re, bb_im, a_re, a_im, c_re, c_im, d_skip, h0_re, h0_im)


def _softmax_pv(s2, v_bf):
    m = jnp.max(s2, axis=-1, keepdims=True)
    p = jnp.exp2(s2 - m)
    l = jnp.sum(p, axis=-1, keepdims=True)
    o = jnp.dot(p.astype(BF16), v_bf, preferred_element_type=F32) / l
    return o, jnp.broadcast_to(m + jnp.log2(l), o.shape)


def _qk(q_bf, k_bf, scale=ATTN_SCALE):
    return lax.dot_general(q_bf, k_bf, (((1,), (1,)), ((), ())), preferred_element_type=F32) * scale


def _attn_prompt_kernel(q0_ref, q1_ref, q2_ref, k0_ref, v0_ref, k1_ref, v1_ref, k2_ref, v2_ref, o_ref,
                        qf_scr, o1_scr, l1_scr, o2_scr, l2_scr, band_scr, first_scr):
    seq = q0_ref.shape[0]
    qb = Q_BLOCK
    row = lax.broadcasted_iota(jnp.int32, (REP * qb, 2 * qb), 0) & (qb - 1)
    col = lax.broadcasted_iota(jnp.int32, (REP * qb, 2 * qb), 1)
    band_scr[...] = jnp.where((col >= row) & (col <= row + qb), 0.0, -jnp.inf)
    row1 = lax.broadcasted_iota(jnp.int32, (REP * qb, qb), 0) & (qb - 1)
    col1 = lax.broadcasted_iota(jnp.int32, (REP * qb, qb), 1)
    first_scr[...] = jnp.where(col1 <= row1, 0.0, -jnp.inf)

    def rows_of(start, d):
        return pl.ds(start, qb) if d == 1 else pl.ds(start, qb, stride=d)

    def load_q(src_ref, start, d):
        if d == 1:
            parts = [src_ref[pl.ds(start, qb), e * HEAD_DIM:(e + 1) * HEAD_DIM] for e in range(REP)]
        else:
            parts = [src_ref[e, rows_of(start, d), :] for e in range(REP)]
        return jnp.concatenate(parts, axis=0).astype(BF16)

    def unit(src_q, k_ref, v_ref, start, d, first):
        q4 = load_q(src_q, start, d)
        k_cur = k_ref[rows_of(start, d), :]
        v_cur = v_ref[rows_of(start, d), :]
        if first:
            s = _qk(q4, k_cur.astype(BF16), ATTN_SCALE_LOG2) + first_scr[...]
            return _softmax_pv(s, v_cur.astype(BF16))
        prev = start - qb * d
        k_band = jnp.concatenate([k_ref[rows_of(prev, d), :], k_cur], axis=0)
        v_band = jnp.concatenate([v_ref[rows_of(prev, d), :], v_cur], axis=0)
        s = _qk(q4, k_band.astype(BF16), ATTN_SCALE_LOG2) + band_scr[...]
        return _softmax_pv(s, v_band.astype(BF16))

    def stage_group(q_ref, k_ref, v_ref, d, o_scr, l_scr):
        n_blocks = seq // (qb * d)

        def copy(c, carry):
            r0 = pl.multiple_of(c * STAGE_ROWS, STAGE_ROWS)
            for e in range(REP):
                qf_scr[e, pl.ds(r0, STAGE_ROWS), :] = \
                    q_ref[pl.ds(r0, STAGE_ROWS), e * HEAD_DIM:(e + 1) * HEAD_DIM].astype(F32)
            return carry

        lax.fori_loop(0, seq // STAGE_ROWS, copy, 0)

        def store(start, o, lse):
            for e in range(REP):
                sl = slice(e * qb, (e + 1) * qb)
                o_scr[e, rows_of(start, d), :] = o[sl]
                l_scr[e, rows_of(start, d), :] = lse[sl]

        def per_class(r, carry):
            store(r, *unit(qf_scr, k_ref, v_ref, r, d, True))

            def per_block(jb, c2):
                start = jb * (qb * d) + r
                store(start, *unit(qf_scr, k_ref, v_ref, start, d, False))
                return c2

            if n_blocks > 1:
                lax.fori_loop(1, n_blocks, per_block, 0, unroll=True)
            return carry

        lax.fori_loop(0, d, per_class, 0, unroll=4 if n_blocks == 1 else 2)

    stage_group(q2_ref, k2_ref, v2_ref, DILATED_PATTERNS[2][1], o2_scr, l2_scr)
    stage_group(q1_ref, k1_ref, v1_ref, DILATED_PATTERNS[1][1], o1_scr, l1_scr)

    def merge(start, o0, lse0):
        for e in range(REP):
            sl = slice(e * qb, (e + 1) * qb)
            cols = slice(e * HEAD_DIM, (e + 1) * HEAD_DIM)
            la = lse0[sl]
            lb = l1_scr[e, pl.ds(start, qb), :]
            lc = l2_scr[e, pl.ds(start, qb), :]
            mx = jnp.maximum(jnp.maximum(la, lb), lc)
            wa = jnp.exp2(la - mx)
            wb = jnp.exp2(lb - mx)
            wc = jnp.exp2(lc - mx)
            inv = 1.0 / (wa + wb + wc)
            acc = (wa * o0[sl] + wb * o1_scr[e, pl.ds(start, qb), :] + wc * o2_scr[e, pl.ds(start, qb), :]) * inv
            o_ref[pl.ds(start, qb), cols] = acc.astype(o_ref.dtype)

    merge(0, *unit(q0_ref, k0_ref, v0_ref, 0, 1, True))

    def per_block0(jb, carry):
        start = pl.multiple_of(jb * qb, qb)
        merge(start, *unit(q0_ref, k0_ref, v0_ref, start, 1, False))
        return carry

    n_blocks0 = seq // qb
    lax.fori_loop(1, n_blocks0, per_block0, 0, unroll=5 if (n_blocks0 - 1) % 5 == 0 else 1)


def _attn_prompt_call(q, kv):
    b, seq, _ = q.shape
    qw = REP * HEAD_DIM
    q_specs = [pl.BlockSpec((None, seq, qw), functools.partial(lambda bi, h, g: (bi, 0, g * KV_HEADS + h), g=g))
               for g in range(N_DIL)]
    kv_specs = []
    for g in range(N_DIL):
        kv_specs.append(pl.BlockSpec((None, None, seq, HEAD_DIM),
                                     functools.partial(lambda bi, h, g: (g, bi, 0, h), g=g)))
        kv_specs.append(pl.BlockSpec((None, None, seq, HEAD_DIM),
                                     functools.partial(lambda bi, h, g: (g, bi, 0, KV_HEADS + h), g=g)))
    kv_args = [kv] * (2 * N_DIL)
    return pl.pallas_call(
        _attn_prompt_kernel,
        grid=(b, KV_HEADS),
        in_specs=q_specs + kv_specs,
        out_specs=pl.BlockSpec((None, seq, qw), lambda bi, h: (bi, 0, h)),
        out_shape=jax.ShapeDtypeStruct((b, seq, HEADS * HEAD_DIM), BF16),
        scratch_shapes=[pltpu.VMEM((REP, seq, HEAD_DIM), F32),
                        pltpu.VMEM((REP, seq, HEAD_DIM), F32), pltpu.VMEM((REP, seq, HEAD_DIM), F32),
                        pltpu.VMEM((REP, seq, HEAD_DIM), F32), pltpu.VMEM((REP, seq, HEAD_DIM), F32),
                        pltpu.VMEM((REP * Q_BLOCK, 2 * Q_BLOCK), F32), pltpu.VMEM((REP * Q_BLOCK, Q_BLOCK), F32)],
        compiler_params=_params("parallel", "arbitrary"),
        name="dilated_attention_prompt",
    )(q, q, q, *kv_args)


def _attn_sample_kernel(q_ref, kvn_ref, c0_ref, c1_ref, c2_ref, o_ref, *, n_tok):
    rows = REP * n_tok
    cache_refs = (c0_ref, c1_ref, c2_ref)
    kv_rows = 2 * KV_HEADS
    for h in range(KV_HEADS):
        outs, lses = [], []
        for g, (_, d) in enumerate(DILATED_PATTERNS):
            q = q_ref[g * KV_HEADS + h]
            lw = cache_refs[g].shape[0] // kv_rows
            k_c = cache_refs[g][pl.ds(h, lw, stride=kv_rows), :].astype(BF16)
            v_c = cache_refs[g][pl.ds(KV_HEADS + h, lw, stride=kv_rows), :].astype(BF16)
            k_n = kvn_ref[g, pl.ds(h, n_tok, stride=kv_rows), :].astype(BF16)
            v_n = kvn_ref[g, pl.ds(KV_HEADS + h, n_tok, stride=kv_rows), :].astype(BF16)
            tq_c = lax.broadcasted_iota(jnp.int32, (rows, lw), 0) & (n_tok - 1)
            idx_c = lax.broadcasted_iota(jnp.int32, (rows, lw), 1)
            ok_c = (idx_c >= tq_c) & (((idx_c - tq_c) & (d - 1)) == 0)
            tq_n = lax.broadcasted_iota(jnp.int32, (rows, n_tok), 0) & (n_tok - 1)
            idx_n = lax.broadcasted_iota(jnp.int32, (rows, n_tok), 1)
            ok_n = (idx_n <= tq_n) & (((tq_n - idx_n) & (d - 1)) == 0)
            s_c = jnp.where(ok_c, _qk(q, k_c), -jnp.inf)
            s_n = jnp.where(ok_n, _qk(q, k_n), -jnp.inf)
            m = jnp.maximum(jnp.max(s_c, axis=-1, keepdims=True), jnp.max(s_n, axis=-1, keepdims=True))
            p_c = jnp.exp(s_c - m)
            p_n = jnp.exp(s_n - m)
            l = jnp.sum(p_c, axis=-1, keepdims=True) + jnp.sum(p_n, axis=-1, keepdims=True)
            o = (jnp.dot(p_c.astype(BF16), v_c, preferred_element_type=F32)
                 + jnp.dot(p_n.astype(BF16), v_n, preferred_element_type=F32)) / l
            outs.append(o)
            lses.append(m + jnp.log(l))
        mx = jnp.maximum(jnp.maximum(lses[0], lses[1]), lses[2])
        w = [jnp.exp(x - mx) for x in lses]
        inv = 1.0 / (w[0] + w[1] + w[2])
        acc = (w[0] * inv) * outs[0] + (w[1] * inv) * outs[1] + (w[2] * inv) * outs[2]
        o_ref[h] = acc.astype(o_ref.dtype)


def _attn_sample_call(q_heads, kv_new, caches, n_tok):
    b = q_heads.shape[0]
    rows = REP * n_tok
    assert n_tok & (n_tok - 1) == 0
    in_specs = [pl.BlockSpec((None, N_DIL * KV_HEADS, rows, HEAD_DIM), lambda bi: (bi, 0, 0, 0)),
                pl.BlockSpec((N_DIL, None, kv_new.shape[2], HEAD_DIM), lambda bi: (0, bi, 0, 0))]
    in_specs += [pl.BlockSpec((None, c.shape[1], HEAD_DIM), lambda bi: (bi, 0, 0)) for c in caches]
    return pl.pallas_call(
        functools.partial(_attn_sample_kernel, n_tok=n_tok),
        grid=(b,),
        in_specs=in_specs,
        out_specs=pl.BlockSpec((None, KV_HEADS, rows, HEAD_DIM), lambda bi: (bi, 0, 0, 0)),
        out_shape=jax.ShapeDtypeStruct((b, KV_HEADS, rows, HEAD_DIM), BF16),
        compiler_params=_params("parallel"),
        name="dilated_attention_sample",
    )(q_heads, kv_new, *caches)


def _s5_weights(lam_re, lam_im, log_dt, b_re, b_im, c_re, c_im):
    a_re, a_im, in_re, in_im, out_re, out_im = _s5_prep_call(lam_re, lam_im, log_dt, b_re, b_im, c_re, c_im)
    lam_bar = lambda a: a.reshape(S5_N_BLOCKS, 1, S5_COL_BLOCK)
    return in_re, in_im, lam_bar(a_re), lam_bar(a_im), out_re, out_im


def _trunks(x_p, x_s, mod_p, mod_s, s5_state, kv_caches, wts, tiles):
    (norm_g, ffn_w_in, ffn_w_out, s5_mats, s5_d, s5_w_glu, kv_norm_g, w_kv,
     attn_w_q, attn_w_o, final_norm_g) = wts
    bp, lp, _ = x_p.shape
    bs, ls, _ = x_s.shape
    mp, ms = bp * lp, bs * ls
    tm = tiles["tm"]
    kv_rows = 2 * KV_HEADS

    def mods_p(*ks):
        return tuple(mod_p[:, k].reshape(bp, 1, D_MODEL) for k in ks)

    def mods_s(*ks):
        return tuple(jnp.repeat(mod_s[:, k], ls, axis=0) for k in ks)

    def as_block(a):
        return a.reshape(1, *a.shape)

    def norm_g_row(a):
        return a.reshape(1, D_MODEL)

    xp = x_p.reshape(mp, D_MODEL)
    xs = x_s.reshape(ms, D_MODEL)
    re_p, im_p, re_s, im_s = [], [], [], []
    for layer in range(DEPTH):
        base = 9 * layer
        if layer == N_A_LAYERS:
            k0 = 9 * DEPTH
            kvf_p, *kv_p, _, kv_s = _norm_matmul_call(xp, *mods_p(k0, k0 + 1), xs, *mods_s(k0, k0 + 1),
                                                     norm_g_row(kv_norm_g), w_kv_bf, (), F32, tm,
                                                     KV_HEADS * HEAD_DIM, head_rows=kv_rows)
        later = (((s5_w_glu, (layer,)),) if layer < N_A_LAYERS else
                 ((attn_w_q, (layer - N_A_LAYERS,)), (attn_w_o, (layer - N_A_LAYERS,))))
        xp, xs, rounded = _ffn_call(xp, mods_p(base, base + 1, base + 2), xs, mods_s(base, base + 1, base + 2),
                                    norm_g_row(norm_g[layer, 0]), ffn_w_in, ffn_w_out, layer, 0, tm, tiles["tf"],
                                    tiles["tn_down"], round_also=later)
        g_mix = norm_g_row(norm_g[layer, 1])
        if layer < N_A_LAYERS:
            d_skip = s5_d[layer].reshape(1, D_MODEL)
            zero_state = jnp.zeros((bp, S5_NSTATE), F32)
            zp, h_re, h_im = _s5_call(xp.reshape(bp, lp, D_MODEL), *mods_p(base + 3, base + 4), g_mix,
                                      *s5_mats[layer], d_skip, zero_state, zero_state,
                                      min(lp, tiles["s5_steps"]), BF16, True)
            re_p.append(h_re.reshape(bp, S5_GROUPS, S5_STATE))
            im_p.append(h_im.reshape(bp, S5_GROUPS, S5_STATE))
            zs, h_re, h_im = _s5_call(xs.reshape(bs, ls, D_MODEL), mod_s[:, base + 3].reshape(bs, 1, D_MODEL),
                                      mod_s[:, base + 4].reshape(bs, 1, D_MODEL), g_mix, *s5_mats[layer], d_skip,
                                      s5_state[0][layer].reshape(bs, S5_NSTATE),
                                      s5_state[1][layer].reshape(bs, S5_NSTATE), ls, F32, False)
            re_s.append(h_re.reshape(bs, S5_GROUPS, S5_STATE))
            im_s.append(h_im.reshape(bs, S5_GROUPS, S5_STATE))
            (w_glu_bf,) = rounded
            xp = _proj_residual_call(zp.reshape(mp, D_MODEL), xp, *mods_p(base + 5), w_glu_bf, (),
                                     True, tm, tiles["tn_mix"], name="s5_glu")
            xs = _proj_residual_call(zs.reshape(ms, D_MODEL).astype(BF16), xs, as_block(*mods_s(base + 5)),
                                     w_glu_bf, (), True, ms, tiles["tn_mix"], name="s5_glu")
        else:
            w_q_bf, w_o_bf = rounded
            q_p, q_s = _norm_matmul_call(xp, *mods_p(base + 3, base + 4), xs, *mods_s(base + 3, base + 4), g_mix,
                                         w_q_bf, (), BF16, tm, tiles["tn_q"])
            o_p = _attn_prompt_call(q_p.reshape(bp, lp, N_DIL * HEADS * HEAD_DIM),
                                    kvf_p.reshape(N_DIL, bp, lp, kv_rows * HEAD_DIM)).reshape(mp, D_MODEL)
            qh = q_s.reshape(bs, ls, N_DIL, KV_HEADS, REP, HEAD_DIM).transpose(0, 2, 3, 4, 1, 5)
            qh = qh.reshape(bs, N_DIL * KV_HEADS, REP * ls, HEAD_DIM)
            o_s = _attn_sample_call(qh, kv_s.reshape(N_DIL, bs, ls * kv_rows, HEAD_DIM), kv_caches, ls)
            o_s = o_s.reshape(bs, KV_HEADS, REP, ls, HEAD_DIM).transpose(0, 3, 1, 2, 4).reshape(ms, D_MODEL)
            xp = _proj_residual_call(o_p, xp, *mods_p(base + 5), w_o_bf, (), False, tiles["tm_out"],
                                     tiles["tn_mix"], name="attn_out")
            xs = _proj_residual_call(o_s, xs, as_block(*mods_s(base + 5)), w_o_bf, (), False, ms,
                                     tiles["tn_mix"], name="attn_out")
        later = ((w_kv, ()),) if layer == N_A_LAYERS - 1 else ()
        xp, xs, rounded = _ffn_call(xp, mods_p(base + 6, base + 7, base + 8), xs, mods_s(base + 6, base + 7, base + 8),
                                    norm_g_row(norm_g[layer, 2]), ffn_w_in, ffn_w_out, layer, 1, tm, tiles["tf"],
                                    tiles["tn_down"], round_also=later)
        if later:
            (w_kv_bf,) = rounded
    k1 = 9 * DEPTH + 2
    y_p = _final_norm_call(xp, *mods_p(k1, k1 + 1), norm_g_row(final_norm_g), tm)
    y_s = _final_norm_call(xs, *(as_block(a) for a in mods_s(k1, k1 + 1)), norm_g_row(final_norm_g), ms)
    kv_p = [a.reshape(bp, lp, 2, KV_HEADS, HEAD_DIM) for a in kv_p]
    kv_s = kv_s.reshape(N_DIL, bs, ls, 2, KV_HEADS, HEAD_DIM)
    return ((y_p.reshape(bp, lp, D_MODEL), jnp.stack(re_p, axis=0), jnp.stack(im_p, axis=0), kv_p),
            (y_s.reshape(bs, ls, D_MODEL), jnp.stack(re_s, axis=0), jnp.stack(im_s, axis=0), kv_s))


def kernel(x_prompt, x_sample, c_prompt, c_sample, state_s5_re, state_s5_im, cache_kv_g0, cache_kv_g1, cache_kv_g2, w_mod, b_mod, norm_g, ffn_w_in, ffn_w_out, s5_lambda_re, s5_lambda_im, s5_log_dt, s5_b_re, s5_b_im, s5_c_re, s5_c_im, s5_d, s5_w_glu, kv_norm_g, w_kv, attn_w_q, attn_w_o, final_norm_g):
    bp, seq_p, _ = x_prompt.shape
    bs, seq_s, _ = x_sample.shape
    caches = (cache_kv_g0, cache_kv_g1, cache_kv_g2)
    for (w, d), c in zip(DILATED_PATTERNS, caches):
        assert c.shape[1] == w and w == d * Q_BLOCK and seq_p % (d * Q_BLOCK) == 0
    assert bp <= SUBLANES and bs <= SUBLANES

    n_c = bp + bs
    c_all = jnp.pad(jnp.concatenate([c_prompt, c_sample], axis=0), ((0, (-n_c) % SUBLANES), (0, 0)))
    mod_all = _mod_call(c_all, w_mod, b_mod)
    mod_p = mod_all[:bp].reshape(bp, N_MOD, D_MODEL)
    mod_s = mod_all[bp:n_c].reshape(bs, N_MOD, D_MODEL)

    s5_mats = [_s5_weights(s5_lambda_re[l], s5_lambda_im[l], s5_log_dt[l], s5_b_re[l], s5_b_im[l],
                           s5_c_re[l], s5_c_im[l]) for l in range(N_A_LAYERS)]
    wts = (norm_g, ffn_w_in, ffn_w_out, s5_mats, s5_d, s5_w_glu, kv_norm_g, w_kv, attn_w_q, attn_w_o, final_norm_g)

    caches_flat = tuple(c.reshape(bs, c.shape[1] * 2 * KV_HEADS, HEAD_DIM) for c in caches)
    (y_p, re_p, im_p, kv_p), (y_s, re_s, im_s, kv_s) = _trunks(
        x_prompt, x_sample, mod_p, mod_s, (state_s5_re, state_s5_im), caches_flat, wts,
        _TILES)

    kvp = [kv_p[g][:, seq_p - min(w, seq_p):] for g, (w, _) in enumerate(DILATED_PATTERNS)]
    return (y_p, y_s, re_p, im_p, kvp[0], kvp[1], kvp[2], re_s, im_s, kv_s[0], kv_s[1], kv_s[2])
```

```python
import functools

import jax
import jax.numpy as jnp
from jax import lax
from jax.experimental import pallas as pl
from jax.experimental.pallas import tpu as pltpu

F32 = jnp.float32
BF16 = jnp.bfloat16

D_MODEL = 2048
DEPTH = 4
N_A_LAYERS = DEPTH // 2
D_FF = 5632
S5_GROUP = 16
S5_GROUPS = D_MODEL // S5_GROUP
S5_STATE = 64
S5_NSTATE = S5_GROUPS * S5_STATE
HEAD_DIM = 128
HEADS = 16
KV_HEADS = 4
REP = HEADS // KV_HEADS
DILATED_PATTERNS = ((128, 1), (512, 4), (2048, 16))
N_DIL = len(DILATED_PATTERNS)
Q_BLOCK = 128
N_MOD = 9 * DEPTH + 4
EPS = 1e-6
ATTN_SCALE = HEAD_DIM ** -0.5
ATTN_SCALE_LOG2 = ATTN_SCALE * 1.4426950408889634

VMEM_LIMIT_BYTES = 56 * 1024 * 1024
SUBLANES = 8
BF16_SUBLANES = 16
NORM_ROWS = 64
STAGE_ROWS = 256
S5_COL_BLOCK = 1024
S5_CH_BLOCK = S5_COL_BLOCK // S5_STATE * S5_GROUP
S5_N_BLOCKS = S5_NSTATE // S5_COL_BLOCK


_TILES = {"tm": 1024, "tf": 512, "tn_down": 512, "tm_out": 2048, "tn_mix": 512, "tn_q": 2048, "tn_mod": 1024,
          "s5_steps": 32}


def _params(*sem):
    return pltpu.CompilerParams(dimension_semantics=sem, vmem_limit_bytes=VMEM_LIMIT_BYTES)


def _rms_mod(x, g, shift, scale):
    xn = x * lax.rsqrt(jnp.mean(x * x, axis=-1, keepdims=True) + EPS)
    return xn * g * (1.0 + scale) + shift


def _norm_rows(x_ref, g_ref, sh_ref, sc_ref, h_ref, row_chunk):
    rows = x_ref.shape[0]
    g = g_ref[...]
    mod_rows = sh_ref.shape[0]
    if mod_rows == 1:
        gs = g * (1.0 + sc_ref[...])
        sh = sh_ref[...]
        row_chunk = min(row_chunk, BF16_SUBLANES)

        def body(c, carry):
            r0 = pl.multiple_of(c * row_chunk, row_chunk)
            x = x_ref[pl.ds(r0, row_chunk), :]
            xn = x * lax.rsqrt(jnp.mean(x * x, axis=-1, keepdims=True) + EPS)
            h_ref[pl.ds(r0, row_chunk), :] = (xn * gs + sh).astype(h_ref.dtype)
            return carry

        n_chunks = rows // row_chunk
        lax.fori_loop(0, n_chunks, body, 0, unroll=4 if n_chunks % 4 == 0 else 1)
        return

    def body(c, carry):
        r0 = pl.multiple_of(c * row_chunk, row_chunk)
        x = x_ref[pl.ds(r0, row_chunk), :]
        if mod_rows == rows and rows != row_chunk:
            sh = sh_ref[pl.ds(r0, row_chunk), :]
            sc = sc_ref[pl.ds(r0, row_chunk), :]
        else:
            sh = sh_ref[...]
            sc = sc_ref[...]
        h_ref[pl.ds(r0, row_chunk), :] = _rms_mod(x, g, sh, sc).astype(h_ref.dtype)
        return carry

    lax.fori_loop(0, rows // row_chunk, body, 0)


def _mod_kernel(c_ref, w_ref, b_ref, o_ref):
    c = c_ref[...]
    a = (c * jax.nn.sigmoid(c)).astype(BF16)
    o_ref[...] = jnp.dot(a, w_ref[...].astype(BF16), preferred_element_type=F32) + b_ref[...]


def _mod_call(c_all, w_mod, b_mod):
    rows = c_all.shape[0]
    n = w_mod.shape[1]
    tn = _TILES["tn_mod"]
    return pl.pallas_call(
        _mod_kernel,
        grid=(n // tn,),
        in_specs=[
            pl.BlockSpec((rows, D_MODEL), lambda j: (0, 0)),
            pl.BlockSpec((D_MODEL, tn), lambda j: (0, j)),
            pl.BlockSpec((1, tn), lambda j: (0, j)),
        ],
        out_specs=pl.BlockSpec((rows, tn), lambda j: (0, j)),
        out_shape=jax.ShapeDtypeStruct((rows, n), F32),
        compiler_params=_params("arbitrary"),
        name="adaln_mod",
    )(c_all, w_mod, b_mod.reshape(1, n))


def _norm_two_row_sets(x_ref, sh_ref, sc_ref, x2_ref, sh2_ref, sc2_ref, g_ref, h_ref, row_chunk):
    tm, r2 = x_ref.shape[0], x2_ref.shape[0]

    @pl.when(pl.program_id(1) == 0)
    def _():
        _norm_rows(x_ref, g_ref, sh_ref, sc_ref, h_ref.at[pl.ds(0, tm)], row_chunk)

    @pl.when((pl.program_id(0) == 0) & (pl.program_id(1) == 0))
    def _():
        _norm_rows(x2_ref, g_ref, sh2_ref, sc2_ref, h_ref.at[pl.ds(tm, r2)], r2)


def _ffn_up_kernel(x_ref, sh_ref, sc_ref, x2_ref, sh2_ref, sc2_ref, g_ref, wg_ref, wu_ref, *rest, row_chunk):
    n_side = (len(rest) - 3) // 2
    side_in, (a_ref, a2_ref), side_out, h_ref = (rest[:n_side], rest[n_side:n_side + 2],
                                                  rest[n_side + 2:2 * n_side + 2], rest[-1])
    tm = x_ref.shape[0]
    _norm_two_row_sets(x_ref, sh_ref, sc_ref, x2_ref, sh2_ref, sc2_ref, g_ref, h_ref, row_chunk)

    def act(h):
        for src, dst in zip(side_in, side_out):
            dst[...] = src[...].astype(dst.dtype)
        g = jnp.dot(h, wg_ref[...].astype(BF16), preferred_element_type=F32)
        u = jnp.dot(h, wu_ref[...].astype(BF16), preferred_element_type=F32)
        return (g * jax.nn.sigmoid(g) * u).astype(a_ref.dtype)

    @pl.when(pl.program_id(0) == 0)
    def _():
        a = act(h_ref[...])
        a_ref[...] = a[:tm]
        a2_ref[...] = a[tm:]

    @pl.when(pl.program_id(0) != 0)
    def _():
        a_ref[...] = act(h_ref[pl.ds(0, tm), :])


def _ffn_call(x, mods, x2, mods2, g_norm, w_in, w_out, layer, which, tm, tf, tn, round_also=()):
    m, r2 = x.shape[0], x2.shape[0]
    nb = mods[0].shape[0]
    tiles_per_seq = (m // nb) // tm
    nk = D_FF // tf
    n_steps = (m // tm) * nk
    mod_spec = pl.BlockSpec((None, 1, D_MODEL), lambda i, k: (i // tiles_per_seq, 0, 0))
    full2 = pl.BlockSpec((r2, D_MODEL), lambda i, k: (0, 0))

    side_in_specs, side_out_specs, side_out_shapes, side_args = [], [], [], []
    for w_side, lead in (((w_out, (layer, which)),) + tuple(round_also)):
        rows, cols = w_side.shape[-2:]
        rps = next(r for r in range(BF16_SUBLANES, rows + 1, BF16_SUBLANES)
                   if rows % r == 0 and r * n_steps >= rows)
        n_blk = rows // rps

        def blk(i, k, n_blk=n_blk):
            return jnp.minimum(i * nk + k, n_blk - 1)

        side_in_specs.append(pl.BlockSpec((None,) * len(lead) + (rps, cols),
                                          functools.partial(lambda i, k, lead, blk: (*lead, blk(i, k), 0),
                                                            lead=lead, blk=blk)))
        side_out_specs.append(pl.BlockSpec((rps, cols), functools.partial(lambda i, k, blk: (blk(i, k), 0), blk=blk)))
        side_out_shapes.append(jax.ShapeDtypeStruct((rows, cols), BF16))
        side_args.append(w_side)

    a, a2, w_out_bf, *rounded = pl.pallas_call(
        functools.partial(_ffn_up_kernel, row_chunk=min(tm, NORM_ROWS)),
        grid=(m // tm, nk),
        in_specs=[
            pl.BlockSpec((tm, D_MODEL), lambda i, k: (i, 0)),
            mod_spec, mod_spec,
            full2, full2, full2,
            pl.BlockSpec((1, D_MODEL), lambda i, k: (0, 0)),
            pl.BlockSpec((None, None, D_MODEL, tf), lambda i, k: (layer, which, 0, k)),
            pl.BlockSpec((None, None, D_MODEL, tf), lambda i, k: (layer, which, 0, k + nk)),
        ] + side_in_specs,
        out_specs=[pl.BlockSpec((tm, tf), lambda i, k: (i, k)),
                   pl.BlockSpec((r2, tf), lambda i, k: (0, jnp.where(i == 0, k, nk - 1)))] + side_out_specs,
        out_shape=[jax.ShapeDtypeStruct((m, D_FF), BF16), jax.ShapeDtypeStruct((r2, D_FF), BF16)] + side_out_shapes,
        scratch_shapes=[pltpu.VMEM((tm + r2, D_MODEL), BF16)],
        compiler_params=_params("arbitrary", "arbitrary"),
        name="ffn_up",
    )(x, mods[0], mods[1], x2, mods2[0], mods2[1], g_norm, w_in, w_in, *side_args)
    out = _proj_residual_call(a, x, mods[2], w_out_bf, (), False, tm, tn, gate_scale=0.5, name="ffn_down")
    out2 = _proj_residual_call(a2, x2, mods2[2].reshape(1, r2, D_MODEL), w_out_bf, (), False, r2, tn,
                               gate_scale=0.5, name="ffn_down")
    return out, out2, rounded


def _norm_matmul_kernel(x_ref, sh_ref, sc_ref, x2_ref, sh2_ref, sc2_ref, g_ref, w_ref, *rest, row_chunk, head_rows):
    h_ref = rest[-1]
    outs = rest[:-1]
    n_large = len(outs) - (2 if head_rows else 1)
    tm = x_ref.shape[0]
    _norm_two_row_sets(x_ref, sh_ref, sc_ref, x2_ref, sh2_ref, sc2_ref, g_ref, h_ref, row_chunk)

    def to_heads(res, ref):
        rows, cols = res.shape
        heads_per_tile = cols // HEAD_DIM
        first = (pl.program_id(1) % (head_rows // heads_per_tile)) * heads_per_tile
        for c in range(heads_per_tile):
            ref[pl.ds(first + c, rows, stride=head_rows), :] = res[:, c * HEAD_DIM:(c + 1) * HEAD_DIM]

    def emit_large(res):
        outs[0][...] = res.astype(outs[0].dtype)
        if head_rows:
            tiles_per_slab = head_rows * HEAD_DIM // res.shape[1]
            for s, ref in enumerate(outs[1:n_large]):
                @pl.when(pl.program_id(1) // tiles_per_slab == s)
                def _(ref=ref):
                    to_heads(res, ref)

    def emit_small(res):
        outs[n_large][...] = res.astype(outs[n_large].dtype)
        if head_rows:
            to_heads(res, outs[n_large + 1])

    @pl.when(pl.program_id(0) == 0)
    def _():
        res = jnp.dot(h_ref[...], w_ref[...].astype(BF16), preferred_element_type=F32)
        emit_large(res[:tm])
        emit_small(res[tm:])

    @pl.when(pl.program_id(0) != 0)
    def _():
        emit_large(jnp.dot(h_ref[pl.ds(0, tm), :], w_ref[...].astype(BF16), preferred_element_type=F32))


def _norm_matmul_call(x, shift, scale, x2, shift2, scale2, g_norm, w, w_lead, out_dtype, tm, tn, head_rows=0,
                      single_buffer_x=False):
    m, r2 = x.shape[0], x2.shape[0]
    nb = shift.shape[0]
    n = w.shape[-1]
    nj = n // tn
    tiles_per_seq = (m // nb) // tm
    mod_spec = pl.BlockSpec((None, 1, D_MODEL), lambda i, j: (i // tiles_per_seq, 0, 0))
    full2 = pl.BlockSpec((r2, D_MODEL), lambda i, j: (0, 0))

    def col2(i, j):
        return jnp.where(i == 0, j, nj - 1)

    if head_rows:
        slab = head_rows * HEAD_DIM
        tps = slab // tn
        n_slabs = n // slab

        def heads_spec(s):
            return pl.BlockSpec((tm * head_rows, HEAD_DIM),
                                lambda i, j: (jnp.where(j // tps >= s, i, jnp.maximum(i - 1, 0)), 0))

        out_spec = ((pl.BlockSpec((None, tm, tn), lambda i, j: (j // tps, i, j % tps)),)
                    + tuple(heads_spec(s) for s in range(n_slabs))
                    + (pl.BlockSpec((None, r2, tn), lambda i, j: (col2(i, j) // tps, 0, col2(i, j) % tps)),
                       pl.BlockSpec((None, r2 * head_rows, HEAD_DIM), lambda i, j: (col2(i, j) // tps, 0, 0))))
        out_shape = ((jax.ShapeDtypeStruct((n_slabs, m, slab), out_dtype),)
                     + tuple(jax.ShapeDtypeStruct((m * head_rows, HEAD_DIM), out_dtype) for _ in range(n_slabs))
                     + (jax.ShapeDtypeStruct((n_slabs, r2, slab), out_dtype),
                        jax.ShapeDtypeStruct((n_slabs, r2 * head_rows, HEAD_DIM), out_dtype)))
    else:
        out_spec = (pl.BlockSpec((tm, tn), lambda i, j: (i, j)),
                    pl.BlockSpec((r2, tn), lambda i, j: (0, col2(i, j))))
        out_shape = (jax.ShapeDtypeStruct((m, n), out_dtype), jax.ShapeDtypeStruct((r2, n), out_dtype))
    return pl.pallas_call(
        functools.partial(_norm_matmul_kernel, row_chunk=min(tm, NORM_ROWS), head_rows=head_rows),
        grid=(m // tm, nj),
        in_specs=[
            (pl.BlockSpec((tm, D_MODEL), lambda i, j: (i, 0), pipeline_mode=pl.Buffered(1)) if single_buffer_x
             else pl.BlockSpec((tm, D_MODEL), lambda i, j: (i, 0))),
            mod_spec, mod_spec,
            full2, full2, full2,
            pl.BlockSpec((1, D_MODEL), lambda i, j: (0, 0)),
            pl.BlockSpec((None,) * len(w_lead) + (D_MODEL, tn), lambda i, j: (*w_lead, 0, j)),
        ],
        out_specs=out_spec,
        out_shape=out_shape,
        scratch_shapes=[pltpu.VMEM((tm + r2, D_MODEL), BF16)],
        compiler_params=_params("arbitrary", "arbitrary"),
        name="norm_matmul",
    )(x, shift, scale, x2, shift2, scale2, g_norm, w)


def _proj_residual_kernel(z_ref, x_ref, gt_ref, *refs, glu, gate_scale):
    z = z_ref[...]
    if glu:
        wa_ref, wg_ref, o_ref = refs
        a = jnp.dot(z, wa_ref[...].astype(BF16), preferred_element_type=F32)
        g = jnp.dot(z, wg_ref[...].astype(BF16), preferred_element_type=F32)
        y = a * jax.nn.sigmoid(g)
    else:
        w_ref, o_ref = refs
        y = jnp.dot(z, w_ref[...].astype(BF16), preferred_element_type=F32)
    gate = gt_ref[...] if gate_scale == 1.0 else gate_scale * gt_ref[...]
    o_ref[...] = x_ref[...] + gate * y


def _proj_residual_call(z, x, gate, w, w_lead, glu, tm, tn, gate_scale=1.0, name="proj_residual"):
    m = x.shape[0]
    kdim = z.shape[1]
    nb, r, _ = gate.shape
    tiles_per_seq = (m // nb) // tm if r == 1 else 1
    n_blocks = D_MODEL // tn
    w_block = (None,) * len(w_lead) + (kdim, tn)
    w_specs = [pl.BlockSpec(w_block, lambda i, j: (*w_lead, 0, j))]
    w_args = [w]
    if glu:
        w_specs.append(pl.BlockSpec(w_block, lambda i, j: (*w_lead, 0, j + n_blocks)))
        w_args.append(w)
    return pl.pallas_call(
        functools.partial(_proj_residual_kernel, glu=glu, gate_scale=gate_scale),
        grid=(m // tm, n_blocks),
        in_specs=[
            pl.BlockSpec((tm, kdim), lambda i, j: (i, 0)),
            pl.BlockSpec((tm, tn), lambda i, j: (i, j)),
            pl.BlockSpec((None, r, tn), lambda i, j: (i // tiles_per_seq, 0, j)),
        ] + w_specs,
        out_specs=pl.BlockSpec((tm, tn), lambda i, j: (i, j)),
        out_shape=jax.ShapeDtypeStruct((m, D_MODEL), F32),
        compiler_params=_params("parallel", "arbitrary"),
        name=name,
    )(z, x, gate, *w_args)


def _final_norm_kernel(x_ref, sh_ref, sc_ref, g_ref, o_ref, *, row_chunk):
    _norm_rows(x_ref, g_ref, sh_ref, sc_ref, o_ref, row_chunk)


def _final_norm_call(x, shift, scale, g_norm, tm):
    m = x.shape[0]
    nb, r, _ = shift.shape
    tiles_per_seq = (m // nb) // tm if r == 1 else 1
    mod_spec = pl.BlockSpec((None, r, D_MODEL), lambda i: (i // tiles_per_seq, 0, 0))
    return pl.pallas_call(
        functools.partial(_final_norm_kernel, row_chunk=min(tm, NORM_ROWS)),
        grid=(m // tm,),
        in_specs=[
            pl.BlockSpec((tm, D_MODEL), lambda i: (i, 0)),
            mod_spec, mod_spec,
            pl.BlockSpec((1, D_MODEL), lambda i: (0, 0)),
        ],
        out_specs=pl.BlockSpec((tm, D_MODEL), lambda i: (i, 0)),
        out_shape=jax.ShapeDtypeStruct((m, D_MODEL), F32),
        compiler_params=_params("parallel"),
        name="final_norm",
    )(x, shift, scale, g_norm)


def _s5_discretise(lam_re, lam_im, log_dt):
    dt = jnp.exp(log_dt)
    decay = jnp.exp(lam_re * dt)
    ab_re = decay * jnp.cos(lam_im * dt)
    ab_im = decay * jnp.sin(lam_im * dt)
    den = lam_re * lam_re + lam_im * lam_im
    f_re = ((ab_re - 1.0) * lam_re + ab_im * lam_im) / den
    f_im = (ab_im * lam_re - (ab_re - 1.0) * lam_im) / den
    return ab_re, ab_im, f_re, f_im


def _s5_prep_kernel(lre_ref, lim_ref, ldt_ref, bre_ref, bim_ref, ctre_ref, ctim_ref, lre_g, lim_g, ldt_g,
                    are_ref, aim_ref, inre_ref, inim_ref, outre_ref, outim_ref):
    c, p = S5_GROUP, S5_STATE
    ab_re, ab_im, _, _ = _s5_discretise(lre_g[...], lim_g[...], ldt_g[...])
    are_ref[...] = ab_re
    aim_ref[...] = ab_im
    _, _, f_re, f_im = _s5_discretise(lre_ref[...], lim_ref[...], ldt_ref[...])
    b_re, b_im = bre_ref[...], bim_ref[...]
    bb_re = f_re * b_re - f_im * b_im
    bb_im = f_re * b_im + f_im * b_re
    for ref in (inre_ref, inim_ref, outre_ref, outim_ref):
        ref[...] = jnp.zeros(ref.shape, ref.dtype)
    for gl in range(S5_CH_BLOCK // c):
        rows_in, cols_in = slice(gl * c, (gl + 1) * c), slice(gl * p, (gl + 1) * p)
        inre_ref[rows_in, cols_in] = bb_re[rows_in].astype(inre_ref.dtype)
        inim_ref[rows_in, cols_in] = bb_im[rows_in].astype(inim_ref.dtype)
        outre_ref[cols_in, rows_in] = ctre_ref[cols_in, :].astype(outre_ref.dtype)
        outim_ref[cols_in, rows_in] = ctim_ref[cols_in, :].astype(outim_ref.dtype)


def _s5_prep_call(lam_re, lam_im, log_dt, b_re, b_im, c_re, c_im):
    g, p, c = S5_GROUPS, S5_STATE, S5_GROUP
    rep = lambda a: jnp.repeat(a, c, axis=0)
    b_t = lambda b: b.transpose(0, 2, 1).reshape(g * c, p)
    c_t = lambda m: m.transpose(0, 2, 1).reshape(g * p, c)
    gpb = S5_CH_BLOCK // c
    row_blk = lambda rows, cols: pl.BlockSpec((rows, cols), lambda i: (i, 0))
    blk3 = lambda rows, cols: pl.BlockSpec((None, rows, cols), lambda i: (i, 0, 0))
    return pl.pallas_call(
        _s5_prep_kernel,
        grid=(S5_N_BLOCKS,),
        in_specs=[row_blk(S5_CH_BLOCK, p), row_blk(S5_CH_BLOCK, p), row_blk(S5_CH_BLOCK, 1),
                  row_blk(S5_CH_BLOCK, p), row_blk(S5_CH_BLOCK, p),
                  row_blk(S5_COL_BLOCK, c), row_blk(S5_COL_BLOCK, c),
                  row_blk(gpb, p), row_blk(gpb, p), row_blk(gpb, 1)],
        out_specs=(row_blk(gpb, p), row_blk(gpb, p),
                   blk3(S5_CH_BLOCK, S5_COL_BLOCK), blk3(S5_CH_BLOCK, S5_COL_BLOCK),
                   blk3(S5_COL_BLOCK, S5_CH_BLOCK), blk3(S5_COL_BLOCK, S5_CH_BLOCK)),
        out_shape=(jax.ShapeDtypeStruct((g, p), F32), jax.ShapeDtypeStruct((g, p), F32),
                   jax.ShapeDtypeStruct((S5_N_BLOCKS, S5_CH_BLOCK, S5_COL_BLOCK), BF16),
                   jax.ShapeDtypeStruct((S5_N_BLOCKS, S5_CH_BLOCK, S5_COL_BLOCK), BF16),
                   jax.ShapeDtypeStruct((S5_N_BLOCKS, S5_COL_BLOCK, S5_CH_BLOCK), BF16),
                   jax.ShapeDtypeStruct((S5_N_BLOCKS, S5_COL_BLOCK, S5_CH_BLOCK), BF16)),
        compiler_params=_params("parallel"),
        name="s5_prepare",
    )(rep(lam_re), rep(lam_im), rep(log_dt.reshape(g, 1)), b_t(b_re), b_t(b_im), c_t(c_re), c_t(c_im),
      lam_re, lam_im, log_dt.reshape(g, 1))


def _s5_kernel(x_ref, sh_ref, sc_ref, g_ref, bbre_ref, bbim_ref, are_ref, aim_ref, cre_ref, cim_ref,
               d_ref, h0re_ref, h0im_ref, z_ref, hre_out, him_out,
               u_scr, sre_scr, sim_scr, y_scr, pwre_scr, pwim_scr, *, row_chunk):
    t = pl.program_id(0)
    nb, steps, _ = x_ref.shape
    rows = steps * nb
    lane = HEAD_DIM
    seg = SUBLANES // nb
    seg_len = steps // seg
    assert seg in (1, 2)

    @pl.when(t == 0)
    def _():
        hre_out[...] = h0re_ref[...]
        him_out[...] = h0im_ref[...]
        if seg > 1:
            zero = jnp.zeros((nb, S5_COL_BLOCK), F32)
            for cb in range(S5_N_BLOCKS):
                a_re = jnp.broadcast_to(are_ref[cb], (nb, S5_COL_BLOCK))
                a_im = jnp.broadcast_to(aim_ref[cb], (nb, S5_COL_BLOCK))
                p_re, p_im = a_re, a_im
                for i in range(seg_len):
                    pwre_scr[cb, i * SUBLANES:(i + 1) * SUBLANES, :] = jnp.concatenate([zero, p_re], axis=0)
                    pwim_scr[cb, i * SUBLANES:(i + 1) * SUBLANES, :] = jnp.concatenate([zero, p_im], axis=0)
                    p_re, p_im = p_re * a_re - p_im * a_im, p_re * a_im + p_im * a_re

    g = g_ref[...]
    for b in range(nb):
        for s in range(seg):
            def norm_chunk(c, carry, b=b, s=s):
                r0 = pl.multiple_of(c * row_chunk, row_chunk)
                u = _rms_mod(x_ref[b, pl.ds(s * seg_len + r0, row_chunk), :], g, sh_ref[b], sc_ref[b])
                for j in range(D_MODEL // lane):
                    u_scr[j, pl.ds(r0 * SUBLANES + s * nb + b, row_chunk, stride=SUBLANES), :] = \
                        u[:, j * lane:(j + 1) * lane]
                return carry

            lax.fori_loop(0, seg_len // row_chunk, norm_chunk, 0)

    tiles_per_block = S5_CH_BLOCK // lane
    per_block_scratch = sre_scr.shape[0] == S5_N_BLOCKS
    for cb in range(S5_N_BLOCKS):
        ch = slice(cb * S5_CH_BLOCK, (cb + 1) * S5_CH_BLOCK)
        st = slice(cb * S5_COL_BLOCK, (cb + 1) * S5_COL_BLOCK)
        slab = cb if per_block_scratch else 0
        sre, sim, ys = sre_scr.at[slab], sim_scr.at[slab], y_scr.at[slab]
        u_blk = jnp.concatenate([u_scr[cb * tiles_per_block + j] for j in range(tiles_per_block)], axis=1)
        u_bf = u_blk.astype(BF16)
        sre[...] = jnp.dot(u_bf, bbre_ref[cb], preferred_element_type=F32)
        sim[...] = jnp.dot(u_bf, bbim_ref[cb], preferred_element_type=F32)

        a_re = jnp.broadcast_to(are_ref[cb], (SUBLANES, S5_COL_BLOCK))
        a_im = jnp.broadcast_to(aim_ref[cb], (SUBLANES, S5_COL_BLOCK))

        def tile_step(i, carry, a_re=a_re, a_im=a_im, sre=sre, sim=sim):
            h_re, h_im = carry
            r0 = i * SUBLANES if isinstance(i, int) else pl.multiple_of(i * SUBLANES, SUBLANES)
            n_re = a_re * h_re - a_im * h_im + sre[pl.ds(r0, SUBLANES), :]
            n_im = a_re * h_im + a_im * h_re + sim[pl.ds(r0, SUBLANES), :]
            sre[pl.ds(r0, SUBLANES), :] = n_re
            sim[pl.ds(r0, SUBLANES), :] = n_im
            return n_re, n_im

        def first_rows(v):
            return jnp.concatenate([v[:nb]] * seg, axis=0)

        h_re, h_im = hre_out[:, st], him_out[:, st]
        if seg > 1:
            zero = jnp.zeros((SUBLANES - nb, S5_COL_BLOCK), F32)
            h_re, h_im = jnp.concatenate([h_re, zero], axis=0), jnp.concatenate([h_im, zero], axis=0)
        carry = (h_re, h_im)
        if per_block_scratch:
            for i in range(seg_len):
                carry = tile_step(i, carry)
        else:
            carry = lax.fori_loop(0, seg_len, tile_step, carry, unroll=min(seg_len, 4))
        h_re, h_im = carry
        if seg > 1:
            c_re, c_im = first_rows(h_re), first_rows(h_im)
            for i in range(seg_len):
                rows_i = slice(i * SUBLANES, (i + 1) * SUBLANES)
                p_re, p_im = pwre_scr[cb, rows_i, :], pwim_scr[cb, rows_i, :]
                f_re = sre[rows_i, :] + (p_re * c_re - p_im * c_im)
                f_im = sim[rows_i, :] + (p_re * c_im + p_im * c_re)
                sre[rows_i, :] = f_re
                sim[rows_i, :] = f_im
            h_re, h_im = f_re[SUBLANES - nb:], f_im[SUBLANES - nb:]
        hre_out[:, st] = h_re
        him_out[:, st] = h_im

        y = (jnp.dot(sre[...].astype(BF16), cre_ref[cb], preferred_element_type=F32)
             - jnp.dot(sim[...].astype(BF16), cim_ref[cb], preferred_element_type=F32))
        zf = jax.nn.gelu(y + d_ref[:, ch] * u_blk)
        for j in range(tiles_per_block):
            ys[j] = zf[:, j * lane:(j + 1) * lane]
        for b in range(nb):
            for s in range(seg):
                for j in range(tiles_per_block):
                    c0 = cb * S5_CH_BLOCK + j * lane
                    z_ref[b, s * seg_len:(s + 1) * seg_len, c0:c0 + lane] = \
                        ys[j, pl.ds(s * nb + b, seg_len, stride=SUBLANES), :].astype(z_ref.dtype)


def _s5_call(x, shift, scale, g_norm, bb_re, bb_im, a_re, a_im, c_re, c_im, d_skip,
             h0_re, h0_im, steps_per_chunk, z_dtype, per_block_scratch):
    nb, seq, _ = x.shape
    assert SUBLANES % nb == 0 and seq % steps_per_chunk == 0
    tr = steps_per_chunk * nb
    slabs = S5_N_BLOCKS if per_block_scratch else 1
    seg = SUBLANES // nb
    pw_rows = tr if seg > 1 else SUBLANES
    row_chunk = min(steps_per_chunk // (SUBLANES // nb), NORM_ROWS)
    const2 = lambda t: (0, 0)
    const3 = lambda t: (0, 0, 0)
    state_spec = pl.BlockSpec((nb, S5_NSTATE), const2)
    io_spec = pl.BlockSpec((nb, steps_per_chunk, D_MODEL), lambda t: (0, t, 0))
    return pl.pallas_call(
        functools.partial(_s5_kernel, row_chunk=row_chunk),
        grid=(seq // steps_per_chunk,),
        in_specs=[
            io_spec,
            pl.BlockSpec((nb, 1, D_MODEL), const3),
            pl.BlockSpec((nb, 1, D_MODEL), const3),
            pl.BlockSpec((1, D_MODEL), const2),
            pl.BlockSpec((S5_N_BLOCKS, S5_CH_BLOCK, S5_COL_BLOCK), const3, pipeline_mode=pl.Buffered(1)),
            pl.BlockSpec((S5_N_BLOCKS, S5_CH_BLOCK, S5_COL_BLOCK), const3, pipeline_mode=pl.Buffered(1)),
            pl.BlockSpec((S5_N_BLOCKS, 1, S5_COL_BLOCK), const3),
            pl.BlockSpec((S5_N_BLOCKS, 1, S5_COL_BLOCK), const3),
            pl.BlockSpec((S5_N_BLOCKS, S5_COL_BLOCK, S5_CH_BLOCK), const3, pipeline_mode=pl.Buffered(1)),
            pl.BlockSpec((S5_N_BLOCKS, S5_COL_BLOCK, S5_CH_BLOCK), const3, pipeline_mode=pl.Buffered(1)),
            pl.BlockSpec((1, D_MODEL), const2),
            state_spec, state_spec,
        ],
        out_specs=(io_spec, state_spec, state_spec),
        out_shape=(jax.ShapeDtypeStruct((nb, seq, D_MODEL), z_dtype),
                   jax.ShapeDtypeStruct((nb, S5_NSTATE), F32),
                   jax.ShapeDtypeStruct((nb, S5_NSTATE), F32)),
        scratch_shapes=[pltpu.VMEM((D_MODEL // HEAD_DIM, tr, HEAD_DIM), F32),
                        pltpu.VMEM((slabs, tr, S5_COL_BLOCK), F32),
                        pltpu.VMEM((slabs, tr, S5_COL_BLOCK), F32),
                        pltpu.VMEM((slabs, S5_CH_BLOCK // HEAD_DIM, tr, HEAD_DIM), F32),
                        pltpu.VMEM((S5_N_BLOCKS, pw_rows, S5_COL_BLOCK), F32),
                        pltpu.VMEM((S5_N_BLOCKS, pw_rows, S5_COL_BLOCK), F32)],
        compiler_params=_params("arbitrary"),
        name="s5_mixer",
    )(x, shift, scale, g_norm, bb_re, bb_im, a_re, a_im, c_re, c_im, d_skip, h0_re, h0_im)


def _softmax_pv(s2, v_bf):
    m = jnp.max(s2, axis=-1, keepdims=True)
    p = jnp.exp2(s2 - m)
    l = jnp.sum(p, axis=-1, keepdims=True)
    o = jnp.dot(p.astype(BF16), v_bf, preferred_element_type=F32) / l
    return o, jnp.broadcast_to(m + jnp.log2(l), o.shape)


def _qk(q_bf, k_bf, scale=ATTN_SCALE):
    return lax.dot_general(q_bf, k_bf, (((1,), (1,)), ((), ())), preferred_element_type=F32) * scale


def _attn_prompt_kernel(q0_ref, q1_ref, q2_ref, k0_ref, v0_ref, k1_ref, v1_ref, k2_ref, v2_ref, o_ref,
                        qf_scr, o1_scr, l1_scr, o2_scr, l2_scr, band_scr, first_scr):
    seq = q0_ref.shape[0]
    qb = Q_BLOCK
    row = lax.broadcasted_iota(jnp.int32, (REP * qb, 2 * qb), 0) & (qb - 1)
    col = lax.broadcasted_iota(jnp.int32, (REP * qb, 2 * qb), 1)
    band_scr[...] = jnp.where((col >= row) & (col <= row + qb), 0.0, -jnp.inf)
    row1 = lax.broadcasted_iota(jnp.int32, (REP * qb, qb), 0) & (qb - 1)
    col1 = lax.broadcasted_iota(jnp.int32, (REP * qb, qb), 1)
    first_scr[...] = jnp.where(col1 <= row1, 0.0, -jnp.inf)

    def rows_of(start, d):
        return pl.ds(start, qb) if d == 1 else pl.ds(start, qb, stride=d)

    def load_q(src_ref, start, d):
        if d == 1:
            parts = [src_ref[pl.ds(start, qb), e * HEAD_DIM:(e + 1) * HEAD_DIM] for e in range(REP)]
        else:
            parts = [src_ref[e, rows_of(start, d), :] for e in range(REP)]
        return jnp.concatenate(parts, axis=0).astype(BF16)

    def unit(src_q, k_ref, v_ref, start, d, first):
        q4 = load_q(src_q, start, d)
        k_cur = k_ref[rows_of(start, d), :]
        v_cur = v_ref[rows_of(start, d), :]
        if first:
            s = _qk(q4, k_cur.astype(BF16), ATTN_SCALE_LOG2) + first_scr[...]
            return _softmax_pv(s, v_cur.astype(BF16))
        prev = start - qb * d
        k_band = jnp.concatenate([k_ref[rows_of(prev, d), :], k_cur], axis=0)
        v_band = jnp.concatenate([v_ref[rows_of(prev, d), :], v_cur], axis=0)
        s = _qk(q4, k_band.astype(BF16), ATTN_SCALE_LOG2) + band_scr[...]
        return _softmax_pv(s, v_band.astype(BF16))

    def stage_group(q_ref, k_ref, v_ref, d, o_scr, l_scr):
        n_blocks = seq // (qb * d)

        def copy(c, carry):
            r0 = pl.multiple_of(c * STAGE_ROWS, STAGE_ROWS)
            for e in range(REP):
                qf_scr[e, pl.ds(r0, STAGE_ROWS), :] = \
                    q_ref[pl.ds(r0, STAGE_ROWS), e * HEAD_DIM:(e + 1) * HEAD_DIM].astype(F32)
            return carry

        lax.fori_loop(0, seq // STAGE_ROWS, copy, 0)

        def store(start, o, lse):
            for e in range(REP):
                sl = slice(e * qb, (e + 1) * qb)
                o_scr[e, rows_of(start, d), :] = o[sl]
                l_scr[e, rows_of(start, d), :] = lse[sl]

        def per_class(r, carry):
            store(r, *unit(qf_scr, k_ref, v_ref, r, d, True))

            def per_block(jb, c2):
                start = jb * (qb * d) + r
                store(start, *unit(qf_scr, k_ref, v_ref, start, d, False))
                return c2

            if n_blocks > 1:
                lax.fori_loop(1, n_blocks, per_block, 0, unroll=True)
            return carry

        lax.fori_loop(0, d, per_class, 0, unroll=4 if n_blocks == 1 else 2)

    stage_group(q2_ref, k2_ref, v2_ref, DILATED_PATTERNS[2][1], o2_scr, l2_scr)
    stage_group(q1_ref, k1_ref, v1_ref, DILATED_PATTERNS[1][1], o1_scr, l1_scr)

    def merge(start, o0, lse0):
        for e in range(REP):
            sl = slice(e * qb, (e + 1) * qb)
            cols = slice(e * HEAD_DIM, (e + 1) * HEAD_DIM)
            la = lse0[sl]
            lb = l1_scr[e, pl.ds(start, qb), :]
            lc = l2_scr[e, pl.ds(start, qb), :]
            mx = jnp.maximum(jnp.maximum(la, lb), lc)
            wa = jnp.exp2(la - mx)
            wb = jnp.exp2(lb - mx)
            wc = jnp.exp2(lc - mx)
            inv = 1.0 / (wa + wb + wc)
            acc = (wa * o0[sl] + wb * o1_scr[e, pl.ds(start, qb), :] + wc * o2_scr[e, pl.ds(start, qb), :]) * inv
            o_ref[pl.ds(start, qb), cols] = acc.astype(o_ref.dtype)

    merge(0, *unit(q0_ref, k0_ref, v0_ref, 0, 1, True))

    def per_block0(jb, carry):
        start = pl.multiple_of(jb * qb, qb)
        merge(start, *unit(q0_ref, k0_ref, v0_ref, start, 1, False))
        return carry

    n_blocks0 = seq // qb
    lax.fori_loop(1, n_blocks0, per_block0, 0, unroll=5 if (n_blocks0 - 1) % 5 == 0 else 1)


def _attn_prompt_call(q, kv):
    b, seq, _ = q.shape
    qw = REP * HEAD_DIM
    q_specs = [pl.BlockSpec((None, seq, qw), functools.partial(lambda bi, h, g: (bi, 0, g * KV_HEADS + h), g=g))
               for g in range(N_DIL)]
    kv_specs = []
    for g in range(N_DIL):
        kv_specs.append(pl.BlockSpec((None, None, seq, HEAD_DIM),
                                     functools.partial(lambda bi, h, g: (g, bi, 0, h), g=g)))
        kv_specs.append(pl.BlockSpec((None, None, seq, HEAD_DIM),
                                     functools.partial(lambda bi, h, g: (g, bi, 0, KV_HEADS + h), g=g)))
    kv_args = [kv] * (2 * N_DIL)
    return pl.pallas_call(
        _attn_prompt_kernel,
        grid=(b, KV_HEADS),
        in_specs=q_specs + kv_specs,
        out_specs=pl.BlockSpec((None, seq, qw), lambda bi, h: (bi, 0, h)),
        out_shape=jax.ShapeDtypeStruct((b, seq, HEADS * HEAD_DIM), BF16),
        scratch_shapes=[pltpu.VMEM((REP, seq, HEAD_DIM), F32),
                        pltpu.VMEM((REP, seq, HEAD_DIM), F32), pltpu.VMEM((REP, seq, HEAD_DIM), F32),
                        pltpu.VMEM((REP, seq, HEAD_DIM), F32), pltpu.VMEM((REP, seq, HEAD_DIM), F32),
                        pltpu.VMEM((REP * Q_BLOCK, 2 * Q_BLOCK), F32), pltpu.VMEM((REP * Q_BLOCK, Q_BLOCK), F32)],
        compiler_params=_params("parallel", "arbitrary"),
        name="dilated_attention_prompt",
    )(q, q, q, *kv_args)


def _attn_sample_kernel(q_ref, kvn_ref, c0_ref, c1_ref, c2_ref, o_ref, *, n_tok):
    rows = REP * n_tok
    cache_refs = (c0_ref, c1_ref, c2_ref)
    kv_rows = 2 * KV_HEADS
    for h in range(KV_HEADS):
        outs, lses = [], []
        for g, (_, d) in enumerate(DILATED_PATTERNS):
            q = q_ref[g * KV_HEADS + h]
            lw = cache_refs[g].shape[0] // kv_rows
            k_c = cache_refs[g][pl.ds(h, lw, stride=kv_rows), :].astype(BF16)
            v_c = cache_refs[g][pl.ds(KV_HEADS + h, lw, stride=kv_rows), :].astype(BF16)
            k_n = kvn_ref[g, pl.ds(h, n_tok, stride=kv_rows), :].astype(BF16)
            v_n = kvn_ref[g, pl.ds(KV_HEADS + h, n_tok, stride=kv_rows), :].astype(BF16)
            tq_c = lax.broadcasted_iota(jnp.int32, (rows, lw), 0) & (n_tok - 1)
            idx_c = lax.broadcasted_iota(jnp.int32, (rows, lw), 1)
            ok_c = (idx_c >= tq_c) & (((idx_c - tq_c) & (d - 1)) == 0)
            tq_n = lax.broadcasted_iota(jnp.int32, (rows, n_tok), 0) & (n_tok - 1)
            idx_n = lax.broadcasted_iota(jnp.int32, (rows, n_tok), 1)
            ok_n = (idx_n <= tq_n) & (((tq_n - idx_n) & (d - 1)) == 0)
            s_c = jnp.where(ok_c, _qk(q, k_c), -jnp.inf)
            s_n = jnp.where(ok_n, _qk(q, k_n), -jnp.inf)
            m = jnp.maximum(jnp.max(s_c, axis=-1, keepdims=True), jnp.max(s_n, axis=-1, keepdims=True))
            p_c = jnp.exp(s_c - m)
            p_n = jnp.exp(s_n - m)
            l = jnp.sum(p_c, axis=-1, keepdims=True) + jnp.sum(p_n, axis=-1, keepdims=True)
            o = (jnp.dot(p_c.astype(BF16), v_c, preferred_element_type=F32)
                 + jnp.dot(p_n.astype(BF16), v_n, preferred_element_type=F32)) / l
            outs.append(o)
            lses.append(m + jnp.log(l))
        mx = jnp.maximum(jnp.maximum(lses[0], lses[1]), lses[2])
        w = [jnp.exp(x - mx) for x in lses]
        inv = 1.0 / (w[0] + w[1] + w[2])
        acc = (w[0] * inv) * outs[0] + (w[1] * inv) * outs[1] + (w[2] * inv) * outs[2]
        o_ref[h] = acc.astype(o_ref.dtype)


def _attn_sample_call(q_heads, kv_new, caches, n_tok):
    b = q_heads.shape[0]
    rows = REP * n_tok
    assert n_tok & (n_tok - 1) == 0
    in_specs = [pl.BlockSpec((None, N_DIL * KV_HEADS, rows, HEAD_DIM), lambda bi: (bi, 0, 0, 0)),
                pl.BlockSpec((N_DIL, None, kv_new.shape[2], HEAD_DIM), lambda bi: (0, bi, 0, 0))]
    in_specs += [pl.BlockSpec((None, c.shape[1], HEAD_DIM), lambda bi: (bi, 0, 0)) for c in caches]
    return pl.pallas_call(
        functools.partial(_attn_sample_kernel, n_tok=n_tok),
        grid=(b,),
        in_specs=in_specs,
        out_specs=pl.BlockSpec((None, KV_HEADS, rows, HEAD_DIM), lambda bi: (bi, 0, 0, 0)),
        out_shape=jax.ShapeDtypeStruct((b, KV_HEADS, rows, HEAD_DIM), BF16),
        compiler_params=_params("parallel"),
        name="dilated_attention_sample",
    )(q_heads, kv_new, *caches)


def _s5_weights(lam_re, lam_im, log_dt, b_re, b_im, c_re, c_im):
    a_re, a_im, in_re, in_im, out_re, out_im = _s5_prep_call(lam_re, lam_im, log_dt, b_re, b_im, c_re, c_im)
    lam_bar = lambda a: a.reshape(S5_N_BLOCKS, 1, S5_COL_BLOCK)
    return in_re, in_im, lam_bar(a_re), lam_bar(a_im), out_re, out_im


def _trunks(x_p, x_s, mod_p, mod_s, s5_state, kv_caches, wts, tiles):
    (norm_g, ffn_w_in, ffn_w_out, s5_mats, s5_d, s5_w_glu, kv_norm_g, w_kv,
     attn_w_q, attn_w_o, final_norm_g) = wts
    bp, lp, _ = x_p.shape
    bs, ls, _ = x_s.shape
    mp, ms = bp * lp, bs * ls
    tm = tiles["tm"]
    kv_rows = 2 * KV_HEADS

    def mods_p(*ks):
        return tuple(mod_p[:, k].reshape(bp, 1, D_MODEL) for k in ks)

    def mods_s(*ks):
        return tuple(jnp.repeat(mod_s[:, k], ls, axis=0) for k in ks)

    def as_block(a):
        return a.reshape(1, *a.shape)

    def norm_g_row(a):
        return a.reshape(1, D_MODEL)

    xp = x_p.reshape(mp, D_MODEL)
    xs = x_s.reshape(ms, D_MODEL)
    re_p, im_p, re_s, im_s = [], [], [], []
    for layer in range(DEPTH):
        base = 9 * layer
        if layer == N_A_LAYERS:
            k0 = 9 * DEPTH
            kvf_p, *kv_p, _, kv_s = _norm_matmul_call(xp, *mods_p(k0, k0 + 1), xs, *mods_s(k0, k0 + 1),
                                                     norm_g_row(kv_norm_g), w_kv_bf, (), F32, tm,
                                                     KV_HEADS * HEAD_DIM, head_rows=kv_rows)
        later = (((s5_w_glu, (layer,)),) if layer < N_A_LAYERS else
                 ((attn_w_q, (layer - N_A_LAYERS,)), (attn_w_o, (layer - N_A_LAYERS,))))
        xp, xs, rounded = _ffn_call(xp, mods_p(base, base + 1, base + 2), xs, mods_s(base, base + 1, base + 2),
                                    norm_g_row(norm_g[layer, 0]), ffn_w_in, ffn_w_out, layer, 0, tm, tiles["tf"],
                                    tiles["tn_down"], round_also=later)
        g_mix = norm_g_row(norm_g[layer, 1])
        if layer < N_A_LAYERS:
            d_skip = s5_d[layer].reshape(1, D_MODEL)
            zero_state = jnp.zeros((bp, S5_NSTATE), F32)
            zp, h_re, h_im = _s5_call(xp.reshape(bp, lp, D_MODEL), *mods_p(base + 3, base + 4), g_mix,
                                      *s5_mats[layer], d_skip, zero_state, zero_state,
                                      min(lp, tiles["s5_steps"]), BF16, True)
            re_p.append(h_re.reshape(bp, S5_GROUPS, S5_STATE))
            im_p.append(h_im.reshape(bp, S5_GROUPS, S5_STATE))
            zs, h_re, h_im = _s5_call(xs.reshape(bs, ls, D_MODEL), mod_s[:, base + 3].reshape(bs, 1, D_MODEL),
                                      mod_s[:, base + 4].reshape(bs, 1, D_MODEL), g_mix, *s5_mats[layer], d_skip,
                                      s5_state[0][layer].reshape(bs, S5_NSTATE),
                                      s5_state[1][layer].reshape(bs, S5_NSTATE), ls, F32, False)
            re_s.append(h_re.reshape(bs, S5_GROUPS, S5_STATE))
            im_s.append(h_im.reshape(bs, S5_GROUPS, S5_STATE))
            (w_glu_bf,) = rounded
            xp = _proj_residual_call(zp.reshape(mp, D_MODEL), xp, *mods_p(base + 5), w_glu_bf, (),
                                     True, tm, tiles["tn_mix"], name="s5_glu")
            xs = _proj_residual_call(zs.reshape(ms, D_MODEL).astype(BF16), xs, as_block(*mods_s(base + 5)),
                                     w_glu_bf, (), True, ms, tiles["tn_mix"], name="s5_glu")
        else:
            w_q_bf, w_o_bf = rounded
            q_p, q_s = _norm_matmul_call(xp, *mods_p(base + 3, base + 4), xs, *mods_s(base + 3, base + 4), g_mix,
                                         w_q_bf, (), BF16, tm, tiles["tn_q"], single_buffer_x=True)
            o_p = _attn_prompt_call(q_p.reshape(bp, lp, N_DIL * HEADS * HEAD_DIM),
                                    kvf_p.reshape(N_DIL, bp, lp, kv_rows * HEAD_DIM)).reshape(mp, D_MODEL)
            qh = q_s.reshape(bs, ls, N_DIL, KV_HEADS, REP, HEAD_DIM).transpose(0, 2, 3, 4, 1, 5)
            qh = qh.reshape(bs, N_DIL * KV_HEADS, REP * ls, HEAD_DIM)
            o_s = _attn_sample_call(qh, kv_s.reshape(N_DIL, bs, ls * kv_rows, HEAD_DIM), kv_caches, ls)
            o_s = o_s.reshape(bs, KV_HEADS, REP, ls, HEAD_DIM).transpose(0, 3, 1, 2, 4).reshape(ms, D_MODEL)
            xp = _proj_residual_call(o_p, xp, *mods_p(base + 5), w_o_bf, (), False, tiles["tm_out"],
                                     tiles["tn_mix"], name="attn_out")
            xs = _proj_residual_call(o_s, xs, as_block(*mods_s(base + 5)), w_o_bf, (), False, ms,
                                     tiles["tn_mix"], name="attn_out")
        later = ((w_kv, ()),) if layer == N_A_LAYERS - 1 else ()
        xp, xs, rounded = _ffn_call(xp, mods_p(base + 6, base + 7, base + 8), xs, mods_s(base + 6, base + 7, base + 8),
                                    norm_g_row(norm_g[layer, 2]), ffn_w_in, ffn_w_out, layer, 1, tm, tiles["tf"],
                                    tiles["tn_down"], round_also=later)
        if later:
            (w_kv_bf,) = rounded
    k1 = 9 * DEPTH + 2
    y_p = _final_norm_call(xp, *mods_p(k1, k1 + 1), norm_g_row(final_norm_g), tm)
    y_s = _final_norm_call(xs, *(as_block(a) for a in mods_s(k1, k1 + 1)), norm_g_row(final_norm_g), ms)
    kv_p = [a.reshape(bp, lp, 2, KV_HEADS, HEAD_DIM) for a in kv_p]
    kv_s = kv_s.reshape(N_DIL, bs, ls, 2, KV_HEADS, HEAD_DIM)
    return ((y_p.reshape(bp, lp, D_MODEL), jnp.stack(re_p, axis=0), jnp.stack(im_p, axis=0), kv_p),
            (y_s.reshape(bs, ls, D_MODEL), jnp.stack(re_s, axis=0), jnp.stack(im_s, axis=0), kv_s))


def kernel(x_prompt, x_sample, c_prompt, c_sample, state_s5_re, state_s5_im, cache_kv_g0, cache_kv_g1, cache_kv_g2, w_mod, b_mod, norm_g, ffn_w_in, ffn_w_out, s5_lambda_re, s5_lambda_im, s5_log_dt, s5_b_re, s5_b_im, s5_c_re, s5_c_im, s5_d, s5_w_glu, kv_norm_g, w_kv, attn_w_q, attn_w_o, final_norm_g):
    bp, seq_p, _ = x_prompt.shape
    bs, seq_s, _ = x_sample.shape
    caches = (cache_kv_g0, cache_kv_g1, cache_kv_g2)
    for (w, d), c in zip(DILATED_PATTERNS, caches):
        assert c.shape[1] == w and w == d * Q_BLOCK and seq_p % (d * Q_BLOCK) == 0
    assert bp <= SUBLANES and bs <= SUBLANES

    n_c = bp + bs
    c_all = jnp.pad(jnp.concatenate([c_prompt, c_sample], axis=0), ((0, (-n_c) % SUBLANES), (0, 0)))
    mod_all = _mod_call(c_all, w_mod, b_mod)
    mod_p = mod_all[:bp].reshape(bp, N_MOD, D_MODEL)
    mod_s = mod_all[bp:n_c].reshape(bs, N_MOD, D_MODEL)

    s5_mats = [_s5_weights(s5_lambda_re[l], s5_lambda_im[l], s5_log_dt[l], s5_b_re[l], s5_b_im[l],
                           s5_c_re[l], s5_c_im[l]) for l in range(N_A_LAYERS)]
    wts = (norm_g, ffn_w_in, ffn_w_out, s5_mats, s5_d, s5_w_glu, kv_norm_g, w_kv, attn_w_q, attn_w_o, final_norm_g)

    caches_flat = tuple(c.reshape(bs, c.shape[1] * 2 * KV_HEADS, HEAD_DIM) for c in caches)
    (y_p, re_p, im_p, kv_p), (y_s, re_s, im_s, kv_s) = _trunks(
        x_prompt, x_sample, mod_p, mod_s, (state_s5_re, state_s5_im), caches_flat, wts,
        _TILES)

    kvp = [kv_p[g][:, seq_p - min(w, seq_p):] for g, (w, _) in enumerate(DILATED_PATTERNS)]
    return (y_p, y_s, re_p, im_p, kvp[0], kvp[1], kvp[2], re_s, im_s, kv_s[0], kv_s[1], kv_s[2])
```
